```python
import math
import jax
import jax.numpy as jnp
from jax import lax
import numpy as np

D_MODEL = 1024
BATCH = 8
SEQ = 2048
DEPTH = 2
DEC_BATCH = 128
DEC_SEQ = 4
PAST_LEN = 16384
PAGE_SIZE = 128

S5_WIDTH = D_MODEL // 4
S5_GROUP = 16
S5_GROUPS = S5_WIDTH // S5_GROUP
S5_STATE = 64
S5_DT_MIN = 1e-3
S5_DT_MAX = 1e-1
RW_WIDTH = D_MODEL // 2
RW_HEAD = 64
RW_HEADS = RW_WIDTH // RW_HEAD
RW_DECAY_LORA = 32
RW_AAA_LORA = 32
RW_GATE_LORA = 64
RW_COLS = 3 * RW_WIDTH + RW_DECAY_LORA + RW_AAA_LORA + RW_GATE_LORA
RW_SPLITS = (RW_WIDTH, 2 * RW_WIDTH, 3 * RW_WIDTH, 3 * RW_WIDTH + RW_DECAY_LORA,
             3 * RW_WIDTH + RW_DECAY_LORA + RW_AAA_LORA)
RW_GN_EPS = 1e-5 * RW_HEAD
HG_WIDTH = D_MODEL // 4
HG_HEAD = 64
HG_HEADS = HG_WIDTH // HG_HEAD
HG_CHUNK = 16
HG_GATE_FLOOR = 1e-30
N_BRANCH = 3
IN_COLS = S5_WIDTH + RW_COLS + 4 * HG_WIDTH + N_BRANCH * D_MODEL
IN_SPLITS = (S5_WIDTH, S5_WIDTH + RW_COLS, S5_WIDTH + RW_COLS + HG_WIDTH,
             S5_WIDTH + RW_COLS + 2 * HG_WIDTH, S5_WIDTH + RW_COLS + 3 * HG_WIDTH,
             S5_WIDTH + RW_COLS + 4 * HG_WIDTH)
D_FF = 256 * ((8 * D_MODEL // 3 + 255) // 256)
CONV_W = 3
RMS_EPS = 1e-6
F32 = jnp.float32

kernel_name = 'hybrid_s5_rwkv7_hgrn2_convffn_step'


def rms_norm(x, g):
    xf = x.astype(F32)
    return xf * lax.rsqrt(jnp.mean(xf * xf, axis=-1, keepdims=True) + RMS_EPS) * g.astype(F32)


def modulate(x, g, shift, scale):
    return rms_norm(x, g) * (1.0 + scale[:, None, :]) + shift[:, None, :]


def _complex_affine_combine(e1, e2):
    a1r, a1i, b1r, b1i = e1
    a2r, a2i, b2r, b2i = e2
    return (a2r * a1r - a2i * a1i, a2r * a1i + a2i * a1r,
            a2r * b1r - a2i * b1i + b2r, a2r * b1i + a2i * b1r + b2i)


def s5_mixer(u, s_prev, lam_re, lam_im, log_dt, b_re, b_im, c_re, c_im, d_skip, w_glu, b_glu):
    n, l, _ = u.shape
    ug = u.astype(F32).reshape(n, l, S5_GROUPS, S5_GROUP)
    lr = lam_re.astype(F32)
    li = lam_im.astype(F32)
    dt = jnp.exp(log_dt.astype(F32))[:, None]
    mag = jnp.exp(lr * dt)
    ar = mag * jnp.cos(li * dt)
    ai = mag * jnp.sin(li * dt)
    den = lr * lr + li * li
    zr = ((ar - 1.0) * lr + ai * li) / den
    zi = (ai * lr - (ar - 1.0) * li) / den
    bbr = zr[..., None] * b_re - zi[..., None] * b_im
    bbi = zr[..., None] * b_im + zi[..., None] * b_re
    er = jnp.einsum('nlgc,gpc->nlgp', ug, bbr)
    ei = jnp.einsum('nlgc,gpc->nlgp', ug, bbi)
    pr = s_prev[..., 0].astype(F32)
    pi = s_prev[..., 1].astype(F32)
    er = er.at[:, 0].add(ar * pr - ai * pi)
    ei = ei.at[:, 0].add(ar * pi + ai * pr)
    ar_t = jnp.broadcast_to(ar, er.shape)
    ai_t = jnp.broadcast_to(ai, ei.shape)
    _, _, sr, si = lax.associative_scan(_complex_affine_combine, (ar_t, ai_t, er, ei), axis=1)
    y = jnp.einsum('nlgp,gcp->nlgc', sr, c_re) - jnp.einsum('nlgp,gcp->nlgc', si, c_im)
    y = y.reshape(n, l, S5_WIDTH) + d_skip * u
    y = jax.nn.gelu(y)
    y = y * jax.nn.sigmoid(y @ w_glu + b_glu)
    return y, jnp.stack([sr[:, -1], si[:, -1]], axis=-1)


def rwkv7_mixer(p, shift_prev, s_prev, mu, w0, w2, a0, a2, g2, k_k, k_a, r_k, ln_w, ln_b):
    n, l, _ = p.shape
    p = p.astype(F32)
    p_prev = jnp.concatenate([shift_prev[:, None, :].astype(F32), p[:, :-1]], axis=1)
    xm = p + (p_prev - p) * mu
    r, k, v, wd, ad, gd = jnp.split(xm, RW_SPLITS, axis=-1)
    w = -jax.nn.softplus(-(w0 + jnp.tanh(wd) @ w2)) - 0.5
    decay = jnp.exp(-jnp.exp(w))
    a = jax.nn.sigmoid(a0 + ad @ a2)
    g = jax.nn.sigmoid(gd) @ g2

    def hd(t):
        return t.reshape(n, l, RW_HEADS, RW_HEAD)

    kk = hd(k * k_k)
    kk = kk / jnp.maximum(jnp.sqrt(jnp.sum(kk * kk, axis=-1, keepdims=True)), 1e-12)
    k = hd(k * (1.0 + (a - 1.0) * k_a))
    r, v, decay, a = hd(r), hd(v), hd(decay), hd(a)
    seq = tuple(jnp.moveaxis(t, 1, 0) for t in (r, decay, k, v, -kk, kk * a))

    def step(s, inp):
        r_t, w_t, k_t, v_t, a_t, b_t = inp
        sa = jnp.einsum('nhvk,nhk->nhv', s, a_t)
        s = (s * w_t[:, :, None, :] + sa[..., None] * b_t[:, :, None, :]
             + v_t[..., None] * k_t[:, :, None, :])
        return s, jnp.einsum('nhvk,nhk->nhv', s, r_t)

    s_last, y = lax.scan(step, s_prev.astype(F32), seq)
    y = jnp.moveaxis(y, 0, 1)
    mean = jnp.mean(y, axis=-1, keepdims=True)
    var = jnp.mean(jnp.square(y - mean), axis=-1, keepdims=True)
    y = ((y - mean) * lax.rsqrt(var + RW_GN_EPS)).reshape(n, l, RW_WIDTH) * ln_w + ln_b
    bonus = jnp.sum(r * k * r_k, axis=-1, keepdims=True) * v
    y = (y + bonus.reshape(n, l, RW_WIDTH)) * g
    return y, p[:, -1], s_last


def hgrn2_mixer(q, f, i, og, s_prev, lower, norm_g):
    n, l, _ = q.shape
    lower = lower.astype(F32)
    sig_f = jax.nn.sigmoid(f.astype(F32))
    fgate = lower + (1.0 - lower) * sig_f
    log_f = jnp.log(jnp.maximum(fgate, HG_GATE_FLOOR))
    k = 1.0 - fgate
    q = jax.nn.silu(q.astype(F32))
    pad = (-l) % HG_CHUNK
    nc = (l + pad) // HG_CHUNK

    def to_blocks(t):
        t = t.reshape(n, l, HG_HEADS, HG_HEAD)
        t = jnp.pad(t, ((0, 0), (0, pad), (0, 0), (0, 0)))
        return t.reshape(n, nc, HG_CHUNK, HG_HEADS, HG_HEAD).transpose(1, 0, 3, 2, 4)

    causal = jnp.tril(jnp.ones((HG_CHUNK, HG_CHUNK), dtype=bool))[:, :, None]

    def step(s, blk):
        qc, kc, vc, gc = blk
        b = jnp.cumsum(gc, axis=2)
        o_inter = jnp.einsum('nhtk,nhkv->nhtv', qc * jnp.exp(b), s)
        diff = b[:, :, :, None, :] - b[:, :, None, :, :]
        dec = jnp.where(causal, jnp.exp(jnp.where(causal, diff, 0.0)), 0.0)
        att = jnp.einsum('nhtk,nhsk,nhtsk->nhts', qc, kc, dec)
        o_intra = jnp.einsum('nhts,nhsv->nhtv', att, vc)
        b_end = b[:, :, -1]
        s = (jnp.exp(b_end)[..., None] * s
             + jnp.einsum('nhsk,nhsv->nhkv', kc * jnp.exp(b_end[:, :, None, :] - b), vc))
        return s, o_inter + o_intra

    blocks = tuple(to_blocks(t) for t in (q, k, i.astype(F32), log_f))
    s_last, o = lax.scan(step, s_prev.astype(F32), blocks)
    o = o.transpose(1, 0, 3, 2, 4).reshape(n, nc * HG_CHUNK, HG_HEADS, HG_HEAD)[:, :l]
    o = o * lax.rsqrt(jnp.mean(o * o, axis=-1, keepdims=True) + RMS_EPS)
    o = o.reshape(n, l, HG_WIDTH) * norm_g * jax.nn.sigmoid(og)
    return o, s_last


def causal_dwconv(buf, w, b, l):
    out = b + w[0] * buf[:, 0:l]
    for j in range(1, CONV_W):
        out = out + w[j] * buf[:, j:j + l]
    return out


def run_trunk(x, c, st_s5, st_shift, st_rw, st_hg, st_conv, lower, final_g, prm):
    n, l, _ = x.shape
    cs = jax.nn.silu(c.astype(F32))
    h_res = x.astype(F32)
    out_s5, out_shift, out_rw, out_hg, out_conv = [], [], [], [], []
    for layer in range(DEPTH):
        p = {name: arr[layer] for name, arr in prm.items()}
        mod = cs @ p['w_ada'] + p['b_ada']
        sh1, sc1, gt1, sh2, sc2, gt2 = jnp.split(mod, 6, axis=-1)
        h = modulate(h_res, p['g_mix'], sh1, sc1)
        z = h @ p['w_in']
        u_a, p_b, q_c, f_c, i_c, og_c, gate_logits = jnp.split(z, IN_SPLITS, axis=-1)
        y_a, s5_new = s5_mixer(u_a, st_s5[layer], p['s5_lambda_re'], p['s5_lambda_im'],
                               p['s5_log_dt'], p['s5_b_re'], p['s5_b_im'], p['s5_c_re'],
                               p['s5_c_im'], p['s5_d'], p['s5_w_glu'], p['s5_b_glu'])
        y_b, shift_new, rw_new = rwkv7_mixer(p_b, st_shift[layer], st_rw[layer], p['rw_mu'],
                                             p['rw_w0'], p['rw_w2'], p['rw_a0'], p['rw_a2'],
                                             p['rw_g2'], p['rw_k_k'], p['rw_k_a'], p['rw_r_k'],
                                             p['rw_ln_w'], p['rw_ln_b'])
        y_c, hg_new = hgrn2_mixer(q_c, f_c, i_c, og_c, st_hg[layer], lower[layer], p['hg_norm'])
        gates = jax.nn.sigmoid(gate_logits).reshape(n, l, N_BRANCH, D_MODEL)
        merged = (gates[:, :, 0] * (y_a @ p['w_lift_a'])
                  + gates[:, :, 1] * (y_b @ p['w_lift_b'])
                  + gates[:, :, 2] * (y_c @ p['w_lift_c']))
        h_res = h_res + gt1[:, None, :] * (merged @ p['w_out'])
        h2 = modulate(h_res, p['g_ffn'], sh2, sc2)
        up = h2 @ p['w_up']
        buf = jnp.concatenate([st_conv[layer].astype(F32), up], axis=1)
        conv = causal_dwconv(buf, p['conv_w'], p['conv_b'], l)
        a_ff, b_ff = jnp.split(conv, 2, axis=-1)
        h_res = h_res + gt2[:, None, :] * ((jax.nn.gelu(a_ff) * b_ff) @ p['w_down'])
        out_s5.append(s5_new)
        out_shift.append(shift_new)
        out_rw.append(rw_new)
        out_hg.append(hg_new)
        out_conv.append(buf[:, -(CONV_W - 1):])
    y = rms_norm(h_res, final_g).astype(x.dtype)
    return y, (jnp.stack(out_s5), jnp.stack(out_shift), jnp.stack(out_rw),
               jnp.stack(out_hg), jnp.stack(out_conv))


def setup_inputs(seed: int = 0) -> dict:
    keys = iter(jax.random.split(jax.random.key(seed), 64))

    def nrm(shape, std):
        return std * jax.random.normal(next(keys), shape, F32)

    def unif(shape, lo, hi):
        return jax.random.uniform(next(keys), shape, F32, lo, hi)

    L = DEPTH
    d_in = D_MODEL ** -0.5
    n_idx = jnp.arange(S5_STATE, dtype=F32)
    return {
        'x_prompt': nrm((BATCH, SEQ, D_MODEL), 1.0),
        'x_sample': nrm((DEC_BATCH, DEC_SEQ, D_MODEL), 1.0),
        'c_prompt': nrm((BATCH, D_MODEL), 1.0),
        'c_sample': nrm((DEC_BATCH, D_MODEL), 1.0),
        'state_s5': nrm((L, DEC_BATCH, S5_GROUPS, S5_STATE, 2), 0.2),
        'state_rwkv_shift': nrm((L, DEC_BATCH, RW_COLS), 1.0),
        'state_rwkv': nrm((L, DEC_BATCH, RW_HEADS, RW_HEAD, RW_HEAD), 0.3),
        'state_hgrn': nrm((L, DEC_BATCH, HG_HEADS, HG_HEAD, HG_HEAD), 0.5),
        'state_ffn_conv': nrm((L, DEC_BATCH, CONV_W - 1, 2 * D_FF), 1.0),
        'w_ada': nrm((L, D_MODEL, 6 * D_MODEL), 0.5 * d_in),
        'b_ada': nrm((L, 6 * D_MODEL), 0.01),
        'g_mix': 1.0 + nrm((L, D_MODEL), 0.1),
        'g_ffn': 1.0 + nrm((L, D_MODEL), 0.1),
        'w_in': nrm((L, D_MODEL, IN_COLS), d_in),
        's5_lambda_re': -0.5 + nrm((L, S5_GROUPS, S5_STATE), 0.01),
        's5_lambda_im': math.pi * n_idx + nrm((L, S5_GROUPS, S5_STATE), 0.01),
        's5_log_dt': unif((L, S5_GROUPS), math.log(S5_DT_MIN), math.log(S5_DT_MAX)),
        's5_b_re': nrm((L, S5_GROUPS, S5_STATE, S5_GROUP), (2 * S5_GROUP) ** -0.5),
        's5_b_im': nrm((L, S5_GROUPS, S5_STATE, S5_GROUP), (2 * S5_GROUP) ** -0.5),
        's5_c_re': nrm((L, S5_GROUPS, S5_GROUP, S5_STATE), 0.5),
        's5_c_im': nrm((L, S5_GROUPS, S5_GROUP, S5_STATE), 0.5),
        's5_d': nrm((L, S5_WIDTH), 0.5),
        's5_w_glu': nrm((L, S5_WIDTH, S5_WIDTH), S5_WIDTH ** -0.5),
        's5_b_glu': nrm((L, S5_WIDTH), 0.01),
        'rw_mu': unif((L, RW_COLS), 0.0, 1.0),
        'rw_w0': unif((L, RW_WIDTH), -6.0, 1.0),
        'rw_w2': nrm((L, RW_DECAY_LORA, RW_WIDTH), 0.1),
        'rw_a0': nrm((L, RW_WIDTH), 0.5),
        'rw_a2': nrm((L, RW_AAA_LORA, RW_WIDTH), 0.1),
        'rw_g2': nrm((L, RW_GATE_LORA, RW_WIDTH), RW_GATE_LORA ** -0.5),
        'rw_k_k': 0.85 + nrm((L, RW_WIDTH), 0.1),
        'rw_k_a': 1.0 + nrm((L, RW_WIDTH), 0.1),
        'rw_r_k': nrm((L, RW_HEADS, RW_HEAD), 0.1),
        'rw_ln_w': 1.0 + nrm((L, RW_WIDTH), 0.1),
        'rw_ln_b': nrm((L, RW_WIDTH), 0.01),
        'hg_lb': nrm((L, HG_WIDTH), 1.0),
        'hg_norm': 1.0 + nrm((L, HG_WIDTH), 0.1),
        'w_lift_a': nrm((L, S5_WIDTH, D_MODEL), S5_WIDTH ** -0.5),
        'w_lift_b': nrm((L, RW_WIDTH, D_MODEL), RW_WIDTH ** -0.5),
        'w_lift_c': nrm((L, HG_WIDTH, D_MODEL), HG_WIDTH ** -0.5),
        'w_out': nrm((L, D_MODEL, D_MODEL), d_in),
        'w_up': nrm((L, D_MODEL, 2 * D_FF), d_in),
        'conv_w': nrm((L, CONV_W, 2 * D_FF), CONV_W ** -0.5),
        'conv_b': nrm((L, 2 * D_FF), 0.01),
        'w_down': nrm((L, D_FF, D_MODEL), D_FF ** -0.5),
        'final_g': 1.0 + nrm((D_MODEL,), 0.1),
    }


def reference(x_prompt, x_sample, c_prompt, c_sample, state_s5, state_rwkv_shift, state_rwkv,
              state_hgrn, state_ffn_conv, w_ada, b_ada, g_mix, g_ffn, w_in, s5_lambda_re,
              s5_lambda_im, s5_log_dt, s5_b_re, s5_b_im, s5_c_re, s5_c_im, s5_d, s5_w_glu,
              s5_b_glu, rw_mu, rw_w0, rw_w2, rw_a0, rw_a2, rw_g2, rw_k_k, rw_k_a, rw_r_k,
              rw_ln_w, rw_ln_b, hg_lb, hg_norm, w_lift_a, w_lift_b, w_lift_c, w_out, w_up,
              conv_w, conv_b, w_down, final_g):
    prm = {
        'w_ada': w_ada, 'b_ada': b_ada, 'g_mix': g_mix, 'g_ffn': g_ffn, 'w_in': w_in,
        's5_lambda_re': s5_lambda_re, 's5_lambda_im': s5_lambda_im, 's5_log_dt': s5_log_dt,
        's5_b_re': s5_b_re, 's5_b_im': s5_b_im, 's5_c_re': s5_c_re, 's5_c_im': s5_c_im,
        's5_d': s5_d, 's5_w_glu': s5_w_glu, 's5_b_glu': s5_b_glu,
        'rw_mu': rw_mu, 'rw_w0': rw_w0, 'rw_w2': rw_w2, 'rw_a0': rw_a0, 'rw_a2': rw_a2,
        'rw_g2': rw_g2, 'rw_k_k': rw_k_k, 'rw_k_a': rw_k_a, 'rw_r_k': rw_r_k,
        'rw_ln_w': rw_ln_w, 'rw_ln_b': rw_ln_b, 'hg_norm': hg_norm,
        'w_lift_a': w_lift_a, 'w_lift_b': w_lift_b, 'w_lift_c': w_lift_c, 'w_out': w_out,
        'w_up': w_up, 'conv_w': conv_w, 'conv_b': conv_b, 'w_down': w_down,
    }
    lbp = jax.nn.softmax(hg_lb.astype(F32), axis=0)
    lower = jnp.cumsum(lbp, axis=0) - lbp[0]

    nb = x_prompt.shape[0]
    z_s5 = jnp.zeros((DEPTH, nb) + state_s5.shape[2:], F32)
    z_shift = jnp.zeros((DEPTH, nb) + state_rwkv_shift.shape[2:], F32)
    z_rw = jnp.zeros((DEPTH, nb) + state_rwkv.shape[2:], F32)
    z_hg = jnp.zeros((DEPTH, nb) + state_hgrn.shape[2:], F32)
    z_conv = jnp.zeros((DEPTH, nb) + state_ffn_conv.shape[2:], F32)

    y_prompt, (s5_p, shift_p, rw_p, hg_p, conv_p) = run_trunk(
        x_prompt, c_prompt, z_s5, z_shift, z_rw, z_hg, z_conv, lower, final_g, prm)
    y_sample, (s5_s, shift_s, rw_s, hg_s, conv_s) = run_trunk(
        x_sample, c_sample, state_s5, state_rwkv_shift, state_rwkv, state_hgrn, state_ffn_conv,
        lower, final_g, prm)
    return (y_prompt, y_sample, s5_p, shift_p, rw_p, hg_p, conv_p, s5_s, shift_s, rw_s, hg_s, conv_s)
```

```python
import functools
import math

import jax
import jax.numpy as jnp
from jax import lax
from jax.experimental import pallas as pl
from jax.experimental.pallas import tpu as pltpu

F32 = jnp.float32
BF16 = jnp.bfloat16

D_MODEL = 1024
S5_WIDTH = D_MODEL // 4
S5_GROUP = 16
S5_GROUPS = S5_WIDTH // S5_GROUP
S5_STATE = 64
S5_FLAT = S5_GROUPS * S5_STATE
RW_WIDTH = D_MODEL // 2
RW_HEAD = 64
RW_HEADS = RW_WIDTH // RW_HEAD
RW_DECAY_LORA = 32
RW_AAA_LORA = 32
RW_GATE_LORA = 64
RW_LORA = RW_DECAY_LORA + RW_AAA_LORA + RW_GATE_LORA
RW_COLS = 3 * RW_WIDTH + RW_LORA
RW_GN_EPS = 1e-5 * RW_HEAD
HG_WIDTH = D_MODEL // 4
HG_HEAD = 64
HG_HEADS = HG_WIDTH // HG_HEAD
HG_GATE_FLOOR = 1e-30
N_BRANCH = 3
D_FF = 256 * ((8 * D_MODEL // 3 + 255) // 256)
CONV_W = 3
RMS_EPS = 1e-6

LANES = 128
SUBLANES = 8
VMEM_LIMIT_BYTES = 56 * 1024 * 1024

SEQ_BLOCK = SUBLANES
HEAD_PAIR = LANES // RW_HEAD
ROW_TILE = 512
FF_CHUNK = 256
IN_PROJ_CHUNK = 512


def _cparams(*sem):
    return pltpu.CompilerParams(dimension_semantics=sem, vmem_limit_bytes=VMEM_LIMIT_BYTES)


def _const_spec(shape):
    nd = len(shape)
    return pl.BlockSpec(shape, lambda *_: (0,) * nd, pipeline_mode=pl.Buffered(1))


def _dot(a, b):
    return jnp.dot(a, b, preferred_element_type=F32)


def _split_bf16(x):
    hi = x.astype(BF16)
    lo = (x - hi.astype(F32)).astype(BF16)
    return hi, lo


def _dot3(x, w_hi, w_lo):
    x_hi, x_lo = _split_bf16(x)
    return _dot(x_hi, w_hi) + _dot(x_lo, w_hi) + _dot(x_hi, w_lo)


def _head_ones(width):
    r = lax.broadcasted_iota(jnp.int32, (width, width), 0) // RW_HEAD
    c = lax.broadcasted_iota(jnp.int32, (width, width), 1) // RW_HEAD
    return (r == c).astype(BF16)


def _head_sum(x, ones):
    hi, lo = _split_bf16(x)
    return _dot(hi, ones) + _dot(lo, ones)


def _softplus(x):
    return jnp.maximum(x, 0.0) + jnp.log1p(jnp.exp(-jnp.abs(x)))


def _rms_modulate(x, g, shift, scale):
    ms = jnp.mean(x * x, axis=-1, keepdims=True)
    return x * lax.rsqrt(ms + RMS_EPS) * g * (1.0 + scale) + shift


def _ada_kernel(c_ref, w_ref, b_ref, o_ref):
    c = c_ref[...]
    o_ref[...] = _dot(jax.nn.silu(c).astype(BF16), w_ref[...]) + b_ref[...]


def _ada_call(c, w_bf16, b):
    n, d = c.shape
    cols = w_bf16.shape[1]
    tn = cols // 4
    return pl.pallas_call(
        _ada_kernel,
        out_shape=jax.ShapeDtypeStruct((n, cols), F32),
        grid=(cols // tn,),
        in_specs=[_const_spec((n, d)),
                  pl.BlockSpec((d, tn), lambda j: (0, j)),
                  pl.BlockSpec((1, tn), lambda j: (0, j))],
        out_specs=pl.BlockSpec((n, tn), lambda j: (0, j)),
        compiler_params=_cparams("arbitrary"),
        name="ada_mod",
    )(c, w_bf16, b.reshape(1, cols))


def _mod_operand(m, l, tm):
    n, d = m.shape
    if l % tm == 0:
        per = l // tm
        return m.reshape(n, 1, d), pl.BlockSpec((1, 1, d), lambda i: (i // per, 0, 0))
    assert tm % l == 0
    return jnp.repeat(m, l, axis=0).reshape(1, n * l, d), pl.BlockSpec((1, tm, d), lambda i: (0, i, 0))


def _in_proj_kernel(x_ref, g_ref, sh_ref, sc_ref, wa_ref, wb_ref, wc_ref, wg_ref,
                    oa_ref, ob_ref, oc_ref, og_ref):
    h = _rms_modulate(x_ref[...], g_ref[...], sh_ref[0], sc_ref[0]).astype(BF16)

    def project(w_ref, o_ref, fn=None):
        width = w_ref.shape[1]
        for c0 in range(0, width, IN_PROJ_CHUNK):
            cols = slice(c0, min(c0 + IN_PROJ_CHUNK, width))
            z = _dot(h, w_ref[:, cols])
            o_ref[:, cols] = z if fn is None else fn(z)

    project(wa_ref, oa_ref)
    project(wb_ref, ob_ref)
    project(wc_ref, oc_ref)
    project(wg_ref, og_ref, jax.nn.sigmoid)


def _in_proj_call(x2, l, g, shift, scale, w_a, w_b, w_c, w_g, tm):
    rows, d = x2.shape
    sh_arr, sh_spec = _mod_operand(shift, l, tm)
    sc_arr, sc_spec = _mod_operand(scale, l, tm)
    widths = [w.shape[1] for w in (w_a, w_b, w_c, w_g)]
    return pl.pallas_call(
        _in_proj_kernel,
        out_shape=[jax.ShapeDtypeStruct((rows, w), F32) for w in widths],
        grid=(rows // tm,),
        in_specs=[pl.BlockSpec((tm, d), lambda i: (i, 0)), _const_spec((1, d)), sh_spec, sc_spec]
        + [_const_spec(w.shape) for w in (w_a, w_b, w_c, w_g)],
        out_specs=[pl.BlockSpec((tm, w), lambda i: (i, 0)) for w in widths],
        compiler_params=_cparams("arbitrary"),
        name="in_proj",
    )(x2, g.reshape(1, d), sh_arr, sc_arr, w_a, w_b, w_c, w_g)


def _s5_kernel(u_ref, s0_ref, bh_ref, bl_ref, ch_ref, cl_ref, a_ref, d_ref, wg_ref, bg_ref,
               y_ref, sl_ref, e_ref, st_ref, *, t_chunk, n_steps):
    @pl.when(pl.program_id(1) == 0)
    def _():
        st_ref[...] = s0_ref[...]

    rows = SEQ_BLOCK * t_chunk
    ut = jnp.swapaxes(u_ref[...], 0, 1).reshape(rows, S5_WIDTH)
    e_ref[...] = _dot3(ut, bh_ref[...], bl_ref[...])

    ar = jnp.broadcast_to(a_ref[0:1, :], (SEQ_BLOCK, S5_FLAT))
    ai = jnp.broadcast_to(a_ref[1:2, :], (SEQ_BLOCK, S5_FLAT))

    def step(t, carry):
        sr, si = carry
        r = pl.ds(pl.multiple_of(t * SEQ_BLOCK, SEQ_BLOCK), SEQ_BLOCK)
        nr = ar * sr - ai * si + e_ref[r, 0:S5_FLAT]
        ni = ar * si + ai * sr + e_ref[r, S5_FLAT:2 * S5_FLAT]
        e_ref[r, 0:S5_FLAT] = nr
        e_ref[r, S5_FLAT:2 * S5_FLAT] = ni
        return nr, ni

    sr, si = lax.fori_loop(0, n_steps, step, (st_ref[:, 0:S5_FLAT], st_ref[:, S5_FLAT:2 * S5_FLAT]))
    st_ref[:, 0:S5_FLAT] = sr
    st_ref[:, S5_FLAT:2 * S5_FLAT] = si
    sl_ref[...] = st_ref[...]

    y = _dot3(e_ref[...], ch_ref[...], cl_ref[...]) + d_ref[...] * ut
    y = jax.nn.gelu(y)
    y = y * jax.nn.sigmoid(_dot(y.astype(BF16), wg_ref[...]) + bg_ref[...])
    y_ref[...] = jnp.swapaxes(y.reshape(t_chunk, SEQ_BLOCK, S5_WIDTH), 0, 1)


def _s5_call(u3, s0, prm, t_chunk, l_valid):
    n, lp, _ = u3.shape
    nb, nt = n // SEQ_BLOCK, lp // t_chunk
    assert nt == 1 or l_valid == lp
    kern = functools.partial(_s5_kernel, t_chunk=t_chunk, n_steps=min(t_chunk, l_valid))
    return pl.pallas_call(
        kern,
        out_shape=[jax.ShapeDtypeStruct((n, lp, S5_WIDTH), F32),
                   jax.ShapeDtypeStruct((n, 2 * S5_FLAT), F32)],
        grid=(nb, nt),
        in_specs=[pl.BlockSpec((SEQ_BLOCK, t_chunk, S5_WIDTH), lambda b, c: (b, c, 0)),
                  pl.BlockSpec((SEQ_BLOCK, 2 * S5_FLAT), lambda b, c: (b, 0)),
                  _const_spec((S5_WIDTH, 2 * S5_FLAT)), _const_spec((S5_WIDTH, 2 * S5_FLAT)),
                  _const_spec((2 * S5_FLAT, S5_WIDTH)), _const_spec((2 * S5_FLAT, S5_WIDTH)),
                  _const_spec((2, S5_FLAT)), _const_spec((1, S5_WIDTH)),
                  _const_spec((S5_WIDTH, S5_WIDTH)), _const_spec((1, S5_WIDTH))],
        out_specs=[pl.BlockSpec((SEQ_BLOCK, t_chunk, S5_WIDTH), lambda b, c: (b, c, 0)),
                   pl.BlockSpec((SEQ_BLOCK, 2 * S5_FLAT), lambda b, c: (b, 0))],
        scratch_shapes=[pltpu.VMEM((SEQ_BLOCK * t_chunk, 2 * S5_FLAT), F32),
                        pltpu.VMEM((SEQ_BLOCK, 2 * S5_FLAT), F32)],
        compiler_params=_cparams("arbitrary", "arbitrary"),
        name="s5_mixer",
    )(u3, s0, prm["s5_b_hi"], prm["s5_b_lo"], prm["s5_c_hi"], prm["s5_c_lo"], prm["s5_a"],
      prm["s5_d"], prm["s5_w_glu"], prm["s5_b_glu"])


def _recurrent_steps(n_steps, n_pairs, use_ab, s_ref, w_ref, k_ref, v_ref, r_ref, a_ref, b_ref, y_ref):
    tile = (RW_HEAD, LANES)
    ones = _head_ones(LANES)
    ri = lax.broadcasted_iota(jnp.int32, tile, 0)
    ci = lax.broadcasted_iota(jnp.int32, tile, 1)
    diag = (ci % RW_HEAD == ri).astype(F32)
    row8 = lax.broadcasted_iota(jnp.int32, (SUBLANES, LANES), 0)

    def block(base, steps):
        for hp in range(n_pairs):
            cols = pl.ds(hp * LANES, LANES)

            def rows8(ref):
                return [ref[n, pl.ds(base, SUBLANES), cols] for n in range(SEQ_BLOCK)]

            w8, k8, v8, r8 = rows8(w_ref), rows8(k_ref), rows8(v_ref), rows8(r_ref)
            if use_ab:
                a8, b8 = rows8(a_ref), rows8(b_ref)
            y8 = [jnp.zeros((SUBLANES, LANES), F32) for _ in range(SEQ_BLOCK)]
            for j in range(steps):
                def bc(x8, n):
                    return jnp.broadcast_to(x8[n][j:j + 1, :], tile)

                s = [s_ref[hp, n] for n in range(SEQ_BLOCK)]
                vm = jnp.concatenate([diag * bc(v8, n) for n in range(SEQ_BLOCK)], axis=0)
                vcol = _dot(vm.astype(BF16), ones)
                if use_ab:
                    t1 = jnp.concatenate([s[n] * bc(a8, n) for n in range(SEQ_BLOCK)], axis=0)
                    sa = _head_sum(t1, ones)
                new = []
                for n in range(SEQ_BLOCK):
                    rs = slice(n * RW_HEAD, (n + 1) * RW_HEAD)
                    sn = s[n] * bc(w8, n) + vcol[rs] * bc(k8, n)
                    if use_ab:
                        sn = sn + sa[rs] * bc(b8, n)
                    s_ref[hp, n] = sn
                    new.append(sn * bc(r8, n))
                ycol = _dot(jnp.concatenate(new, axis=0).astype(BF16), ones)
                for n in range(SEQ_BLOCK):
                    yrow = jnp.sum(ycol[n * RW_HEAD:(n + 1) * RW_HEAD] * diag, axis=0, keepdims=True)
                    y8[n] = jnp.where(row8 == j, jnp.broadcast_to(yrow, (SUBLANES, LANES)), y8[n])
            for n in range(SEQ_BLOCK):
                y_ref[n, pl.ds(base, SUBLANES), cols] = y8[n]

    full, rem = divmod(n_steps, SUBLANES)
    if full:
        def body(tb, carry):
            block(pl.multiple_of(tb * SUBLANES, SUBLANES), SUBLANES)
            return carry
        lax.fori_loop(0, full, body, 0)
    if rem:
        block(full * SUBLANES, rem)


def _rw_kernel(p_ref, sh0_ref, s0_ref, mu_ref, vec_ref, w2_ref, a2_ref, g2_ref,
               y_ref, shl_ref, sl_ref,
               pbuf_ref, s_ref, qw_ref, qk_ref, qv_ref, qr_ref, qa_ref, qb_ref, ys_ref,
               *, t_chunk, n_steps, last_row):
    rows = SEQ_BLOCK * t_chunk

    @pl.when(pl.program_id(1) == 0)
    def _():
        pbuf_ref[:, SUBLANES - 1:SUBLANES, :] = sh0_ref[...]
        s_ref[...] = s0_ref[...]

    p3 = p_ref[...]
    pbuf_ref[:, SUBLANES:SUBLANES + t_chunk, :] = p3
    prev3 = pbuf_ref[:, SUBLANES - 1:SUBLANES - 1 + t_chunk, :]
    last = p_ref[:, last_row:last_row + 1, :]
    pbuf_ref[:, SUBLANES - 1:SUBLANES, :] = last
    shl_ref[...] = last

    xm = (p3 + (prev3 - p3) * mu_ref[...]).reshape(rows, RW_COLS)
    r = xm[:, 0:RW_WIDTH]
    k = xm[:, RW_WIDTH:2 * RW_WIDTH]
    v = xm[:, 2 * RW_WIDTH:3 * RW_WIDTH]
    lora = xm[:, 3 * RW_WIDTH:RW_COLS]
    w0, a0, k_k, k_a = vec_ref[0:1, :], vec_ref[1:2, :], vec_ref[2:3, :], vec_ref[3:4, :]
    r_k, ln_w, ln_b = vec_ref[4:5, :], vec_ref[5:6, :], vec_ref[6:7, :]

    w = -_softplus(-(w0 + _dot(jnp.tanh(lora).astype(BF16), w2_ref[...]))) - 0.5
    decay = jnp.exp(-jnp.exp(w))
    a = jax.nn.sigmoid(a0 + _dot(lora.astype(BF16), a2_ref[...]))
    g = _dot(jax.nn.sigmoid(lora).astype(BF16), g2_ref[...])

    ones = _head_ones(RW_WIDTH)
    kk = k * k_k
    kk = kk / jnp.maximum(jnp.sqrt(_head_sum(kk * kk, ones)), 1e-12)
    kt = k * (1.0 + (a - 1.0) * k_a)

    shape3 = (SEQ_BLOCK, t_chunk, RW_WIDTH)
    qw_ref[...] = decay.reshape(shape3)
    qk_ref[...] = kt.reshape(shape3)
    qv_ref[...] = v.reshape(shape3)
    qr_ref[...] = r.reshape(shape3)
    qa_ref[...] = (-kk).reshape(shape3)
    qb_ref[...] = (kk * a).reshape(shape3)
    if n_steps < t_chunk:
        ys_ref[...] = jnp.zeros(shape3, F32)

    _recurrent_steps(n_steps, RW_HEADS // HEAD_PAIR, True, s_ref, qw_ref, qk_ref, qv_ref, qr_ref,
                     qa_ref, qb_ref, ys_ref)
    sl_ref[...] = s_ref[...]

    y = ys_ref[...].reshape(rows, RW_WIDTH)
    inv = 1.0 / RW_HEAD
    mean = _head_sum(y, ones) * inv
    yc = y - mean
    var = _head_sum(yc * yc, ones) * inv
    y = yc * lax.rsqrt(var + RW_GN_EPS) * ln_w + ln_b
    bonus = _head_sum(r * kt * r_k, ones) * v
    y_ref[...] = ((y + bonus) * g).reshape(shape3)


def _rw_call(p3, shift0, s0, prm, t_chunk, l_valid):
    n, lp, _ = p3.shape
    nb, nt = n // SEQ_BLOCK, lp // t_chunk
    assert nt == 1 or l_valid == lp
    assert l_valid % SUBLANES == 0 or nt == 1
    n_pairs = RW_HEADS // HEAD_PAIR
    kern = functools.partial(_rw_kernel, t_chunk=t_chunk, n_steps=min(t_chunk, l_valid),
                             last_row=min(t_chunk, l_valid) - 1)
    blk3 = lambda w: pl.BlockSpec((SEQ_BLOCK, t_chunk, w), lambda b, c: (b, c, 0))
    st_spec = pl.BlockSpec((n_pairs, SEQ_BLOCK, RW_HEAD, LANES), lambda b, c: (0, b, 0, 0))
    sh_spec = pl.BlockSpec((SEQ_BLOCK, 1, RW_COLS), lambda b, c: (b, 0, 0))
    q_scr = pltpu.VMEM((SEQ_BLOCK, t_chunk, RW_WIDTH), F32)
    return pl.pallas_call(
        kern,
        out_shape=[jax.ShapeDtypeStruct((n, lp, RW_WIDTH), F32),
                   jax.ShapeDtypeStruct((n, 1, RW_COLS), F32),
                   jax.ShapeDtypeStruct((n_pairs, n, RW_HEAD, LANES), F32)],
        grid=(nb, nt),
        in_specs=[blk3(RW_COLS), sh_spec, st_spec,
                  _const_spec((1, RW_COLS)), _const_spec((SUBLANES, RW_WIDTH)),
                  _const_spec((RW_LORA, RW_WIDTH)), _const_spec((RW_LORA, RW_WIDTH)),
                  _const_spec((RW_LORA, RW_WIDTH))],
        out_specs=[blk3(RW_WIDTH), sh_spec, st_spec],
        scratch_shapes=[pltpu.VMEM((SEQ_BLOCK, t_chunk + SUBLANES, RW_COLS), F32),
                        pltpu.VMEM((n_pairs, SEQ_BLOCK, RW_HEAD, LANES), F32)] + [q_scr] * 7,
        compiler_params=_cparams("arbitrary", "arbitrary"),
        name="rwkv7_mixer",
    )(p3, shift0, s0, prm["rw_mu"], prm["rw_vec"], prm["rw_w2"], prm["rw_a2"], prm["rw_g2"])


def _hg_kernel(z_ref, s0_ref, lower_ref, ng_ref, y_ref, sl_ref,
               s_ref, qw_ref, qk_ref, qv_ref, qr_ref, ys_ref, *, t_chunk, n_steps):
    rows = SEQ_BLOCK * t_chunk

    @pl.when(pl.program_id(1) == 0)
    def _():
        s_ref[...] = s0_ref[...]

    z = z_ref[...].reshape(rows, 4 * HG_WIDTH)
    q = z[:, 0:HG_WIDTH]
    f = z[:, HG_WIDTH:2 * HG_WIDTH]
    i = z[:, 2 * HG_WIDTH:3 * HG_WIDTH]
    og = z[:, 3 * HG_WIDTH:4 * HG_WIDTH]
    lower = lower_ref[...]
    fgate = lower + (1.0 - lower) * jax.nn.sigmoid(f)
    shape3 = (SEQ_BLOCK, t_chunk, HG_WIDTH)
    qw_ref[...] = jnp.maximum(fgate, HG_GATE_FLOOR).reshape(shape3)
    qk_ref[...] = (1.0 - fgate).reshape(shape3)
    qv_ref[...] = i.reshape(shape3)
    qr_ref[...] = jax.nn.silu(q).reshape(shape3)
    if n_steps < t_chunk:
        ys_ref[...] = jnp.zeros(shape3, F32)

    _recurrent_steps(n_steps, HG_HEADS // HEAD_PAIR, False,
                     s_ref, qw_ref, qk_ref, qv_ref, qr_ref, None, None, ys_ref)
    sl_ref[...] = s_ref[...]

    o = ys_ref[...].reshape(rows, HG_WIDTH)
    ms = _head_sum(o * o, _head_ones(HG_WIDTH)) * (1.0 / HG_HEAD)
    o = o * lax.rsqrt(ms + RMS_EPS) * ng_ref[...] * jax.nn.sigmoid(og)
    y_ref[...] = o.reshape(shape3)


def _hg_call(z3, s0, lower, norm_g, t_chunk, l_valid):
    n, lp, _ = z3.shape
    nb, nt = n // SEQ_BLOCK, lp // t_chunk
    assert nt == 1 or l_valid == lp
    n_pairs = HG_HEADS // HEAD_PAIR
    kern = functools.partial(_hg_kernel, t_chunk=t_chunk, n_steps=min(t_chunk, l_valid))
    blk3 = lambda w: pl.BlockSpec((SEQ_BLOCK, t_chunk, w), lambda b, c: (b, c, 0))
    st_spec = pl.BlockSpec((n_pairs, SEQ_BLOCK, HG_HEAD, LANES), lambda b, c: (0, b, 0, 0))
    q_scr = pltpu.VMEM((SEQ_BLOCK, t_chunk, HG_WIDTH), F32)
    return pl.pallas_call(
        kern,
        out_shape=[jax.ShapeDtypeStruct((n, lp, HG_WIDTH), F32),
                   jax.ShapeDtypeStruct((n_pairs, n, HG_HEAD, LANES), F32)],
        grid=(nb, nt),
        in_specs=[blk3(4 * HG_WIDTH), st_spec, _const_spec((1, HG_WIDTH)), _const_spec((1, HG_WIDTH))],
        out_specs=[blk3(HG_WIDTH), st_spec],
        scratch_shapes=[pltpu.VMEM((n_pairs, SEQ_BLOCK, HG_HEAD, LANES), F32)] + [q_scr] * 5,
        compiler_params=_cparams("arbitrary", "arbitrary"),
        name="hgrn2_mixer",
    )(z3, s0, lower.reshape(1, HG_WIDTH), norm_g.reshape(1, HG_WIDTH))


def _merge_kernel(x_ref, ya_ref, yb_ref, yc_ref, gates_ref, gt_ref, la_ref, lb_ref, lc_ref, wo_ref, o_ref):
    d = D_MODEL
    m = (gates_ref[:, 0:d] * _dot(ya_ref[...].astype(BF16), la_ref[...])
         + gates_ref[:, d:2 * d] * _dot(yb_ref[...].astype(BF16), lb_ref[...])
         + gates_ref[:, 2 * d:3 * d] * _dot(yc_ref[...].astype(BF16), lc_ref[...]))
    o_ref[...] = x_ref[...] + gt_ref[0] * _dot(m.astype(BF16), wo_ref[...])


def _merge_call(x2, l, ya, yb, yc, gates, gate1, prm, tm):
    rows, d = x2.shape
    gt_arr, gt_spec = _mod_operand(gate1, l, tm)
    row_spec = lambda w: pl.BlockSpec((tm, w), lambda i: (i, 0))
    return pl.pallas_call(
        _merge_kernel,
        out_shape=jax.ShapeDtypeStruct((rows, d), F32),
        grid=(rows // tm,),
        in_specs=[row_spec(d), row_spec(S5_WIDTH), row_spec(RW_WIDTH), row_spec(HG_WIDTH),
                  row_spec(N_BRANCH * d), gt_spec,
                  _const_spec((S5_WIDTH, d)), _const_spec((RW_WIDTH, d)), _const_spec((HG_WIDTH, d)),
                  _const_spec((d, d))],
        out_specs=row_spec(d),
        compiler_params=_cparams("arbitrary"),
        name="branch_merge",
    )(x2, ya, yb, yc, gates, gt_arr, prm["w_lift_a"], prm["w_lift_b"], prm["w_lift_c"], prm["w_out"])


def _ffn_kernel(*refs, l, tm, has_state, final_norm):
    if has_state:
        (x_ref, g_ref, sh_ref, sc_ref, gt_ref, wup_ref, cw_ref, cb_ref, wdn_ref, fg_ref,
         h0_ref, h1_ref, o_ref, up_ref, h_ref, buf_ref, hist_ref) = refs
    else:
        (x_ref, g_ref, sh_ref, sc_ref, gt_ref, wup_ref, cw_ref, cb_ref, wdn_ref, fg_ref,
         o_ref, tail_ref, h_ref, buf_ref, hist_ref) = refs
    i = pl.program_id(0)
    n_chunks = D_FF // FF_CHUNK

    @pl.when(i == 0)
    def _():
        hist_ref[...] = jnp.zeros_like(hist_ref)

    x = x_ref[...]
    h_ref[...] = _rms_modulate(x, g_ref[...], sh_ref[0], sc_ref[0]).astype(BF16)
    t_idx = (i * tm + lax.broadcasted_iota(jnp.int32, (tm, 1), 0)) % l
    after1 = t_idx >= 1
    after2 = t_idx >= 2

    def conv_half(c, half):
        col0 = half * D_FF + c * FF_CHUNK
        cols = slice(col0, col0 + FF_CHUNK)
        up = _dot(h_ref[...], wup_ref[:, cols])
        buf_ref[half, 0:SUBLANES, :] = hist_ref[c, half]
        buf_ref[half, SUBLANES:SUBLANES + tm, :] = up
        hist_ref[c, half] = up[tm - SUBLANES:tm, :]
        prev1 = jnp.where(after1, buf_ref[half, SUBLANES - 1:SUBLANES - 1 + tm, :], 0.0)
        prev2 = jnp.where(after2, buf_ref[half, SUBLANES - 2:SUBLANES - 2 + tm, :], 0.0)
        if has_state:
            up_ref[:, cols] = up
            prev1 = prev1 + h1_ref[:, cols]
            prev2 = prev2 + h0_ref[:, cols]
        else:
            tail_ref[0, :, cols] = buf_ref[half, SUBLANES + tm - (CONV_W - 1):SUBLANES + tm, :]
        return (cb_ref[:, cols] + cw_ref[0:1, cols] * prev2 + cw_ref[1:2, cols] * prev1
                + cw_ref[2:3, cols] * up)

    acc = jnp.zeros((tm, D_MODEL), F32)
    for c in range(n_chunks):
        act = jax.nn.gelu(conv_half(c, 0)) * conv_half(c, 1)
        acc = acc + _dot(act.astype(BF16), wdn_ref[c * FF_CHUNK:(c + 1) * FF_CHUNK, :])
    out = x + gt_ref[0] * acc
    if final_norm:
        ms = jnp.mean(out * out, axis=-1, keepdims=True)
        out = out * lax.rsqrt(ms + RMS_EPS) * fg_ref[...]
    o_ref[...] = out


def _ffn_call(x2, n, l, g, shift, scale, gate2, prm, final_g, conv_state, tm, final_norm):
    rows, d = x2.shape
    has_state = conv_state is not None
    sh_arr, sh_spec = _mod_operand(shift, l, tm)
    sc_arr, sc_spec = _mod_operand(scale, l, tm)
    gt_arr, gt_spec = _mod_operand(gate2, l, tm)
    row_spec = lambda w: pl.BlockSpec((tm, w), lambda i: (i, 0))
    in_specs = [row_spec(d), _const_spec((1, d)), sh_spec, sc_spec, gt_spec,
                _const_spec((d, 2 * D_FF)), _const_spec((CONV_W, 2 * D_FF)), _const_spec((1, 2 * D_FF)),
                _const_spec((D_FF, d)), _const_spec((1, d))]
    operands = [x2, g.reshape(1, d), sh_arr, sc_arr, gt_arr, prm["w_up"], prm["conv_w"], prm["conv_b"],
                prm["w_down"], final_g.reshape(1, d)]
    if has_state:
        assert tm % l == 0 and l > CONV_W - 1
        zeros = jnp.zeros((n, l - 1, 2 * D_FF), F32)
        hist1 = jnp.concatenate([conv_state[:, 1:2], zeros], axis=1).reshape(rows, 2 * D_FF)
        hist0 = jnp.concatenate([conv_state, zeros[:, 1:]], axis=1).reshape(rows, 2 * D_FF)
        in_specs += [row_spec(2 * D_FF), row_spec(2 * D_FF)]
        operands += [hist0, hist1]
        out_shape = [jax.ShapeDtypeStruct((rows, d), F32), jax.ShapeDtypeStruct((rows, 2 * D_FF), F32)]
        out_specs = [row_spec(d), row_spec(2 * D_FF)]
    else:
        assert l % tm == 0
        per = l // tm
        out_shape = [jax.ShapeDtypeStruct((rows, d), F32), jax.ShapeDtypeStruct((n, CONV_W - 1, 2 * D_FF), F32)]
        out_specs = [row_spec(d), pl.BlockSpec((1, CONV_W - 1, 2 * D_FF), lambda i: (i // per, 0, 0))]
    kern = functools.partial(_ffn_kernel, l=l, tm=tm, has_state=has_state, final_norm=final_norm)
    out, aux = pl.pallas_call(
        kern,
        out_shape=out_shape,
        grid=(rows // tm,),
        in_specs=in_specs,
        out_specs=out_specs,
        scratch_shapes=[pltpu.VMEM((tm, d), BF16),
                        pltpu.VMEM((2, tm + SUBLANES, FF_CHUNK), F32),
                        pltpu.VMEM((D_FF // FF_CHUNK, 2, SUBLANES, FF_CHUNK), F32)],
        compiler_params=_cparams("arbitrary"),
        name="conv_ffn",
    )(*operands)
    if has_state:
        aux = aux.reshape(n, l, 2 * D_FF)[:, l - (CONV_W - 1):]
    return out, aux


def _prepare_layer(p):
    out = {}
    w_in = p["w_in"]
    c1 = S5_WIDTH
    c2 = c1 + RW_COLS
    c3 = c2 + 4 * HG_WIDTH
    out["w_in_a"] = w_in[:, :c1].astype(BF16)
    out["w_in_b"] = w_in[:, c1:c2].astype(BF16)
    out["w_in_c"] = w_in[:, c2:c3].astype(BF16)
    out["w_in_g"] = w_in[:, c3:].astype(BF16)
    out["w_ada"] = p["w_ada"].astype(BF16)
    for name in ("w_lift_a", "w_lift_b", "w_lift_c", "w_out", "w_up", "w_down"):
        out[name] = p[name].astype(BF16)
    out["conv_w"] = p["conv_w"]
    out["conv_b"] = p["conv_b"].reshape(1, 2 * D_FF)

    lr = p["s5_lambda_re"]
    li = p["s5_lambda_im"]
    dt = jnp.exp(p["s5_log_dt"])[:, None]
    mag = jnp.exp(lr * dt)
    ar = mag * jnp.cos(li * dt)
    ai = mag * jnp.sin(li * dt)
    den = lr * lr + li * li
    zr = ((ar - 1.0) * lr + ai * li) / den
    zi = (ai * lr - (ar - 1.0) * li) / den
    bbr = zr[..., None] * p["s5_b_re"] - zi[..., None] * p["s5_b_im"]
    bbi = zr[..., None] * p["s5_b_im"] + zi[..., None] * p["s5_b_re"]
    eye = jnp.eye(S5_GROUPS, dtype=F32)
    bmat = jnp.einsum("gh,rgpc->gcrhp", eye, jnp.stack([bbr, bbi])).reshape(S5_WIDTH, 2 * S5_FLAT)
    cmat = jnp.einsum("hg,rgcp->rhpgc", eye, jnp.stack([p["s5_c_re"], -p["s5_c_im"]])).reshape(
        2 * S5_FLAT, S5_WIDTH)
    out["s5_b_hi"], out["s5_b_lo"] = _split_bf16(bmat)
    out["s5_c_hi"], out["s5_c_lo"] = _split_bf16(cmat)
    out["s5_a"] = jnp.stack([ar.reshape(S5_FLAT), ai.reshape(S5_FLAT)])
    out["s5_d"] = p["s5_d"].reshape(1, S5_WIDTH)
    out["s5_w_glu"] = p["s5_w_glu"].astype(BF16)
    out["s5_b_glu"] = p["s5_b_glu"].reshape(1, S5_WIDTH)

    out["rw_mu"] = p["rw_mu"].reshape(1, RW_COLS)
    out["rw_vec"] = jnp.stack([p["rw_w0"], p["rw_a0"], p["rw_k_k"], p["rw_k_a"],
                               p["rw_r_k"].reshape(RW_WIDTH), p["rw_ln_w"], p["rw_ln_b"],
                               jnp.zeros((RW_WIDTH,), F32)])
    zw = jnp.zeros((RW_LORA, RW_WIDTH), F32)
    out["rw_w2"] = zw.at[0:RW_DECAY_LORA].set(p["rw_w2"]).astype(BF16)
    out["rw_a2"] = zw.at[RW_DECAY_LORA:RW_DECAY_LORA + RW_AAA_LORA].set(p["rw_a2"]).astype(BF16)
    out["rw_g2"] = zw.at[RW_DECAY_LORA + RW_AAA_LORA:].set(p["rw_g2"]).astype(BF16)
    out["hg_norm"] = p["hg_norm"]
    out["g_mix"] = p["g_mix"]
    out["g_ffn"] = p["g_ffn"]
    out["b_ada"] = p["b_ada"]
    return out


def _rw_state_in(s):
    n = s.shape[0]
    s = s.reshape(n, RW_HEADS // HEAD_PAIR, HEAD_PAIR, RW_HEAD, RW_HEAD)
    return s.transpose(1, 0, 3, 2, 4).reshape(RW_HEADS // HEAD_PAIR, n, RW_HEAD, LANES)


def _rw_state_out(t):
    n = t.shape[1]
    t = t.reshape(RW_HEADS // HEAD_PAIR, n, RW_HEAD, HEAD_PAIR, RW_HEAD)
    return t.transpose(1, 0, 3, 2, 4).reshape(n, RW_HEADS, RW_HEAD, RW_HEAD)


def _hg_state_in(s):
    n = s.shape[0]
    s = s.reshape(n, HG_HEADS // HEAD_PAIR, HEAD_PAIR, HG_HEAD, HG_HEAD)
    return s.transpose(1, 0, 4, 2, 3).reshape(HG_HEADS // HEAD_PAIR, n, HG_HEAD, LANES)


def _hg_state_out(t):
    n = t.shape[1]
    t = t.reshape(HG_HEADS // HEAD_PAIR, n, HG_HEAD, HEAD_PAIR, HG_HEAD)
    return t.transpose(1, 0, 3, 4, 2).reshape(n, HG_HEADS, HG_HEAD, HG_HEAD)


def _pad_time(a, lp):
    n, l, w = a.shape
    if l == lp:
        return a
    return jnp.concatenate([a, jnp.zeros((n, lp - l, w), a.dtype)], axis=1)


def _run_trunk(x, mods, st_s5, st_shift, st_rw, st_hg, st_conv, lower, final_g, prms, t_chunk):
    n, l, d = x.shape
    rows = n * l
    tm = min(ROW_TILE, rows)
    lp = -(-l // SUBLANES) * SUBLANES
    tc = min(t_chunk, lp)
    depth = len(prms)
    h = x.reshape(rows, d)
    out_s5, out_shift, out_rw, out_hg, out_conv = [], [], [], [], []
    for layer in range(depth):
        prm = prms[layer]
        sh1, sc1, gt1, sh2, sc2, gt2 = jnp.split(mods[layer], 6, axis=-1)
        za, zb, zc, gates = _in_proj_call(h, l, prm["g_mix"], sh1, sc1, prm["w_in_a"], prm["w_in_b"],
                                          prm["w_in_c"], prm["w_in_g"], tm)

        s5_in = jnp.concatenate([st_s5[layer][..., 0].reshape(n, S5_FLAT),
                                 st_s5[layer][..., 1].reshape(n, S5_FLAT)], axis=1)
        ya, s5_new = _s5_call(_pad_time(za.reshape(n, l, S5_WIDTH), lp), s5_in, prm, tc, l)
        yb, shift_new, rw_new = _rw_call(_pad_time(zb.reshape(n, l, RW_COLS), lp),
                                         st_shift[layer].reshape(n, 1, RW_COLS),
                                         _rw_state_in(st_rw[layer]), prm, tc, l)
        yc, hg_new = _hg_call(_pad_time(zc.reshape(n, l, 4 * HG_WIDTH), lp), _hg_state_in(st_hg[layer]),
                              lower[layer], prm["hg_norm"], tc, l)
        ya = ya[:, :l].reshape(rows, S5_WIDTH)
        yb = yb[:, :l].reshape(rows, RW_WIDTH)
        yc = yc[:, :l].reshape(rows, HG_WIDTH)

        h = _merge_call(h, l, ya, yb, yc, gates, gt1, prm, tm)
        ffn_tm = tm if st_conv is None else min(tm, 128)
        h, conv_new = _ffn_call(h, n, l, prm["g_ffn"], sh2, sc2, gt2, prm, final_g,
                                None if st_conv is None else st_conv[layer], ffn_tm,
                                final_norm=(layer == depth - 1))
        out_s5.append(jnp.stack([s5_new[:, :S5_FLAT].reshape(n, S5_GROUPS, S5_STATE),
                                 s5_new[:, S5_FLAT:].reshape(n, S5_GROUPS, S5_STATE)], axis=-1))
        out_shift.append(shift_new.reshape(n, RW_COLS))
        out_rw.append(_rw_state_out(rw_new))
        out_hg.append(_hg_state_out(hg_new))
        out_conv.append(conv_new)
    y = h.reshape(n, l, d).astype(x.dtype)
    return y, (jnp.stack(out_s5), jnp.stack(out_shift), jnp.stack(out_rw), jnp.stack(out_hg),
               jnp.stack(out_conv))


def kernel(x_prompt, x_sample, c_prompt, c_sample, state_s5, state_rwkv_shift, state_rwkv, state_hgrn, state_ffn_conv, w_ada, b_ada, g_mix, g_ffn, w_in, s5_lambda_re, s5_lambda_im, s5_log_dt, s5_b_re, s5_b_im, s5_c_re, s5_c_im, s5_d, s5_w_glu, s5_b_glu, rw_mu, rw_w0, rw_w2, rw_a0, rw_a2, rw_g2, rw_k_k, rw_k_a, rw_r_k, rw_ln_w, rw_ln_b, hg_lb, hg_norm, w_lift_a, w_lift_b, w_lift_c, w_out, w_up, conv_w, conv_b, w_down, final_g):
    per_layer = {
        "w_ada": w_ada, "b_ada": b_ada, "g_mix": g_mix, "g_ffn": g_ffn, "w_in": w_in,
        "s5_lambda_re": s5_lambda_re, "s5_lambda_im": s5_lambda_im, "s5_log_dt": s5_log_dt,
        "s5_b_re": s5_b_re, "s5_b_im": s5_b_im, "s5_c_re": s5_c_re, "s5_c_im": s5_c_im,
        "s5_d": s5_d, "s5_w_glu": s5_w_glu, "s5_b_glu": s5_b_glu,
        "rw_mu": rw_mu, "rw_w0": rw_w0, "rw_w2": rw_w2, "rw_a0": rw_a0, "rw_a2": rw_a2,
        "rw_g2": rw_g2, "rw_k_k": rw_k_k, "rw_k_a": rw_k_a, "rw_r_k": rw_r_k,
        "rw_ln_w": rw_ln_w, "rw_ln_b": rw_ln_b, "hg_norm": hg_norm,
        "w_lift_a": w_lift_a, "w_lift_b": w_lift_b, "w_lift_c": w_lift_c, "w_out": w_out,
        "w_up": w_up, "conv_w": conv_w, "conv_b": conv_b, "w_down": w_down,
    }
    depth = w_ada.shape[0]
    prms = [_prepare_layer({k: v[layer] for k, v in per_layer.items()}) for layer in range(depth)]

    lbp = jax.nn.softmax(hg_lb.astype(F32), axis=0)
    lower = jnp.cumsum(lbp, axis=0) - lbp[0]

    nb = x_prompt.shape[0]
    ns = x_sample.shape[0]
    c_all = jnp.concatenate([c_prompt, c_sample], axis=0).astype(F32)
    mods = [_ada_call(c_all, prms[layer]["w_ada"], prms[layer]["b_ada"]) for layer in range(depth)]
    mods_p = [m[:nb] for m in mods]
    mods_s = [m[nb:] for m in mods]

    z_s5 = jnp.zeros((depth, nb) + state_s5.shape[2:], F32)
    z_shift = jnp.zeros((depth, nb) + state_rwkv_shift.shape[2:], F32)
    z_rw = jnp.zeros((depth, nb) + state_rwkv.shape[2:], F32)
    z_hg = jnp.zeros((depth, nb) + state_hgrn.shape[2:], F32)

    y_prompt, (s5_p, shift_p, rw_p, hg_p, conv_p) = _run_trunk(
        x_prompt, mods_p, z_s5, z_shift, z_rw, z_hg, None, lower, final_g, prms, t_chunk=64)
    y_sample, (s5_s, shift_s, rw_s, hg_s, conv_s) = _run_trunk(
        x_sample, mods_s, state_s5, state_rwkv_shift, state_rwkv, state_hgrn, state_ffn_conv,
        lower, final_g, prms, t_chunk=64)
    return (y_prompt, y_sample, s5_p, shift_p, rw_p, hg_p, conv_p, s5_s, shift_s, rw_s, hg_s, conv_s)
```

```python
import functools
import math

import jax
import jax.numpy as jnp
from jax import lax
from jax.experimental import pallas as pl
from jax.experimental.pallas import tpu as pltpu

F32 = jnp.float32
BF16 = jnp.bfloat16

D_MODEL = 1024
S5_WIDTH = D_MODEL // 4
S5_GROUP = 16
S5_GROUPS = S5_WIDTH // S5_GROUP
S5_STATE = 64
S5_FLAT = S5_GROUPS * S5_STATE
RW_WIDTH = D_MODEL // 2
RW_HEAD = 64
RW_HEADS = RW_WIDTH // RW_HEAD
RW_DECAY_LORA = 32
RW_AAA_LORA = 32
RW_GATE_LORA = 64
RW_LORA = RW_DECAY_LORA + RW_AAA_LORA + RW_GATE_LORA
RW_COLS = 3 * RW_WIDTH + RW_LORA
RW_GN_EPS = 1e-5 * RW_HEAD
HG_WIDTH = D_MODEL // 4
HG_HEAD = 64
HG_HEADS = HG_WIDTH // HG_HEAD
HG_GATE_FLOOR = 1e-30
N_BRANCH = 3
D_FF = 256 * ((8 * D_MODEL // 3 + 255) // 256)
CONV_W = 3
RMS_EPS = 1e-6

LANES = 128
SUBLANES = 8
VMEM_LIMIT_BYTES = 56 * 1024 * 1024

SEQ_BLOCK = SUBLANES
HEAD_PAIR = LANES // RW_HEAD
ROW_TILE = 512
FF_CHUNK = 256
IN_PROJ_CHUNK = 512
RW_SUB = 16
RW_PAIRS = RW_HEADS // HEAD_PAIR
RW_SOLVE_PASSES = 1
RW_APPLY_PASSES = 1
RW_STATE_PASSES = 1

_NN = (((1,), (0,)), ((), ()))
_NT = (((1,), (1,)), ((), ()))


def _cparams(*sem):
    return pltpu.CompilerParams(dimension_semantics=sem, vmem_limit_bytes=VMEM_LIMIT_BYTES)


def _const_spec(shape):
    nd = len(shape)
    return pl.BlockSpec(shape, lambda *_: (0,) * nd, pipeline_mode=pl.Buffered(1))


def _dot(a, b):
    return jnp.dot(a, b, preferred_element_type=F32)


def _split_bf16(x):
    hi = x.astype(BF16)
    lo = (x - hi.astype(F32)).astype(BF16)
    return hi, lo


def _dot3(x, w_hi, w_lo):
    x_hi, x_lo = _split_bf16(x)
    return _dot(x_hi, w_hi) + _dot(x_lo, w_hi) + _dot(x_hi, w_lo)


def _head_ones(width):
    r = lax.broadcasted_iota(jnp.int32, (width, width), 0) // RW_HEAD
    c = lax.broadcasted_iota(jnp.int32, (width, width), 1) // RW_HEAD
    return (r == c).astype(BF16)


def _head_sum(x, ones):
    hi, lo = _split_bf16(x)
    return _dot(hi, ones) + _dot(lo, ones)


def _softplus(x):
    return jnp.maximum(x, 0.0) + jnp.log1p(jnp.exp(-jnp.abs(x)))


def _rms_modulate(x, g, shift, scale):
    ms = jnp.mean(x * x, axis=-1, keepdims=True)
    return x * lax.rsqrt(ms + RMS_EPS) * g * (1.0 + scale) + shift


def _ada_kernel(c_ref, w_ref, b_ref, o_ref):
    c = c_ref[...]
    o_ref[...] = _dot(jax.nn.silu(c).astype(BF16), w_ref[...]) + b_ref[...]


def _ada_call(c, w_bf16, b):
    n, d = c.shape
    cols = w_bf16.shape[1]
    tn = cols // 4
    return pl.pallas_call(
        _ada_kernel,
        out_shape=jax.ShapeDtypeStruct((n, cols), F32),
        grid=(cols // tn,),
        in_specs=[_const_spec((n, d)),
                  pl.BlockSpec((d, tn), lambda j: (0, j)),
                  pl.BlockSpec((1, tn), lambda j: (0, j))],
        out_specs=pl.BlockSpec((n, tn), lambda j: (0, j)),
        compiler_params=_cparams("arbitrary"),
        name="ada_mod",
    )(c, w_bf16, b.reshape(1, cols))


def _mod_operand(m, l, tm):
    n, d = m.shape
    if l % tm == 0:
        per = l // tm
        return m.reshape(n, 1, d), pl.BlockSpec((1, 1, d), lambda i: (i // per, 0, 0))
    assert tm % l == 0
    return jnp.repeat(m, l, axis=0).reshape(1, n * l, d), pl.BlockSpec((1, tm, d), lambda i: (0, i, 0))


def _in_proj_kernel(x_ref, g_ref, sh_ref, sc_ref, wa_ref, wb_ref, wc_ref, wg_ref,
                    oa_ref, ob_ref, oc_ref, og_ref):
    h = _rms_modulate(x_ref[...], g_ref[...], sh_ref[0], sc_ref[0]).astype(BF16)

    def project(w_ref, o_ref, fn=None):
        width = w_ref.shape[1]
        for c0 in range(0, width, IN_PROJ_CHUNK):
            cols = slice(c0, min(c0 + IN_PROJ_CHUNK, width))
            z = _dot(h, w_ref[:, cols])
            o_ref[:, cols] = z if fn is None else fn(z)

    project(wa_ref, oa_ref)
    project(wb_ref, ob_ref)
    project(wc_ref, oc_ref)
    project(wg_ref, og_ref, jax.nn.sigmoid)


def _in_proj_call(x2, l, g, shift, scale, w_a, w_b, w_c, w_g, tm):
    rows, d = x2.shape
    sh_arr, sh_spec = _mod_operand(shift, l, tm)
    sc_arr, sc_spec = _mod_operand(scale, l, tm)
    widths = [w.shape[1] for w in (w_a, w_b, w_c, w_g)]
    return pl.pallas_call(
        _in_proj_kernel,
        out_shape=[jax.ShapeDtypeStruct((rows, w), F32) for w in widths],
        grid=(rows // tm,),
        in_specs=[pl.BlockSpec((tm, d), lambda i: (i, 0)), _const_spec((1, d)), sh_spec, sc_spec]
        + [_const_spec(w.shape) for w in (w_a, w_b, w_c, w_g)],
        out_specs=[pl.BlockSpec((tm, w), lambda i: (i, 0)) for w in widths],
        compiler_params=_cparams("arbitrary"),
        name="in_proj",
    )(x2, g.reshape(1, d), sh_arr, sc_arr, w_a, w_b, w_c, w_g)


def _s5_kernel(u_ref, s0_ref, bh_ref, bl_ref, ch_ref, cl_ref, a_ref, d_ref, wg_ref, bg_ref,
               y_ref, sl_ref, e_ref, st_ref, *, t_chunk, n_steps):
    @pl.when(pl.program_id(1) == 0)
    def _():
        st_ref[...] = s0_ref[...]

    rows = SEQ_BLOCK * t_chunk
    ut = jnp.swapaxes(u_ref[...], 0, 1).reshape(rows, S5_WIDTH)
    e_ref[...] = _dot3(ut, bh_ref[...], bl_ref[...])

    ar = jnp.broadcast_to(a_ref[0:1, :], (SEQ_BLOCK, S5_FLAT))
    ai = jnp.broadcast_to(a_ref[1:2, :], (SEQ_BLOCK, S5_FLAT))

    def step(t, carry):
        sr, si = carry
        r = pl.ds(pl.multiple_of(t * SEQ_BLOCK, SEQ_BLOCK), SEQ_BLOCK)
        nr = ar * sr - ai * si + e_ref[r, 0:S5_FLAT]
        ni = ar * si + ai * sr + e_ref[r, S5_FLAT:2 * S5_FLAT]
        e_ref[r, 0:S5_FLAT] = nr
        e_ref[r, S5_FLAT:2 * S5_FLAT] = ni
        return nr, ni

    sr, si = lax.fori_loop(0, n_steps, step, (st_ref[:, 0:S5_FLAT], st_ref[:, S5_FLAT:2 * S5_FLAT]))
    st_ref[:, 0:S5_FLAT] = sr
    st_ref[:, S5_FLAT:2 * S5_FLAT] = si
    sl_ref[...] = st_ref[...]

    y = _dot3(e_ref[...], ch_ref[...], cl_ref[...]) + d_ref[...] * ut
    y = jax.nn.gelu(y)
    y = y * jax.nn.sigmoid(_dot(y.astype(BF16), wg_ref[...]) + bg_ref[...])
    y_ref[...] = jnp.swapaxes(y.reshape(t_chunk, SEQ_BLOCK, S5_WIDTH), 0, 1)


def _s5_call(u3, s0, prm, t_chunk, l_valid):
    n, lp, _ = u3.shape
    nb, nt = n // SEQ_BLOCK, lp // t_chunk
    assert nt == 1 or l_valid == lp
    kern = functools.partial(_s5_kernel, t_chunk=t_chunk, n_steps=min(t_chunk, l_valid))
    return pl.pallas_call(
        kern,
        out_shape=[jax.ShapeDtypeStruct((n, lp, S5_WIDTH), F32),
                   jax.ShapeDtypeStruct((n, 2 * S5_FLAT), F32)],
        grid=(nb, nt),
        in_specs=[pl.BlockSpec((SEQ_BLOCK, t_chunk, S5_WIDTH), lambda b, c: (b, c, 0)),
                  pl.BlockSpec((SEQ_BLOCK, 2 * S5_FLAT), lambda b, c: (b, 0)),
                  _const_spec((S5_WIDTH, 2 * S5_FLAT)), _const_spec((S5_WIDTH, 2 * S5_FLAT)),
                  _const_spec((2 * S5_FLAT, S5_WIDTH)), _const_spec((2 * S5_FLAT, S5_WIDTH)),
                  _const_spec((2, S5_FLAT)), _const_spec((1, S5_WIDTH)),
                  _const_spec((S5_WIDTH, S5_WIDTH)), _const_spec((1, S5_WIDTH))],
        out_specs=[pl.BlockSpec((SEQ_BLOCK, t_chunk, S5_WIDTH), lambda b, c: (b, c, 0)),
                   pl.BlockSpec((SEQ_BLOCK, 2 * S5_FLAT), lambda b, c: (b, 0))],
        scratch_shapes=[pltpu.VMEM((SEQ_BLOCK * t_chunk, 2 * S5_FLAT), F32),
                        pltpu.VMEM((SEQ_BLOCK, 2 * S5_FLAT), F32)],
        compiler_params=_cparams("arbitrary", "arbitrary"),
        name="s5_mixer",
    )(u3, s0, prm["s5_b_hi"], prm["s5_b_lo"], prm["s5_c_hi"], prm["s5_c_lo"], prm["s5_a"],
      prm["s5_d"], prm["s5_w_glu"], prm["s5_b_glu"])


def _recurrent_steps(n_steps, n_pairs, s_ref, w_ref, k_ref, v_ref, r_ref, y_ref):
    tile = (RW_HEAD, LANES)
    ones = _head_ones(LANES)
    ri = lax.broadcasted_iota(jnp.int32, tile, 0)
    ci = lax.broadcasted_iota(jnp.int32, tile, 1)
    diag = (ci % RW_HEAD == ri).astype(F32)
    row8 = lax.broadcasted_iota(jnp.int32, (SUBLANES, LANES), 0)

    def block(base, steps):
        for hp in range(n_pairs):
            cols = pl.ds(hp * LANES, LANES)

            def rows8(ref):
                return [ref[n, pl.ds(base, SUBLANES), cols] for n in range(SEQ_BLOCK)]

            w8, k8, v8, r8 = rows8(w_ref), rows8(k_ref), rows8(v_ref), rows8(r_ref)
            y8 = [jnp.zeros((SUBLANES, LANES), F32) for _ in range(SEQ_BLOCK)]
            for j in range(steps):
                def bc(x8, n):
                    return jnp.broadcast_to(x8[n][j:j + 1, :], tile)

                s = [s_ref[hp, n] for n in range(SEQ_BLOCK)]
                vm = jnp.concatenate([diag * bc(v8, n) for n in range(SEQ_BLOCK)], axis=0)
                vcol = _dot(vm.astype(BF16), ones)
                new = []
                for n in range(SEQ_BLOCK):
                    rs = slice(n * RW_HEAD, (n + 1) * RW_HEAD)
                    sn = s[n] * bc(w8, n) + vcol[rs] * bc(k8, n)
                    s_ref[hp, n] = sn
                    new.append(sn * bc(r8, n))
                ycol = _dot(jnp.concatenate(new, axis=0).astype(BF16), ones)
                for n in range(SEQ_BLOCK):
                    yrow = jnp.sum(ycol[n * RW_HEAD:(n + 1) * RW_HEAD] * diag, axis=0, keepdims=True)
                    y8[n] = jnp.where(row8 == j, jnp.broadcast_to(yrow, (SUBLANES, LANES)), y8[n])
            for n in range(SEQ_BLOCK):
                y_ref[n, pl.ds(base, SUBLANES), cols] = y8[n]

    full, rem = divmod(n_steps, SUBLANES)
    if full:
        def body(tb, carry):
            block(pl.multiple_of(tb * SUBLANES, SUBLANES), SUBLANES)
            return carry
        lax.fori_loop(0, full, body, 0)
    if rem:
        block(full * SUBLANES, rem)


def _mm(a, b, dims, passes):
    dg = lambda x, y: lax.dot_general(x, y, dims, preferred_element_type=F32)
    if passes == 1:
        return dg(a.astype(BF16), b.astype(BF16))
    a_hi, a_lo = _split_bf16(a)
    if passes == 2:
        b_hi = b.astype(BF16)
        return dg(a_hi, b_hi) + dg(a_lo, b_hi)
    b_hi, b_lo = _split_bf16(b)
    return dg(a_hi, b_hi) + dg(a_lo, b_hi) + dg(a_hi, b_lo)


def _cumsum_groups(x, group):
    pos = lax.broadcasted_iota(jnp.int32, (x.shape[0], 1), 0) % group
    s = 1
    while s < group:
        x = x + jnp.where(pos >= s, pltpu.roll(x, s, 0), 0.0)
        s *= 2
    return x


def _rw_kernel(p_ref, sh0_ref, s0_ref, mu_ref, vec_ref, w2_ref, a2_ref, g2_ref,
               y_ref, shl_ref, sl_ref,
               pbuf_ref, s_ref, at_ref, bt_ref, kt_ref, rt_ref, bh_ref, kh_ref, v_ref, gc_ref, ys_ref,
               *, t_chunk, l_valid, last_row):
    rows = SEQ_BLOCK * t_chunk
    n_sub = t_chunk // RW_SUB

    @pl.when(pl.program_id(1) == 0)
    def _():
        pbuf_ref[:, SUBLANES - 1:SUBLANES, :] = sh0_ref[...]
        s_ref[...] = s0_ref[...]

    p3 = p_ref[...]
    pbuf_ref[:, SUBLANES:SUBLANES + t_chunk, :] = p3
    prev3 = pbuf_ref[:, SUBLANES - 1:SUBLANES - 1 + t_chunk, :]
    last = p_ref[:, last_row:last_row + 1, :]
    pbuf_ref[:, SUBLANES - 1:SUBLANES, :] = last
    shl_ref[...] = last

    xm = (p3 + (prev3 - p3) * mu_ref[...]).reshape(rows, RW_COLS)
    r = xm[:, 0:RW_WIDTH]
    k = xm[:, RW_WIDTH:2 * RW_WIDTH]
    v = xm[:, 2 * RW_WIDTH:3 * RW_WIDTH]
    lora = xm[:, 3 * RW_WIDTH:RW_COLS]
    w0, a0, k_k, k_a = vec_ref[0:1, :], vec_ref[1:2, :], vec_ref[2:3, :], vec_ref[3:4, :]
    r_k, ln_w, ln_b = vec_ref[4:5, :], vec_ref[5:6, :], vec_ref[6:7, :]

    w = -_softplus(-(w0 + _dot(jnp.tanh(lora).astype(BF16), w2_ref[...]))) - 0.5
    log_decay = -jnp.exp(w)
    a = jax.nn.sigmoid(a0 + _dot(lora.astype(BF16), a2_ref[...]))
    g = _dot(jax.nn.sigmoid(lora).astype(BF16), g2_ref[...])

    ones = _head_ones(RW_WIDTH)
    kk = k * k_k
    kk = kk / jnp.maximum(jnp.sqrt(_head_sum(kk * kk, ones)), 1e-12)
    kt = k * (1.0 + (a - 1.0) * k_a)
    kka = kk * a
    if l_valid < t_chunk:
        live = lax.broadcasted_iota(jnp.int32, (rows, 1), 0) % t_chunk < l_valid
        log_decay = jnp.where(live, log_decay, 0.0)
        kk = jnp.where(live, kk, 0.0)
        kka = jnp.where(live, kka, 0.0)
        kt_live = jnp.where(live, kt, 0.0)
    else:
        kt_live = kt

    cum = _cumsum_groups(log_decay, RW_SUB)
    shape_g = (rows // RW_SUB, RW_SUB, RW_WIDTH)
    cum_end = jnp.broadcast_to(cum.reshape(shape_g)[:, RW_SUB - 1:RW_SUB, :], shape_g).reshape(rows, RW_WIDTH)
    g_in = jnp.exp(cum)
    g_inv = jnp.exp(-cum)
    g_ex = jnp.exp(cum - log_decay)
    g_out = jnp.exp(cum_end - cum)
    shape3 = (SEQ_BLOCK, t_chunk, RW_WIDTH)
    at_ref[...] = (-kk * g_ex).reshape(shape3)
    bt_ref[...] = (kka * g_inv).reshape(shape3)
    kt_ref[...] = (kt_live * g_inv).reshape(shape3)
    rt_ref[...] = (r * g_in).reshape(shape3)
    bh_ref[...] = (kka * g_out).reshape(shape3)
    kh_ref[...] = (kt_live * g_out).reshape(shape3)
    v_ref[...] = v.reshape(shape3)
    gc_ref[...] = jnp.exp(cum_end).reshape(shape3)

    m_rows = SEQ_BLOCK * RW_SUB
    ri = lax.broadcasted_iota(jnp.int32, (m_rows, m_rows), 0)
    ci = lax.broadcasted_iota(jnp.int32, (m_rows, m_rows), 1)
    same_seq = (ri // RW_SUB) == (ci // RW_SUB)
    before = same_seq & ((ci % RW_SUB) < (ri % RW_SUB))
    upto = same_seq & ((ci % RW_SUB) <= (ri % RW_SUB))
    eye = (ri == ci).astype(F32)
    pair_mask = (ri // 2) == (ci // 2)
    level_masks = []
    s = 2
    while s < RW_SUB:
        level_masks.append(((ri // (2 * s)) == (ci // (2 * s))) & ((ri // s) != (ci // s)))
        s *= 2
    lane = lax.broadcasted_iota(jnp.int32, (m_rows, LANES), 1)
    head_mask = [lane < RW_HEAD, lane >= RW_HEAD]
    wide = (m_rows, SEQ_BLOCK * LANES)
    own_rows = (lax.broadcasted_iota(jnp.int32, wide, 1) // LANES
                == lax.broadcasted_iota(jnp.int32, wide, 0) // RW_SUB)
    same_head = (lax.broadcasted_iota(jnp.int32, wide, 0) // RW_HEAD
                 == (lax.broadcasted_iota(jnp.int32, wide, 1) % LANES) // RW_HEAD)
    heads = [(hp, h) for hp in range(RW_PAIRS) for h in range(HEAD_PAIR)]

    def sub_chunk(c, carry):
        rs = pl.ds(pl.multiple_of(c * RW_SUB, RW_SUB), RW_SUB)

        def ld(ref, hp):
            return ref[:, rs, pl.ds(hp * LANES, LANES)].reshape(m_rows, LANES)

        at, bt, ktl, rt, bh, kh, vv, gcv = ([ld(ref, hp) for hp in range(RW_PAIRS)]
                                            for ref in (at_ref, bt_ref, kt_ref, rt_ref, bh_ref, kh_ref,
                                                        v_ref, gc_ref))
        state = [[s_ref[hp, n] for n in range(SEQ_BLOCK)] for hp in range(RW_PAIRS)]
        zero = jnp.zeros((m_rows, LANES), F32)
        msk = lambda x, h: jnp.where(head_mask[h], x, zero)

        gram = [_mm(jnp.concatenate([msk(at[hp], 0), msk(rt[hp], 0), msk(at[hp], 1), msk(rt[hp], 1)], axis=0),
                    jnp.concatenate([bt[hp], ktl[hp]], axis=0), _NT, RW_SOLVE_PASSES)
                for hp in range(RW_PAIRS)]

        def quad(hp, h, row, col, keep):
            blk = gram[hp][(2 * h + row) * m_rows:(2 * h + row + 1) * m_rows, col * m_rows:(col + 1) * m_rows]
            return jnp.where(keep, blk, 0.0)

        m_ab = [quad(hp, h, 0, 0, before) for hp, h in heads]
        m_ak = [quad(hp, h, 0, 1, before) for hp, h in heads]
        n_rb = [quad(hp, h, 1, 0, upto) for hp, h in heads]
        n_rk = [quad(hp, h, 1, 1, upto) for hp, h in heads]
        tinv = [eye + jnp.where(pair_mask, m, 0.0) for m in m_ab]
        for lm in level_masks:
            prod = [_mm(jnp.where(lm, m, 0.0), t, _NN, RW_SOLVE_PASSES) for m, t in zip(m_ab, tinv)]
            tinv = [t + _mm(t, q, _NN, RW_SOLVE_PASSES) for t, q in zip(tinv, prod)]
        vh = [msk(vv[hp], h) for hp, h in heads]
        w1 = [_mm(m, x, _NN, RW_APPLY_PASSES) for m, x in zip(m_ak, vh)]
        ap = [_mm(t, jnp.concatenate([msk(at[hp], h), w], axis=1), _NN, RW_APPLY_PASSES)
              for t, (hp, h), w in zip(tinv, heads, w1)]
        nb = [_mm(m, x, _NN, RW_APPLY_PASSES) for m, x in zip(n_rb, ap)]
        nv = [_mm(m, x, _NN, RW_APPLY_PASSES) for m, x in zip(n_rk, vh)]

        def per_seq(x):
            return jnp.where(own_rows, jnp.concatenate([x] * SEQ_BLOCK, axis=1), 0.0)

        for hp in range(RW_PAIRS):
            i0, i1 = hp * HEAD_PAIR, hp * HEAD_PAIR + 1
            a_hat = ap[i0][:, 0:LANES] + ap[i1][:, 0:LANES]
            p_zero = ap[i0][:, LANES:2 * LANES] + ap[i1][:, LANES:2 * LANES]
            r_hat = rt[hp] + nb[i0][:, 0:LANES] + nb[i1][:, 0:LANES]
            y_zero = nb[i0][:, LANES:2 * LANES] + nb[i1][:, LANES:2 * LANES] + nv[i0] + nv[i1]
            bh_x = per_seq(bh[hp])
            kh_x = per_seq(kh[hp])
            gam = jnp.where(same_head, _mm(a_hat.T, bh_x, _NN, RW_STATE_PASSES), 0.0)
            u = jnp.where(same_head, _mm(jnp.concatenate([p_zero.T, vv[hp].T], axis=1),
                                         jnp.concatenate([bh_x, kh_x], axis=0), _NN, RW_STATE_PASSES), 0.0)
            cols = pl.ds(hp * LANES, LANES)
            for n in range(SEQ_BLOCK):
                q = slice(n * RW_SUB, (n + 1) * RW_SUB)
                blk = slice(n * LANES, (n + 1) * LANES)
                st = state[hp][n]
                ys_ref[n, rs, cols] = _mm(r_hat[q], st, _NT, RW_STATE_PASSES) + y_zero[q]
                s_ref[hp, n] = (st * gcv[hp][n * RW_SUB:n * RW_SUB + 1, :]
                                + _mm(st, gam[:, blk], _NN, RW_STATE_PASSES) + u[:, blk])
        return carry

    lax.fori_loop(0, n_sub, sub_chunk, 0)
    sl_ref[...] = s_ref[...]

    y = ys_ref[...].reshape(rows, RW_WIDTH)
    inv = 1.0 / RW_HEAD
    mean = _head_sum(y, ones) * inv
    yc = y - mean
    var = _head_sum(yc * yc, ones) * inv
    y = yc * lax.rsqrt(var + RW_GN_EPS) * ln_w + ln_b
    bonus = _head_sum(r * kt * r_k, ones) * v
    y_ref[...] = ((y + bonus) * g).reshape(shape3)


def _rw_call(p3, shift0, s0, prm, t_chunk, l_valid):
    n, lp, _ = p3.shape
    nb, nt = n // SEQ_BLOCK, lp // t_chunk
    assert nt == 1 or l_valid == lp
    assert t_chunk % RW_SUB == 0
    valid_in_chunk = min(t_chunk, l_valid)
    kern = functools.partial(_rw_kernel, t_chunk=t_chunk, l_valid=valid_in_chunk, last_row=valid_in_chunk - 1)
    blk3 = lambda w: pl.BlockSpec((SEQ_BLOCK, t_chunk, w), lambda b, c: (b, c, 0))
    st_spec = pl.BlockSpec((RW_PAIRS, SEQ_BLOCK, LANES, LANES), lambda b, c: (0, b, 0, 0))
    sh_spec = pl.BlockSpec((SEQ_BLOCK, 1, RW_COLS), lambda b, c: (b, 0, 0))
    q_scr = pltpu.VMEM((SEQ_BLOCK, t_chunk, RW_WIDTH), F32)
    return pl.pallas_call(
        kern,
        out_shape=[jax.ShapeDtypeStruct((n, lp, RW_WIDTH), F32),
                   jax.ShapeDtypeStruct((n, 1, RW_COLS), F32),
                   jax.ShapeDtypeStruct((RW_PAIRS, n, LANES, LANES), F32)],
        grid=(nb, nt),
        in_specs=[blk3(RW_COLS), sh_spec, st_spec,
                  _const_spec((1, RW_COLS)), _const_spec((SUBLANES, RW_WIDTH)),
                  _const_spec((RW_LORA, RW_WIDTH)), _const_spec((RW_LORA, RW_WIDTH)),
                  _const_spec((RW_LORA, RW_WIDTH))],
        out_specs=[blk3(RW_WIDTH), sh_spec, st_spec],
        scratch_shapes=[pltpu.VMEM((SEQ_BLOCK, t_chunk + SUBLANES, RW_COLS), F32),
                        pltpu.VMEM((RW_PAIRS, SEQ_BLOCK, LANES, LANES), F32)] + [q_scr] * 9,
        compiler_params=_cparams("arbitrary", "arbitrary"),
        name="rwkv7_mixer",
    )(p3, shift0, s0, prm["rw_mu"], prm["rw_vec"], prm["rw_w2"], prm["rw_a2"], prm["rw_g2"])


def _hg_kernel(z_ref, s0_ref, lower_ref, ng_ref, y_ref, sl_ref,
               s_ref, qw_ref, qk_ref, qv_ref, qr_ref, ys_ref, *, t_chunk, n_steps):
    rows = SEQ_BLOCK * t_chunk

    @pl.when(pl.program_id(1) == 0)
    def _():
        s_ref[...] = s0_ref[...]

    z = z_ref[...].reshape(rows, 4 * HG_WIDTH)
    q = z[:, 0:HG_WIDTH]
    f = z[:, HG_WIDTH:2 * HG_WIDTH]
    i = z[:, 2 * HG_WIDTH:3 * HG_WIDTH]
    og = z[:, 3 * HG_WIDTH:4 * HG_WIDTH]
    lower = lower_ref[...]
    fgate = lower + (1.0 - lower) * jax.nn.sigmoid(f)
    shape3 = (SEQ_BLOCK, t_chunk, HG_WIDTH)
    qw_ref[...] = jnp.maximum(fgate, HG_GATE_FLOOR).reshape(shape3)
    qk_ref[...] = (1.0 - fgate).reshape(shape3)
    qv_ref[...] = i.reshape(shape3)
    qr_ref[...] = jax.nn.silu(q).reshape(shape3)
    if n_steps < t_chunk:
        ys_ref[...] = jnp.zeros(shape3, F32)

    _recurrent_steps(n_steps, HG_HEADS // HEAD_PAIR, s_ref, qw_ref, qk_ref, qv_ref, qr_ref, ys_ref)
    sl_ref[...] = s_ref[...]

    o = ys_ref[...].reshape(rows, HG_WIDTH)
    ms = _head_sum(o * o, _head_ones(HG_WIDTH)) * (1.0 / HG_HEAD)
    o = o * lax.rsqrt(ms + RMS_EPS) * ng_ref[...] * jax.nn.sigmoid(og)
    y_ref[...] = o.reshape(shape3)


def _hg_call(z3, s0, lower, norm_g, t_chunk, l_valid):
    n, lp, _ = z3.shape
    nb, nt = n // SEQ_BLOCK, lp // t_chunk
    assert nt == 1 or l_valid == lp
    n_pairs = HG_HEADS // HEAD_PAIR
    kern = functools.partial(_hg_kernel, t_chunk=t_chunk, n_steps=min(t_chunk, l_valid))
    blk3 = lambda w: pl.BlockSpec((SEQ_BLOCK, t_chunk, w), lambda b, c: (b, c, 0))
    st_spec = pl.BlockSpec((n_pairs, SEQ_BLOCK, HG_HEAD, LANES), lambda b, c: (0, b, 0, 0))
    q_scr = pltpu.VMEM((SEQ_BLOCK, t_chunk, HG_WIDTH), F32)
    return pl.pallas_call(
        kern,
        out_shape=[jax.ShapeDtypeStruct((n, lp, HG_WIDTH), F32),
                   jax.ShapeDtypeStruct((n_pairs, n, HG_HEAD, LANES), F32)],
        grid=(nb, nt),
        in_specs=[blk3(4 * HG_WIDTH), st_spec, _const_spec((1, HG_WIDTH)), _const_spec((1, HG_WIDTH))],
        out_specs=[blk3(HG_WIDTH), st_spec],
        scratch_shapes=[pltpu.VMEM((n_pairs, SEQ_BLOCK, HG_HEAD, LANES), F32)] + [q_scr] * 5,
        compiler_params=_cparams("arbitrary", "arbitrary"),
        name="hgrn2_mixer",
    )(z3, s0, lower.reshape(1, HG_WIDTH), norm_g.reshape(1, HG_WIDTH))


def _merge_kernel(x_ref, ya_ref, yb_ref, yc_ref, gates_ref, gt_ref, la_ref, lb_ref, lc_ref, wo_ref, o_ref):
    d = D_MODEL
    m = (gates_ref[:, 0:d] * _dot(ya_ref[...].astype(BF16), la_ref[...])
         + gates_ref[:, d:2 * d] * _dot(yb_ref[...].astype(BF16), lb_ref[...])
         + gates_ref[:, 2 * d:3 * d] * _dot(yc_ref[...].astype(BF16), lc_ref[...]))
    o_ref[...] = x_ref[...] + gt_ref[0] * _dot(m.astype(BF16), wo_ref[...])


def _merge_call(x2, l, ya, yb, yc, gates, gate1, prm, tm):
    rows, d = x2.shape
    gt_arr, gt_spec = _mod_operand(gate1, l, tm)
    row_spec = lambda w: pl.BlockSpec((tm, w), lambda i: (i, 0))
    return pl.pallas_call(
        _merge_kernel,
        out_shape=jax.ShapeDtypeStruct((rows, d), F32),
        grid=(rows // tm,),
        in_specs=[row_spec(d), row_spec(S5_WIDTH), row_spec(RW_WIDTH), row_spec(HG_WIDTH),
                  row_spec(N_BRANCH * d), gt_spec,
                  _const_spec((S5_WIDTH, d)), _const_spec((RW_WIDTH, d)), _const_spec((HG_WIDTH, d)),
                  _const_spec((d, d))],
        out_specs=row_spec(d),
        compiler_params=_cparams("arbitrary"),
        name="branch_merge",
    )(x2, ya, yb, yc, gates, gt_arr, prm["w_lift_a"], prm["w_lift_b"], prm["w_lift_c"], prm["w_out"])


def _ffn_kernel(*refs, l, tm, has_state, final_norm):
    if has_state:
        (x_ref, g_ref, sh_ref, sc_ref, gt_ref, wup_ref, cw_ref, cb_ref, wdn_ref, fg_ref,
         h0_ref, h1_ref, o_ref, up_ref, h_ref, buf_ref, hist_ref) = refs
    else:
        (x_ref, g_ref, sh_ref, sc_ref, gt_ref, wup_ref, cw_ref, cb_ref, wdn_ref, fg_ref,
         o_ref, tail_ref, h_ref, buf_ref, hist_ref) = refs
    i = pl.program_id(0)
    n_chunks = D_FF // FF_CHUNK

    @pl.when(i == 0)
    def _():
        hist_ref[...] = jnp.zeros_like(hist_ref)

    x = x_ref[...]
    h_ref[...] = _rms_modulate(x, g_ref[...], sh_ref[0], sc_ref[0]).astype(BF16)
    t_idx = (i * tm + lax.broadcasted_iota(jnp.int32, (tm, 1), 0)) % l
    after1 = t_idx >= 1
    after2 = t_idx >= 2

    def conv_half(c, half):
        col0 = half * D_FF + c * FF_CHUNK
        cols = slice(col0, col0 + FF_CHUNK)
        up = _dot(h_ref[...], wup_ref[:, cols])
        buf_ref[half, 0:SUBLANES, :] = hist_ref[c, half]
        buf_ref[half, SUBLANES:SUBLANES + tm, :] = up
        hist_ref[c, half] = up[tm - SUBLANES:tm, :]
        prev1 = jnp.where(after1, buf_ref[half, SUBLANES - 1:SUBLANES - 1 + tm, :], 0.0)
        prev2 = jnp.where(after2, buf_ref[half, SUBLANES - 2:SUBLANES - 2 + tm, :], 0.0)
        if has_state:
            up_ref[:, cols] = up
            prev1 = prev1 + h1_ref[:, cols]
            prev2 = prev2 + h0_ref[:, cols]
        else:
            tail_ref[0, :, cols] = buf_ref[half, SUBLANES + tm - (CONV_W - 1):SUBLANES + tm, :]
        return (cb_ref[:, cols] + cw_ref[0:1, cols] * prev2 + cw_ref[1:2, cols] * prev1
                + cw_ref[2:3, cols] * up)

    acc = jnp.zeros((tm, D_MODEL), F32)
    for c in range(n_chunks):
        act = jax.nn.gelu(conv_half(c, 0)) * conv_half(c, 1)
        acc = acc + _dot(act.astype(BF16), wdn_ref[c * FF_CHUNK:(c + 1) * FF_CHUNK, :])
    out = x + gt_ref[0] * acc
    if final_norm:
        ms = jnp.mean(out * out, axis=-1, keepdims=True)
        out = out * lax.rsqrt(ms + RMS_EPS) * fg_ref[...]
    o_ref[...] = out


def _ffn_call(x2, n, l, g, shift, scale, gate2, prm, final_g, conv_state, tm, final_norm):
    rows, d = x2.shape
    has_state = conv_state is not None
    sh_arr, sh_spec = _mod_operand(shift, l, tm)
    sc_arr, sc_spec = _mod_operand(scale, l, tm)
    gt_arr, gt_spec = _mod_operand(gate2, l, tm)
    row_spec = lambda w: pl.BlockSpec((tm, w), lambda i: (i, 0))
    in_specs = [row_spec(d), _const_spec((1, d)), sh_spec, sc_spec, gt_spec,
                _const_spec((d, 2 * D_FF)), _const_spec((CONV_W, 2 * D_FF)), _const_spec((1, 2 * D_FF)),
                _const_spec((D_FF, d)), _const_spec((1, d))]
    operands = [x2, g.reshape(1, d), sh_arr, sc_arr, gt_arr, prm["w_up"], prm["conv_w"], prm["conv_b"],
                prm["w_down"], final_g.reshape(1, d)]
    if has_state:
        assert tm % l == 0 and l > CONV_W - 1
        zeros = jnp.zeros((n, l - 1, 2 * D_FF), F32)
        hist1 = jnp.concatenate([conv_state[:, 1:2], zeros], axis=1).reshape(rows, 2 * D_FF)
        hist0 = jnp.concatenate([conv_state, zeros[:, 1:]], axis=1).reshape(rows, 2 * D_FF)
        in_specs += [row_spec(2 * D_FF), row_spec(2 * D_FF)]
        operands += [hist0, hist1]
        out_shape = [jax.ShapeDtypeStruct((rows, d), F32), jax.ShapeDtypeStruct((rows, 2 * D_FF), F32)]
        out_specs = [row_spec(d), row_spec(2 * D_FF)]
    else:
        assert l % tm == 0
        per = l // tm
        out_shape = [jax.ShapeDtypeStruct((rows, d), F32), jax.ShapeDtypeStruct((n, CONV_W - 1, 2 * D_FF), F32)]
        out_specs = [row_spec(d), pl.BlockSpec((1, CONV_W - 1, 2 * D_FF), lambda i: (i // per, 0, 0))]
    kern = functools.partial(_ffn_kernel, l=l, tm=tm, has_state=has_state, final_norm=final_norm)
    out, aux = pl.pallas_call(
        kern,
        out_shape=out_shape,
        grid=(rows // tm,),
        in_specs=in_specs,
        out_specs=out_specs,
        scratch_shapes=[pltpu.VMEM((tm, d), BF16),
                        pltpu.VMEM((2, tm + SUBLANES, FF_CHUNK), F32),
                        pltpu.VMEM((D_FF // FF_CHUNK, 2, SUBLANES, FF_CHUNK), F32)],
        compiler_params=_cparams("arbitrary"),
        name="conv_ffn",
    )(*operands)
    if has_state:
        aux = aux.reshape(n, l, 2 * D_FF)[:, l - (CONV_W - 1):]
    return out, aux


def _prepare_layer(p):
    out = {}
    w_in = p["w_in"]
    c1 = S5_WIDTH
    c2 = c1 + RW_COLS
    c3 = c2 + 4 * HG_WIDTH
    out["w_in_a"] = w_in[:, :c1].astype(BF16)
    out["w_in_b"] = w_in[:, c1:c2].astype(BF16)
    out["w_in_c"] = w_in[:, c2:c3].astype(BF16)
    out["w_in_g"] = w_in[:, c3:].astype(BF16)
    out["w_ada"] = p["w_ada"].astype(BF16)
    for name in ("w_lift_a", "w_lift_b", "w_lift_c", "w_out", "w_up", "w_down"):
        out[name] = p[name].astype(BF16)
    out["conv_w"] = p["conv_w"]
    out["conv_b"] = p["conv_b"].reshape(1, 2 * D_FF)

    lr = p["s5_lambda_re"]
    li = p["s5_lambda_im"]
    dt = jnp.exp(p["s5_log_dt"])[:, None]
    mag = jnp.exp(lr * dt)
    ar = mag * jnp.cos(li * dt)
    ai = mag * jnp.sin(li * dt)
    den = lr * lr + li * li
    zr = ((ar - 1.0) * lr + ai * li) / den
    zi = (ai * lr - (ar - 1.0) * li) / den
    bbr = zr[..., None] * p["s5_b_re"] - zi[..., None] * p["s5_b_im"]
    bbi = zr[..., None] * p["s5_b_im"] + zi[..., None] * p["s5_b_re"]
    eye = jnp.eye(S5_GROUPS, dtype=F32)
    bmat = jnp.einsum("gh,rgpc->gcrhp", eye, jnp.stack([bbr, bbi])).reshape(S5_WIDTH, 2 * S5_FLAT)
    cmat = jnp.einsum("hg,rgcp->rhpgc", eye, jnp.stack([p["s5_c_re"], -p["s5_c_im"]])).reshape(
        2 * S5_FLAT, S5_WIDTH)
    out["s5_b_hi"], out["s5_b_lo"] = _split_bf16(bmat)
    out["s5_c_hi"], out["s5_c_lo"] = _split_bf16(cmat)
    out["s5_a"] = jnp.stack([ar.reshape(S5_FLAT), ai.reshape(S5_FLAT)])
    out["s5_d"] = p["s5_d"].reshape(1, S5_WIDTH)
    out["s5_w_glu"] = p["s5_w_glu"].astype(BF16)
    out["s5_b_glu"] = p["s5_b_glu"].reshape(1, S5_WIDTH)

    out["rw_mu"] = p["rw_mu"].reshape(1, RW_COLS)
    out["rw_vec"] = jnp.stack([p["rw_w0"], p["rw_a0"], p["rw_k_k"], p["rw_k_a"],
                               p["rw_r_k"].reshape(RW_WIDTH), p["rw_ln_w"], p["rw_ln_b"],
                               jnp.zeros((RW_WIDTH,), F32)])
    zw = jnp.zeros((RW_LORA, RW_WIDTH), F32)
    out["rw_w2"] = zw.at[0:RW_DECAY_LORA].set(p["rw_w2"]).astype(BF16)
    out["rw_a2"] = zw.at[RW_DECAY_LORA:RW_DECAY_LORA + RW_AAA_LORA].set(p["rw_a2"]).astype(BF16)
    out["rw_g2"] = zw.at[RW_DECAY_LORA + RW_AAA_LORA:].set(p["rw_g2"]).astype(BF16)
    out["hg_norm"] = p["hg_norm"]
    out["g_mix"] = p["g_mix"]
    out["g_ffn"] = p["g_ffn"]
    out["b_ada"] = p["b_ada"]
    return out


def _rw_state_in(s):
    n = s.shape[0]
    s5 = s.reshape(n, RW_PAIRS, HEAD_PAIR, RW_HEAD, RW_HEAD)
    eye = jnp.eye(HEAD_PAIR, dtype=s.dtype)
    return jnp.einsum("ab,nhavk->hnavbk", eye, s5).reshape(RW_PAIRS, n, LANES, LANES)


def _rw_state_out(t):
    n = t.shape[1]
    t6 = t.reshape(RW_PAIRS, n, HEAD_PAIR, RW_HEAD, HEAD_PAIR, RW_HEAD)
    return jnp.einsum("hnavak->nhavk", t6).reshape(n, RW_HEADS, RW_HEAD, RW_HEAD)


def _hg_state_in(s):
    n = s.shape[0]
    s = s.reshape(n, HG_HEADS // HEAD_PAIR, HEAD_PAIR, HG_HEAD, HG_HEAD)
    return s.transpose(1, 0, 4, 2, 3).reshape(HG_HEADS // HEAD_PAIR, n, HG_HEAD, LANES)


def _hg_state_out(t):
    n = t.shape[1]
    t = t.reshape(HG_HEADS // HEAD_PAIR, n, HG_HEAD, HEAD_PAIR, HG_HEAD)
    return t.transpose(1, 0, 3, 4, 2).reshape(n, HG_HEADS, HG_HEAD, HG_HEAD)


def _pad_time(a, lp):
    n, l, w = a.shape
    if l == lp:
        return a
    return jnp.concatenate([a, jnp.zeros((n, lp - l, w), a.dtype)], axis=1)


def _run_trunk(x, mods, st_s5, st_shift, st_rw, st_hg, st_conv, lower, final_g, prms, t_chunk):
    n, l, d = x.shape
    rows = n * l
    tm = min(ROW_TILE, rows)
    lp = -(-l // SUBLANES) * SUBLANES
    tc = min(t_chunk, lp)
    depth = len(prms)
    h = x.reshape(rows, d)
    out_s5, out_shift, out_rw, out_hg, out_conv = [], [], [], [], []
    for layer in range(depth):
        prm = prms[layer]
        sh1, sc1, gt1, sh2, sc2, gt2 = jnp.split(mods[layer], 6, axis=-1)
        za, zb, zc, gates = _in_proj_call(h, l, prm["g_mix"], sh1, sc1, prm["w_in_a"], prm["w_in_b"],
                                          prm["w_in_c"], prm["w_in_g"], tm)

        s5_in = jnp.concatenate([st_s5[layer][..., 0].reshape(n, S5_FLAT),
                                 st_s5[layer][..., 1].reshape(n, S5_FLAT)], axis=1)
        ya, s5_new = _s5_call(_pad_time(za.reshape(n, l, S5_WIDTH), lp), s5_in, prm, tc, l)
        lp_rw = -(-l // RW_SUB) * RW_SUB
        yb, shift_new, rw_new = _rw_call(_pad_time(zb.reshape(n, l, RW_COLS), lp_rw),
                                         st_shift[layer].reshape(n, 1, RW_COLS),
                                         _rw_state_in(st_rw[layer]), prm, min(t_chunk, lp_rw), l)
        yc, hg_new = _hg_call(_pad_time(zc.reshape(n, l, 4 * HG_WIDTH), lp), _hg_state_in(st_hg[layer]),
                              lower[layer], prm["hg_norm"], tc, l)
        ya = ya[:, :l].reshape(rows, S5_WIDTH)
        yb = yb[:, :l].reshape(rows, RW_WIDTH)
        yc = yc[:, :l].reshape(rows, HG_WIDTH)

        h = _merge_call(h, l, ya, yb, yc, gates, gt1, prm, tm)
        ffn_tm = tm if st_conv is None else min(tm, 128)
        h, conv_new = _ffn_call(h, n, l, prm["g_ffn"], sh2, sc2, gt2, prm, final_g,
                                None if st_conv is None else st_conv[layer], ffn_tm,
                                final_norm=(layer == depth - 1))
        out_s5.append(jnp.stack([s5_new[:, :S5_FLAT].reshape(n, S5_GROUPS, S5_STATE),
                                 s5_new[:, S5_FLAT:].reshape(n, S5_GROUPS, S5_STATE)], axis=-1))
        out_shift.append(shift_new.reshape(n, RW_COLS))
        out_rw.append(_rw_state_out(rw_new))
        out_hg.append(_hg_state_out(hg_new))
        out_conv.append(conv_new)
    y = h.reshape(n, l, d).astype(x.dtype)
    return y, (jnp.stack(out_s5), jnp.stack(out_shift), jnp.stack(out_rw), jnp.stack(out_hg),
               jnp.stack(out_conv))


def kernel(x_prompt, x_sample, c_prompt, c_sample, state_s5, state_rwkv_shift, state_rwkv, state_hgrn, state_ffn_conv, w_ada, b_ada, g_mix, g_ffn, w_in, s5_lambda_re, s5_lambda_im, s5_log_dt, s5_b_re, s5_b_im, s5_c_re, s5_c_im, s5_d, s5_w_glu, s5_b_glu, rw_mu, rw_w0, rw_w2, rw_a0, rw_a2, rw_g2, rw_k_k, rw_k_a, rw_r_k, rw_ln_w, rw_ln_b, hg_lb, hg_norm, w_lift_a, w_lift_b, w_lift_c, w_out, w_up, conv_w, conv_b, w_down, final_g):
    per_layer = {
        "w_ada": w_ada, "b_ada": b_ada, "g_mix": g_mix, "g_ffn": g_ffn, "w_in": w_in,
        "s5_lambda_re": s5_lambda_re, "s5_lambda_im": s5_lambda_im, "s5_log_dt": s5_log_dt,
        "s5_b_re": s5_b_re, "s5_b_im": s5_b_im, "s5_c_re": s5_c_re, "s5_c_im": s5_c_im,
        "s5_d": s5_d, "s5_w_glu": s5_w_glu, "s5_b_glu": s5_b_glu,
        "rw_mu": rw_mu, "rw_w0": rw_w0, "rw_w2": rw_w2, "rw_a0": rw_a0, "rw_a2": rw_a2,
        "rw_g2": rw_g2, "rw_k_k": rw_k_k, "rw_k_a": rw_k_a, "rw_r_k": rw_r_k,
        "rw_ln_w": rw_ln_w, "rw_ln_b": rw_ln_b, "hg_norm": hg_norm,
        "w_lift_a": w_lift_a, "w_lift_b": w_lift_b, "w_lift_c": w_lift_c, "w_out": w_out,
        "w_up": w_up, "conv_w": conv_w, "conv_b": conv_b, "w_down": w_down,
    }
    depth = w_ada.shape[0]
    prms = [_prepare_layer({k: v[layer] for k, v in per_layer.items()}) for layer in range(depth)]

    lbp = jax.nn.softmax(hg_lb.astype(F32), axis=0)
    lower = jnp.cumsum(lbp, axis=0) - lbp[0]

    nb = x_prompt.shape[0]
    ns = x_sample.shape[0]
    c_all = jnp.concatenate([c_prompt, c_sample], axis=0).astype(F32)
    mods = [_ada_call(c_all, prms[layer]["w_ada"], prms[layer]["b_ada"]) for layer in range(depth)]
    mods_p = [m[:nb] for m in mods]
    mods_s = [m[nb:] for m in mods]

    z_s5 = jnp.zeros((depth, nb) + state_s5.shape[2:], F32)
    z_shift = jnp.zeros((depth, nb) + state_rwkv_shift.shape[2:], F32)
    z_rw = jnp.zeros((depth, nb) + state_rwkv.shape[2:], F32)
    z_hg = jnp.zeros((depth, nb) + state_hgrn.shape[2:], F32)

    y_prompt, (s5_p, shift_p, rw_p, hg_p, conv_p) = _run_trunk(
        x_prompt, mods_p, z_s5, z_shift, z_rw, z_hg, None, lower, final_g, prms, t_chunk=64)
    y_sample, (s5_s, shift_s, rw_s, hg_s, conv_s) = _run_trunk(
        x_sample, mods_s, state_s5, state_rwkv_shift, state_rwkv, state_hgrn, state_ffn_conv,
        lower, final_g, prms, t_chunk=64)
    return (y_prompt, y_sample, s5_p, shift_p, rw_p, hg_p, conv_p, s5_s, shift_s, rw_s, hg_s, conv_s)
```

```python
import functools
import math

import jax
import jax.numpy as jnp
from jax import lax
from jax.experimental import pallas as pl
from jax.experimental.pallas import tpu as pltpu

F32 = jnp.float32
BF16 = jnp.bfloat16

D_MODEL = 1024
S5_WIDTH = D_MODEL // 4
S5_GROUP = 16
S5_GROUPS = S5_WIDTH // S5_GROUP
S5_STATE = 64
S5_FLAT = S5_GROUPS * S5_STATE
RW_WIDTH = D_MODEL // 2
RW_HEAD = 64
RW_HEADS = RW_WIDTH // RW_HEAD
RW_DECAY_LORA = 32
RW_AAA_LORA = 32
RW_GATE_LORA = 64
RW_LORA = RW_DECAY_LORA + RW_AAA_LORA + RW_GATE_LORA
RW_COLS = 3 * RW_WIDTH + RW_LORA
RW_GN_EPS = 1e-5 * RW_HEAD
HG_WIDTH = D_MODEL // 4
HG_HEAD = 64
HG_HEADS = HG_WIDTH // HG_HEAD
HG_GATE_FLOOR = 1e-30
N_BRANCH = 3
D_FF = 256 * ((8 * D_MODEL // 3 + 255) // 256)
CONV_W = 3
RMS_EPS = 1e-6

LANES = 128
SUBLANES = 8
MXU_COLS = 256
VMEM_LIMIT_BYTES = 56 * 1024 * 1024

SEQ_BLOCK = SUBLANES
HEAD_PAIR = LANES // RW_HEAD
ROW_TILE = 512
FF_CHUNK = 256
IN_PROJ_CHUNK = 512
RW_SUB = 16
RW_PAIRS = RW_HEADS // HEAD_PAIR
HG_SUB = 16
HG_PAIRS = HG_HEADS // HEAD_PAIR
RW_SOLVE_PASSES = 1
RW_APPLY_PASSES = 1
RW_STATE_PASSES = 1

_NN = (((1,), (0,)), ((), ()))
_NT = (((1,), (1,)), ((), ()))


def _cparams(*sem):
    return pltpu.CompilerParams(dimension_semantics=sem, vmem_limit_bytes=VMEM_LIMIT_BYTES)


def _const_spec(shape):
    nd = len(shape)
    return pl.BlockSpec(shape, lambda *_: (0,) * nd, pipeline_mode=pl.Buffered(1))


def _dot(a, b):
    return jnp.dot(a, b, preferred_element_type=F32)


def _split_bf16(x):
    hi = x.astype(BF16)
    lo = (x - hi.astype(F32)).astype(BF16)
    return hi, lo


def _head_ones():
    r = lax.broadcasted_iota(jnp.int32, (LANES, LANES), 0) // RW_HEAD
    c = lax.broadcasted_iota(jnp.int32, (LANES, LANES), 1) // RW_HEAD
    return (r == c).astype(BF16)


def _head_sum(x, ones):
    hi, lo = _split_bf16(x)
    tiles = [_dot(hi[:, c:c + LANES], ones) + _dot(lo[:, c:c + LANES], ones)
             for c in range(0, x.shape[1], LANES)]
    return tiles[0] if len(tiles) == 1 else jnp.concatenate(tiles, axis=1)


def _softplus(x):
    return jnp.maximum(x, 0.0) + jnp.log1p(jnp.exp(-jnp.abs(x)))


def _gelu(x):
    c = 2.0 * math.sqrt(2.0 / math.pi)
    return x * jax.nn.sigmoid(x * (c + (c * 0.044715) * (x * x)))


def _rms_modulate(x, g, shift, scale):
    ms = jnp.mean(x * x, axis=-1, keepdims=True)
    return x * lax.rsqrt(ms + RMS_EPS) * g * (1.0 + scale) + shift


def _ada_kernel(c_ref, w_ref, b_ref, o_ref):
    c = c_ref[...]
    o_ref[...] = _dot(jax.nn.silu(c).astype(BF16), w_ref[...]) + b_ref[...]


def _ada_call(c, w_bf16, b):
    n, d = c.shape
    cols = w_bf16.shape[1]
    tn = cols // 4
    return pl.pallas_call(
        _ada_kernel,
        out_shape=jax.ShapeDtypeStruct((n, cols), F32),
        grid=(cols // tn,),
        in_specs=[_const_spec((n, d)),
                  pl.BlockSpec((d, tn), lambda j: (0, j)),
                  pl.BlockSpec((1, tn), lambda j: (0, j))],
        out_specs=pl.BlockSpec((n, tn), lambda j: (0, j)),
        compiler_params=_cparams("arbitrary"),
        name="ada_mod",
    )(c, w_bf16, b.reshape(1, cols))


def _mod_operand(m, l, tm):
    n, d = m.shape
    if l % tm == 0:
        per = l // tm
        return m.reshape(n, 1, d), pl.BlockSpec((1, 1, d), lambda i: (i // per, 0, 0))
    assert tm % l == 0
    return jnp.repeat(m, l, axis=0).reshape(1, n * l, d), pl.BlockSpec((1, tm, d), lambda i: (0, i, 0))


def _in_proj_kernel(x_ref, g_ref, sh_ref, sc_ref, wa_ref, wb_ref, wc_ref, wg_ref,
                    oa_ref, ob_ref, oc_ref, og_ref):
    h = _rms_modulate(x_ref[...], g_ref[...], sh_ref[0], sc_ref[0]).astype(BF16)

    def project(w_ref, o_ref, fn=None):
        width = w_ref.shape[1]
        for c0 in range(0, width, IN_PROJ_CHUNK):
            cols = slice(c0, min(c0 + IN_PROJ_CHUNK, width))
            z = _dot(h, w_ref[:, cols])
            o_ref[:, cols] = z if fn is None else fn(z)

    project(wa_ref, oa_ref)
    project(wb_ref, ob_ref)
    project(wc_ref, oc_ref)
    project(wg_ref, og_ref, jax.nn.sigmoid)


def _in_proj_call(x2, l, g, shift, scale, w_a, w_b, w_c, w_g, tm):
    rows, d = x2.shape
    sh_arr, sh_spec = _mod_operand(shift, l, tm)
    sc_arr, sc_spec = _mod_operand(scale, l, tm)
    widths = [w.shape[1] for w in (w_a, w_b, w_c, w_g)]
    return pl.pallas_call(
        _in_proj_kernel,
        out_shape=[jax.ShapeDtypeStruct((rows, w), F32) for w in widths],
        grid=(rows // tm,),
        in_specs=[pl.BlockSpec((tm, d), lambda i: (i, 0)), _const_spec((1, d)), sh_spec, sc_spec]
        + [_const_spec(w.shape) for w in (w_a, w_b, w_c, w_g)],
        out_specs=[pl.BlockSpec((tm, w), lambda i: (i, 0)) for w in widths],
        compiler_params=_cparams("arbitrary"),
        name="in_proj",
    )(x2, g.reshape(1, d), sh_arr, sc_arr, w_a, w_b, w_c, w_g)


def _s5_kernel(u_ref, s0_ref, b_ref, c_ref, a_ref, d_ref, wg_ref, bg_ref,
               y_ref, sl_ref, e_ref, st_ref, *, t_chunk, n_steps):
    @pl.when(pl.program_id(1) == 0)
    def _():
        st_ref[...] = s0_ref[...]

    rows = SEQ_BLOCK * t_chunk
    ut = jnp.swapaxes(u_ref[...], 0, 1).reshape(rows, S5_WIDTH)
    ub = ut.astype(BF16)
    for c0 in range(0, 2 * S5_FLAT, MXU_COLS):
        ch0 = (c0 % S5_FLAT) // S5_STATE * S5_GROUP
        k0 = ch0 // LANES * LANES
        assert ch0 + MXU_COLS // S5_STATE * S5_GROUP <= k0 + LANES
        e_ref[:, c0:c0 + MXU_COLS] = _dot(ub[:, k0:k0 + LANES], b_ref[k0:k0 + LANES, c0:c0 + MXU_COLS])

    ar = jnp.broadcast_to(a_ref[0:1, :], (SEQ_BLOCK, S5_FLAT))
    ai = jnp.broadcast_to(a_ref[1:2, :], (SEQ_BLOCK, S5_FLAT))

    def step(t, carry):
        sr, si = carry
        r = pl.ds(pl.multiple_of(t * SEQ_BLOCK, SEQ_BLOCK), SEQ_BLOCK)
        nr = ar * sr - ai * si + e_ref[r, 0:S5_FLAT]
        ni = ar * si + ai * sr + e_ref[r, S5_FLAT:2 * S5_FLAT]
        e_ref[r, 0:S5_FLAT] = nr
        e_ref[r, S5_FLAT:2 * S5_FLAT] = ni
        return nr, ni

    sr, si = lax.fori_loop(0, n_steps, step, (st_ref[:, 0:S5_FLAT], st_ref[:, S5_FLAT:2 * S5_FLAT]))
    st_ref[:, 0:S5_FLAT] = sr
    st_ref[:, S5_FLAT:2 * S5_FLAT] = si
    sl_ref[...] = st_ref[...]

    y = _dot(e_ref[...].astype(BF16), c_ref[...]) + d_ref[...] * ut
    y = _gelu(y)
    y = y * jax.nn.sigmoid(_dot(y.astype(BF16), wg_ref[...]) + bg_ref[...])
    y_ref[...] = jnp.swapaxes(y.reshape(t_chunk, SEQ_BLOCK, S5_WIDTH), 0, 1)


def _s5_call(u3, s0, prm, t_chunk, l_valid):
    n, lp, _ = u3.shape
    nb, nt = n // SEQ_BLOCK, lp // t_chunk
    assert nt == 1 or l_valid == lp
    kern = functools.partial(_s5_kernel, t_chunk=t_chunk, n_steps=min(t_chunk, l_valid))
    return pl.pallas_call(
        kern,
        out_shape=[jax.ShapeDtypeStruct((n, lp, S5_WIDTH), F32),
                   jax.ShapeDtypeStruct((n, 2 * S5_FLAT), F32)],
        grid=(nb, nt),
        in_specs=[pl.BlockSpec((SEQ_BLOCK, t_chunk, S5_WIDTH), lambda b, c: (b, c, 0)),
                  pl.BlockSpec((SEQ_BLOCK, 2 * S5_FLAT), lambda b, c: (b, 0)),
                  _const_spec((S5_WIDTH, 2 * S5_FLAT)), _const_spec((2 * S5_FLAT, S5_WIDTH)),
                  _const_spec((2, S5_FLAT)), _const_spec((1, S5_WIDTH)),
                  _const_spec((S5_WIDTH, S5_WIDTH)), _const_spec((1, S5_WIDTH))],
        out_specs=[pl.BlockSpec((SEQ_BLOCK, t_chunk, S5_WIDTH), lambda b, c: (b, c, 0)),
                   pl.BlockSpec((SEQ_BLOCK, 2 * S5_FLAT), lambda b, c: (b, 0))],
        scratch_shapes=[pltpu.VMEM((SEQ_BLOCK * t_chunk, 2 * S5_FLAT), F32),
                        pltpu.VMEM((SEQ_BLOCK, 2 * S5_FLAT), F32)],
        compiler_params=_cparams("arbitrary", "arbitrary"),
        name="s5_mixer",
    )(u3, s0, prm["s5_b"], prm["s5_c"], prm["s5_a"], prm["s5_d"], prm["s5_w_glu"], prm["s5_b_glu"])


def _mm(a, b, dims, passes):
    dg = lambda x, y: lax.dot_general(x, y, dims, preferred_element_type=F32)
    if passes == 1:
        return dg(a.astype(BF16), b.astype(BF16))
    a_hi, a_lo = _split_bf16(a)
    if passes == 2:
        b_hi = b.astype(BF16)
        return dg(a_hi, b_hi) + dg(a_lo, b_hi)
    b_hi, b_lo = _split_bf16(b)
    return dg(a_hi, b_hi) + dg(a_lo, b_hi) + dg(a_hi, b_lo)


def _cumsum_groups(x, group):
    pos = lax.broadcasted_iota(jnp.int32, (x.shape[0], 1), 0) % group
    s = 1
    while s < group:
        x = x + jnp.where(pos >= s, pltpu.roll(x, s, 0), 0.0)
        s *= 2
    return x


def _rw_kernel(p_ref, sh0_ref, s0_ref, mu_ref, vec_ref, w2_ref, a2_ref, g2_ref,
               y_ref, shl_ref, sl_ref,
               pbuf_ref, s_ref, at_ref, bt_ref, kt_ref, rt_ref, bh_ref, kh_ref, v_ref, gc_ref, ys_ref,
               *, t_chunk, l_valid, last_row):
    rows = SEQ_BLOCK * t_chunk
    n_sub = t_chunk // RW_SUB

    @pl.when(pl.program_id(1) == 0)
    def _():
        pbuf_ref[:, SUBLANES - 1:SUBLANES, :] = sh0_ref[...]
        s_ref[...] = s0_ref[...]

    p3 = p_ref[...]
    pbuf_ref[:, SUBLANES:SUBLANES + t_chunk, :] = p3
    prev3 = pbuf_ref[:, SUBLANES - 1:SUBLANES - 1 + t_chunk, :]
    last = p_ref[:, last_row:last_row + 1, :]
    pbuf_ref[:, SUBLANES - 1:SUBLANES, :] = last
    shl_ref[...] = last

    xm = (p3 + (prev3 - p3) * mu_ref[...]).reshape(rows, RW_COLS)
    r = xm[:, 0:RW_WIDTH]
    k = xm[:, RW_WIDTH:2 * RW_WIDTH]
    v = xm[:, 2 * RW_WIDTH:3 * RW_WIDTH]
    lora = xm[:, 3 * RW_WIDTH:RW_COLS]
    w0, a0, k_k, k_a = vec_ref[0:1, :], vec_ref[1:2, :], vec_ref[2:3, :], vec_ref[3:4, :]
    r_k, ln_w, ln_b = vec_ref[4:5, :], vec_ref[5:6, :], vec_ref[6:7, :]

    w = -_softplus(-(w0 + _dot(jnp.tanh(lora).astype(BF16), w2_ref[...]))) - 0.5
    log_decay = -jnp.exp(w)
    a = jax.nn.sigmoid(a0 + _dot(lora.astype(BF16), a2_ref[...]))
    g = _dot(jax.nn.sigmoid(lora).astype(BF16), g2_ref[...])

    ones = _head_ones()
    kk = k * k_k
    kk = kk / jnp.maximum(jnp.sqrt(_head_sum(kk * kk, ones)), 1e-12)
    kt = k * (1.0 + (a - 1.0) * k_a)
    kka = kk * a
    if l_valid < t_chunk:
        live = lax.broadcasted_iota(jnp.int32, (rows, 1), 0) % t_chunk < l_valid
        log_decay = jnp.where(live, log_decay, 0.0)
        kk = jnp.where(live, kk, 0.0)
        kka = jnp.where(live, kka, 0.0)
        kt_live = jnp.where(live, kt, 0.0)
    else:
        kt_live = kt

    cum = _cumsum_groups(log_decay, RW_SUB)
    shape_g = (rows // RW_SUB, RW_SUB, RW_WIDTH)
    cum_end = jnp.broadcast_to(cum.reshape(shape_g)[:, RW_SUB - 1:RW_SUB, :], shape_g).reshape(rows, RW_WIDTH)
    g_in = jnp.exp(cum)
    g_inv = jnp.exp(-cum)
    g_ex = jnp.exp(cum - log_decay)
    g_out = jnp.exp(cum_end - cum)
    shape3 = (SEQ_BLOCK, t_chunk, RW_WIDTH)
    at_ref[...] = (-kk * g_ex).reshape(shape3)
    bt_ref[...] = (kka * g_inv).reshape(shape3)
    kt_ref[...] = (kt_live * g_inv).reshape(shape3)
    rt_ref[...] = (r * g_in).reshape(shape3)
    bh_ref[...] = (kka * g_out).reshape(shape3)
    kh_ref[...] = (kt_live * g_out).reshape(shape3)
    v_ref[...] = v.reshape(shape3)
    gc_ref[...] = jnp.exp(cum_end).reshape(shape3)

    m_rows = SEQ_BLOCK * RW_SUB
    ri = lax.broadcasted_iota(jnp.int32, (m_rows, m_rows), 0)
    ci = lax.broadcasted_iota(jnp.int32, (m_rows, m_rows), 1)
    same_seq = (ri // RW_SUB) == (ci // RW_SUB)
    before = same_seq & ((ci % RW_SUB) < (ri % RW_SUB))
    upto = same_seq & ((ci % RW_SUB) <= (ri % RW_SUB))
    eye = (ri == ci).astype(F32)
    pair_mask = (ri // 2) == (ci // 2)
    level_masks = []
    s = 2
    while s < RW_SUB:
        level_masks.append(((ri // (2 * s)) == (ci // (2 * s))) & ((ri // s) != (ci // s)))
        s *= 2
    lane = lax.broadcasted_iota(jnp.int32, (m_rows, LANES), 1)
    head_mask = [lane < RW_HEAD, lane >= RW_HEAD]
    wide = (m_rows, SEQ_BLOCK * LANES)
    own_rows = (lax.broadcasted_iota(jnp.int32, wide, 1) // LANES
                == lax.broadcasted_iota(jnp.int32, wide, 0) // RW_SUB)
    same_head = (lax.broadcasted_iota(jnp.int32, wide, 0) // RW_HEAD
                 == (lax.broadcasted_iota(jnp.int32, wide, 1) % LANES) // RW_HEAD)
    heads = [(hp, h) for hp in range(RW_PAIRS) for h in range(HEAD_PAIR)]

    def sub_chunk(c, carry):
        rs = pl.ds(pl.multiple_of(c * RW_SUB, RW_SUB), RW_SUB)

        def ld(ref, hp):
            return ref[:, rs, pl.ds(hp * LANES, LANES)].reshape(m_rows, LANES)

        at, bt, ktl, rt, bh, kh, vv, gcv = ([ld(ref, hp) for hp in range(RW_PAIRS)]
                                            for ref in (at_ref, bt_ref, kt_ref, rt_ref, bh_ref, kh_ref,
                                                        v_ref, gc_ref))
        state = [[s_ref[hp, n] for n in range(SEQ_BLOCK)] for hp in range(RW_PAIRS)]
        zero = jnp.zeros((m_rows, LANES), F32)
        msk = lambda x, h: jnp.where(head_mask[h], x, zero)

        gram = [_mm(jnp.concatenate([msk(at[hp], 0), msk(rt[hp], 0), msk(at[hp], 1), msk(rt[hp], 1)], axis=0),
                    jnp.concatenate([bt[hp], ktl[hp]], axis=0), _NT, RW_SOLVE_PASSES)
                for hp in range(RW_PAIRS)]

        def quad(hp, h, row, col, keep):
            blk = gram[hp][(2 * h + row) * m_rows:(2 * h + row + 1) * m_rows, col * m_rows:(col + 1) * m_rows]
            return jnp.where(keep, blk, 0.0)

        m_ab = [quad(hp, h, 0, 0, before) for hp, h in heads]
        m_ak = [quad(hp, h, 0, 1, before) for hp, h in heads]
        n_rb = [quad(hp, h, 1, 0, upto) for hp, h in heads]
        n_rk = [quad(hp, h, 1, 1, upto) for hp, h in heads]
        tinv = [eye + jnp.where(pair_mask, m, 0.0) for m in m_ab]
        for lm in level_masks:
            prod = [_mm(jnp.where(lm, m, 0.0), t, _NN, RW_SOLVE_PASSES) for m, t in zip(m_ab, tinv)]
            tinv = [t + _mm(t, q, _NN, RW_SOLVE_PASSES) for t, q in zip(tinv, prod)]
        vh = [msk(vv[hp], h) for hp, h in heads]
        w1 = [_mm(m, x, _NN, RW_APPLY_PASSES) for m, x in zip(m_ak, vh)]
        ap = [_mm(t, jnp.concatenate([msk(at[hp], h), w], axis=1), _NN, RW_APPLY_PASSES)
              for t, (hp, h), w in zip(tinv, heads, w1)]
        nb = [_mm(m, x, _NN, RW_APPLY_PASSES) for m, x in zip(n_rb, ap)]
        nv = [_mm(m, x, _NN, RW_APPLY_PASSES) for m, x in zip(n_rk, vh)]

        def per_seq(x):
            return jnp.where(own_rows, jnp.concatenate([x] * SEQ_BLOCK, axis=1), 0.0)

        for hp in range(RW_PAIRS):
            i0, i1 = hp * HEAD_PAIR, hp * HEAD_PAIR + 1
            a_hat = ap[i0][:, 0:LANES] + ap[i1][:, 0:LANES]
            p_zero = ap[i0][:, LANES:2 * LANES] + ap[i1][:, LANES:2 * LANES]
            r_hat = rt[hp] + nb[i0][:, 0:LANES] + nb[i1][:, 0:LANES]
            y_zero = nb[i0][:, LANES:2 * LANES] + nb[i1][:, LANES:2 * LANES] + nv[i0] + nv[i1]
            bh_x = per_seq(bh[hp])
            kh_x = per_seq(kh[hp])
            gam = jnp.where(same_head, _mm(a_hat.T, bh_x, _NN, RW_STATE_PASSES), 0.0)
            u = jnp.where(same_head, _mm(jnp.concatenate([p_zero.T, vv[hp].T], axis=1),
                                         jnp.concatenate([bh_x, kh_x], axis=0), _NN, RW_STATE_PASSES), 0.0)
            cols = pl.ds(hp * LANES, LANES)
            for n in range(SEQ_BLOCK):
                q = slice(n * RW_SUB, (n + 1) * RW_SUB)
                blk = slice(n * LANES, (n + 1) * LANES)
                st = state[hp][n]
                ys_ref[n, rs, cols] = _mm(r_hat[q], st, _NT, RW_STATE_PASSES) + y_zero[q]
                s_ref[hp, n] = (st * gcv[hp][n * RW_SUB:n * RW_SUB + 1, :]
                                + _mm(st, gam[:, blk], _NN, RW_STATE_PASSES) + u[:, blk])
        return carry

    lax.fori_loop(0, n_sub, sub_chunk, 0)
    sl_ref[...] = s_ref[...]

    y = ys_ref[...].reshape(rows, RW_WIDTH)
    inv = 1.0 / RW_HEAD
    mean = _head_sum(y, ones) * inv
    yc = y - mean
    var = _head_sum(yc * yc, ones) * inv
    y = yc * lax.rsqrt(var + RW_GN_EPS) * ln_w + ln_b
    bonus = _head_sum(r * kt * r_k, ones) * v
    y_ref[...] = ((y + bonus) * g).reshape(shape3)


def _rw_call(p3, shift0, s0, prm, t_chunk, l_valid):
    n, lp, _ = p3.shape
    nb, nt = n // SEQ_BLOCK, lp // t_chunk
    assert nt == 1 or l_valid == lp
    assert t_chunk % RW_SUB == 0
    valid_in_chunk = min(t_chunk, l_valid)
    kern = functools.partial(_rw_kernel, t_chunk=t_chunk, l_valid=valid_in_chunk, last_row=valid_in_chunk - 1)
    blk3 = lambda w: pl.BlockSpec((SEQ_BLOCK, t_chunk, w), lambda b, c: (b, c, 0))
    st_spec = pl.BlockSpec((RW_PAIRS, SEQ_BLOCK, LANES, LANES), lambda b, c: (0, b, 0, 0))
    sh_spec = pl.BlockSpec((SEQ_BLOCK, 1, RW_COLS), lambda b, c: (b, 0, 0))
    q_scr = pltpu.VMEM((SEQ_BLOCK, t_chunk, RW_WIDTH), F32)
    return pl.pallas_call(
        kern,
        out_shape=[jax.ShapeDtypeStruct((n, lp, RW_WIDTH), F32),
                   jax.ShapeDtypeStruct((n, 1, RW_COLS), F32),
                   jax.ShapeDtypeStruct((RW_PAIRS, n, LANES, LANES), F32)],
        grid=(nb, nt),
        in_specs=[blk3(RW_COLS), sh_spec, st_spec,
                  _const_spec((1, RW_COLS)), _const_spec((SUBLANES, RW_WIDTH)),
                  _const_spec((RW_LORA, RW_WIDTH)), _const_spec((RW_LORA, RW_WIDTH)),
                  _const_spec((RW_LORA, RW_WIDTH))],
        out_specs=[blk3(RW_WIDTH), sh_spec, st_spec],
        scratch_shapes=[pltpu.VMEM((SEQ_BLOCK, t_chunk + SUBLANES, RW_COLS), F32),
                        pltpu.VMEM((RW_PAIRS, SEQ_BLOCK, LANES, LANES), F32)] + [q_scr] * 9,
        compiler_params=_cparams("arbitrary", "arbitrary"),
        name="rwkv7_mixer",
    )(p3, shift0, s0, prm["rw_mu"], prm["rw_vec"], prm["rw_w2"], prm["rw_a2"], prm["rw_g2"])


def _hg_kernel(z_ref, s0_ref, lower_ref, ng_ref, y_ref, sl_ref,
               s_ref, qs_ref, q_ref, k_ref, kh_ref, v_ref, b_ref, gc_ref, ys_ref, *, t_chunk, l_valid):
    rows = SEQ_BLOCK * t_chunk
    n_sub = t_chunk // HG_SUB

    @pl.when(pl.program_id(1) == 0)
    def _():
        s_ref[...] = s0_ref[...]

    z = z_ref[...].reshape(rows, 4 * HG_WIDTH)
    q = jax.nn.silu(z[:, 0:HG_WIDTH])
    f = z[:, HG_WIDTH:2 * HG_WIDTH]
    i = z[:, 2 * HG_WIDTH:3 * HG_WIDTH]
    og = z[:, 3 * HG_WIDTH:4 * HG_WIDTH]
    lower = lower_ref[...]
    fgate = lower + (1.0 - lower) * jax.nn.sigmoid(f)
    log_f = jnp.log(jnp.maximum(fgate, HG_GATE_FLOOR))
    k = 1.0 - fgate
    if l_valid < t_chunk:
        live = lax.broadcasted_iota(jnp.int32, (rows, 1), 0) % t_chunk < l_valid
        log_f = jnp.where(live, log_f, 0.0)
        k = jnp.where(live, k, 0.0)
    cum = _cumsum_groups(log_f, HG_SUB)
    shape_g = (rows // HG_SUB, HG_SUB, HG_WIDTH)
    cum_end = jnp.broadcast_to(cum.reshape(shape_g)[:, HG_SUB - 1:HG_SUB, :], shape_g).reshape(rows, HG_WIDTH)
    shape3 = (SEQ_BLOCK, t_chunk, HG_WIDTH)
    qs_ref[...] = (q * jnp.exp(cum)).reshape(shape3)
    q_ref[...] = q.reshape(shape3)
    k_ref[...] = k.reshape(shape3)
    kh_ref[...] = (k * jnp.exp(cum_end - cum)).reshape(shape3)
    v_ref[...] = i.reshape(shape3)
    b_ref[...] = cum.reshape(shape3)
    gc_ref[...] = jnp.exp(cum_end).reshape(shape3)

    m_rows = SEQ_BLOCK * HG_SUB
    ones = _head_ones()
    wide = (m_rows, SEQ_BLOCK * LANES)
    own_rows = (lax.broadcasted_iota(jnp.int32, wide, 1) // LANES
                == lax.broadcasted_iota(jnp.int32, wide, 0) // HG_SUB)
    same_head = (lax.broadcasted_iota(jnp.int32, wide, 0) // HG_HEAD
                 == (lax.broadcasted_iota(jnp.int32, wide, 1) % LANES) // HG_HEAD)
    step = lax.broadcasted_iota(jnp.int32, (HG_SUB, LANES), 0)

    def sub_chunk(c, carry):
        rs = pl.ds(pl.multiple_of(c * HG_SUB, HG_SUB), HG_SUB)
        for hp in range(HG_PAIRS):
            cols = pl.ds(hp * LANES, LANES)
            ld = lambda ref: ref[:, rs, cols].reshape(m_rows, LANES)
            qs, qq, kk, kh, vv, bb, gcv = (ld(qs_ref), ld(q_ref), ld(k_ref), ld(kh_ref), ld(v_ref),
                                           ld(b_ref), ld(gc_ref))
            state = [s_ref[hp, n] for n in range(SEQ_BLOCK)]
            kh_x = jnp.where(own_rows, jnp.concatenate([kh] * SEQ_BLOCK, axis=1), 0.0)
            u = jnp.where(same_head, _mm(vv.T, kh_x, _NN, 1), 0.0)
            for n in range(SEQ_BLOCK):
                sl = slice(n * HG_SUB, (n + 1) * HG_SUB)
                bn, qn, kn, vn = bb[sl], qq[sl], kk[sl], vv[sl]
                prods = []
                for t in range(HG_SUB):
                    keep = step <= t
                    diff = jnp.where(keep, bn[t:t + 1, :] - bn, 0.0)
                    prods.append(jnp.where(keep, qn[t:t + 1, :] * kn * jnp.exp(diff), 0.0))
                att = _dot(jnp.concatenate(prods, axis=0).astype(BF16), ones)
                o_rows = [jnp.sum(att[t * HG_SUB:(t + 1) * HG_SUB] * vn, axis=0, keepdims=True)
                          for t in range(HG_SUB)]
                ys_ref[n, rs, cols] = _mm(qs[sl], state[n], _NT, 1) + jnp.concatenate(o_rows, axis=0)
                blk = slice(n * LANES, (n + 1) * LANES)
                s_ref[hp, n] = state[n] * gcv[n * HG_SUB:n * HG_SUB + 1, :] + u[:, blk]
        return carry

    lax.fori_loop(0, n_sub, sub_chunk, 0)
    sl_ref[...] = s_ref[...]

    o = ys_ref[...].reshape(rows, HG_WIDTH)
    ms = _head_sum(o * o, ones) * (1.0 / HG_HEAD)
    o = o * lax.rsqrt(ms + RMS_EPS) * ng_ref[...] * jax.nn.sigmoid(og)
    y_ref[...] = o.reshape(shape3)


def _hg_call(z3, s0, lower, norm_g, t_chunk, l_valid):
    n, lp, _ = z3.shape
    nb, nt = n // SEQ_BLOCK, lp // t_chunk
    assert nt == 1 or l_valid == lp
    assert t_chunk % HG_SUB == 0
    kern = functools.partial(_hg_kernel, t_chunk=t_chunk, l_valid=min(t_chunk, l_valid))
    blk3 = lambda w: pl.BlockSpec((SEQ_BLOCK, t_chunk, w), lambda b, c: (b, c, 0))
    st_spec = pl.BlockSpec((HG_PAIRS, SEQ_BLOCK, LANES, LANES), lambda b, c: (0, b, 0, 0))
    q_scr = pltpu.VMEM((SEQ_BLOCK, t_chunk, HG_WIDTH), F32)
    return pl.pallas_call(
        kern,
        out_shape=[jax.ShapeDtypeStruct((n, lp, HG_WIDTH), F32),
                   jax.ShapeDtypeStruct((HG_PAIRS, n, LANES, LANES), F32)],
        grid=(nb, nt),
        in_specs=[blk3(4 * HG_WIDTH), st_spec, _const_spec((1, HG_WIDTH)), _const_spec((1, HG_WIDTH))],
        out_specs=[blk3(HG_WIDTH), st_spec],
        scratch_shapes=[pltpu.VMEM((HG_PAIRS, SEQ_BLOCK, LANES, LANES), F32)] + [q_scr] * 8,
        compiler_params=_cparams("arbitrary", "arbitrary"),
        name="hgrn2_mixer",
    )(z3, s0, lower.reshape(1, HG_WIDTH), norm_g.reshape(1, HG_WIDTH))


def _merge_kernel(x_ref, ya_ref, yb_ref, yc_ref, gates_ref, gt_ref, la_ref, lb_ref, lc_ref, wo_ref, o_ref):
    d = D_MODEL
    m = (gates_ref[:, 0:d] * _dot(ya_ref[...].astype(BF16), la_ref[...])
         + gates_ref[:, d:2 * d] * _dot(yb_ref[...].astype(BF16), lb_ref[...])
         + gates_ref[:, 2 * d:3 * d] * _dot(yc_ref[...].astype(BF16), lc_ref[...]))
    o_ref[...] = x_ref[...] + gt_ref[0] * _dot(m.astype(BF16), wo_ref[...])


def _merge_call(x2, l, ya, yb, yc, gates, gate1, prm, tm):
    rows, d = x2.shape
    gt_arr, gt_spec = _mod_operand(gate1, l, tm)
    row_spec = lambda w: pl.BlockSpec((tm, w), lambda i: (i, 0))
    return pl.pallas_call(
        _merge_kernel,
        out_shape=jax.ShapeDtypeStruct((rows, d), F32),
        grid=(rows // tm,),
        in_specs=[row_spec(d), row_spec(S5_WIDTH), row_spec(RW_WIDTH), row_spec(HG_WIDTH),
                  row_spec(N_BRANCH * d), gt_spec,
                  _const_spec((S5_WIDTH, d)), _const_spec((RW_WIDTH, d)), _const_spec((HG_WIDTH, d)),
                  _const_spec((d, d))],
        out_specs=row_spec(d),
        compiler_params=_cparams("arbitrary"),
        name="branch_merge",
    )(x2, ya, yb, yc, gates, gt_arr, prm["w_lift_a"], prm["w_lift_b"], prm["w_lift_c"], prm["w_out"])


def _ffn_kernel(*refs, l, tm, has_state, final_norm):
    if has_state:
        (x_ref, g_ref, sh_ref, sc_ref, gt_ref, wup_ref, cw_ref, cb_ref, wdn_ref, fg_ref,
         h0_ref, h1_ref, o_ref, up_ref, h_ref, buf_ref, hist_ref) = refs
    else:
        (x_ref, g_ref, sh_ref, sc_ref, gt_ref, wup_ref, cw_ref, cb_ref, wdn_ref, fg_ref,
         o_ref, tail_ref, h_ref, buf_ref, hist_ref) = refs
    i = pl.program_id(0)
    n_chunks = D_FF // FF_CHUNK

    if has_state:
        t_idx = lax.broadcasted_iota(jnp.int32, (tm, 1), 0) % l
        after1 = t_idx >= 1
        after2 = t_idx >= 2
        first_tile = i == 0
    else:
        first_tile = i % (l // tm) == 0

    @pl.when(first_tile)
    def _():
        hist_ref[...] = jnp.zeros_like(hist_ref)

    x = x_ref[...]
    h_ref[...] = _rms_modulate(x, g_ref[...], sh_ref[0], sc_ref[0]).astype(BF16)

    def conv_half(c, half):
        col0 = half * D_FF + c * FF_CHUNK
        cols = slice(col0, col0 + FF_CHUNK)
        up = _dot(h_ref[...], wup_ref[:, cols])
        buf_ref[half, 0:SUBLANES, :] = hist_ref[c, half]
        buf_ref[half, SUBLANES:SUBLANES + tm, :] = up
        hist_ref[c, half] = up[tm - SUBLANES:tm, :]
        prev1 = buf_ref[half, SUBLANES - 1:SUBLANES - 1 + tm, :]
        prev2 = buf_ref[half, SUBLANES - 2:SUBLANES - 2 + tm, :]
        if has_state:
            up_ref[:, cols] = up
            prev1 = jnp.where(after1, prev1, 0.0) + h1_ref[:, cols]
            prev2 = jnp.where(after2, prev2, 0.0) + h0_ref[:, cols]
        else:
            tail_ref[0, :, cols] = buf_ref[half, SUBLANES + tm - (CONV_W - 1):SUBLANES + tm, :]
        return (cb_ref[:, cols] + cw_ref[0:1, cols] * prev2 + cw_ref[1:2, cols] * prev1
                + cw_ref[2:3, cols] * up)

    acc = jnp.zeros((tm, D_MODEL), F32)
    for c in range(n_chunks):
        act = _gelu(conv_half(c, 0)) * conv_half(c, 1)
        acc = acc + _dot(act.astype(BF16), wdn_ref[c * FF_CHUNK:(c + 1) * FF_CHUNK, :])
    out = x + gt_ref[0] * acc
    if final_norm:
        ms = jnp.mean(out * out, axis=-1, keepdims=True)
        out = out * lax.rsqrt(ms + RMS_EPS) * fg_ref[...]
    o_ref[...] = out


def _ffn_call(x2, n, l, g, shift, scale, gate2, prm, final_g, conv_state, tm, final_norm):
    rows, d = x2.shape
    has_state = conv_state is not None
    sh_arr, sh_spec = _mod_operand(shift, l, tm)
    sc_arr, sc_spec = _mod_operand(scale, l, tm)
    gt_arr, gt_spec = _mod_operand(gate2, l, tm)
    row_spec = lambda w: pl.BlockSpec((tm, w), lambda i: (i, 0))
    in_specs = [row_spec(d), _const_spec((1, d)), sh_spec, sc_spec, gt_spec,
                _const_spec((d, 2 * D_FF)), _const_spec((CONV_W, 2 * D_FF)), _const_spec((1, 2 * D_FF)),
                _const_spec((D_FF, d)), _const_spec((1, d))]
    operands = [x2, g.reshape(1, d), sh_arr, sc_arr, gt_arr, prm["w_up"], prm["conv_w"], prm["conv_b"],
                prm["w_down"], final_g.reshape(1, d)]
    if has_state:
        assert tm % l == 0 and l > CONV_W - 1
        zeros = jnp.zeros((n, l - 1, 2 * D_FF), F32)
        hist1 = jnp.concatenate([conv_state[:, 1:2], zeros], axis=1).reshape(rows, 2 * D_FF)
        hist0 = jnp.concatenate([conv_state, zeros[:, 1:]], axis=1).reshape(rows, 2 * D_FF)
        in_specs += [row_spec(2 * D_FF), row_spec(2 * D_FF)]
        operands += [hist0, hist1]
        out_shape = [jax.ShapeDtypeStruct((rows, d), F32), jax.ShapeDtypeStruct((rows, 2 * D_FF), F32)]
        out_specs = [row_spec(d), row_spec(2 * D_FF)]
    else:
        assert l % tm == 0
        per = l // tm
        out_shape = [jax.ShapeDtypeStruct((rows, d), F32), jax.ShapeDtypeStruct((n, CONV_W - 1, 2 * D_FF), F32)]
        out_specs = [row_spec(d), pl.BlockSpec((1, CONV_W - 1, 2 * D_FF), lambda i: (i // per, 0, 0))]
    kern = functools.partial(_ffn_kernel, l=l, tm=tm, has_state=has_state, final_norm=final_norm)
    out, aux = pl.pallas_call(
        kern,
        out_shape=out_shape,
        grid=(rows // tm,),
        in_specs=in_specs,
        out_specs=out_specs,
        scratch_shapes=[pltpu.VMEM((tm, d), BF16),
                        pltpu.VMEM((2, tm + SUBLANES, FF_CHUNK), F32),
                        pltpu.VMEM((D_FF // FF_CHUNK, 2, SUBLANES, FF_CHUNK), F32)],
        compiler_params=_cparams("arbitrary"),
        name="conv_ffn",
    )(*operands)
    if has_state:
        aux = aux.reshape(n, l, 2 * D_FF)[:, l - (CONV_W - 1):]
    return out, aux


def _prepare_layer(p):
    out = {}
    w_in = p["w_in"]
    c1 = S5_WIDTH
    c2 = c1 + RW_COLS
    c3 = c2 + 4 * HG_WIDTH
    out["w_in_a"] = w_in[:, :c1].astype(BF16)
    out["w_in_b"] = w_in[:, c1:c2].astype(BF16)
    out["w_in_c"] = w_in[:, c2:c3].astype(BF16)
    out["w_in_g"] = w_in[:, c3:].astype(BF16)
    out["w_ada"] = p["w_ada"].astype(BF16)
    for name in ("w_lift_a", "w_lift_b", "w_lift_c", "w_out", "w_up", "w_down"):
        out[name] = p[name].astype(BF16)
    out["conv_w"] = p["conv_w"]
    out["conv_b"] = p["conv_b"].reshape(1, 2 * D_FF)

    lr = p["s5_lambda_re"]
    li = p["s5_lambda_im"]
    dt = jnp.exp(p["s5_log_dt"])[:, None]
    mag = jnp.exp(lr * dt)
    ar = mag * jnp.cos(li * dt)
    ai = mag * jnp.sin(li * dt)
    den = lr * lr + li * li
    zr = ((ar - 1.0) * lr + ai * li) / den
    zi = (ai * lr - (ar - 1.0) * li) / den
    bbr = zr[..., None] * p["s5_b_re"] - zi[..., None] * p["s5_b_im"]
    bbi = zr[..., None] * p["s5_b_im"] + zi[..., None] * p["s5_b_re"]
    eye = jnp.eye(S5_GROUPS, dtype=F32)
    bmat = jnp.einsum("gh,rgpc->gcrhp", eye, jnp.stack([bbr, bbi])).reshape(S5_WIDTH, 2 * S5_FLAT)
    cmat = jnp.einsum("hg,rgcp->rhpgc", eye, jnp.stack([p["s5_c_re"], -p["s5_c_im"]])).reshape(
        2 * S5_FLAT, S5_WIDTH)
    out["s5_b"] = bmat.astype(BF16)
    out["s5_c"] = cmat.astype(BF16)
    out["s5_a"] = jnp.stack([ar.reshape(S5_FLAT), ai.reshape(S5_FLAT)])
    out["s5_d"] = p["s5_d"].reshape(1, S5_WIDTH)
    out["s5_w_glu"] = p["s5_w_glu"].astype(BF16)
    out["s5_b_glu"] = p["s5_b_glu"].reshape(1, S5_WIDTH)

    out["rw_mu"] = p["rw_mu"].reshape(1, RW_COLS)
    out["rw_vec"] = jnp.stack([p["rw_w0"], p["rw_a0"], p["rw_k_k"], p["rw_k_a"],
                               p["rw_r_k"].reshape(RW_WIDTH), p["rw_ln_w"], p["rw_ln_b"],
                               jnp.zeros((RW_WIDTH,), F32)])
    zw = jnp.zeros((RW_LORA, RW_WIDTH), F32)
    out["rw_w2"] = zw.at[0:RW_DECAY_LORA].set(p["rw_w2"]).astype(BF16)
    out["rw_a2"] = zw.at[RW_DECAY_LORA:RW_DECAY_LORA + RW_AAA_LORA].set(p["rw_a2"]).astype(BF16)
    out["rw_g2"] = zw.at[RW_DECAY_LORA + RW_AAA_LORA:].set(p["rw_g2"]).astype(BF16)
    out["hg_norm"] = p["hg_norm"]
    out["g_mix"] = p["g_mix"]
    out["g_ffn"] = p["g_ffn"]
    out["b_ada"] = p["b_ada"]
    return out


def _rw_state_in(s):
    n = s.shape[0]
    s5 = s.reshape(n, RW_PAIRS, HEAD_PAIR, RW_HEAD, RW_HEAD)
    eye = jnp.eye(HEAD_PAIR, dtype=s.dtype)
    return jnp.einsum("ab,nhavk->hnavbk", eye, s5).reshape(RW_PAIRS, n, LANES, LANES)


def _rw_state_out(t):
    n = t.shape[1]
    t6 = t.reshape(RW_PAIRS, n, HEAD_PAIR, RW_HEAD, HEAD_PAIR, RW_HEAD)
    return jnp.einsum("hnavak->nhavk", t6).reshape(n, RW_HEADS, RW_HEAD, RW_HEAD)


def _hg_state_in(s):
    n = s.shape[0]
    s5 = s.reshape(n, HG_PAIRS, HEAD_PAIR, HG_HEAD, HG_HEAD)
    eye = jnp.eye(HEAD_PAIR, dtype=s.dtype)
    return jnp.einsum("ab,nhakv->hnavbk", eye, s5).reshape(HG_PAIRS, n, LANES, LANES)


def _hg_state_out(t):
    n = t.shape[1]
    t6 = t.reshape(HG_PAIRS, n, HEAD_PAIR, HG_HEAD, HEAD_PAIR, HG_HEAD)
    return jnp.einsum("hnavak->nhakv", t6).reshape(n, HG_HEADS, HG_HEAD, HG_HEAD)


def _pad_time(a, lp):
    n, l, w = a.shape
    if l == lp:
        return a
    return jnp.concatenate([a, jnp.zeros((n, lp - l, w), a.dtype)], axis=1)


def _run_trunk(x, mods, st_s5, st_shift, st_rw, st_hg, st_conv, lower, final_g, prms, t_chunk):
    n, l, d = x.shape
    rows = n * l
    tm = min(ROW_TILE, rows)
    lp = -(-l // SUBLANES) * SUBLANES
    tc = min(t_chunk, lp)
    depth = len(prms)
    h = x.reshape(rows, d)
    out_s5, out_shift, out_rw, out_hg, out_conv = [], [], [], [], []
    for layer in range(depth):
        prm = prms[layer]
        sh1, sc1, gt1, sh2, sc2, gt2 = jnp.split(mods[layer], 6, axis=-1)
        za, zb, zc, gates = _in_proj_call(h, l, prm["g_mix"], sh1, sc1, prm["w_in_a"], prm["w_in_b"],
                                          prm["w_in_c"], prm["w_in_g"], tm)

        s5_in = jnp.concatenate([st_s5[layer][..., 0].reshape(n, S5_FLAT),
                                 st_s5[layer][..., 1].reshape(n, S5_FLAT)], axis=1)
        ya, s5_new = _s5_call(_pad_time(za.reshape(n, l, S5_WIDTH), lp), s5_in, prm, tc, l)
        lp_rw = -(-l // RW_SUB) * RW_SUB
        yb, shift_new, rw_new = _rw_call(_pad_time(zb.reshape(n, l, RW_COLS), lp_rw),
                                         st_shift[layer].reshape(n, 1, RW_COLS),
                                         _rw_state_in(st_rw[layer]), prm, min(t_chunk, lp_rw), l)
        lp_hg = -(-l // HG_SUB) * HG_SUB
        yc, hg_new = _hg_call(_pad_time(zc.reshape(n, l, 4 * HG_WIDTH), lp_hg), _hg_state_in(st_hg[layer]),
                              lower[layer], prm["hg_norm"], min(t_chunk, lp_hg), l)
        ya = ya[:, :l].reshape(rows, S5_WIDTH)
        yb = yb[:, :l].reshape(rows, RW_WIDTH)
        yc = yc[:, :l].reshape(rows, HG_WIDTH)

        h = _merge_call(h, l, ya, yb, yc, gates, gt1, prm, tm)
        ffn_tm = tm if st_conv is None else min(tm, 128)
        h, conv_new = _ffn_call(h, n, l, prm["g_ffn"], sh2, sc2, gt2, prm, final_g,
                                None if st_conv is None else st_conv[layer], ffn_tm,
                                final_norm=(layer == depth - 1))
        out_s5.append(jnp.stack([s5_new[:, :S5_FLAT].reshape(n, S5_GROUPS, S5_STATE),
                                 s5_new[:, S5_FLAT:].reshape(n, S5_GROUPS, S5_STATE)], axis=-1))
        out_shift.append(shift_new.reshape(n, RW_COLS))
        out_rw.append(_rw_state_out(rw_new))
        out_hg.append(_hg_state_out(hg_new))
        out_conv.append(conv_new)
    y = h.reshape(n, l, d).astype(x.dtype)
    return y, (jnp.stack(out_s5), jnp.stack(out_shift), jnp.stack(out_rw), jnp.stack(out_hg),
               jnp.stack(out_conv))


def kernel(x_prompt, x_sample, c_prompt, c_sample, state_s5, state_rwkv_shift, state_rwkv, state_hgrn, state_ffn_conv, w_ada, b_ada, g_mix, g_ffn, w_in, s5_lambda_re, s5_lambda_im, s5_log_dt, s5_b_re, s5_b_im, s5_c_re, s5_c_im, s5_d, s5_w_glu, s5_b_glu, rw_mu, rw_w0, rw_w2, rw_a0, rw_a2, rw_g2, rw_k_k, rw_k_a, rw_r_k, rw_ln_w, rw_ln_b, hg_lb, hg_norm, w_lift_a, w_lift_b, w_lift_c, w_out, w_up, conv_w, conv_b, w_down, final_g):
    per_layer = {
        "w_ada": w_ada, "b_ada": b_ada, "g_mix": g_mix, "g_ffn": g_ffn, "w_in": w_in,
        "s5_lambda_re": s5_lambda_re, "s5_lambda_im": s5_lambda_im, "s5_log_dt": s5_log_dt,
        "s5_b_re": s5_b_re, "s5_b_im": s5_b_im, "s5_c_re": s5_c_re, "s5_c_im": s5_c_im,
        "s5_d": s5_d, "s5_w_glu": s5_w_glu, "s5_b_glu": s5_b_glu,
        "rw_mu": rw_mu, "rw_w0": rw_w0, "rw_w2": rw_w2, "rw_a0": rw_a0, "rw_a2": rw_a2,
        "rw_g2": rw_g2, "rw_k_k": rw_k_k, "rw_k_a": rw_k_a, "rw_r_k": rw_r_k,
        "rw_ln_w": rw_ln_w, "rw_ln_b": rw_ln_b, "hg_norm": hg_norm,
        "w_lift_a": w_lift_a, "w_lift_b": w_lift_b, "w_lift_c": w_lift_c, "w_out": w_out,
        "w_up": w_up, "conv_w": conv_w, "conv_b": conv_b, "w_down": w_down,
    }
    depth = w_ada.shape[0]
    prms = [_prepare_layer({k: v[layer] for k, v in per_layer.items()}) for layer in range(depth)]

    lbp = jax.nn.softmax(hg_lb.astype(F32), axis=0)
    lower = jnp.cumsum(lbp, axis=0) - lbp[0]

    nb = x_prompt.shape[0]
    ns = x_sample.shape[0]
    c_all = jnp.concatenate([c_prompt, c_sample], axis=0).astype(F32)
    mods = [_ada_call(c_all, prms[layer]["w_ada"], prms[layer]["b_ada"]) for layer in range(depth)]
    mods_p = [m[:nb] for m in mods]
    mods_s = [m[nb:] for m in mods]

    z_s5 = jnp.zeros((depth, nb) + state_s5.shape[2:], F32)
    z_shift = jnp.zeros((depth, nb) + state_rwkv_shift.shape[2:], F32)
    z_rw = jnp.zeros((depth, nb) + state_rwkv.shape[2:], F32)
    z_hg = jnp.zeros((depth, nb) + state_hgrn.shape[2:], F32)

    y_prompt, (s5_p, shift_p, rw_p, hg_p, conv_p) = _run_trunk(
        x_prompt, mods_p, z_s5, z_shift, z_rw, z_hg, None, lower, final_g, prms, t_chunk=64)
    y_sample, (s5_s, shift_s, rw_s, hg_s, conv_s) = _run_trunk(
        x_sample, mods_s, state_s5, state_rwkv_shift, state_rwkv, state_hgrn, state_ffn_conv,
        lower, final_g, prms, t_chunk=64)
    return (y_prompt, y_sample, s5_p, shift_p, rw_p, hg_p, conv_p, s5_s, shift_s, rw_s, hg_s, conv_s)
```

```python
import functools
import math

import jax
import jax.numpy as jnp
from jax import lax
from jax.experimental import pallas as pl
from jax.experimental.pallas import tpu as pltpu

F32 = jnp.float32
BF16 = jnp.bfloat16

D_MODEL = 1024
S5_WIDTH = D_MODEL // 4
S5_GROUP = 16
S5_GROUPS = S5_WIDTH // S5_GROUP
S5_STATE = 64
S5_FLAT = S5_GROUPS * S5_STATE
RW_WIDTH = D_MODEL // 2
RW_HEAD = 64
RW_HEADS = RW_WIDTH // RW_HEAD
RW_DECAY_LORA = 32
RW_AAA_LORA = 32
RW_GATE_LORA = 64
RW_LORA = RW_DECAY_LORA + RW_AAA_LORA + RW_GATE_LORA
RW_COLS = 3 * RW_WIDTH + RW_LORA
RW_GN_EPS = 1e-5 * RW_HEAD
HG_WIDTH = D_MODEL // 4
HG_HEAD = 64
HG_HEADS = HG_WIDTH // HG_HEAD
HG_GATE_FLOOR = 1e-30
N_BRANCH = 3
D_FF = 256 * ((8 * D_MODEL // 3 + 255) // 256)
CONV_W = 3
RMS_EPS = 1e-6

LANES = 128
SUBLANES = 8
MXU_COLS = 256
VMEM_LIMIT_BYTES = 56 * 1024 * 1024

SEQ_BLOCK = SUBLANES
HEAD_PAIR = LANES // RW_HEAD
ROW_TILE = 512
FF_CHUNK = 256
IN_PROJ_CHUNK = 512
RW_SUB = 16
RW_PAIRS = RW_HEADS // HEAD_PAIR
HG_SUB = 16
HG_PAIRS = HG_HEADS // HEAD_PAIR
RW_SOLVE_PASSES = 1
RW_APPLY_PASSES = 1
RW_STATE_PASSES = 1

_NN = (((1,), (0,)), ((), ()))
_NT = (((1,), (1,)), ((), ()))


def _cparams(*sem):
    return pltpu.CompilerParams(dimension_semantics=sem, vmem_limit_bytes=VMEM_LIMIT_BYTES)


def _const_spec(shape):
    nd = len(shape)
    return pl.BlockSpec(shape, lambda *_: (0,) * nd, pipeline_mode=pl.Buffered(1))


def _dot(a, b):
    return jnp.dot(a, b, preferred_element_type=F32)


def _split_bf16(x):
    hi = x.astype(BF16)
    lo = (x - hi.astype(F32)).astype(BF16)
    return hi, lo


def _head_ones():
    r = lax.broadcasted_iota(jnp.int32, (LANES, LANES), 0) // RW_HEAD
    c = lax.broadcasted_iota(jnp.int32, (LANES, LANES), 1) // RW_HEAD
    return (r == c).astype(BF16)


def _head_sum(x, ones):
    hi, lo = _split_bf16(x)
    tiles = [_dot(hi[:, c:c + LANES], ones) + _dot(lo[:, c:c + LANES], ones)
             for c in range(0, x.shape[1], LANES)]
    return tiles[0] if len(tiles) == 1 else jnp.concatenate(tiles, axis=1)


def _softplus(x):
    return jnp.maximum(x, 0.0) + jnp.log1p(jnp.exp(-jnp.abs(x)))


def _gelu(x):
    c = 2.0 * math.sqrt(2.0 / math.pi)
    return x * jax.nn.sigmoid(x * (c + (c * 0.044715) * (x * x)))


def _rms_modulate(x, g, shift, scale):
    ms = jnp.mean(x * x, axis=-1, keepdims=True)
    return x * lax.rsqrt(ms + RMS_EPS) * g * (1.0 + scale) + shift


def _seq_specs(n, l, t_chunk, widths_in, widths_out):
    nb = n // SEQ_BLOCK
    if l % t_chunk == 0:
        spec = lambda w: pl.BlockSpec((SEQ_BLOCK, t_chunk, w), lambda b, c: (b, c, 0))
        shape = lambda w: (n, l, w)
        nt = l // t_chunk
    else:
        assert l < t_chunk
        spec = lambda w: pl.BlockSpec((SEQ_BLOCK * l, w), lambda b, c: (b, 0))
        shape = lambda w: (n * l, w)
        nt = 1
    return ((nb, nt), [spec(w) for w in widths_in], [spec(w) for w in widths_out],
            [shape(w) for w in widths_in], [shape(w) for w in widths_out])


def _seq_block_load(z_ref, pad_ref):
    if len(z_ref.shape) == 3:
        return z_ref[...]
    l = z_ref.shape[0] // SEQ_BLOCK
    pad_ref[...] = jnp.zeros_like(pad_ref)
    for n in range(SEQ_BLOCK):
        pad_ref[n, 0:l, :] = z_ref[n * l:(n + 1) * l, :]
    return pad_ref[...]


def _seq_block_store(y_ref, stage_ref, y3):
    if len(y_ref.shape) == 3:
        y_ref[...] = y3
        return
    l = y_ref.shape[0] // SEQ_BLOCK
    stage_ref[...] = y3
    for n in range(SEQ_BLOCK):
        y_ref[n * l:(n + 1) * l, :] = stage_ref[n, 0:l, :]


def _heads_to_tiles(s_ref, tile_ref, n_pairs, transpose):
    zero = jnp.zeros((RW_HEAD, RW_HEAD), F32)
    for n in range(SEQ_BLOCK):
        for hp in range(n_pairs):
            a, b = s_ref[n, HEAD_PAIR * hp], s_ref[n, HEAD_PAIR * hp + 1]
            if transpose:
                a, b = a.T, b.T
            tile_ref[hp, n] = jnp.concatenate([jnp.concatenate([a, zero], axis=1),
                                               jnp.concatenate([zero, b], axis=1)], axis=0)


def _tiles_to_heads(tile_ref, s_ref, n_pairs, transpose):
    for n in range(SEQ_BLOCK):
        for hp in range(n_pairs):
            t = tile_ref[hp, n]
            a, b = t[0:RW_HEAD, 0:RW_HEAD], t[RW_HEAD:LANES, RW_HEAD:LANES]
            if transpose:
                a, b = a.T, b.T
            s_ref[n, HEAD_PAIR * hp] = a
            s_ref[n, HEAD_PAIR * hp + 1] = b


def _ada_kernel(c_ref, w_ref, b_ref, o_ref):
    c = c_ref[...]
    o_ref[...] = _dot(jax.nn.silu(c).astype(BF16), w_ref[...]) + b_ref[...]


def _ada_call(c, w_bf16, b):
    n, d = c.shape
    cols = w_bf16.shape[1]
    tn = cols // 4
    return pl.pallas_call(
        _ada_kernel,
        out_shape=jax.ShapeDtypeStruct((n, cols), F32),
        grid=(cols // tn,),
        in_specs=[_const_spec((n, d)),
                  pl.BlockSpec((d, tn), lambda j: (0, j)),
                  pl.BlockSpec((1, tn), lambda j: (0, j))],
        out_specs=pl.BlockSpec((n, tn), lambda j: (0, j)),
        compiler_params=_cparams("arbitrary"),
        name="ada_mod",
    )(c, w_bf16, b.reshape(1, cols))


def _mod_operand(m, l, tm):
    n, d = m.shape
    if l % tm == 0:
        per = l // tm
        return m.reshape(n, 1, d), pl.BlockSpec((1, 1, d), lambda i: (i // per, 0, 0))
    assert tm % l == 0
    return jnp.repeat(m, l, axis=0).reshape(1, n * l, d), pl.BlockSpec((1, tm, d), lambda i: (0, i, 0))


def _in_proj_kernel(x_ref, g_ref, sh_ref, sc_ref, wa_ref, wb_ref, wc_ref, wg_ref,
                    oa_ref, ob_ref, oc_ref, og_ref):
    h = _rms_modulate(x_ref[...], g_ref[...], sh_ref[0], sc_ref[0]).astype(BF16)

    def project(w_ref, o_ref, fn=None):
        width = w_ref.shape[1]
        for c0 in range(0, width, IN_PROJ_CHUNK):
            cols = slice(c0, min(c0 + IN_PROJ_CHUNK, width))
            z = _dot(h, w_ref[:, cols])
            o_ref[:, cols] = z if fn is None else fn(z)

    project(wa_ref, oa_ref)
    project(wb_ref, ob_ref)
    project(wc_ref, oc_ref)
    project(wg_ref, og_ref, jax.nn.sigmoid)


def _in_proj_call(x2, l, g, shift, scale, w_a, w_b, w_c, w_g, tm):
    rows, d = x2.shape
    sh_arr, sh_spec = _mod_operand(shift, l, tm)
    sc_arr, sc_spec = _mod_operand(scale, l, tm)
    widths = [w.shape[1] for w in (w_a, w_b, w_c, w_g)]
    return pl.pallas_call(
        _in_proj_kernel,
        out_shape=[jax.ShapeDtypeStruct((rows, w), F32) for w in widths],
        grid=(rows // tm,),
        in_specs=[pl.BlockSpec((tm, d), lambda i: (i, 0)), _const_spec((1, d)), sh_spec, sc_spec]
        + [_const_spec(w.shape) for w in (w_a, w_b, w_c, w_g)],
        out_specs=[pl.BlockSpec((tm, w), lambda i: (i, 0)) for w in widths],
        compiler_params=_cparams("arbitrary"),
        name="in_proj",
    )(x2, g.reshape(1, d), sh_arr, sc_arr, w_a, w_b, w_c, w_g)


def _s5_kernel(u_ref, s0_ref, b_ref, c_ref, a_ref, d_ref, wg_ref, bg_ref,
               y_ref, sl_ref, e_ref, st_ref, pad_ref, stage_ref, *, t_chunk, n_steps):
    @pl.when(pl.program_id(1) == 0)
    def _():
        st_ref[...] = s0_ref[...]

    rows = SEQ_BLOCK * t_chunk
    ut = jnp.swapaxes(_seq_block_load(u_ref, pad_ref), 0, 1).reshape(rows, S5_WIDTH)
    ub = ut.astype(BF16)
    for c0 in range(0, 2 * S5_FLAT, MXU_COLS):
        ch0 = (c0 % S5_FLAT) // S5_STATE * S5_GROUP
        k0 = ch0 // LANES * LANES
        assert ch0 + MXU_COLS // S5_STATE * S5_GROUP <= k0 + LANES
        e_ref[:, c0:c0 + MXU_COLS] = _dot(ub[:, k0:k0 + LANES], b_ref[k0:k0 + LANES, c0:c0 + MXU_COLS])

    ar = jnp.broadcast_to(a_ref[0:1, :], (SEQ_BLOCK, S5_FLAT))
    ai = jnp.broadcast_to(a_ref[1:2, :], (SEQ_BLOCK, S5_FLAT))

    def step(t, carry):
        sr, si = carry
        r = pl.ds(pl.multiple_of(t * SEQ_BLOCK, SEQ_BLOCK), SEQ_BLOCK)
        nr = ar * sr - ai * si + e_ref[r, 0:S5_FLAT]
        ni = ar * si + ai * sr + e_ref[r, S5_FLAT:2 * S5_FLAT]
        e_ref[r, 0:S5_FLAT] = nr
        e_ref[r, S5_FLAT:2 * S5_FLAT] = ni
        return nr, ni

    sr, si = lax.fori_loop(0, n_steps, step, (st_ref[:, 0:S5_FLAT], st_ref[:, S5_FLAT:2 * S5_FLAT]))
    st_ref[:, 0:S5_FLAT] = sr
    st_ref[:, S5_FLAT:2 * S5_FLAT] = si
    sl_ref[...] = st_ref[...]

    y = _dot(e_ref[...].astype(BF16), c_ref[...]) + d_ref[...] * ut
    y = _gelu(y)
    y = y * jax.nn.sigmoid(_dot(y.astype(BF16), wg_ref[...]) + bg_ref[...])
    _seq_block_store(y_ref, stage_ref, jnp.swapaxes(y.reshape(t_chunk, SEQ_BLOCK, S5_WIDTH), 0, 1))


def _io_scratch(l, t_chunk, width_in, width_out):
    if l % t_chunk == 0:
        return [pltpu.VMEM((SUBLANES, LANES), F32)] * 2
    return [pltpu.VMEM((SEQ_BLOCK, t_chunk, width_in), F32), pltpu.VMEM((SEQ_BLOCK, t_chunk, width_out), F32)]


def _s5_call(u2, n, l, s0, prm, t_chunk):
    grid, in_specs, out_specs, in_shapes, out_shapes = _seq_specs(n, l, t_chunk, [S5_WIDTH], [S5_WIDTH])
    kern = functools.partial(_s5_kernel, t_chunk=t_chunk, n_steps=min(t_chunk, l))
    y, s_new = pl.pallas_call(
        kern,
        out_shape=[jax.ShapeDtypeStruct(out_shapes[0], F32),
                   jax.ShapeDtypeStruct((n, 2 * S5_FLAT), F32)],
        grid=grid,
        in_specs=in_specs + [
            pl.BlockSpec((SEQ_BLOCK, 2 * S5_FLAT), lambda b, c: (b, 0)),
            _const_spec((S5_WIDTH, 2 * S5_FLAT)), _const_spec((2 * S5_FLAT, S5_WIDTH)),
            _const_spec((2, S5_FLAT)), _const_spec((1, S5_WIDTH)),
            _const_spec((S5_WIDTH, S5_WIDTH)), _const_spec((1, S5_WIDTH))],
        out_specs=out_specs + [pl.BlockSpec((SEQ_BLOCK, 2 * S5_FLAT), lambda b, c: (b, 0))],
        scratch_shapes=[pltpu.VMEM((SEQ_BLOCK * t_chunk, 2 * S5_FLAT), F32),
                        pltpu.VMEM((SEQ_BLOCK, 2 * S5_FLAT), F32)]
        + _io_scratch(l, t_chunk, S5_WIDTH, S5_WIDTH),
        compiler_params=_cparams("arbitrary", "arbitrary"),
        name="s5_mixer",
    )(u2.reshape(in_shapes[0]), s0, prm["s5_b"], prm["s5_c"], prm["s5_a"], prm["s5_d"], prm["s5_w_glu"],
      prm["s5_b_glu"])
    return y.reshape(n * l, S5_WIDTH), s_new


def _mm(a, b, dims, passes):
    dg = lambda x, y: lax.dot_general(x, y, dims, preferred_element_type=F32)
    if passes == 1:
        return dg(a.astype(BF16), b.astype(BF16))
    a_hi, a_lo = _split_bf16(a)
    if passes == 2:
        b_hi = b.astype(BF16)
        return dg(a_hi, b_hi) + dg(a_lo, b_hi)
    b_hi, b_lo = _split_bf16(b)
    return dg(a_hi, b_hi) + dg(a_lo, b_hi) + dg(a_hi, b_lo)


def _cumsum_groups(x, group):
    pos = lax.broadcasted_iota(jnp.int32, (x.shape[0], 1), 0) % group
    s = 1
    while s < group:
        x = x + jnp.where(pos >= s, pltpu.roll(x, s, 0), 0.0)
        s *= 2
    return x


def _rw_kernel(p_ref, sh0_ref, s0_ref, mu_ref, vec_ref, w2_ref, a2_ref, g2_ref,
               y_ref, shl_ref, sl_ref,
               pbuf_ref, s_ref, at_ref, bt_ref, kt_ref, rt_ref, bh_ref, kh_ref, v_ref, gc_ref, ys_ref,
               pad_ref, stage_ref, *, t_chunk, l_valid, last_row):
    rows = SEQ_BLOCK * t_chunk
    n_sub = t_chunk // RW_SUB

    @pl.when(pl.program_id(1) == 0)
    def _():
        pbuf_ref[:, SUBLANES - 1:SUBLANES, :] = sh0_ref[...]
        _heads_to_tiles(s0_ref, s_ref, RW_PAIRS, transpose=False)

    p3 = _seq_block_load(p_ref, pad_ref)
    pbuf_ref[:, SUBLANES:SUBLANES + t_chunk, :] = p3
    prev3 = pbuf_ref[:, SUBLANES - 1:SUBLANES - 1 + t_chunk, :]
    last = pbuf_ref[:, SUBLANES + last_row:SUBLANES + last_row + 1, :]
    pbuf_ref[:, SUBLANES - 1:SUBLANES, :] = last
    shl_ref[...] = last

    xm = (p3 + (prev3 - p3) * mu_ref[...]).reshape(rows, RW_COLS)
    r = xm[:, 0:RW_WIDTH]
    k = xm[:, RW_WIDTH:2 * RW_WIDTH]
    v = xm[:, 2 * RW_WIDTH:3 * RW_WIDTH]
    lora = xm[:, 3 * RW_WIDTH:RW_COLS]
    w0, a0, k_k, k_a = vec_ref[0:1, :], vec_ref[1:2, :], vec_ref[2:3, :], vec_ref[3:4, :]
    r_k, ln_w, ln_b = vec_ref[4:5, :], vec_ref[5:6, :], vec_ref[6:7, :]

    w = -_softplus(-(w0 + _dot(jnp.tanh(lora).astype(BF16), w2_ref[...]))) - 0.5
    log_decay = -jnp.exp(w)
    a = jax.nn.sigmoid(a0 + _dot(lora.astype(BF16), a2_ref[...]))
    g = _dot(jax.nn.sigmoid(lora).astype(BF16), g2_ref[...])

    ones = _head_ones()
    kk = k * k_k
    kk = kk / jnp.maximum(jnp.sqrt(_head_sum(kk * kk, ones)), 1e-12)
    kt = k * (1.0 + (a - 1.0) * k_a)
    kka = kk * a
    if l_valid < t_chunk:
        live = lax.broadcasted_iota(jnp.int32, (rows, 1), 0) % t_chunk < l_valid
        log_decay = jnp.where(live, log_decay, 0.0)
        kk = jnp.where(live, kk, 0.0)
        kka = jnp.where(live, kka, 0.0)
        kt_live = jnp.where(live, kt, 0.0)
    else:
        kt_live = kt

    cum = _cumsum_groups(log_decay, RW_SUB)
    shape_g = (rows // RW_SUB, RW_SUB, RW_WIDTH)
    cum_end = jnp.broadcast_to(cum.reshape(shape_g)[:, RW_SUB - 1:RW_SUB, :], shape_g).reshape(rows, RW_WIDTH)
    g_in = jnp.exp(cum)
    g_inv = jnp.exp(-cum)
    g_ex = jnp.exp(cum - log_decay)
    g_out = jnp.exp(cum_end - cum)
    shape3 = (SEQ_BLOCK, t_chunk, RW_WIDTH)
    at_ref[...] = (-kk * g_ex).reshape(shape3)
    bt_ref[...] = (kka * g_inv).reshape(shape3)
    kt_ref[...] = (kt_live * g_inv).reshape(shape3)
    rt_ref[...] = (r * g_in).reshape(shape3)
    bh_ref[...] = (kka * g_out).reshape(shape3)
    kh_ref[...] = (kt_live * g_out).reshape(shape3)
    v_ref[...] = v.reshape(shape3)
    gc_ref[...] = jnp.exp(cum_end).reshape(shape3)

    m_rows = SEQ_BLOCK * RW_SUB
    ri = lax.broadcasted_iota(jnp.int32, (m_rows, m_rows), 0)
    ci = lax.broadcasted_iota(jnp.int32, (m_rows, m_rows), 1)
    same_seq = (ri // RW_SUB) == (ci // RW_SUB)
    before = same_seq & ((ci % RW_SUB) < (ri % RW_SUB))
    upto = same_seq & ((ci % RW_SUB) <= (ri % RW_SUB))
    eye = (ri == ci).astype(F32)
    pair_mask = (ri // 2) == (ci // 2)
    level_masks = []
    s = 2
    while s < RW_SUB:
        level_masks.append(((ri // (2 * s)) == (ci // (2 * s))) & ((ri // s) != (ci // s)))
        s *= 2
    lane = lax.broadcasted_iota(jnp.int32, (m_rows, LANES), 1)
    head_mask = [lane < RW_HEAD, lane >= RW_HEAD]
    wide = (m_rows, SEQ_BLOCK * LANES)
    own_rows = (lax.broadcasted_iota(jnp.int32, wide, 1) // LANES
                == lax.broadcasted_iota(jnp.int32, wide, 0) // RW_SUB)
    same_head = (lax.broadcasted_iota(jnp.int32, wide, 0) // RW_HEAD
                 == (lax.broadcasted_iota(jnp.int32, wide, 1) % LANES) // RW_HEAD)
    heads = [(hp, h) for hp in range(RW_PAIRS) for h in range(HEAD_PAIR)]

    def sub_chunk(c, carry):
        rs = pl.ds(pl.multiple_of(c * RW_SUB, RW_SUB), RW_SUB)

        def ld(ref, hp):
            return ref[:, rs, pl.ds(hp * LANES, LANES)].reshape(m_rows, LANES)

        at, bt, ktl, rt, bh, kh, vv, gcv = ([ld(ref, hp) for hp in range(RW_PAIRS)]
                                            for ref in (at_ref, bt_ref, kt_ref, rt_ref, bh_ref, kh_ref,
                                                        v_ref, gc_ref))
        state = [[s_ref[hp, n] for n in range(SEQ_BLOCK)] for hp in range(RW_PAIRS)]
        zero = jnp.zeros((m_rows, LANES), F32)
        msk = lambda x, h: jnp.where(head_mask[h], x, zero)

        gram = [_mm(jnp.concatenate([msk(at[hp], 0), msk(rt[hp], 0), msk(at[hp], 1), msk(rt[hp], 1)], axis=0),
                    jnp.concatenate([bt[hp], ktl[hp]], axis=0), _NT, RW_SOLVE_PASSES)
                for hp in range(RW_PAIRS)]

        def quad(hp, h, row, col, keep):
            blk = gram[hp][(2 * h + row) * m_rows:(2 * h + row + 1) * m_rows, col * m_rows:(col + 1) * m_rows]
            return jnp.where(keep, blk, 0.0)

        m_ab = [quad(hp, h, 0, 0, before) for hp, h in heads]
        m_ak = [quad(hp, h, 0, 1, before) for hp, h in heads]
        n_rb = [quad(hp, h, 1, 0, upto) for hp, h in heads]
        n_rk = [quad(hp, h, 1, 1, upto) for hp, h in heads]
        tinv = [eye + jnp.where(pair_mask, m, 0.0) for m in m_ab]
        for lm in level_masks:
            prod = [_mm(jnp.where(lm, m, 0.0), t, _NN, RW_SOLVE_PASSES) for m, t in zip(m_ab, tinv)]
            tinv = [t + _mm(t, q, _NN, RW_SOLVE_PASSES) for t, q in zip(tinv, prod)]
        vh = [msk(vv[hp], h) for hp, h in heads]
        w1 = [_mm(m, x, _NN, RW_APPLY_PASSES) for m, x in zip(m_ak, vh)]
        ap = [_mm(t, jnp.concatenate([msk(at[hp], h), w], axis=1), _NN, RW_APPLY_PASSES)
              for t, (hp, h), w in zip(tinv, heads, w1)]
        nb = [_mm(m, x, _NN, RW_APPLY_PASSES) for m, x in zip(n_rb, ap)]
        nv = [_mm(m, x, _NN, RW_APPLY_PASSES) for m, x in zip(n_rk, vh)]

        def per_seq(x):
            return jnp.where(own_rows, jnp.concatenate([x] * SEQ_BLOCK, axis=1), 0.0)

        for hp in range(RW_PAIRS):
            i0, i1 = hp * HEAD_PAIR, hp * HEAD_PAIR + 1
            a_hat = ap[i0][:, 0:LANES] + ap[i1][:, 0:LANES]
            p_zero = ap[i0][:, LANES:2 * LANES] + ap[i1][:, LANES:2 * LANES]
            r_hat = rt[hp] + nb[i0][:, 0:LANES] + nb[i1][:, 0:LANES]
            y_zero = nb[i0][:, LANES:2 * LANES] + nb[i1][:, LANES:2 * LANES] + nv[i0] + nv[i1]
            bh_x = per_seq(bh[hp])
            kh_x = per_seq(kh[hp])
            gam = jnp.where(same_head, _mm(a_hat.T, bh_x, _NN, RW_STATE_PASSES), 0.0)
            u = jnp.where(same_head, _mm(jnp.concatenate([p_zero.T, vv[hp].T], axis=1),
                                         jnp.concatenate([bh_x, kh_x], axis=0), _NN, RW_STATE_PASSES), 0.0)
            cols = pl.ds(hp * LANES, LANES)
            for n in range(SEQ_BLOCK):
                q = slice(n * RW_SUB, (n + 1) * RW_SUB)
                blk = slice(n * LANES, (n + 1) * LANES)
                st = state[hp][n]
                ys_ref[n, rs, cols] = _mm(r_hat[q], st, _NT, RW_STATE_PASSES) + y_zero[q]
                s_ref[hp, n] = (st * gcv[hp][n * RW_SUB:n * RW_SUB + 1, :]
                                + _mm(st, gam[:, blk], _NN, RW_STATE_PASSES) + u[:, blk])
        return carry

    lax.fori_loop(0, n_sub, sub_chunk, 0)

    @pl.when(pl.program_id(1) == pl.num_programs(1) - 1)
    def _():
        _tiles_to_heads(s_ref, sl_ref, RW_PAIRS, transpose=False)

    y = ys_ref[...].reshape(rows, RW_WIDTH)
    inv = 1.0 / RW_HEAD
    mean = _head_sum(y, ones) * inv
    yc = y - mean
    var = _head_sum(yc * yc, ones) * inv
    y = yc * lax.rsqrt(var + RW_GN_EPS) * ln_w + ln_b
    bonus = _head_sum(r * kt * r_k, ones) * v
    _seq_block_store(y_ref, stage_ref, ((y + bonus) * g).reshape(shape3))


def _rw_call(p2, n, l, shift0, s0, prm, t_chunk):
    assert t_chunk % RW_SUB == 0
    grid, in_specs, out_specs, in_shapes, out_shapes = _seq_specs(n, l, t_chunk, [RW_COLS], [RW_WIDTH])
    valid_in_chunk = min(t_chunk, l)
    kern = functools.partial(_rw_kernel, t_chunk=t_chunk, l_valid=valid_in_chunk, last_row=valid_in_chunk - 1)
    st_spec = pl.BlockSpec((SEQ_BLOCK, RW_HEADS, RW_HEAD, RW_HEAD), lambda b, c: (b, 0, 0, 0))
    sh_spec = pl.BlockSpec((SEQ_BLOCK, 1, RW_COLS), lambda b, c: (b, 0, 0))
    q_scr = pltpu.VMEM((SEQ_BLOCK, t_chunk, RW_WIDTH), F32)
    y, shift_new, s_new = pl.pallas_call(
        kern,
        out_shape=[jax.ShapeDtypeStruct(out_shapes[0], F32),
                   jax.ShapeDtypeStruct((n, 1, RW_COLS), F32),
                   jax.ShapeDtypeStruct((n, RW_HEADS, RW_HEAD, RW_HEAD), F32)],
        grid=grid,
        in_specs=in_specs + [sh_spec, st_spec,
                             _const_spec((1, RW_COLS)), _const_spec((SUBLANES, RW_WIDTH)),
                             _const_spec((RW_LORA, RW_WIDTH)), _const_spec((RW_LORA, RW_WIDTH)),
                             _const_spec((RW_LORA, RW_WIDTH))],
        out_specs=out_specs + [sh_spec, st_spec],
        scratch_shapes=[pltpu.VMEM((SEQ_BLOCK, t_chunk + SUBLANES, RW_COLS), F32),
                        pltpu.VMEM((RW_PAIRS, SEQ_BLOCK, LANES, LANES), F32)] + [q_scr] * 9
        + _io_scratch(l, t_chunk, RW_COLS, RW_WIDTH),
        compiler_params=_cparams("arbitrary", "arbitrary"),
        name="rwkv7_mixer",
    )(p2.reshape(in_shapes[0]), shift0, s0, prm["rw_mu"], prm["rw_vec"], prm["rw_w2"], prm["rw_a2"], prm["rw_g2"])
    return y.reshape(n * l, RW_WIDTH), shift_new, s_new


def _hg_kernel(z_ref, s0_ref, lower_ref, ng_ref, y_ref, sl_ref,
               s_ref, qs_ref, q_ref, k_ref, kh_ref, v_ref, b_ref, gc_ref, ys_ref, pad_ref, stage_ref,
               *, t_chunk, l_valid):
    rows = SEQ_BLOCK * t_chunk
    n_sub = t_chunk // HG_SUB

    @pl.when(pl.program_id(1) == 0)
    def _():
        _heads_to_tiles(s0_ref, s_ref, HG_PAIRS, transpose=True)

    z = _seq_block_load(z_ref, pad_ref).reshape(rows, 4 * HG_WIDTH)
    q = jax.nn.silu(z[:, 0:HG_WIDTH])
    f = z[:, HG_WIDTH:2 * HG_WIDTH]
    i = z[:, 2 * HG_WIDTH:3 * HG_WIDTH]
    og = z[:, 3 * HG_WIDTH:4 * HG_WIDTH]
    lower = lower_ref[...]
    fgate = lower + (1.0 - lower) * jax.nn.sigmoid(f)
    log_f = jnp.log(jnp.maximum(fgate, HG_GATE_FLOOR))
    k = 1.0 - fgate
    if l_valid < t_chunk:
        live = lax.broadcasted_iota(jnp.int32, (rows, 1), 0) % t_chunk < l_valid
        log_f = jnp.where(live, log_f, 0.0)
        k = jnp.where(live, k, 0.0)
    cum = _cumsum_groups(log_f, HG_SUB)
    shape_g = (rows // HG_SUB, HG_SUB, HG_WIDTH)
    cum_end = jnp.broadcast_to(cum.reshape(shape_g)[:, HG_SUB - 1:HG_SUB, :], shape_g).reshape(rows, HG_WIDTH)
    shape3 = (SEQ_BLOCK, t_chunk, HG_WIDTH)
    qs_ref[...] = (q * jnp.exp(cum)).reshape(shape3)
    q_ref[...] = q.reshape(shape3)
    k_ref[...] = k.reshape(shape3)
    kh_ref[...] = (k * jnp.exp(cum_end - cum)).reshape(shape3)
    v_ref[...] = i.reshape(shape3)
    b_ref[...] = cum.reshape(shape3)
    gc_ref[...] = jnp.exp(cum_end).reshape(shape3)

    m_rows = SEQ_BLOCK * HG_SUB
    ones = _head_ones()
    wide = (m_rows, SEQ_BLOCK * LANES)
    own_rows = (lax.broadcasted_iota(jnp.int32, wide, 1) // LANES
                == lax.broadcasted_iota(jnp.int32, wide, 0) // HG_SUB)
    same_head = (lax.broadcasted_iota(jnp.int32, wide, 0) // HG_HEAD
                 == (lax.broadcasted_iota(jnp.int32, wide, 1) % LANES) // HG_HEAD)
    step = lax.broadcasted_iota(jnp.int32, (HG_SUB, LANES), 0)

    def sub_chunk(c, carry):
        rs = pl.ds(pl.multiple_of(c * HG_SUB, HG_SUB), HG_SUB)
        for hp in range(HG_PAIRS):
            cols = pl.ds(hp * LANES, LANES)
            ld = lambda ref: ref[:, rs, cols].reshape(m_rows, LANES)
            qs, qq, kk, kh, vv, bb, gcv = (ld(qs_ref), ld(q_ref), ld(k_ref), ld(kh_ref), ld(v_ref),
                                           ld(b_ref), ld(gc_ref))
            state = [s_ref[hp, n] for n in range(SEQ_BLOCK)]
            kh_x = jnp.where(own_rows, jnp.concatenate([kh] * SEQ_BLOCK, axis=1), 0.0)
            u = jnp.where(same_head, _mm(vv.T, kh_x, _NN, 1), 0.0)
            for n in range(SEQ_BLOCK):
                sl = slice(n * HG_SUB, (n + 1) * HG_SUB)
                bn, qn, kn, vn = bb[sl], qq[sl], kk[sl], vv[sl]
                prods = []
                for t in range(HG_SUB):
                    keep = step <= t
                    diff = jnp.where(keep, bn[t:t + 1, :] - bn, 0.0)
                    prods.append(jnp.where(keep, qn[t:t + 1, :] * kn * jnp.exp(diff), 0.0))
                att = _dot(jnp.concatenate(prods, axis=0).astype(BF16), ones)
                o_rows = [jnp.sum(att[t * HG_SUB:(t + 1) * HG_SUB] * vn, axis=0, keepdims=True)
                          for t in range(HG_SUB)]
                ys_ref[n, rs, cols] = _mm(qs[sl], state[n], _NT, 1) + jnp.concatenate(o_rows, axis=0)
                blk = slice(n * LANES, (n + 1) * LANES)
                s_ref[hp, n] = state[n] * gcv[n * HG_SUB:n * HG_SUB + 1, :] + u[:, blk]
        return carry

    lax.fori_loop(0, n_sub, sub_chunk, 0)

    @pl.when(pl.program_id(1) == pl.num_programs(1) - 1)
    def _():
        _tiles_to_heads(s_ref, sl_ref, HG_PAIRS, transpose=True)

    o = ys_ref[...].reshape(rows, HG_WIDTH)
    ms = _head_sum(o * o, ones) * (1.0 / HG_HEAD)
    o = o * lax.rsqrt(ms + RMS_EPS) * ng_ref[...] * jax.nn.sigmoid(og)
    _seq_block_store(y_ref, stage_ref, o.reshape(shape3))


def _hg_call(z2, n, l, s0, lower, norm_g, t_chunk):
    assert t_chunk % HG_SUB == 0
    grid, in_specs, out_specs, in_shapes, out_shapes = _seq_specs(n, l, t_chunk, [4 * HG_WIDTH], [HG_WIDTH])
    kern = functools.partial(_hg_kernel, t_chunk=t_chunk, l_valid=min(t_chunk, l))
    st_spec = pl.BlockSpec((SEQ_BLOCK, HG_HEADS, HG_HEAD, HG_HEAD), lambda b, c: (b, 0, 0, 0))
    q_scr = pltpu.VMEM((SEQ_BLOCK, t_chunk, HG_WIDTH), F32)
    y, s_new = pl.pallas_call(
        kern,
        out_shape=[jax.ShapeDtypeStruct(out_shapes[0], F32),
                   jax.ShapeDtypeStruct((n, HG_HEADS, HG_HEAD, HG_HEAD), F32)],
        grid=grid,
        in_specs=in_specs + [st_spec, _const_spec((1, HG_WIDTH)), _const_spec((1, HG_WIDTH))],
        out_specs=out_specs + [st_spec],
        scratch_shapes=[pltpu.VMEM((HG_PAIRS, SEQ_BLOCK, LANES, LANES), F32)] + [q_scr] * 8
        + _io_scratch(l, t_chunk, 4 * HG_WIDTH, HG_WIDTH),
        compiler_params=_cparams("arbitrary", "arbitrary"),
        name="hgrn2_mixer",
    )(z2.reshape(in_shapes[0]), s0, lower.reshape(1, HG_WIDTH), norm_g.reshape(1, HG_WIDTH))
    return y.reshape(n * l, HG_WIDTH), s_new


def _merge_kernel(x_ref, ya_ref, yb_ref, yc_ref, gates_ref, gt_ref, la_ref, lb_ref, lc_ref, wo_ref, o_ref):
    d = D_MODEL
    m = (gates_ref[:, 0:d] * _dot(ya_ref[...].astype(BF16), la_ref[...])
         + gates_ref[:, d:2 * d] * _dot(yb_ref[...].astype(BF16), lb_ref[...])
         + gates_ref[:, 2 * d:3 * d] * _dot(yc_ref[...].astype(BF16), lc_ref[...]))
    o_ref[...] = x_ref[...] + gt_ref[0] * _dot(m.astype(BF16), wo_ref[...])


def _merge_call(x2, l, ya, yb, yc, gates, gate1, prm, tm):
    rows, d = x2.shape
    gt_arr, gt_spec = _mod_operand(gate1, l, tm)
    row_spec = lambda w: pl.BlockSpec((tm, w), lambda i: (i, 0))
    return pl.pallas_call(
        _merge_kernel,
        out_shape=jax.ShapeDtypeStruct((rows, d), F32),
        grid=(rows // tm,),
        in_specs=[row_spec(d), row_spec(S5_WIDTH), row_spec(RW_WIDTH), row_spec(HG_WIDTH),
                  row_spec(N_BRANCH * d), gt_spec,
                  _const_spec((S5_WIDTH, d)), _const_spec((RW_WIDTH, d)), _const_spec((HG_WIDTH, d)),
                  _const_spec((d, d))],
        out_specs=row_spec(d),
        compiler_params=_cparams("arbitrary"),
        name="branch_merge",
    )(x2, ya, yb, yc, gates, gt_arr, prm["w_lift_a"], prm["w_lift_b"], prm["w_lift_c"], prm["w_out"])


def _ffn_kernel(*refs, l, tm, has_state, final_norm):
    if has_state:
        (x_ref, g_ref, sh_ref, sc_ref, gt_ref, wup_ref, cw_ref, cb_ref, wdn_ref, fg_ref,
         h0_ref, h1_ref, o_ref, up_ref, h_ref, act_ref, buf_ref, hist_ref) = refs
    else:
        (x_ref, g_ref, sh_ref, sc_ref, gt_ref, wup_ref, cw_ref, cb_ref, wdn_ref, fg_ref,
         o_ref, tail_ref, h_ref, act_ref, buf_ref, hist_ref) = refs
    i = pl.program_id(0)
    n_chunks = D_FF // FF_CHUNK

    if has_state:
        t_idx = lax.broadcasted_iota(jnp.int32, (tm, 1), 0) % l
        after1 = t_idx >= 1
        after2 = t_idx >= 2
        first_tile = i == 0
    else:
        first_tile = i % (l // tm) == 0

    @pl.when(first_tile)
    def _():
        hist_ref[...] = jnp.zeros_like(hist_ref)

    x = x_ref[...]
    h = _rms_modulate(x, g_ref[...], sh_ref[0], sc_ref[0])
    groups = tm // SUBLANES
    if not has_state:
        h = jnp.swapaxes(h.reshape(SUBLANES, groups, D_MODEL), 0, 1).reshape(tm, D_MODEL)
        row8 = lax.broadcasted_iota(jnp.int32, (SUBLANES, FF_CHUNK), 0)
    h_ref[...] = h.astype(BF16)

    def conv_half(c, half):
        col0 = half * D_FF + c * FF_CHUNK
        cols = slice(col0, col0 + FF_CHUNK)
        up = _dot(h_ref[...], wup_ref[:, cols])
        if has_state:
            buf_ref[half, 0:SUBLANES, :] = hist_ref[c, half, 0]
            buf_ref[half, SUBLANES:SUBLANES + tm, :] = up
            hist_ref[c, half, 0] = up[tm - SUBLANES:tm, :]
            up_ref[:, cols] = up
            prev1 = jnp.where(after1, buf_ref[half, SUBLANES - 1:SUBLANES - 1 + tm, :], 0.0) + h1_ref[:, cols]
            prev2 = jnp.where(after2, buf_ref[half, SUBLANES - 2:SUBLANES - 2 + tm, :], 0.0) + h0_ref[:, cols]
        else:
            late = [up[tm - (j + 1) * SUBLANES:tm - j * SUBLANES, :] for j in range(CONV_W - 1)]
            first = [jnp.where(row8 == 0, pltpu.roll(hist_ref[c, half, j], 1, 0), pltpu.roll(late[j], 1, 0))
                     for j in range(CONV_W - 1)]
            for j in range(CONV_W - 1):
                hist_ref[c, half, j] = late[j]
            prev1 = jnp.concatenate([first[0], up[0:tm - SUBLANES, :]], axis=0)
            prev2 = jnp.concatenate([first[1], first[0], up[0:tm - 2 * SUBLANES, :]], axis=0)
            tail_ref[0, 0:1, cols] = up[tm - SUBLANES - 1:tm - SUBLANES, :]
            tail_ref[0, 1:2, cols] = up[tm - 1:tm, :]
        return (cb_ref[:, cols] + cw_ref[0:1, cols] * prev2 + cw_ref[1:2, cols] * prev1
                + cw_ref[2:3, cols] * up)

    for c in range(n_chunks):
        act = _gelu(conv_half(c, 0)) * conv_half(c, 1)
        act_ref[:, c * FF_CHUNK:(c + 1) * FF_CHUNK] = act.astype(BF16)
    acc = _dot(act_ref[...], wdn_ref[...])
    if not has_state:
        acc = jnp.swapaxes(acc.reshape(groups, SUBLANES, D_MODEL), 0, 1).reshape(tm, D_MODEL)
    out = x + gt_ref[0] * acc
    if final_norm:
        ms = jnp.mean(out * out, axis=-1, keepdims=True)
        out = out * lax.rsqrt(ms + RMS_EPS) * fg_ref[...]
    o_ref[...] = out


def _ffn_call(x2, n, l, g, shift, scale, gate2, prm, final_g, conv_state, tm, final_norm):
    rows, d = x2.shape
    has_state = conv_state is not None
    sh_arr, sh_spec = _mod_operand(shift, l, tm)
    sc_arr, sc_spec = _mod_operand(scale, l, tm)
    gt_arr, gt_spec = _mod_operand(gate2, l, tm)
    row_spec = lambda w: pl.BlockSpec((tm, w), lambda i: (i, 0))
    in_specs = [row_spec(d), _const_spec((1, d)), sh_spec, sc_spec, gt_spec,
                _const_spec((d, 2 * D_FF)), _const_spec((CONV_W, 2 * D_FF)), _const_spec((1, 2 * D_FF)),
                _const_spec((D_FF, d)), _const_spec((1, d))]
    operands = [x2, g.reshape(1, d), sh_arr, sc_arr, gt_arr, prm["w_up"], prm["conv_w"], prm["conv_b"],
                prm["w_down"], final_g.reshape(1, d)]
    if has_state:
        assert tm % l == 0 and l > CONV_W - 1
        zeros = jnp.zeros((n, l - 1, 2 * D_FF), F32)
        hist1 = jnp.concatenate([conv_state[:, 1:2], zeros], axis=1).reshape(rows, 2 * D_FF)
        hist0 = jnp.concatenate([conv_state, zeros[:, 1:]], axis=1).reshape(rows, 2 * D_FF)
        in_specs += [row_spec(2 * D_FF), row_spec(2 * D_FF)]
        operands += [hist0, hist1]
        out_shape = [jax.ShapeDtypeStruct((rows, d), F32), jax.ShapeDtypeStruct((rows, 2 * D_FF), F32)]
        out_specs = [row_spec(d), row_spec(2 * D_FF)]
    else:
        assert l % tm == 0
        per = l // tm
        out_shape = [jax.ShapeDtypeStruct((rows, d), F32), jax.ShapeDtypeStruct((n, CONV_W - 1, 2 * D_FF), F32)]
        out_specs = [row_spec(d), pl.BlockSpec((1, CONV_W - 1, 2 * D_FF), lambda i: (i // per, 0, 0))]
    kern = functools.partial(_ffn_kernel, l=l, tm=tm, has_state=has_state, final_norm=final_norm)
    out, aux = pl.pallas_call(
        kern,
        out_shape=out_shape,
        grid=(rows // tm,),
        in_specs=in_specs,
        out_specs=out_specs,
        scratch_shapes=[pltpu.VMEM((tm, d), BF16), pltpu.VMEM((tm, D_FF), BF16),
                        pltpu.VMEM((2, tm + SUBLANES, FF_CHUNK), F32),
                        pltpu.VMEM((D_FF // FF_CHUNK, 2, CONV_W - 1, SUBLANES, FF_CHUNK), F32)],
        compiler_params=_cparams("arbitrary"),
        name="conv_ffn",
    )(*operands)
    if has_state:
        aux = aux.reshape(n, l, 2 * D_FF)[:, l - (CONV_W - 1):]
    return out, aux


def _prepare_layer(p):
    out = {}
    w_in = p["w_in"]
    c1 = S5_WIDTH
    c2 = c1 + RW_COLS
    c3 = c2 + 4 * HG_WIDTH
    out["w_in_a"] = w_in[:, :c1].astype(BF16)
    out["w_in_b"] = w_in[:, c1:c2].astype(BF16)
    out["w_in_c"] = w_in[:, c2:c3].astype(BF16)
    out["w_in_g"] = w_in[:, c3:].astype(BF16)
    out["w_ada"] = p["w_ada"].astype(BF16)
    for name in ("w_lift_a", "w_lift_b", "w_lift_c", "w_out", "w_up", "w_down"):
        out[name] = p[name].astype(BF16)
    out["conv_w"] = p["conv_w"]
    out["conv_b"] = p["conv_b"].reshape(1, 2 * D_FF)

    lr = p["s5_lambda_re"]
    li = p["s5_lambda_im"]
    dt = jnp.exp(p["s5_log_dt"])[:, None]
    mag = jnp.exp(lr * dt)
    ar = mag * jnp.cos(li * dt)
    ai = mag * jnp.sin(li * dt)
    den = lr * lr + li * li
    zr = ((ar - 1.0) * lr + ai * li) / den
    zi = (ai * lr - (ar - 1.0) * li) / den
    bbr = zr[..., None] * p["s5_b_re"] - zi[..., None] * p["s5_b_im"]
    bbi = zr[..., None] * p["s5_b_im"] + zi[..., None] * p["s5_b_re"]
    eye = jnp.eye(S5_GROUPS, dtype=F32)
    bmat = jnp.einsum("gh,rgpc->gcrhp", eye, jnp.stack([bbr, bbi])).reshape(S5_WIDTH, 2 * S5_FLAT)
    cmat = jnp.einsum("hg,rgcp->rhpgc", eye, jnp.stack([p["s5_c_re"], -p["s5_c_im"]])).reshape(
        2 * S5_FLAT, S5_WIDTH)
    out["s5_b"] = bmat.astype(BF16)
    out["s5_c"] = cmat.astype(BF16)
    out["s5_a"] = jnp.stack([ar.reshape(S5_FLAT), ai.reshape(S5_FLAT)])
    out["s5_d"] = p["s5_d"].reshape(1, S5_WIDTH)
    out["s5_w_glu"] = p["s5_w_glu"].astype(BF16)
    out["s5_b_glu"] = p["s5_b_glu"].reshape(1, S5_WIDTH)

    out["rw_mu"] = p["rw_mu"].reshape(1, RW_COLS)
    out["rw_vec"] = jnp.stack([p["rw_w0"], p["rw_a0"], p["rw_k_k"], p["rw_k_a"],
                               p["rw_r_k"].reshape(RW_WIDTH), p["rw_ln_w"], p["rw_ln_b"],
                               jnp.zeros((RW_WIDTH,), F32)])
    zw = jnp.zeros((RW_LORA, RW_WIDTH), F32)
    out["rw_w2"] = zw.at[0:RW_DECAY_LORA].set(p["rw_w2"]).astype(BF16)
    out["rw_a2"] = zw.at[RW_DECAY_LORA:RW_DECAY_LORA + RW_AAA_LORA].set(p["rw_a2"]).astype(BF16)
    out["rw_g2"] = zw.at[RW_DECAY_LORA + RW_AAA_LORA:].set(p["rw_g2"]).astype(BF16)
    out["hg_norm"] = p["hg_norm"]
    out["g_mix"] = p["g_mix"]
    out["g_ffn"] = p["g_ffn"]
    out["b_ada"] = p["b_ada"]
    return out


def _chunk_for(l, sub, t_chunk):
    return t_chunk if l % t_chunk == 0 else -(-l // sub) * sub


def _run_trunk(x, mods, st_s5, st_shift, st_rw, st_hg, st_conv, lower, final_g, prms, t_chunk):
    n, l, d = x.shape
    rows = n * l
    tm = min(ROW_TILE, rows)
    depth = len(prms)
    h = x.reshape(rows, d)
    out_s5, out_shift, out_rw, out_hg, out_conv = [], [], [], [], []
    for layer in range(depth):
        prm = prms[layer]
        sh1, sc1, gt1, sh2, sc2, gt2 = jnp.split(mods[layer], 6, axis=-1)
        za, zb, zc, gates = _in_proj_call(h, l, prm["g_mix"], sh1, sc1, prm["w_in_a"], prm["w_in_b"],
                                          prm["w_in_c"], prm["w_in_g"], tm)

        s5_in = jnp.concatenate([st_s5[layer][..., 0].reshape(n, S5_FLAT),
                                 st_s5[layer][..., 1].reshape(n, S5_FLAT)], axis=1)
        ya, s5_new = _s5_call(za, n, l, s5_in, prm, _chunk_for(l, SUBLANES, t_chunk))
        yb, shift_new, rw_new = _rw_call(zb, n, l, st_shift[layer].reshape(n, 1, RW_COLS), st_rw[layer], prm,
                                         _chunk_for(l, RW_SUB, t_chunk))
        yc, hg_new = _hg_call(zc, n, l, st_hg[layer], lower[layer], prm["hg_norm"],
                              _chunk_for(l, HG_SUB, t_chunk))

        h = _merge_call(h, l, ya, yb, yc, gates, gt1, prm, tm)
        ffn_tm = tm if st_conv is None else min(tm, 128)
        h, conv_new = _ffn_call(h, n, l, prm["g_ffn"], sh2, sc2, gt2, prm, final_g,
                                None if st_conv is None else st_conv[layer], ffn_tm,
                                final_norm=(layer == depth - 1))
        out_s5.append(jnp.stack([s5_new[:, :S5_FLAT].reshape(n, S5_GROUPS, S5_STATE),
                                 s5_new[:, S5_FLAT:].reshape(n, S5_GROUPS, S5_STATE)], axis=-1))
        out_shift.append(shift_new.reshape(n, RW_COLS))
        out_rw.append(rw_new)
        out_hg.append(hg_new)
        out_conv.append(conv_new)
    y = h.reshape(n, l, d).astype(x.dtype)
    return y, (jnp.stack(out_s5), jnp.stack(out_shift), jnp.stack(out_rw), jnp.stack(out_hg),
               jnp.stack(out_conv))


def kernel(x_prompt, x_sample, c_prompt, c_sample, state_s5, state_rwkv_shift, state_rwkv, state_hgrn, state_ffn_conv, w_ada, b_ada, g_mix, g_ffn, w_in, s5_lambda_re, s5_lambda_im, s5_log_dt, s5_b_re, s5_b_im, s5_c_re, s5_c_im, s5_d, s5_w_glu, s5_b_glu, rw_mu, rw_w0, rw_w2, rw_a0, rw_a2, rw_g2, rw_k_k, rw_k_a, rw_r_k, rw_ln_w, rw_ln_b, hg_lb, hg_norm, w_lift_a, w_lift_b, w_lift_c, w_out, w_up, conv_w, conv_b, w_down, final_g):
    per_layer = {
        "w_ada": w_ada, "b_ada": b_ada, "g_mix": g_mix, "g_ffn": g_ffn, "w_in": w_in,
        "s5_lambda_re": s5_lambda_re, "s5_lambda_im": s5_lambda_im, "s5_log_dt": s5_log_dt,
        "s5_b_re": s5_b_re, "s5_b_im": s5_b_im, "s5_c_re": s5_c_re, "s5_c_im": s5_c_im,
        "s5_d": s5_d, "s5_w_glu": s5_w_glu, "s5_b_glu": s5_b_glu,
        "rw_mu": rw_mu, "rw_w0": rw_w0, "rw_w2": rw_w2, "rw_a0": rw_a0, "rw_a2": rw_a2,
        "rw_g2": rw_g2, "rw_k_k": rw_k_k, "rw_k_a": rw_k_a, "rw_r_k": rw_r_k,
        "rw_ln_w": rw_ln_w, "rw_ln_b": rw_ln_b, "hg_norm": hg_norm,
        "w_lift_a": w_lift_a, "w_lift_b": w_lift_b, "w_lift_c": w_lift_c, "w_out": w_out,
        "w_up": w_up, "conv_w": conv_w, "conv_b": conv_b, "w_down": w_down,
    }
    depth = w_ada.shape[0]
    prms = [_prepare_layer({k: v[layer] for k, v in per_layer.items()}) for layer in range(depth)]

    lbp = jax.nn.softmax(hg_lb.astype(F32), axis=0)
    lower = jnp.cumsum(lbp, axis=0) - lbp[0]

    nb = x_prompt.shape[0]
    ns = x_sample.shape[0]
    c_all = jnp.concatenate([c_prompt, c_sample], axis=0).astype(F32)
    mods = [_ada_call(c_all, prms[layer]["w_ada"], prms[layer]["b_ada"]) for layer in range(depth)]
    mods_p = [m[:nb] for m in mods]
    mods_s = [m[nb:] for m in mods]

    z_s5 = jnp.zeros((depth, nb) + state_s5.shape[2:], F32)
    z_shift = jnp.zeros((depth, nb) + state_rwkv_shift.shape[2:], F32)
    z_rw = jnp.zeros((depth, nb) + state_rwkv.shape[2:], F32)
    z_hg = jnp.zeros((depth, nb) + state_hgrn.shape[2:], F32)

    y_prompt, (s5_p, shift_p, rw_p, hg_p, conv_p) = _run_trunk(
        x_prompt, mods_p, z_s5, z_shift, z_rw, z_hg, None, lower, final_g, prms, t_chunk=64)
    y_sample, (s5_s, shift_s, rw_s, hg_s, conv_s) = _run_trunk(
        x_sample, mods_s, state_s5, state_rwkv_shift, state_rwkv, state_hgrn, state_ffn_conv,
        lower, final_g, prms, t_chunk=64)
    return (y_prompt, y_sample, s5_p, shift_p, rw_p, hg_p, conv_p, s5_s, shift_s, rw_s, hg_s, conv_s)
```

```python
import functools
import math

import jax
import jax.numpy as jnp
from jax import lax
from jax.experimental import pallas as pl
from jax.experimental.pallas import tpu as pltpu

F32 = jnp.float32
BF16 = jnp.bfloat16

D_MODEL = 1024
S5_WIDTH = D_MODEL // 4
S5_GROUP = 16
S5_GROUPS = S5_WIDTH // S5_GROUP
S5_STATE = 64
S5_FLAT = S5_GROUPS * S5_STATE
RW_WIDTH = D_MODEL // 2
RW_HEAD = 64
RW_HEADS = RW_WIDTH // RW_HEAD
RW_DECAY_LORA = 32
RW_AAA_LORA = 32
RW_GATE_LORA = 64
RW_LORA = RW_DECAY_LORA + RW_AAA_LORA + RW_GATE_LORA
RW_COLS = 3 * RW_WIDTH + RW_LORA
RW_GN_EPS = 1e-5 * RW_HEAD
HG_WIDTH = D_MODEL // 4
HG_HEAD = 64
HG_HEADS = HG_WIDTH // HG_HEAD
HG_GATE_FLOOR = 1e-30
N_BRANCH = 3
D_FF = 256 * ((8 * D_MODEL // 3 + 255) // 256)
CONV_W = 3
RMS_EPS = 1e-6

LANES = 128
SUBLANES = 8
MXU_COLS = 256
VMEM_LIMIT_BYTES = 56 * 1024 * 1024

SEQ_BLOCK = SUBLANES
HEAD_PAIR = LANES // RW_HEAD
ROW_TILE = 512
FF_CHUNK = 256
IN_PROJ_CHUNK = 512
RW_SUB = 16
RW_PAIRS = RW_HEADS // HEAD_PAIR
HG_SUB = 16
HG_PAIRS = HG_HEADS // HEAD_PAIR
RW_SOLVE_PASSES = 1
RW_APPLY_PASSES = 1
RW_STATE_PASSES = 1

_NN = (((1,), (0,)), ((), ()))
_NT = (((1,), (1,)), ((), ()))


def _cparams(*sem):
    return pltpu.CompilerParams(dimension_semantics=sem, vmem_limit_bytes=VMEM_LIMIT_BYTES)


def _const_spec(shape):
    nd = len(shape)
    return pl.BlockSpec(shape, lambda *_: (0,) * nd, pipeline_mode=pl.Buffered(1))


def _dot(a, b):
    return jnp.dot(a, b, preferred_element_type=F32)


def _split_bf16(x):
    hi = x.astype(BF16)
    lo = (x - hi.astype(F32)).astype(BF16)
    return hi, lo


def _head_ones():
    r = lax.broadcasted_iota(jnp.int32, (LANES, LANES), 0) // RW_HEAD
    c = lax.broadcasted_iota(jnp.int32, (LANES, LANES), 1) // RW_HEAD
    return (r == c).astype(BF16)


def _head_sum(x, ones):
    hi, lo = _split_bf16(x)
    tiles = [_dot(hi[:, c:c + LANES], ones) + _dot(lo[:, c:c + LANES], ones)
             for c in range(0, x.shape[1], LANES)]
    return tiles[0] if len(tiles) == 1 else jnp.concatenate(tiles, axis=1)


def _softplus(x):
    return jnp.maximum(x, 0.0) + jnp.log1p(jnp.exp(-jnp.abs(x)))


def _gelu(x):
    c = 2.0 * math.sqrt(2.0 / math.pi)
    return x * jax.nn.sigmoid(x * (c + (c * 0.044715) * (x * x)))


def _rms_modulate(x, g, shift, scale):
    ms = jnp.mean(x * x, axis=-1, keepdims=True)
    return x * lax.rsqrt(ms + RMS_EPS) * g * (1.0 + scale) + shift


def _seq_specs(n, l, t_chunk, widths_in, widths_out):
    nb = n // SEQ_BLOCK
    if l % t_chunk == 0:
        spec = lambda w: pl.BlockSpec((SEQ_BLOCK, t_chunk, w), lambda b, c: (b, c, 0))
        shape = lambda w: (n, l, w)
        nt = l // t_chunk
    else:
        assert l < t_chunk
        spec = lambda w: pl.BlockSpec((SEQ_BLOCK * l, w), lambda b, c: (b, 0))
        shape = lambda w: (n * l, w)
        nt = 1
    return ((nb, nt), [spec(w) for w in widths_in], [spec(w) for w in widths_out],
            [shape(w) for w in widths_in], [shape(w) for w in widths_out])


def _branch_dtype(shape):
    return BF16 if len(shape) == 3 else F32


def _seq_block_load(z_ref, pad_ref):
    if len(z_ref.shape) == 3:
        return z_ref[...]
    l = z_ref.shape[0] // SEQ_BLOCK
    pad_ref[...] = jnp.zeros_like(pad_ref)
    for n in range(SEQ_BLOCK):
        pad_ref[n, 0:l, :] = z_ref[n * l:(n + 1) * l, :]
    return pad_ref[...]


def _seq_block_store(y_ref, stage_ref, y3):
    if len(y_ref.shape) == 3:
        y_ref[...] = y3.astype(y_ref.dtype)
        return
    l = y_ref.shape[0] // SEQ_BLOCK
    stage_ref[...] = y3
    for n in range(SEQ_BLOCK):
        y_ref[n * l:(n + 1) * l, :] = stage_ref[n, 0:l, :]


def _heads_to_tiles(s_ref, tile_ref, n_pairs, transpose):
    zero = jnp.zeros((RW_HEAD, RW_HEAD), F32)
    for n in range(SEQ_BLOCK):
        for hp in range(n_pairs):
            a, b = s_ref[n, HEAD_PAIR * hp], s_ref[n, HEAD_PAIR * hp + 1]
            if transpose:
                a, b = a.T, b.T
            tile_ref[hp, n] = jnp.concatenate([jnp.concatenate([a, zero], axis=1),
                                               jnp.concatenate([zero, b], axis=1)], axis=0)


def _tiles_to_heads(tile_ref, s_ref, n_pairs, transpose):
    for n in range(SEQ_BLOCK):
        for hp in range(n_pairs):
            t = tile_ref[hp, n]
            a, b = t[0:RW_HEAD, 0:RW_HEAD], t[RW_HEAD:LANES, RW_HEAD:LANES]
            if transpose:
                a, b = a.T, b.T
            s_ref[n, HEAD_PAIR * hp] = a
            s_ref[n, HEAD_PAIR * hp + 1] = b


def _ada_kernel(c_ref, w_ref, b_ref, o_ref):
    c = c_ref[...]
    o_ref[...] = _dot(jax.nn.silu(c).astype(BF16), w_ref[...]) + b_ref[...]


def _ada_call(c, w_bf16, b):
    n, d = c.shape
    cols = w_bf16.shape[1]
    tn = cols // 4
    return pl.pallas_call(
        _ada_kernel,
        out_shape=jax.ShapeDtypeStruct((n, cols), F32),
        grid=(cols // tn,),
        in_specs=[_const_spec((n, d)),
                  pl.BlockSpec((d, tn), lambda j: (0, j)),
                  pl.BlockSpec((1, tn), lambda j: (0, j))],
        out_specs=pl.BlockSpec((n, tn), lambda j: (0, j)),
        compiler_params=_cparams("arbitrary"),
        name="ada_mod",
    )(c, w_bf16, b.reshape(1, cols))


def _mod_operand(m, l, tm):
    n, d = m.shape
    if l % tm == 0:
        per = l // tm
        return m.reshape(n, 1, d), pl.BlockSpec((1, 1, d), lambda i: (i // per, 0, 0))
    assert tm % l == 0
    return jnp.repeat(m, l, axis=0).reshape(1, n * l, d), pl.BlockSpec((1, tm, d), lambda i: (0, i, 0))


def _tok_layout(n, l, tm, w):
    if l % tm == 0:
        per = l // tm
        return (n, l, w), pl.BlockSpec((1, tm, w), lambda i: (i // per, i % per, 0))
    return (n * l, w), pl.BlockSpec((tm, w), lambda i: (i, 0))


def _tile(ref):
    return ref.at[0] if len(ref.shape) == 3 else ref


def _in_proj_kernel(x_ref, g_ref, sh_ref, sc_ref, wa_ref, wb_ref, wc_ref, wg_ref,
                    oa_ref, ob_ref, oc_ref, og_ref):
    h = _rms_modulate(_tile(x_ref)[...], g_ref[...], sh_ref[0], sc_ref[0]).astype(BF16)

    def project(w_ref, o_ref, fn=None):
        width = w_ref.shape[1]
        out = _tile(o_ref)
        for c0 in range(0, width, IN_PROJ_CHUNK):
            cols = slice(c0, min(c0 + IN_PROJ_CHUNK, width))
            z = _dot(h, w_ref[:, cols])
            out[:, cols] = (z if fn is None else fn(z)).astype(out.dtype)

    project(wa_ref, oa_ref)
    project(wb_ref, ob_ref)
    project(wc_ref, oc_ref)
    project(wg_ref, og_ref, jax.nn.sigmoid)


def _in_proj_call(x, n, l, g, shift, scale, w_a, w_b, w_c, w_g, tm):
    d = x.shape[-1]
    sh_arr, sh_spec = _mod_operand(shift, l, tm)
    sc_arr, sc_spec = _mod_operand(scale, l, tm)
    widths = [w.shape[1] for w in (w_a, w_b, w_c, w_g)]
    dtypes = [F32, F32, F32, BF16]
    outs = [_tok_layout(n, l, tm, w) for w in widths]
    return pl.pallas_call(
        _in_proj_kernel,
        out_shape=[jax.ShapeDtypeStruct(shape, dt) for (shape, _), dt in zip(outs, dtypes)],
        grid=(n * l // tm,),
        in_specs=[_tok_layout(n, l, tm, d)[1], _const_spec((1, d)), sh_spec, sc_spec]
        + [_const_spec(w.shape) for w in (w_a, w_b, w_c, w_g)],
        out_specs=[spec for _, spec in outs],
        compiler_params=_cparams("arbitrary"),
        name="in_proj",
    )(x, g.reshape(1, d), sh_arr, sc_arr, w_a, w_b, w_c, w_g)


def _s5_kernel(u_ref, s0_ref, b_ref, c_ref, a_ref, d_ref, wg_ref, bg_ref,
               y_ref, sl_ref, e_ref, st_ref, pad_ref, stage_ref, *, t_chunk, n_steps):
    @pl.when(pl.program_id(1) == 0)
    def _():
        st_ref[...] = s0_ref[...]

    rows = SEQ_BLOCK * t_chunk
    ut = jnp.swapaxes(_seq_block_load(u_ref, pad_ref), 0, 1).reshape(rows, S5_WIDTH)
    ub = ut.astype(BF16)
    for c0 in range(0, 2 * S5_FLAT, MXU_COLS):
        ch0 = (c0 % S5_FLAT) // S5_STATE * S5_GROUP
        k0 = ch0 // LANES * LANES
        assert ch0 + MXU_COLS // S5_STATE * S5_GROUP <= k0 + LANES
        e_ref[:, c0:c0 + MXU_COLS] = _dot(ub[:, k0:k0 + LANES], b_ref[k0:k0 + LANES, c0:c0 + MXU_COLS])

    ar = jnp.broadcast_to(a_ref[0:1, :], (SEQ_BLOCK, S5_FLAT))
    ai = jnp.broadcast_to(a_ref[1:2, :], (SEQ_BLOCK, S5_FLAT))

    def step(t, carry):
        sr, si = carry
        r = pl.ds(pl.multiple_of(t * SEQ_BLOCK, SEQ_BLOCK), SEQ_BLOCK)
        nr = ar * sr - ai * si + e_ref[r, 0:S5_FLAT]
        ni = ar * si + ai * sr + e_ref[r, S5_FLAT:2 * S5_FLAT]
        e_ref[r, 0:S5_FLAT] = nr
        e_ref[r, S5_FLAT:2 * S5_FLAT] = ni
        return nr, ni

    sr, si = lax.fori_loop(0, n_steps, step, (st_ref[:, 0:S5_FLAT], st_ref[:, S5_FLAT:2 * S5_FLAT]))
    st_ref[:, 0:S5_FLAT] = sr
    st_ref[:, S5_FLAT:2 * S5_FLAT] = si
    sl_ref[...] = st_ref[...]

    y = _dot(e_ref[...].astype(BF16), c_ref[...]) + d_ref[...] * ut
    y = _gelu(y)
    y = y * jax.nn.sigmoid(_dot(y.astype(BF16), wg_ref[...]) + bg_ref[...])
    _seq_block_store(y_ref, stage_ref, jnp.swapaxes(y.reshape(t_chunk, SEQ_BLOCK, S5_WIDTH), 0, 1))


def _io_scratch(l, t_chunk, width_in, width_out):
    if l % t_chunk == 0:
        return [pltpu.VMEM((SUBLANES, LANES), F32)] * 2
    return [pltpu.VMEM((SEQ_BLOCK, t_chunk, width_in), F32), pltpu.VMEM((SEQ_BLOCK, t_chunk, width_out), F32)]


def _s5_call(u2, n, l, s0, prm, t_chunk):
    grid, in_specs, out_specs, in_shapes, out_shapes = _seq_specs(n, l, t_chunk, [S5_WIDTH], [S5_WIDTH])
    kern = functools.partial(_s5_kernel, t_chunk=t_chunk, n_steps=min(t_chunk, l))
    y, s_new = pl.pallas_call(
        kern,
        out_shape=[jax.ShapeDtypeStruct(out_shapes[0], _branch_dtype(out_shapes[0])),
                   jax.ShapeDtypeStruct((n, 2 * S5_FLAT), F32)],
        grid=grid,
        in_specs=in_specs + [
            pl.BlockSpec((SEQ_BLOCK, 2 * S5_FLAT), lambda b, c: (b, 0)),
            _const_spec((S5_WIDTH, 2 * S5_FLAT)), _const_spec((2 * S5_FLAT, S5_WIDTH)),
            _const_spec((2, S5_FLAT)), _const_spec((1, S5_WIDTH)),
            _const_spec((S5_WIDTH, S5_WIDTH)), _const_spec((1, S5_WIDTH))],
        out_specs=out_specs + [pl.BlockSpec((SEQ_BLOCK, 2 * S5_FLAT), lambda b, c: (b, 0))],
        scratch_shapes=[pltpu.VMEM((SEQ_BLOCK * t_chunk, 2 * S5_FLAT), F32),
                        pltpu.VMEM((SEQ_BLOCK, 2 * S5_FLAT), F32)]
        + _io_scratch(l, t_chunk, S5_WIDTH, S5_WIDTH),
        compiler_params=_cparams("arbitrary", "arbitrary"),
        name="s5_mixer",
    )(u2.reshape(in_shapes[0]), s0, prm["s5_b"], prm["s5_c"], prm["s5_a"], prm["s5_d"], prm["s5_w_glu"],
      prm["s5_b_glu"])
    return y, s_new


def _mm(a, b, dims, passes):
    dg = lambda x, y: lax.dot_general(x, y, dims, preferred_element_type=F32)
    if passes == 1:
        return dg(a.astype(BF16), b.astype(BF16))
    a_hi, a_lo = _split_bf16(a)
    if passes == 2:
        b_hi = b.astype(BF16)
        return dg(a_hi, b_hi) + dg(a_lo, b_hi)
    b_hi, b_lo = _split_bf16(b)
    return dg(a_hi, b_hi) + dg(a_lo, b_hi) + dg(a_hi, b_lo)


def _cumsum_groups(x, group):
    pos = lax.broadcasted_iota(jnp.int32, (x.shape[0], 1), 0) % group
    s = 1
    while s < group:
        x = x + jnp.where(pos >= s, pltpu.roll(x, s, 0), 0.0)
        s *= 2
    return x


def _rw_kernel(p_ref, sh0_ref, s0_ref, mu_ref, vec_ref, w2_ref, a2_ref, g2_ref,
               y_ref, shl_ref, sl_ref,
               pbuf_ref, s_ref, at_ref, bt_ref, kt_ref, rt_ref, bh_ref, kh_ref, v_ref, gc_ref, ys_ref,
               pad_ref, stage_ref, *, t_chunk, l_valid, last_row):
    rows = SEQ_BLOCK * t_chunk
    n_sub = t_chunk // RW_SUB

    @pl.when(pl.program_id(1) == 0)
    def _():
        pbuf_ref[:, SUBLANES - 1:SUBLANES, :] = sh0_ref[...]
        _heads_to_tiles(s0_ref, s_ref, RW_PAIRS, transpose=False)

    p3 = _seq_block_load(p_ref, pad_ref)
    pbuf_ref[:, SUBLANES:SUBLANES + t_chunk, :] = p3
    prev3 = pbuf_ref[:, SUBLANES - 1:SUBLANES - 1 + t_chunk, :]
    last = pbuf_ref[:, SUBLANES + last_row:SUBLANES + last_row + 1, :]
    pbuf_ref[:, SUBLANES - 1:SUBLANES, :] = last
    shl_ref[...] = last

    xm = (p3 + (prev3 - p3) * mu_ref[...]).reshape(rows, RW_COLS)
    r = xm[:, 0:RW_WIDTH]
    k = xm[:, RW_WIDTH:2 * RW_WIDTH]
    v = xm[:, 2 * RW_WIDTH:3 * RW_WIDTH]
    lora = xm[:, 3 * RW_WIDTH:RW_COLS]
    w0, a0, k_k, k_a = vec_ref[0:1, :], vec_ref[1:2, :], vec_ref[2:3, :], vec_ref[3:4, :]
    r_k, ln_w, ln_b = vec_ref[4:5, :], vec_ref[5:6, :], vec_ref[6:7, :]

    w = -_softplus(-(w0 + _dot(jnp.tanh(lora).astype(BF16), w2_ref[...]))) - 0.5
    log_decay = -jnp.exp(w)
    a = jax.nn.sigmoid(a0 + _dot(lora.astype(BF16), a2_ref[...]))
    g = _dot(jax.nn.sigmoid(lora).astype(BF16), g2_ref[...])

    ones = _head_ones()
    kk = k * k_k
    kk = kk / jnp.maximum(jnp.sqrt(_head_sum(kk * kk, ones)), 1e-12)
    kt = k * (1.0 + (a - 1.0) * k_a)
    kka = kk * a
    if l_valid < t_chunk:
        live = lax.broadcasted_iota(jnp.int32, (rows, 1), 0) % t_chunk < l_valid
        log_decay = jnp.where(live, log_decay, 0.0)
        kk = jnp.where(live, kk, 0.0)
        kka = jnp.where(live, kka, 0.0)
        kt_live = jnp.where(live, kt, 0.0)
    else:
        kt_live = kt

    cum = _cumsum_groups(log_decay, RW_SUB)
    shape_g = (rows // RW_SUB, RW_SUB, RW_WIDTH)
    cum_end = jnp.broadcast_to(cum.reshape(shape_g)[:, RW_SUB - 1:RW_SUB, :], shape_g).reshape(rows, RW_WIDTH)
    g_in = jnp.exp(cum)
    g_inv = jnp.exp(-cum)
    g_ex = jnp.exp(cum - log_decay)
    g_out = jnp.exp(cum_end - cum)
    shape3 = (SEQ_BLOCK, t_chunk, RW_WIDTH)
    at_ref[...] = (-kk * g_ex).reshape(shape3)
    bt_ref[...] = (kka * g_inv).reshape(shape3)
    kt_ref[...] = (kt_live * g_inv).reshape(shape3)
    rt_ref[...] = (r * g_in).reshape(shape3)
    bh_ref[...] = (kka * g_out).reshape(shape3)
    kh_ref[...] = (kt_live * g_out).reshape(shape3)
    v_ref[...] = v.reshape(shape3)
    gc_ref[...] = jnp.exp(cum_end).reshape(shape3)

    m_rows = SEQ_BLOCK * RW_SUB
    ri = lax.broadcasted_iota(jnp.int32, (m_rows, m_rows), 0)
    ci = lax.broadcasted_iota(jnp.int32, (m_rows, m_rows), 1)
    same_seq = (ri // RW_SUB) == (ci // RW_SUB)
    before = same_seq & ((ci % RW_SUB) < (ri % RW_SUB))
    upto = same_seq & ((ci % RW_SUB) <= (ri % RW_SUB))
    eye = (ri == ci).astype(F32)
    pair_mask = (ri // 2) == (ci // 2)
    level_masks = []
    s = 2
    while s < RW_SUB:
        level_masks.append(((ri // (2 * s)) == (ci // (2 * s))) & ((ri // s) != (ci // s)))
        s *= 2
    lane = lax.broadcasted_iota(jnp.int32, (m_rows, LANES), 1)
    head_mask = [lane < RW_HEAD, lane >= RW_HEAD]
    wide = (m_rows, SEQ_BLOCK * LANES)
    own_rows = (lax.broadcasted_iota(jnp.int32, wide, 1) // LANES
                == lax.broadcasted_iota(jnp.int32, wide, 0) // RW_SUB)
    same_head = (lax.broadcasted_iota(jnp.int32, wide, 0) // RW_HEAD
                 == (lax.broadcasted_iota(jnp.int32, wide, 1) % LANES) // RW_HEAD)
    heads = [(hp, h) for hp in range(RW_PAIRS) for h in range(HEAD_PAIR)]

    def sub_chunk(c, carry):
        rs = pl.ds(pl.multiple_of(c * RW_SUB, RW_SUB), RW_SUB)

        def ld(ref, hp):
            return ref[:, rs, pl.ds(hp * LANES, LANES)].reshape(m_rows, LANES)

        at, bt, ktl, rt, bh, kh, vv, gcv = ([ld(ref, hp) for hp in range(RW_PAIRS)]
                                            for ref in (at_ref, bt_ref, kt_ref, rt_ref, bh_ref, kh_ref,
                                                        v_ref, gc_ref))
        state = [[s_ref[hp, n] for n in range(SEQ_BLOCK)] for hp in range(RW_PAIRS)]
        zero = jnp.zeros((m_rows, LANES), F32)
        msk = lambda x, h: jnp.where(head_mask[h], x, zero)

        gram = [_mm(jnp.concatenate([msk(at[hp], 0), msk(rt[hp], 0), msk(at[hp], 1), msk(rt[hp], 1)], axis=0),
                    jnp.concatenate([bt[hp], ktl[hp]], axis=0), _NT, RW_SOLVE_PASSES)
                for hp in range(RW_PAIRS)]

        def quad(hp, h, row, col, keep):
            blk = gram[hp][(2 * h + row) * m_rows:(2 * h + row + 1) * m_rows, col * m_rows:(col + 1) * m_rows]
            return jnp.where(keep, blk, 0.0)

        m_ab = [quad(hp, h, 0, 0, before) for hp, h in heads]
        m_ak = [quad(hp, h, 0, 1, before) for hp, h in heads]
        n_rb = [quad(hp, h, 1, 0, upto) for hp, h in heads]
        n_rk = [quad(hp, h, 1, 1, upto) for hp, h in heads]
        tinv = [eye + jnp.where(pair_mask, m, 0.0) for m in m_ab]
        for lm in level_masks:
            prod = [_mm(jnp.where(lm, m, 0.0), t, _NN, RW_SOLVE_PASSES) for m, t in zip(m_ab, tinv)]
            tinv = [t + _mm(t, q, _NN, RW_SOLVE_PASSES) for t, q in zip(tinv, prod)]
        vh = [msk(vv[hp], h) for hp, h in heads]
        w1 = [_mm(m, x, _NN, RW_APPLY_PASSES) for m, x in zip(m_ak, vh)]
        ap = [_mm(t, jnp.concatenate([msk(at[hp], h), w], axis=1), _NN, RW_APPLY_PASSES)
              for t, (hp, h), w in zip(tinv, heads, w1)]
        nb = [_mm(m, x, _NN, RW_APPLY_PASSES) for m, x in zip(n_rb, ap)]
        nv = [_mm(m, x, _NN, RW_APPLY_PASSES) for m, x in zip(n_rk, vh)]

        def per_seq(x):
            return jnp.where(own_rows, jnp.concatenate([x] * SEQ_BLOCK, axis=1), 0.0)

        pairs = range(RW_PAIRS)
        both = lambda xs, hp, cols: xs[HEAD_PAIR * hp][:, cols] + xs[HEAD_PAIR * hp + 1][:, cols]
        lo, hi = slice(0, LANES), slice(LANES, 2 * LANES)
        bh_x = [per_seq(bh[hp]) for hp in pairs]
        kh_x = [per_seq(kh[hp]) for hp in pairs]
        gam = [jnp.where(same_head, _mm(both(ap, hp, lo).T, bh_x[hp], _NN, RW_STATE_PASSES), 0.0) for hp in pairs]
        u = [jnp.where(same_head, _mm(jnp.concatenate([both(ap, hp, hi).T, vv[hp].T], axis=1),
                                      jnp.concatenate([bh_x[hp], kh_x[hp]], axis=0), _NN, RW_STATE_PASSES), 0.0)
             for hp in pairs]
        for hp in pairs:
            r_hat = rt[hp] + both(nb, hp, lo)
            y_zero = both(nb, hp, hi) + nv[HEAD_PAIR * hp] + nv[HEAD_PAIR * hp + 1]
            cols = pl.ds(hp * LANES, LANES)
            for n in range(SEQ_BLOCK):
                q = slice(n * RW_SUB, (n + 1) * RW_SUB)
                blk = slice(n * LANES, (n + 1) * LANES)
                st = state[hp][n]
                ys_ref[n, rs, cols] = _mm(r_hat[q], st, _NT, RW_STATE_PASSES) + y_zero[q]
                s_ref[hp, n] = (st * gcv[hp][n * RW_SUB:n * RW_SUB + 1, :]
                                + _mm(st, gam[hp][:, blk], _NN, RW_STATE_PASSES) + u[hp][:, blk])
        return carry

    lax.fori_loop(0, n_sub, sub_chunk, 0)

    @pl.when(pl.program_id(1) == pl.num_programs(1) - 1)
    def _():
        _tiles_to_heads(s_ref, sl_ref, RW_PAIRS, transpose=False)

    y = ys_ref[...].reshape(rows, RW_WIDTH)
    inv = 1.0 / RW_HEAD
    mean = _head_sum(y, ones) * inv
    yc = y - mean
    var = _head_sum(yc * yc, ones) * inv
    y = yc * lax.rsqrt(var + RW_GN_EPS) * ln_w + ln_b
    bonus = _head_sum(r * kt * r_k, ones) * v
    _seq_block_store(y_ref, stage_ref, ((y + bonus) * g).reshape(shape3))


def _rw_call(p2, n, l, shift0, s0, prm, t_chunk):
    assert t_chunk % RW_SUB == 0
    grid, in_specs, out_specs, in_shapes, out_shapes = _seq_specs(n, l, t_chunk, [RW_COLS], [RW_WIDTH])
    valid_in_chunk = min(t_chunk, l)
    kern = functools.partial(_rw_kernel, t_chunk=t_chunk, l_valid=valid_in_chunk, last_row=valid_in_chunk - 1)
    st_spec = pl.BlockSpec((SEQ_BLOCK, RW_HEADS, RW_HEAD, RW_HEAD), lambda b, c: (b, 0, 0, 0))
    sh_spec = pl.BlockSpec((SEQ_BLOCK, 1, RW_COLS), lambda b, c: (b, 0, 0))
    q_scr = pltpu.VMEM((SEQ_BLOCK, t_chunk, RW_WIDTH), F32)
    y, shift_new, s_new = pl.pallas_call(
        kern,
        out_shape=[jax.ShapeDtypeStruct(out_shapes[0], _branch_dtype(out_shapes[0])),
                   jax.ShapeDtypeStruct((n, 1, RW_COLS), F32),
                   jax.ShapeDtypeStruct((n, RW_HEADS, RW_HEAD, RW_HEAD), F32)],
        grid=grid,
        in_specs=in_specs + [sh_spec, st_spec,
                             _const_spec((1, RW_COLS)), _const_spec((SUBLANES, RW_WIDTH)),
                             _const_spec((RW_LORA, RW_WIDTH)), _const_spec((RW_LORA, RW_WIDTH)),
                             _const_spec((RW_LORA, RW_WIDTH))],
        out_specs=out_specs + [sh_spec, st_spec],
        scratch_shapes=[pltpu.VMEM((SEQ_BLOCK, t_chunk + SUBLANES, RW_COLS), F32),
                        pltpu.VMEM((RW_PAIRS, SEQ_BLOCK, LANES, LANES), F32)] + [q_scr] * 9
        + _io_scratch(l, t_chunk, RW_COLS, RW_WIDTH),
        compiler_params=_cparams("arbitrary", "arbitrary"),
        name="rwkv7_mixer",
    )(p2.reshape(in_shapes[0]), shift0, s0, prm["rw_mu"], prm["rw_vec"], prm["rw_w2"], prm["rw_a2"], prm["rw_g2"])
    return y, shift_new, s_new


def _hg_kernel(z_ref, s0_ref, lower_ref, ng_ref, y_ref, sl_ref,
               s_ref, qs_ref, q_ref, k_ref, kh_ref, v_ref, b_ref, gc_ref, ys_ref, pad_ref, stage_ref,
               *, t_chunk, l_valid):
    rows = SEQ_BLOCK * t_chunk
    n_sub = t_chunk // HG_SUB

    @pl.when(pl.program_id(1) == 0)
    def _():
        _heads_to_tiles(s0_ref, s_ref, HG_PAIRS, transpose=True)

    z = _seq_block_load(z_ref, pad_ref).reshape(rows, 4 * HG_WIDTH)
    q = jax.nn.silu(z[:, 0:HG_WIDTH])
    f = z[:, HG_WIDTH:2 * HG_WIDTH]
    i = z[:, 2 * HG_WIDTH:3 * HG_WIDTH]
    og = z[:, 3 * HG_WIDTH:4 * HG_WIDTH]
    lower = lower_ref[...]
    fgate = lower + (1.0 - lower) * jax.nn.sigmoid(f)
    log_f = jnp.log(jnp.maximum(fgate, HG_GATE_FLOOR))
    k = 1.0 - fgate
    if l_valid < t_chunk:
        live = lax.broadcasted_iota(jnp.int32, (rows, 1), 0) % t_chunk < l_valid
        log_f = jnp.where(live, log_f, 0.0)
        k = jnp.where(live, k, 0.0)
    cum = _cumsum_groups(log_f, HG_SUB)
    shape_g = (rows // HG_SUB, HG_SUB, HG_WIDTH)
    cum_end = jnp.broadcast_to(cum.reshape(shape_g)[:, HG_SUB - 1:HG_SUB, :], shape_g).reshape(rows, HG_WIDTH)
    shape3 = (SEQ_BLOCK, t_chunk, HG_WIDTH)
    qs_ref[...] = (q * jnp.exp(cum)).reshape(shape3)
    q_ref[...] = q.reshape(shape3)
    k_ref[...] = k.reshape(shape3)
    kh_ref[...] = (k * jnp.exp(cum_end - cum)).reshape(shape3)
    v_ref[...] = i.reshape(shape3)
    b_ref[...] = cum.reshape(shape3)
    gc_ref[...] = jnp.exp(cum_end).reshape(shape3)

    m_rows = SEQ_BLOCK * HG_SUB
    ones = _head_ones()
    wide = (m_rows, SEQ_BLOCK * LANES)
    own_rows = (lax.broadcasted_iota(jnp.int32, wide, 1) // LANES
                == lax.broadcasted_iota(jnp.int32, wide, 0) // HG_SUB)
    same_head = (lax.broadcasted_iota(jnp.int32, wide, 0) // HG_HEAD
                 == (lax.broadcasted_iota(jnp.int32, wide, 1) % LANES) // HG_HEAD)
    step = lax.broadcasted_iota(jnp.int32, (HG_SUB, LANES), 0)

    def sub_chunk(c, carry):
        rs = pl.ds(pl.multiple_of(c * HG_SUB, HG_SUB), HG_SUB)
        for hp in range(HG_PAIRS):
            cols = pl.ds(hp * LANES, LANES)
            ld = lambda ref: ref[:, rs, cols].reshape(m_rows, LANES)
            qs, qq, kk, kh, vv, bb, gcv = (ld(qs_ref), ld(q_ref), ld(k_ref), ld(kh_ref), ld(v_ref),
                                           ld(b_ref), ld(gc_ref))
            state = [s_ref[hp, n] for n in range(SEQ_BLOCK)]
            kh_x = jnp.where(own_rows, jnp.concatenate([kh] * SEQ_BLOCK, axis=1), 0.0)
            u = jnp.where(same_head, _mm(vv.T, kh_x, _NN, 1), 0.0)
            for n in range(SEQ_BLOCK):
                sl = slice(n * HG_SUB, (n + 1) * HG_SUB)
                bn, qn, kn, vn = bb[sl], qq[sl], kk[sl], vv[sl]
                prods = []
                for t in range(HG_SUB):
                    keep = step <= t
                    diff = jnp.where(keep, bn[t:t + 1, :] - bn, 0.0)
                    prods.append(jnp.where(keep, qn[t:t + 1, :] * kn * jnp.exp(diff), 0.0))
                att = _dot(jnp.concatenate(prods, axis=0).astype(BF16), ones)
                o_rows = [jnp.sum(att[t * HG_SUB:(t + 1) * HG_SUB] * vn, axis=0, keepdims=True)
                          for t in range(HG_SUB)]
                ys_ref[n, rs, cols] = _mm(qs[sl], state[n], _NT, 1) + jnp.concatenate(o_rows, axis=0)
                blk = slice(n * LANES, (n + 1) * LANES)
                s_ref[hp, n] = state[n] * gcv[n * HG_SUB:n * HG_SUB + 1, :] + u[:, blk]
        return carry

    lax.fori_loop(0, n_sub, sub_chunk, 0)

    @pl.when(pl.program_id(1) == pl.num_programs(1) - 1)
    def _():
        _tiles_to_heads(s_ref, sl_ref, HG_PAIRS, transpose=True)

    o = ys_ref[...].reshape(rows, HG_WIDTH)
    ms = _head_sum(o * o, ones) * (1.0 / HG_HEAD)
    o = o * lax.rsqrt(ms + RMS_EPS) * ng_ref[...] * jax.nn.sigmoid(og)
    _seq_block_store(y_ref, stage_ref, o.reshape(shape3))


def _hg_call(z2, n, l, s0, lower, norm_g, t_chunk):
    assert t_chunk % HG_SUB == 0
    grid, in_specs, out_specs, in_shapes, out_shapes = _seq_specs(n, l, t_chunk, [4 * HG_WIDTH], [HG_WIDTH])
    kern = functools.partial(_hg_kernel, t_chunk=t_chunk, l_valid=min(t_chunk, l))
    st_spec = pl.BlockSpec((SEQ_BLOCK, HG_HEADS, HG_HEAD, HG_HEAD), lambda b, c: (b, 0, 0, 0))
    q_scr = pltpu.VMEM((SEQ_BLOCK, t_chunk, HG_WIDTH), F32)
    y, s_new = pl.pallas_call(
        kern,
        out_shape=[jax.ShapeDtypeStruct(out_shapes[0], _branch_dtype(out_shapes[0])),
                   jax.ShapeDtypeStruct((n, HG_HEADS, HG_HEAD, HG_HEAD), F32)],
        grid=grid,
        in_specs=in_specs + [st_spec, _const_spec((1, HG_WIDTH)), _const_spec((1, HG_WIDTH))],
        out_specs=out_specs + [st_spec],
        scratch_shapes=[pltpu.VMEM((HG_PAIRS, SEQ_BLOCK, LANES, LANES), F32)] + [q_scr] * 8
        + _io_scratch(l, t_chunk, 4 * HG_WIDTH, HG_WIDTH),
        compiler_params=_cparams("arbitrary", "arbitrary"),
        name="hgrn2_mixer",
    )(z2.reshape(in_shapes[0]), s0, lower.reshape(1, HG_WIDTH), norm_g.reshape(1, HG_WIDTH))
    return y, s_new


def _merge_kernel(x_ref, ya_ref, yb_ref, yc_ref, gates_ref, gt_ref, la_ref, lb_ref, lc_ref, wo_ref, o_ref):
    d = D_MODEL
    gates = _tile(gates_ref)
    lift = lambda y_ref, w_ref: _dot(_tile(y_ref)[...].astype(BF16), w_ref[...])
    m = (gates[:, 0:d].astype(F32) * lift(ya_ref, la_ref)
         + gates[:, d:2 * d].astype(F32) * lift(yb_ref, lb_ref)
         + gates[:, 2 * d:3 * d].astype(F32) * lift(yc_ref, lc_ref))
    _tile(o_ref)[...] = _tile(x_ref)[...] + gt_ref[0] * _dot(m.astype(BF16), wo_ref[...])


def _merge_call(x, n, l, ya, yb, yc, gates, gate1, prm, tm):
    d = x.shape[-1]
    gt_arr, gt_spec = _mod_operand(gate1, l, tm)
    shape, x_spec = _tok_layout(n, l, tm, d)
    row_spec = lambda w: _tok_layout(n, l, tm, w)[1]
    return pl.pallas_call(
        _merge_kernel,
        out_shape=jax.ShapeDtypeStruct(shape, F32),
        grid=(n * l // tm,),
        in_specs=[x_spec, row_spec(S5_WIDTH), row_spec(RW_WIDTH), row_spec(HG_WIDTH),
                  row_spec(N_BRANCH * d), gt_spec,
                  _const_spec((S5_WIDTH, d)), _const_spec((RW_WIDTH, d)), _const_spec((HG_WIDTH, d)),
                  _const_spec((d, d))],
        out_specs=x_spec,
        compiler_params=_cparams("arbitrary"),
        name="branch_merge",
    )(x, ya, yb, yc, gates, gt_arr, prm["w_lift_a"], prm["w_lift_b"], prm["w_lift_c"], prm["w_out"])


def _ffn_kernel(*refs, l, tm, has_state, final_norm):
    if has_state:
        (x_ref, g_ref, sh_ref, sc_ref, gt_ref, wup_ref, cw_ref, cb_ref, wdn_ref, fg_ref, st_ref,
         o_ref, tail_ref, h_ref, act_ref, hist_ref, buf_ref, old_ref) = refs
    else:
        (x_ref, g_ref, sh_ref, sc_ref, gt_ref, wup_ref, cw_ref, cb_ref, wdn_ref, fg_ref,
         o_ref, tail_ref, h_ref, act_ref, hist_ref) = refs
    i = pl.program_id(0)
    n_chunks = D_FF // FF_CHUNK
    n_hist = CONV_W - 1

    if has_state:
        t_idx = lax.broadcasted_iota(jnp.int32, (tm, 1), 0) % l
        after = [t_idx >= j + 1 for j in range(n_hist)]
        old_ref[...] = jnp.zeros_like(old_ref)
        buf_ref[:, 0:SUBLANES, :] = jnp.zeros((2, SUBLANES, FF_CHUNK), F32)
        for s in range(tm // l):
            for j in range(n_hist):
                old_ref[j, s * l:s * l + j + 1, :] = st_ref[s, n_hist - 1 - j:n_hist, :]
    else:
        @pl.when(i % (l // tm) == 0)
        def _():
            hist_ref[...] = jnp.zeros_like(hist_ref)

    x = _tile(x_ref)[...]
    h = _rms_modulate(x, g_ref[...], sh_ref[0], sc_ref[0])
    groups = tm // SUBLANES
    if not has_state:
        h = jnp.swapaxes(h.reshape(SUBLANES, groups, D_MODEL), 0, 1).reshape(tm, D_MODEL)
        row8 = lax.broadcasted_iota(jnp.int32, (SUBLANES, FF_CHUNK), 0)
    h_ref[...] = h.astype(BF16)

    def conv_half(c, half):
        col0 = half * D_FF + c * FF_CHUNK
        cols = slice(col0, col0 + FF_CHUNK)
        up = _dot(h_ref[...], wup_ref[:, cols])
        if has_state:
            buf_ref[half, SUBLANES:SUBLANES + tm, :] = up
            prev = [jnp.where(after[j], buf_ref[half, SUBLANES - 1 - j:SUBLANES - 1 - j + tm, :], 0.0)
                    + old_ref[j, :, cols] for j in range(n_hist)]
            for s in range(tm // l):
                r0 = SUBLANES + (s + 1) * l - n_hist
                tail_ref[s, :, cols] = buf_ref[half, r0:r0 + n_hist, :]
        else:
            late = [up[tm - (j + 1) * SUBLANES:tm - j * SUBLANES, :] for j in range(n_hist)]
            first = [jnp.where(row8 == 0, pltpu.roll(hist_ref[c, half, j], 1, 0), pltpu.roll(late[j], 1, 0))
                     for j in range(n_hist)]
            for j in range(n_hist):
                hist_ref[c, half, j] = late[j]
            prev = [jnp.concatenate([first[0], up[0:tm - SUBLANES, :]], axis=0),
                    jnp.concatenate([first[1], first[0], up[0:tm - 2 * SUBLANES, :]], axis=0)]
            tail_ref[0, 0:1, cols] = up[tm - SUBLANES - 1:tm - SUBLANES, :]
            tail_ref[0, 1:2, cols] = up[tm - 1:tm, :]
        return (cb_ref[:, cols] + cw_ref[0:1, cols] * prev[1] + cw_ref[1:2, cols] * prev[0]
                + cw_ref[2:3, cols] * up)

    for c in range(n_chunks):
        act = _gelu(conv_half(c, 0)) * conv_half(c, 1)
        act_ref[:, c * FF_CHUNK:(c + 1) * FF_CHUNK] = act.astype(BF16)
    acc = _dot(act_ref[...], wdn_ref[...])
    if not has_state:
        acc = jnp.swapaxes(acc.reshape(groups, SUBLANES, D_MODEL), 0, 1).reshape(tm, D_MODEL)
    out = x + gt_ref[0] * acc
    if final_norm:
        ms = jnp.mean(out * out, axis=-1, keepdims=True)
        out = out * lax.rsqrt(ms + RMS_EPS) * fg_ref[...]
    _tile(o_ref)[...] = out


def _ffn_call(x, n, l, g, shift, scale, gate2, prm, final_g, conv_state, tm, final_norm):
    assert CONV_W == 3
    d = x.shape[-1]
    has_state = conv_state is not None
    sh_arr, sh_spec = _mod_operand(shift, l, tm)
    sc_arr, sc_spec = _mod_operand(scale, l, tm)
    gt_arr, gt_spec = _mod_operand(gate2, l, tm)
    shape, x_spec = _tok_layout(n, l, tm, d)
    in_specs = [x_spec, _const_spec((1, d)), sh_spec, sc_spec, gt_spec,
                _const_spec((d, 2 * D_FF)), _const_spec((CONV_W, 2 * D_FF)), _const_spec((1, 2 * D_FF)),
                _const_spec((D_FF, d)), _const_spec((1, d))]
    operands = [x, g.reshape(1, d), sh_arr, sc_arr, gt_arr, prm["w_up"], prm["conv_w"], prm["conv_b"],
                prm["w_down"], final_g.reshape(1, d)]
    scratch = [pltpu.VMEM((tm, d), BF16), pltpu.VMEM((tm, D_FF), BF16),
               pltpu.VMEM((D_FF // FF_CHUNK, 2, CONV_W - 1, SUBLANES, FF_CHUNK), F32)]
    if has_state:
        assert tm % l == 0 and l >= CONV_W - 1
        state_spec = pl.BlockSpec((tm // l, CONV_W - 1, 2 * D_FF), lambda i: (i, 0, 0))
        in_specs.append(state_spec)
        operands.append(conv_state)
        tail_spec = state_spec
        scratch += [pltpu.VMEM((2, tm + SUBLANES, FF_CHUNK), F32), pltpu.VMEM((CONV_W - 1, tm, 2 * D_FF), F32)]
    else:
        assert l % tm == 0
        per = l // tm
        tail_spec = pl.BlockSpec((1, CONV_W - 1, 2 * D_FF), lambda i: (i // per, 0, 0))
    kern = functools.partial(_ffn_kernel, l=l, tm=tm, has_state=has_state, final_norm=final_norm)
    return pl.pallas_call(
        kern,
        out_shape=[jax.ShapeDtypeStruct(shape, F32), jax.ShapeDtypeStruct((n, CONV_W - 1, 2 * D_FF), F32)],
        grid=(n * l // tm,),
        in_specs=in_specs,
        out_specs=[x_spec, tail_spec],
        scratch_shapes=scratch,
        compiler_params=_cparams("arbitrary"),
        name="conv_ffn",
    )(*operands)


def _prepare_layer(p):
    out = {}
    w_in = p["w_in"]
    c1 = S5_WIDTH
    c2 = c1 + RW_COLS
    c3 = c2 + 4 * HG_WIDTH
    out["w_in_a"] = w_in[:, :c1].astype(BF16)
    out["w_in_b"] = w_in[:, c1:c2].astype(BF16)
    out["w_in_c"] = w_in[:, c2:c3].astype(BF16)
    out["w_in_g"] = w_in[:, c3:].astype(BF16)
    out["w_ada"] = p["w_ada"].astype(BF16)
    for name in ("w_lift_a", "w_lift_b", "w_lift_c", "w_out", "w_up", "w_down"):
        out[name] = p[name].astype(BF16)
    out["conv_w"] = p["conv_w"]
    out["conv_b"] = p["conv_b"].reshape(1, 2 * D_FF)

    lr = p["s5_lambda_re"]
    li = p["s5_lambda_im"]
    dt = jnp.exp(p["s5_log_dt"])[:, None]
    mag = jnp.exp(lr * dt)
    ar = mag * jnp.cos(li * dt)
    ai = mag * jnp.sin(li * dt)
    den = lr * lr + li * li
    zr = ((ar - 1.0) * lr + ai * li) / den
    zi = (ai * lr - (ar - 1.0) * li) / den
    bbr = zr[..., None] * p["s5_b_re"] - zi[..., None] * p["s5_b_im"]
    bbi = zr[..., None] * p["s5_b_im"] + zi[..., None] * p["s5_b_re"]
    eye = jnp.eye(S5_GROUPS, dtype=F32)
    bmat = jnp.einsum("gh,rgpc->gcrhp", eye, jnp.stack([bbr, bbi])).reshape(S5_WIDTH, 2 * S5_FLAT)
    cmat = jnp.einsum("hg,rgcp->rhpgc", eye, jnp.stack([p["s5_c_re"], -p["s5_c_im"]])).reshape(
        2 * S5_FLAT, S5_WIDTH)
    out["s5_b"] = bmat.astype(BF16)
    out["s5_c"] = cmat.astype(BF16)
    out["s5_a"] = jnp.stack([ar.reshape(S5_FLAT), ai.reshape(S5_FLAT)])
    out["s5_d"] = p["s5_d"].reshape(1, S5_WIDTH)
    out["s5_w_glu"] = p["s5_w_glu"].astype(BF16)
    out["s5_b_glu"] = p["s5_b_glu"].reshape(1, S5_WIDTH)

    out["rw_mu"] = p["rw_mu"].reshape(1, RW_COLS)
    out["rw_vec"] = jnp.stack([p["rw_w0"], p["rw_a0"], p["rw_k_k"], p["rw_k_a"],
                               p["rw_r_k"].reshape(RW_WIDTH), p["rw_ln_w"], p["rw_ln_b"],
                               jnp.zeros((RW_WIDTH,), F32)])
    zw = jnp.zeros((RW_LORA, RW_WIDTH), F32)
    out["rw_w2"] = zw.at[0:RW_DECAY_LORA].set(p["rw_w2"]).astype(BF16)
    out["rw_a2"] = zw.at[RW_DECAY_LORA:RW_DECAY_LORA + RW_AAA_LORA].set(p["rw_a2"]).astype(BF16)
    out["rw_g2"] = zw.at[RW_DECAY_LORA + RW_AAA_LORA:].set(p["rw_g2"]).astype(BF16)
    out["hg_norm"] = p["hg_norm"]
    out["g_mix"] = p["g_mix"]
    out["g_ffn"] = p["g_ffn"]
    out["b_ada"] = p["b_ada"]
    return out


def _chunk_for(l, sub, t_chunk):
    return t_chunk if l % t_chunk == 0 else -(-l // sub) * sub


def _run_trunk(x, mods, st_s5, st_shift, st_rw, st_hg, st_conv, lower, final_g, prms, t_chunk):
    n, l, d = x.shape
    rows = n * l
    tm = min(ROW_TILE, rows)
    depth = len(prms)
    assert (l % tm == 0) == (l % t_chunk == 0)
    h = x.reshape(_tok_layout(n, l, tm, d)[0])
    out_s5, out_shift, out_rw, out_hg, out_conv = [], [], [], [], []
    for layer in range(depth):
        prm = prms[layer]
        sh1, sc1, gt1, sh2, sc2, gt2 = jnp.split(mods[layer], 6, axis=-1)
        za, zb, zc, gates = _in_proj_call(h, n, l, prm["g_mix"], sh1, sc1, prm["w_in_a"], prm["w_in_b"],
                                          prm["w_in_c"], prm["w_in_g"], tm)

        s5_in = jnp.concatenate([st_s5[layer][..., 0].reshape(n, S5_FLAT),
                                 st_s5[layer][..., 1].reshape(n, S5_FLAT)], axis=1)
        ya, s5_new = _s5_call(za, n, l, s5_in, prm, _chunk_for(l, SUBLANES, t_chunk))
        yb, shift_new, rw_new = _rw_call(zb, n, l, st_shift[layer].reshape(n, 1, RW_COLS), st_rw[layer], prm,
                                         _chunk_for(l, RW_SUB, t_chunk))
        yc, hg_new = _hg_call(zc, n, l, st_hg[layer], lower[layer], prm["hg_norm"],
                              _chunk_for(l, HG_SUB, t_chunk))

        h = _merge_call(h, n, l, ya, yb, yc, gates, gt1, prm, tm)
        ffn_tm = tm if st_conv is None else min(tm, 128)
        h, conv_new = _ffn_call(h, n, l, prm["g_ffn"], sh2, sc2, gt2, prm, final_g,
                                None if st_conv is None else st_conv[layer], ffn_tm,
                                final_norm=(layer == depth - 1))
        out_s5.append(jnp.stack([s5_new[:, :S5_FLAT].reshape(n, S5_GROUPS, S5_STATE),
                                 s5_new[:, S5_FLAT:].reshape(n, S5_GROUPS, S5_STATE)], axis=-1))
        out_shift.append(shift_new.reshape(n, RW_COLS))
        out_rw.append(rw_new)
        out_hg.append(hg_new)
        out_conv.append(conv_new)
    y = h.reshape(n, l, d).astype(x.dtype)
    return y, (jnp.stack(out_s5), jnp.stack(out_shift), jnp.stack(out_rw), jnp.stack(out_hg),
               jnp.stack(out_conv))


def kernel(x_prompt, x_sample, c_prompt, c_sample, state_s5, state_rwkv_shift, state_rwkv, state_hgrn, state_ffn_conv, w_ada, b_ada, g_mix, g_ffn, w_in, s5_lambda_re, s5_lambda_im, s5_log_dt, s5_b_re, s5_b_im, s5_c_re, s5_c_im, s5_d, s5_w_glu, s5_b_glu, rw_mu, rw_w0, rw_w2, rw_a0, rw_a2, rw_g2, rw_k_k, rw_k_a, rw_r_k, rw_ln_w, rw_ln_b, hg_lb, hg_norm, w_lift_a, w_lift_b, w_lift_c, w_out, w_up, conv_w, conv_b, w_down, final_g):
    per_layer = {
        "w_ada": w_ada, "b_ada": b_ada, "g_mix": g_mix, "g_ffn": g_ffn, "w_in": w_in,
        "s5_lambda_re": s5_lambda_re, "s5_lambda_im": s5_lambda_im, "s5_log_dt": s5_log_dt,
        "s5_b_re": s5_b_re, "s5_b_im": s5_b_im, "s5_c_re": s5_c_re, "s5_c_im": s5_c_im,
        "s5_d": s5_d, "s5_w_glu": s5_w_glu, "s5_b_glu": s5_b_glu,
        "rw_mu": rw_mu, "rw_w0": rw_w0, "rw_w2": rw_w2, "rw_a0": rw_a0, "rw_a2": rw_a2,
        "rw_g2": rw_g2, "rw_k_k": rw_k_k, "rw_k_a": rw_k_a, "rw_r_k": rw_r_k,
        "rw_ln_w": rw_ln_w, "rw_ln_b": rw_ln_b, "hg_norm": hg_norm,
        "w_lift_a": w_lift_a, "w_lift_b": w_lift_b, "w_lift_c": w_lift_c, "w_out": w_out,
        "w_up": w_up, "conv_w": conv_w, "conv_b": conv_b, "w_down": w_down,
    }
    depth = w_ada.shape[0]
    prms = [_prepare_layer({k: v[layer] for k, v in per_layer.items()}) for layer in range(depth)]

    lbp = jax.nn.softmax(hg_lb.astype(F32), axis=0)
    lower = jnp.cumsum(lbp, axis=0) - lbp[0]

    nb = x_prompt.shape[0]
    ns = x_sample.shape[0]
    c_all = jnp.concatenate([c_prompt, c_sample], axis=0).astype(F32)
    mods = [_ada_call(c_all, prms[layer]["w_ada"], prms[layer]["b_ada"]) for layer in range(depth)]
    mods_p = [m[:nb] for m in mods]
    mods_s = [m[nb:] for m in mods]

    z_s5 = jnp.zeros((depth, nb) + state_s5.shape[2:], F32)
    z_shift = jnp.zeros((depth, nb) + state_rwkv_shift.shape[2:], F32)
    z_rw = jnp.zeros((depth, nb) + state_rwkv.shape[2:], F32)
    z_hg = jnp.zeros((depth, nb) + state_hgrn.shape[2:], F32)

    y_prompt, (s5_p, shift_p, rw_p, hg_p, conv_p) = _run_trunk(
        x_prompt, mods_p, z_s5, z_shift, z_rw, z_hg, None, lower, final_g, prms, t_chunk=64)
    y_sample, (s5_s, shift_s, rw_s, hg_s, conv_s) = _run_trunk(
        x_sample, mods_s, state_s5, state_rwkv_shift, state_rwkv, state_hgrn, state_ffn_conv,
        lower, final_g, prms, t_chunk=64)
    return (y_prompt, y_sample, s5_p, shift_p, rw_p, hg_p, conv_p, s5_s, shift_s, rw_s, hg_s, conv_s)
```

```python
import functools
import math

import jax
import jax.numpy as jnp
from jax import lax
from jax.experimental import pallas as pl
from jax.experimental.pallas import tpu as pltpu

F32 = jnp.float32
BF16 = jnp.bfloat16

D_MODEL = 1024
S5_WIDTH = D_MODEL // 4
S5_GROUP = 16
S5_GROUPS = S5_WIDTH // S5_GROUP
S5_STATE = 64
S5_FLAT = S5_GROUPS * S5_STATE
RW_WIDTH = D_MODEL // 2
RW_HEAD = 64
RW_HEADS = RW_WIDTH // RW_HEAD
RW_DECAY_LORA = 32
RW_AAA_LORA = 32
RW_GATE_LORA = 64
RW_LORA = RW_DECAY_LORA + RW_AAA_LORA + RW_GATE_LORA
RW_COLS = 3 * RW_WIDTH + RW_LORA
RW_GN_EPS = 1e-5 * RW_HEAD
HG_WIDTH = D_MODEL // 4
HG_HEAD = 64
HG_HEADS = HG_WIDTH // HG_HEAD
HG_GATE_FLOOR = 1e-30
N_BRANCH = 3
D_FF = 256 * ((8 * D_MODEL // 3 + 255) // 256)
CONV_W = 3
RMS_EPS = 1e-6

LANES = 128
SUBLANES = 8
MXU_COLS = 256
VMEM_LIMIT_BYTES = 56 * 1024 * 1024

SEQ_BLOCK = SUBLANES
HEAD_PAIR = LANES // RW_HEAD
ROW_TILE = 512
FF_CHUNK = 256
IN_PROJ_CHUNK = 512
RW_SUB = 16
RW_PAIRS = RW_HEADS // HEAD_PAIR
HG_SUB = 16
HG_PAIRS = HG_HEADS // HEAD_PAIR
RW_SOLVE_PASSES = 1
RW_APPLY_PASSES = 1
RW_STATE_PASSES = 1

_NN = (((1,), (0,)), ((), ()))
_NT = (((1,), (1,)), ((), ()))


def _cparams(*sem):
    return pltpu.CompilerParams(dimension_semantics=sem, vmem_limit_bytes=VMEM_LIMIT_BYTES)


def _const_spec(shape):
    nd = len(shape)
    return pl.BlockSpec(shape, lambda *_: (0,) * nd, pipeline_mode=pl.Buffered(1))


def _dot(a, b):
    return jnp.dot(a, b, preferred_element_type=F32)


def _split_bf16(x):
    hi = x.astype(BF16)
    lo = (x - hi.astype(F32)).astype(BF16)
    return hi, lo


def _head_ones():
    r = lax.broadcasted_iota(jnp.int32, (LANES, LANES), 0) // RW_HEAD
    c = lax.broadcasted_iota(jnp.int32, (LANES, LANES), 1) // RW_HEAD
    return (r == c).astype(BF16)


def _head_sum(x, ones):
    hi, lo = _split_bf16(x)
    tiles = [_dot(hi[:, c:c + LANES], ones) + _dot(lo[:, c:c + LANES], ones)
             for c in range(0, x.shape[1], LANES)]
    return tiles[0] if len(tiles) == 1 else jnp.concatenate(tiles, axis=1)


def _softplus(x):
    return jnp.maximum(x, 0.0) + jnp.log1p(jnp.exp(-jnp.abs(x)))


def _gelu(x):
    c = 2.0 * math.sqrt(2.0 / math.pi)
    return x * jax.nn.sigmoid(x * (c + (c * 0.044715) * (x * x)))


def _rms_modulate(x, g, shift, scale):
    ms = jnp.mean(x * x, axis=-1, keepdims=True)
    return x * lax.rsqrt(ms + RMS_EPS) * g * (1.0 + scale) + shift


def _seq_specs(n, l, t_chunk, widths_in, widths_out):
    nb = n // SEQ_BLOCK
    if l % t_chunk == 0:
        spec = lambda w: pl.BlockSpec((SEQ_BLOCK, t_chunk, w), lambda b, c: (b, c, 0))
        shape = lambda w: (n, l, w)
        nt = l // t_chunk
    else:
        assert l < t_chunk
        spec = lambda w: pl.BlockSpec((SEQ_BLOCK * l, w), lambda b, c: (b, 0))
        shape = lambda w: (n * l, w)
        nt = 1
    return ((nb, nt), [spec(w) for w in widths_in], [spec(w) for w in widths_out],
            [shape(w) for w in widths_in], [shape(w) for w in widths_out])


def _branch_dtype(shape):
    return BF16 if len(shape) == 3 else F32


def _seq_block_load(z_ref, pad_ref):
    if len(z_ref.shape) == 3:
        return z_ref[...]
    l = z_ref.shape[0] // SEQ_BLOCK
    pad_ref[...] = jnp.zeros_like(pad_ref)
    for n in range(SEQ_BLOCK):
        pad_ref[n, 0:l, :] = z_ref[n * l:(n + 1) * l, :]
    return pad_ref[...]


def _seq_block_store(y_ref, stage_ref, y3):
    if len(y_ref.shape) == 3:
        y_ref[...] = y3.astype(y_ref.dtype)
        return
    l = y_ref.shape[0] // SEQ_BLOCK
    stage_ref[...] = y3
    for n in range(SEQ_BLOCK):
        y_ref[n * l:(n + 1) * l, :] = stage_ref[n, 0:l, :]


def _heads_to_tiles(s_ref, tile_ref, n_pairs, transpose):
    zero = jnp.zeros((RW_HEAD, RW_HEAD), F32)
    for n in range(SEQ_BLOCK):
        for hp in range(n_pairs):
            a, b = s_ref[n, HEAD_PAIR * hp], s_ref[n, HEAD_PAIR * hp + 1]
            if transpose:
                a, b = a.T, b.T
            tile_ref[hp, n] = jnp.concatenate([jnp.concatenate([a, zero], axis=1),
                                               jnp.concatenate([zero, b], axis=1)], axis=0)


def _tiles_to_heads(tile_ref, s_ref, n_pairs, transpose):
    for n in range(SEQ_BLOCK):
        for hp in range(n_pairs):
            t = tile_ref[hp, n]
            a, b = t[0:RW_HEAD, 0:RW_HEAD], t[RW_HEAD:LANES, RW_HEAD:LANES]
            if transpose:
                a, b = a.T, b.T
            s_ref[n, HEAD_PAIR * hp] = a
            s_ref[n, HEAD_PAIR * hp + 1] = b


def _ada_kernel(c_ref, w_ref, b_ref, o_ref):
    c = c_ref[...]
    o_ref[...] = _dot(jax.nn.silu(c).astype(BF16), w_ref[...]) + b_ref[...]


def _ada_call(c, w_bf16, b):
    n, d = c.shape
    cols = w_bf16.shape[1]
    tn = cols // 4
    return pl.pallas_call(
        _ada_kernel,
        out_shape=jax.ShapeDtypeStruct((n, cols), F32),
        grid=(cols // tn,),
        in_specs=[_const_spec((n, d)),
                  pl.BlockSpec((d, tn), lambda j: (0, j)),
                  pl.BlockSpec((1, tn), lambda j: (0, j))],
        out_specs=pl.BlockSpec((n, tn), lambda j: (0, j)),
        compiler_params=_cparams("arbitrary"),
        name="ada_mod",
    )(c, w_bf16, b.reshape(1, cols))


def _mod_operand(m, l, tm):
    n, d = m.shape
    if l % tm == 0:
        per = l // tm
        return m.reshape(n, 1, d), pl.BlockSpec((1, 1, d), lambda i: (i // per, 0, 0))
    assert tm % l == 0
    return jnp.repeat(m, l, axis=0).reshape(1, n * l, d), pl.BlockSpec((1, tm, d), lambda i: (0, i, 0))


def _tok_layout(n, l, tm, w):
    if l % tm == 0:
        per = l // tm
        return (n, l, w), pl.BlockSpec((1, tm, w), lambda i: (i // per, i % per, 0))
    return (n * l, w), pl.BlockSpec((tm, w), lambda i: (i, 0))


def _tile(ref):
    return ref.at[0] if len(ref.shape) == 3 else ref


def _in_proj_kernel(x_ref, g_ref, sh_ref, sc_ref, wa_ref, wb_ref, wc_ref, wg_ref,
                    oa_ref, ob_ref, oc_ref, og_ref):
    h = _rms_modulate(_tile(x_ref)[...], g_ref[...], sh_ref[0], sc_ref[0]).astype(BF16)

    def project(w_ref, o_ref, fn=None):
        width = w_ref.shape[1]
        out = _tile(o_ref)
        for c0 in range(0, width, IN_PROJ_CHUNK):
            cols = slice(c0, min(c0 + IN_PROJ_CHUNK, width))
            z = _dot(h, w_ref[:, cols])
            out[:, cols] = (z if fn is None else fn(z)).astype(out.dtype)

    project(wa_ref, oa_ref)
    project(wb_ref, ob_ref)
    project(wc_ref, oc_ref)
    project(wg_ref, og_ref, jax.nn.sigmoid)


def _in_proj_call(x, n, l, g, shift, scale, w_a, w_b, w_c, w_g, tm):
    d = x.shape[-1]
    sh_arr, sh_spec = _mod_operand(shift, l, tm)
    sc_arr, sc_spec = _mod_operand(scale, l, tm)
    widths = [w.shape[1] for w in (w_a, w_b, w_c, w_g)]
    dtypes = [F32, F32, F32, BF16]
    outs = [_tok_layout(n, l, tm, w) for w in widths]
    return pl.pallas_call(
        _in_proj_kernel,
        out_shape=[jax.ShapeDtypeStruct(shape, dt) for (shape, _), dt in zip(outs, dtypes)],
        grid=(n * l // tm,),
        in_specs=[_tok_layout(n, l, tm, d)[1], _const_spec((1, d)), sh_spec, sc_spec]
        + [_const_spec(w.shape) for w in (w_a, w_b, w_c, w_g)],
        out_specs=[spec for _, spec in outs],
        compiler_params=_cparams("arbitrary"),
        name="in_proj",
    )(x, g.reshape(1, d), sh_arr, sc_arr, w_a, w_b, w_c, w_g)


def _s5_kernel(u_ref, s0_ref, b_ref, c_ref, a_ref, d_ref, wg_ref, bg_ref,
               y_ref, sl_ref, e_ref, st_ref, pad_ref, stage_ref, *, t_chunk, n_steps):
    @pl.when(pl.program_id(1) == 0)
    def _():
        st_ref[...] = s0_ref[...]

    rows = SEQ_BLOCK * t_chunk
    ut = jnp.swapaxes(_seq_block_load(u_ref, pad_ref), 0, 1).reshape(rows, S5_WIDTH)
    ub = ut.astype(BF16)
    for c0 in range(0, 2 * S5_FLAT, MXU_COLS):
        ch0 = (c0 % S5_FLAT) // S5_STATE * S5_GROUP
        k0 = ch0 // LANES * LANES
        assert ch0 + MXU_COLS // S5_STATE * S5_GROUP <= k0 + LANES
        e_ref[:, c0:c0 + MXU_COLS] = _dot(ub[:, k0:k0 + LANES], b_ref[k0:k0 + LANES, c0:c0 + MXU_COLS])

    ar = jnp.broadcast_to(a_ref[0:1, :], (SEQ_BLOCK, S5_FLAT))
    ai = jnp.broadcast_to(a_ref[1:2, :], (SEQ_BLOCK, S5_FLAT))

    def step(t, carry):
        sr, si = carry
        r = pl.ds(pl.multiple_of(t * SEQ_BLOCK, SEQ_BLOCK), SEQ_BLOCK)
        nr = ar * sr - ai * si + e_ref[r, 0:S5_FLAT]
        ni = ar * si + ai * sr + e_ref[r, S5_FLAT:2 * S5_FLAT]
        e_ref[r, 0:S5_FLAT] = nr
        e_ref[r, S5_FLAT:2 * S5_FLAT] = ni
        return nr, ni

    sr, si = lax.fori_loop(0, n_steps, step, (st_ref[:, 0:S5_FLAT], st_ref[:, S5_FLAT:2 * S5_FLAT]))
    st_ref[:, 0:S5_FLAT] = sr
    st_ref[:, S5_FLAT:2 * S5_FLAT] = si
    sl_ref[...] = st_ref[...]

    y = _dot(e_ref[...].astype(BF16), c_ref[...]) + d_ref[...] * ut
    y = _gelu(y)
    y = y * jax.nn.sigmoid(_dot(y.astype(BF16), wg_ref[...]) + bg_ref[...])
    _seq_block_store(y_ref, stage_ref, jnp.swapaxes(y.reshape(t_chunk, SEQ_BLOCK, S5_WIDTH), 0, 1))


def _io_scratch(l, t_chunk, width_in, width_out):
    if l % t_chunk == 0:
        return [pltpu.VMEM((SUBLANES, LANES), F32)] * 2
    return [pltpu.VMEM((SEQ_BLOCK, t_chunk, width_in), F32), pltpu.VMEM((SEQ_BLOCK, t_chunk, width_out), F32)]


def _s5_call(u2, n, l, s0, prm, t_chunk):
    grid, in_specs, out_specs, in_shapes, out_shapes = _seq_specs(n, l, t_chunk, [S5_WIDTH], [S5_WIDTH])
    kern = functools.partial(_s5_kernel, t_chunk=t_chunk, n_steps=min(t_chunk, l))
    y, s_new = pl.pallas_call(
        kern,
        out_shape=[jax.ShapeDtypeStruct(out_shapes[0], _branch_dtype(out_shapes[0])),
                   jax.ShapeDtypeStruct((n, 2 * S5_FLAT), F32)],
        grid=grid,
        in_specs=in_specs + [
            pl.BlockSpec((SEQ_BLOCK, 2 * S5_FLAT), lambda b, c: (b, 0)),
            _const_spec((S5_WIDTH, 2 * S5_FLAT)), _const_spec((2 * S5_FLAT, S5_WIDTH)),
            _const_spec((2, S5_FLAT)), _const_spec((1, S5_WIDTH)),
            _const_spec((S5_WIDTH, S5_WIDTH)), _const_spec((1, S5_WIDTH))],
        out_specs=out_specs + [pl.BlockSpec((SEQ_BLOCK, 2 * S5_FLAT), lambda b, c: (b, 0))],
        scratch_shapes=[pltpu.VMEM((SEQ_BLOCK * t_chunk, 2 * S5_FLAT), F32),
                        pltpu.VMEM((SEQ_BLOCK, 2 * S5_FLAT), F32)]
        + _io_scratch(l, t_chunk, S5_WIDTH, S5_WIDTH),
        compiler_params=_cparams("arbitrary", "arbitrary"),
        name="s5_mixer",
    )(u2.reshape(in_shapes[0]), s0, prm["s5_b"], prm["s5_c"], prm["s5_a"], prm["s5_d"], prm["s5_w_glu"],
      prm["s5_b_glu"])
    return y, s_new


def _mm(a, b, dims, passes):
    dg = lambda x, y: lax.dot_general(x, y, dims, preferred_element_type=F32)
    if passes == 1:
        return dg(a.astype(BF16), b.astype(BF16))
    a_hi, a_lo = _split_bf16(a)
    if passes == 2:
        b_hi = b.astype(BF16)
        return dg(a_hi, b_hi) + dg(a_lo, b_hi)
    b_hi, b_lo = _split_bf16(b)
    return dg(a_hi, b_hi) + dg(a_lo, b_hi) + dg(a_hi, b_lo)


def _cumsum_groups(x, group):
    pos = lax.broadcasted_iota(jnp.int32, (x.shape[0], 1), 0) % group
    s = 1
    while s < group:
        x = x + jnp.where(pos >= s, pltpu.roll(x, s, 0), 0.0)
        s *= 2
    return x


def _rw_kernel(p_ref, sh0_ref, s0_ref, mu_ref, vec_ref, w2_ref, a2_ref, g2_ref, *rest,
               t_chunk, l_valid, last_row, layer):
    earlier_ref = rest[0] if layer else None
    (y_ref, shl_ref, sl_ref,
     pbuf_ref, s_ref, at_ref, bt_ref, kt_ref, rt_ref, bh_ref, kh_ref, v_ref, gc_ref, ys_ref,
     pad_ref, stage_ref) = rest[1 if layer else 0:]
    rows = SEQ_BLOCK * t_chunk
    n_sub = t_chunk // RW_SUB

    @pl.when(pl.program_id(1) == 0)
    def _():
        pbuf_ref[:, SUBLANES - 1:SUBLANES, :] = sh0_ref[...]
        _heads_to_tiles(s0_ref, s_ref, RW_PAIRS, transpose=False)

    p3 = _seq_block_load(p_ref, pad_ref)
    pbuf_ref[:, SUBLANES:SUBLANES + t_chunk, :] = p3
    prev3 = pbuf_ref[:, SUBLANES - 1:SUBLANES - 1 + t_chunk, :]
    last = pbuf_ref[:, SUBLANES + last_row:SUBLANES + last_row + 1, :]
    pbuf_ref[:, SUBLANES - 1:SUBLANES, :] = last
    shl_ref[...] = last

    xm = (p3 + (prev3 - p3) * mu_ref[...]).reshape(rows, RW_COLS)
    r = xm[:, 0:RW_WIDTH]
    k = xm[:, RW_WIDTH:2 * RW_WIDTH]
    v = xm[:, 2 * RW_WIDTH:3 * RW_WIDTH]
    lora = xm[:, 3 * RW_WIDTH:RW_COLS]
    w0, a0, k_k, k_a = vec_ref[0:1, :], vec_ref[1:2, :], vec_ref[2:3, :], vec_ref[3:4, :]
    r_k, ln_w, ln_b = vec_ref[4:5, :], vec_ref[5:6, :], vec_ref[6:7, :]

    w = -_softplus(-(w0 + _dot(jnp.tanh(lora).astype(BF16), w2_ref[...]))) - 0.5
    log_decay = -jnp.exp(w)
    a = jax.nn.sigmoid(a0 + _dot(lora.astype(BF16), a2_ref[...]))
    g = _dot(jax.nn.sigmoid(lora).astype(BF16), g2_ref[...])

    ones = _head_ones()
    kk = k * k_k
    kk = kk / jnp.maximum(jnp.sqrt(_head_sum(kk * kk, ones)), 1e-12)
    kt = k * (1.0 + (a - 1.0) * k_a)
    kka = kk * a
    if l_valid < t_chunk:
        live = lax.broadcasted_iota(jnp.int32, (rows, 1), 0) % t_chunk < l_valid
        log_decay = jnp.where(live, log_decay, 0.0)
        kk = jnp.where(live, kk, 0.0)
        kka = jnp.where(live, kka, 0.0)
        kt_live = jnp.where(live, kt, 0.0)
    else:
        kt_live = kt

    cum = _cumsum_groups(log_decay, RW_SUB)
    shape_g = (rows // RW_SUB, RW_SUB, RW_WIDTH)
    cum_end = jnp.broadcast_to(cum.reshape(shape_g)[:, RW_SUB - 1:RW_SUB, :], shape_g).reshape(rows, RW_WIDTH)
    g_in = jnp.exp(cum)
    g_inv = jnp.exp(-cum)
    g_ex = jnp.exp(cum - log_decay)
    g_out = jnp.exp(cum_end - cum)
    shape3 = (SEQ_BLOCK, t_chunk, RW_WIDTH)
    at_ref[...] = (-kk * g_ex).reshape(shape3)
    bt_ref[...] = (kka * g_inv).reshape(shape3)
    kt_ref[...] = (kt_live * g_inv).reshape(shape3)
    rt_ref[...] = (r * g_in).reshape(shape3)
    bh_ref[...] = (kka * g_out).reshape(shape3)
    kh_ref[...] = (kt_live * g_out).reshape(shape3)
    v_ref[...] = v.reshape(shape3)
    gc_ref[...] = jnp.exp(cum_end).reshape(shape3)

    m_rows = SEQ_BLOCK * RW_SUB
    ri = lax.broadcasted_iota(jnp.int32, (m_rows, m_rows), 0)
    ci = lax.broadcasted_iota(jnp.int32, (m_rows, m_rows), 1)
    same_seq = (ri // RW_SUB) == (ci // RW_SUB)
    before = same_seq & ((ci % RW_SUB) < (ri % RW_SUB))
    upto = same_seq & ((ci % RW_SUB) <= (ri % RW_SUB))
    eye = (ri == ci).astype(F32)
    pair_mask = (ri // 2) == (ci // 2)
    level_masks = []
    s = 2
    while s < RW_SUB:
        level_masks.append(((ri // (2 * s)) == (ci // (2 * s))) & ((ri // s) != (ci // s)))
        s *= 2
    lane = lax.broadcasted_iota(jnp.int32, (m_rows, LANES), 1)
    head_mask = [lane < RW_HEAD, lane >= RW_HEAD]
    wide = (m_rows, SEQ_BLOCK * LANES)
    own_rows = (lax.broadcasted_iota(jnp.int32, wide, 1) // LANES
                == lax.broadcasted_iota(jnp.int32, wide, 0) // RW_SUB)
    same_head = (lax.broadcasted_iota(jnp.int32, wide, 0) // RW_HEAD
                 == (lax.broadcasted_iota(jnp.int32, wide, 1) % LANES) // RW_HEAD)
    group = max(g for g in (1, 2, 4) if n_sub % g == 0)
    units = [(j, hp) for j in range(group) for hp in range(RW_PAIRS)]
    heads = [(u, h) for u in range(len(units)) for h in range(HEAD_PAIR)]

    def sub_chunk(c, carry):
        rs = [pl.ds(pl.multiple_of((c * group + j) * RW_SUB, RW_SUB), RW_SUB) for j in range(group)]

        def ld(ref, j, hp):
            return ref[:, rs[j], pl.ds(hp * LANES, LANES)].reshape(m_rows, LANES)

        at, bt, ktl, rt, bh, kh, vv, gcv = ([ld(ref, j, hp) for j, hp in units]
                                            for ref in (at_ref, bt_ref, kt_ref, rt_ref, bh_ref, kh_ref,
                                                        v_ref, gc_ref))
        state = [[s_ref[hp, n] for n in range(SEQ_BLOCK)] for hp in range(RW_PAIRS)]
        zero = jnp.zeros((m_rows, LANES), F32)
        msk = lambda x, h: jnp.where(head_mask[h], x, zero)

        gram = [_mm(jnp.concatenate([msk(at[hp], 0), msk(rt[hp], 0), msk(at[hp], 1), msk(rt[hp], 1)], axis=0),
                    jnp.concatenate([bt[hp], ktl[hp]], axis=0), _NT, RW_SOLVE_PASSES)
                for hp in range(len(units))]

        def quad(hp, h, row, col, keep):
            blk = gram[hp][(2 * h + row) * m_rows:(2 * h + row + 1) * m_rows, col * m_rows:(col + 1) * m_rows]
            return jnp.where(keep, blk, 0.0)

        m_ab = [quad(hp, h, 0, 0, before) for hp, h in heads]
        m_ak = [quad(hp, h, 0, 1, before) for hp, h in heads]
        n_rb = [quad(hp, h, 1, 0, upto) for hp, h in heads]
        n_rk = [quad(hp, h, 1, 1, upto) for hp, h in heads]
        tinv = [eye + jnp.where(pair_mask, m, 0.0) for m in m_ab]
        for lm in level_masks:
            prod = [_mm(jnp.where(lm, m, 0.0), t, _NN, RW_SOLVE_PASSES) for m, t in zip(m_ab, tinv)]
            tinv = [t + _mm(t, q, _NN, RW_SOLVE_PASSES) for t, q in zip(tinv, prod)]
        vh = [msk(vv[hp], h) for hp, h in heads]
        w1 = [_mm(m, x, _NN, RW_APPLY_PASSES) for m, x in zip(m_ak, vh)]
        ap = [_mm(t, jnp.concatenate([msk(at[hp], h), w], axis=1), _NN, RW_APPLY_PASSES)
              for t, (hp, h), w in zip(tinv, heads, w1)]
        nb = [_mm(m, x, _NN, RW_APPLY_PASSES) for m, x in zip(n_rb, ap)]
        nv = [_mm(m, x, _NN, RW_APPLY_PASSES) for m, x in zip(n_rk, vh)]

        def per_seq(x):
            return jnp.where(own_rows, jnp.concatenate([x] * SEQ_BLOCK, axis=1), 0.0)

        pairs = range(len(units))
        both = lambda xs, hp, cols: xs[HEAD_PAIR * hp][:, cols] + xs[HEAD_PAIR * hp + 1][:, cols]
        lo, hi = slice(0, LANES), slice(LANES, 2 * LANES)
        bh_x = [per_seq(bh[hp]) for hp in pairs]
        kh_x = [per_seq(kh[hp]) for hp in pairs]
        gam = [jnp.where(same_head, _mm(both(ap, hp, lo).T, bh_x[hp], _NN, RW_STATE_PASSES), 0.0) for hp in pairs]
        u = [jnp.where(same_head, _mm(jnp.concatenate([both(ap, hp, hi).T, vv[hp].T], axis=1),
                                      jnp.concatenate([bh_x[hp], kh_x[hp]], axis=0), _NN, RW_STATE_PASSES), 0.0)
             for hp in pairs]
        for hp, (j, pair) in enumerate(units):
            r_hat = rt[hp] + both(nb, hp, lo)
            y_zero = both(nb, hp, hi) + nv[HEAD_PAIR * hp] + nv[HEAD_PAIR * hp + 1]
            cols = pl.ds(pair * LANES, LANES)
            for n in range(SEQ_BLOCK):
                q = slice(n * RW_SUB, (n + 1) * RW_SUB)
                blk = slice(n * LANES, (n + 1) * LANES)
                st = state[pair][n]
                ys_ref[n, rs[j], cols] = _mm(r_hat[q], st, _NT, RW_STATE_PASSES) + y_zero[q]
                state[pair][n] = (st * gcv[hp][n * RW_SUB:n * RW_SUB + 1, :]
                                  + _mm(st, gam[hp][:, blk], _NN, RW_STATE_PASSES) + u[hp][:, blk])
        for pair in range(RW_PAIRS):
            for n in range(SEQ_BLOCK):
                s_ref[pair, n] = state[pair][n]
        return carry

    lax.fori_loop(0, n_sub // group, sub_chunk, 0)

    @pl.when(pl.program_id(1) == pl.num_programs(1) - 1)
    def _():
        if layer:
            sl_ref[0:layer] = earlier_ref[...]
        _tiles_to_heads(s_ref, sl_ref.at[layer], RW_PAIRS, transpose=False)

    y = ys_ref[...].reshape(rows, RW_WIDTH)
    inv = 1.0 / RW_HEAD
    mean = _head_sum(y, ones) * inv
    yc = y - mean
    var = _head_sum(yc * yc, ones) * inv
    y = yc * lax.rsqrt(var + RW_GN_EPS) * ln_w + ln_b
    bonus = _head_sum(r * kt * r_k, ones) * v
    _seq_block_store(y_ref, stage_ref, ((y + bonus) * g).reshape(shape3))


def _layer_state_specs(states, layer):
    tail = states.shape[2:]
    zeros = (0,) * len(tail)
    read = pl.BlockSpec((None, SEQ_BLOCK) + tail, lambda b, c: (layer, b) + zeros)
    upto = lambda k: pl.BlockSpec((k, SEQ_BLOCK) + tail, lambda b, c: (0, b) + zeros)
    return read, upto(layer + 1), ([upto(layer)] if layer else [])


def _rw_call(p2, n, l, shift0, states, layer, earlier, prm, t_chunk):
    assert t_chunk % RW_SUB == 0
    grid, in_specs, out_specs, in_shapes, out_shapes = _seq_specs(n, l, t_chunk, [RW_COLS], [RW_WIDTH])
    valid_in_chunk = min(t_chunk, l)
    kern = functools.partial(_rw_kernel, t_chunk=t_chunk, l_valid=valid_in_chunk, last_row=valid_in_chunk - 1,
                             layer=layer)
    st_in, st_out, st_earlier = _layer_state_specs(states, layer)
    sh_spec = pl.BlockSpec((SEQ_BLOCK, 1, RW_COLS), lambda b, c: (b, 0, 0))
    q_scr = pltpu.VMEM((SEQ_BLOCK, t_chunk, RW_WIDTH), F32)
    return pl.pallas_call(
        kern,
        out_shape=[jax.ShapeDtypeStruct(out_shapes[0], _branch_dtype(out_shapes[0])),
                   jax.ShapeDtypeStruct((n, 1, RW_COLS), F32),
                   jax.ShapeDtypeStruct((layer + 1,) + states.shape[1:], F32)],
        grid=grid,
        in_specs=in_specs + [sh_spec, st_in,
                             _const_spec((1, RW_COLS)), _const_spec((SUBLANES, RW_WIDTH)),
                             _const_spec((RW_LORA, RW_WIDTH)), _const_spec((RW_LORA, RW_WIDTH)),
                             _const_spec((RW_LORA, RW_WIDTH))] + st_earlier,
        out_specs=out_specs + [sh_spec, st_out],
        scratch_shapes=[pltpu.VMEM((SEQ_BLOCK, t_chunk + SUBLANES, RW_COLS), F32),
                        pltpu.VMEM((RW_PAIRS, SEQ_BLOCK, LANES, LANES), F32)] + [q_scr] * 9
        + _io_scratch(l, t_chunk, RW_COLS, RW_WIDTH),
        compiler_params=_cparams("arbitrary", "arbitrary"),
        name="rwkv7_mixer",
    )(p2.reshape(in_shapes[0]), shift0, states, prm["rw_mu"], prm["rw_vec"], prm["rw_w2"], prm["rw_a2"],
      prm["rw_g2"], *([earlier] if layer else []))


def _hg_kernel(z_ref, s0_ref, lower_ref, ng_ref, *rest, t_chunk, l_valid, layer):
    earlier_ref = rest[0] if layer else None
    (y_ref, sl_ref, s_ref, qs_ref, q_ref, k_ref, kh_ref, v_ref, b_ref, gc_ref, ys_ref, pad_ref,
     stage_ref) = rest[1 if layer else 0:]
    rows = SEQ_BLOCK * t_chunk
    n_sub = t_chunk // HG_SUB

    @pl.when(pl.program_id(1) == 0)
    def _():
        _heads_to_tiles(s0_ref, s_ref, HG_PAIRS, transpose=True)

    z = _seq_block_load(z_ref, pad_ref).reshape(rows, 4 * HG_WIDTH)
    q = jax.nn.silu(z[:, 0:HG_WIDTH])
    f = z[:, HG_WIDTH:2 * HG_WIDTH]
    i = z[:, 2 * HG_WIDTH:3 * HG_WIDTH]
    og = z[:, 3 * HG_WIDTH:4 * HG_WIDTH]
    lower = lower_ref[...]
    fgate = lower + (1.0 - lower) * jax.nn.sigmoid(f)
    log_f = jnp.log(jnp.maximum(fgate, HG_GATE_FLOOR))
    k = 1.0 - fgate
    if l_valid < t_chunk:
        live = lax.broadcasted_iota(jnp.int32, (rows, 1), 0) % t_chunk < l_valid
        log_f = jnp.where(live, log_f, 0.0)
        k = jnp.where(live, k, 0.0)
    cum = _cumsum_groups(log_f, HG_SUB)
    shape_g = (rows // HG_SUB, HG_SUB, HG_WIDTH)
    cum_end = jnp.broadcast_to(cum.reshape(shape_g)[:, HG_SUB - 1:HG_SUB, :], shape_g).reshape(rows, HG_WIDTH)
    shape3 = (SEQ_BLOCK, t_chunk, HG_WIDTH)
    qs_ref[...] = (q * jnp.exp(cum)).reshape(shape3)
    q_ref[...] = q.reshape(shape3)
    k_ref[...] = k.reshape(shape3)
    kh_ref[...] = (k * jnp.exp(cum_end - cum)).reshape(shape3)
    v_ref[...] = i.reshape(shape3)
    b_ref[...] = cum.reshape(shape3)
    gc_ref[...] = jnp.exp(cum_end).reshape(shape3)

    m_rows = SEQ_BLOCK * HG_SUB
    ones = _head_ones()
    wide = (m_rows, SEQ_BLOCK * LANES)
    own_rows = (lax.broadcasted_iota(jnp.int32, wide, 1) // LANES
                == lax.broadcasted_iota(jnp.int32, wide, 0) // HG_SUB)
    same_head = (lax.broadcasted_iota(jnp.int32, wide, 0) // HG_HEAD
                 == (lax.broadcasted_iota(jnp.int32, wide, 1) % LANES) // HG_HEAD)
    step = lax.broadcasted_iota(jnp.int32, (HG_SUB, LANES), 0)

    def sub_chunk(c, carry):
        rs = pl.ds(pl.multiple_of(c * HG_SUB, HG_SUB), HG_SUB)
        for hp in range(HG_PAIRS):
            cols = pl.ds(hp * LANES, LANES)
            ld = lambda ref: ref[:, rs, cols].reshape(m_rows, LANES)
            qs, qq, kk, kh, vv, bb, gcv = (ld(qs_ref), ld(q_ref), ld(k_ref), ld(kh_ref), ld(v_ref),
                                           ld(b_ref), ld(gc_ref))
            state = [s_ref[hp, n] for n in range(SEQ_BLOCK)]
            kh_x = jnp.where(own_rows, jnp.concatenate([kh] * SEQ_BLOCK, axis=1), 0.0)
            u = jnp.where(same_head, _mm(vv.T, kh_x, _NN, 1), 0.0)
            for n in range(SEQ_BLOCK):
                sl = slice(n * HG_SUB, (n + 1) * HG_SUB)
                bn, qn, kn, vn = bb[sl], qq[sl], kk[sl], vv[sl]
                prods = []
                for t in range(HG_SUB):
                    keep = step <= t
                    diff = jnp.where(keep, bn[t:t + 1, :] - bn, 0.0)
                    prods.append(jnp.where(keep, qn[t:t + 1, :] * kn * jnp.exp(diff), 0.0))
                att = _dot(jnp.concatenate(prods, axis=0).astype(BF16), ones)
                o_rows = [jnp.sum(att[t * HG_SUB:(t + 1) * HG_SUB] * vn, axis=0, keepdims=True)
                          for t in range(HG_SUB)]
                ys_ref[n, rs, cols] = _mm(qs[sl], state[n], _NT, 1) + jnp.concatenate(o_rows, axis=0)
                blk = slice(n * LANES, (n + 1) * LANES)
                s_ref[hp, n] = state[n] * gcv[n * HG_SUB:n * HG_SUB + 1, :] + u[:, blk]
        return carry

    lax.fori_loop(0, n_sub, sub_chunk, 0)

    @pl.when(pl.program_id(1) == pl.num_programs(1) - 1)
    def _():
        if layer:
            sl_ref[0:layer] = earlier_ref[...]
        _tiles_to_heads(s_ref, sl_ref.at[layer], HG_PAIRS, transpose=True)

    o = ys_ref[...].reshape(rows, HG_WIDTH)
    ms = _head_sum(o * o, ones) * (1.0 / HG_HEAD)
    o = o * lax.rsqrt(ms + RMS_EPS) * ng_ref[...] * jax.nn.sigmoid(og)
    _seq_block_store(y_ref, stage_ref, o.reshape(shape3))


def _hg_call(z2, n, l, states, layer, earlier, lower, norm_g, t_chunk):
    assert t_chunk % HG_SUB == 0
    grid, in_specs, out_specs, in_shapes, out_shapes = _seq_specs(n, l, t_chunk, [4 * HG_WIDTH], [HG_WIDTH])
    kern = functools.partial(_hg_kernel, t_chunk=t_chunk, l_valid=min(t_chunk, l), layer=layer)
    st_in, st_out, st_earlier = _layer_state_specs(states, layer)
    q_scr = pltpu.VMEM((SEQ_BLOCK, t_chunk, HG_WIDTH), F32)
    return pl.pallas_call(
        kern,
        out_shape=[jax.ShapeDtypeStruct(out_shapes[0], _branch_dtype(out_shapes[0])),
                   jax.ShapeDtypeStruct((layer + 1,) + states.shape[1:], F32)],
        grid=grid,
        in_specs=in_specs + [st_in, _const_spec((1, HG_WIDTH)), _const_spec((1, HG_WIDTH))] + st_earlier,
        out_specs=out_specs + [st_out],
        scratch_shapes=[pltpu.VMEM((HG_PAIRS, SEQ_BLOCK, LANES, LANES), F32)] + [q_scr] * 8
        + _io_scratch(l, t_chunk, 4 * HG_WIDTH, HG_WIDTH),
        compiler_params=_cparams("arbitrary", "arbitrary"),
        name="hgrn2_mixer",
    )(z2.reshape(in_shapes[0]), states, lower.reshape(1, HG_WIDTH), norm_g.reshape(1, HG_WIDTH),
      *([earlier] if layer else []))


def _merge_kernel(x_ref, ya_ref, yb_ref, yc_ref, gates_ref, gt_ref, la_ref, lb_ref, lc_ref, wo_ref, o_ref):
    d = D_MODEL
    gates = _tile(gates_ref)
    lift = lambda y_ref, w_ref: _dot(_tile(y_ref)[...].astype(BF16), w_ref[...])
    m = (gates[:, 0:d].astype(F32) * lift(ya_ref, la_ref)
         + gates[:, d:2 * d].astype(F32) * lift(yb_ref, lb_ref)
         + gates[:, 2 * d:3 * d].astype(F32) * lift(yc_ref, lc_ref))
    _tile(o_ref)[...] = _tile(x_ref)[...] + gt_ref[0] * _dot(m.astype(BF16), wo_ref[...])


def _merge_call(x, n, l, ya, yb, yc, gates, gate1, prm, tm):
    d = x.shape[-1]
    gt_arr, gt_spec = _mod_operand(gate1, l, tm)
    shape, x_spec = _tok_layout(n, l, tm, d)
    row_spec = lambda w: _tok_layout(n, l, tm, w)[1]
    return pl.pallas_call(
        _merge_kernel,
        out_shape=jax.ShapeDtypeStruct(shape, F32),
        grid=(n * l // tm,),
        in_specs=[x_spec, row_spec(S5_WIDTH), row_spec(RW_WIDTH), row_spec(HG_WIDTH),
                  row_spec(N_BRANCH * d), gt_spec,
                  _const_spec((S5_WIDTH, d)), _const_spec((RW_WIDTH, d)), _const_spec((HG_WIDTH, d)),
                  _const_spec((d, d))],
        out_specs=x_spec,
        compiler_params=_cparams("arbitrary"),
        name="branch_merge",
    )(x, ya, yb, yc, gates, gt_arr, prm["w_lift_a"], prm["w_lift_b"], prm["w_lift_c"], prm["w_out"])


def _ffn_kernel(*refs, l, tm, has_state, final_norm):
    if has_state:
        (x_ref, g_ref, sh_ref, sc_ref, gt_ref, wup_ref, cw_ref, cb_ref, wdn_ref, fg_ref, st_ref,
         o_ref, tail_ref, h_ref, act_ref, hist_ref, buf_ref, old_ref) = refs
    else:
        (x_ref, g_ref, sh_ref, sc_ref, gt_ref, wup_ref, cw_ref, cb_ref, wdn_ref, fg_ref,
         o_ref, tail_ref, h_ref, act_ref, hist_ref) = refs
    i = pl.program_id(0)
    n_chunks = D_FF // FF_CHUNK
    n_hist = CONV_W - 1

    if has_state:
        t_idx = lax.broadcasted_iota(jnp.int32, (tm, 1), 0) % l
        after = [t_idx >= j + 1 for j in range(n_hist)]
        old_ref[...] = jnp.zeros_like(old_ref)
        buf_ref[:, 0:SUBLANES, :] = jnp.zeros((2, SUBLANES, FF_CHUNK), F32)
        for s in range(tm // l):
            for j in range(n_hist):
                old_ref[j, s * l:s * l + j + 1, :] = st_ref[s, n_hist - 1 - j:n_hist, :]
    else:
        @pl.when(i % (l // tm) == 0)
        def _():
            hist_ref[...] = jnp.zeros_like(hist_ref)

    x = _tile(x_ref)[...]
    h = _rms_modulate(x, g_ref[...], sh_ref[0], sc_ref[0])
    groups = tm // SUBLANES
    if not has_state:
        h = jnp.swapaxes(h.reshape(SUBLANES, groups, D_MODEL), 0, 1).reshape(tm, D_MODEL)
        row8 = lax.broadcasted_iota(jnp.int32, (SUBLANES, FF_CHUNK), 0)
    h_ref[...] = h.astype(BF16)

    def conv_half(c, half):
        col0 = half * D_FF + c * FF_CHUNK
        cols = slice(col0, col0 + FF_CHUNK)
        up = _dot(h_ref[...], wup_ref[:, cols])
        if has_state:
            buf_ref[half, SUBLANES:SUBLANES + tm, :] = up
            prev = [jnp.where(after[j], buf_ref[half, SUBLANES - 1 - j:SUBLANES - 1 - j + tm, :], 0.0)
                    + old_ref[j, :, cols] for j in range(n_hist)]
            for s in range(tm // l):
                r0 = SUBLANES + (s + 1) * l - n_hist
                tail_ref[s, :, cols] = buf_ref[half, r0:r0 + n_hist, :]
        else:
            late = [up[tm - (j + 1) * SUBLANES:tm - j * SUBLANES, :] for j in range(n_hist)]
            first = [jnp.where(row8 == 0, pltpu.roll(hist_ref[c, half, j], 1, 0), pltpu.roll(late[j], 1, 0))
                     for j in range(n_hist)]
            for j in range(n_hist):
                hist_ref[c, half, j] = late[j]
            prev = [jnp.concatenate([first[0], up[0:tm - SUBLANES, :]], axis=0),
                    jnp.concatenate([first[1], first[0], up[0:tm - 2 * SUBLANES, :]], axis=0)]
            tail_ref[0, 0:1, cols] = up[tm - SUBLANES - 1:tm - SUBLANES, :]
            tail_ref[0, 1:2, cols] = up[tm - 1:tm, :]
        return (cb_ref[:, cols] + cw_ref[0:1, cols] * prev[1] + cw_ref[1:2, cols] * prev[0]
                + cw_ref[2:3, cols] * up)

    for c in range(n_chunks):
        act = _gelu(conv_half(c, 0)) * conv_half(c, 1)
        act_ref[:, c * FF_CHUNK:(c + 1) * FF_CHUNK] = act.astype(BF16)
    acc = _dot(act_ref[...], wdn_ref[...])
    if not has_state:
        acc = jnp.swapaxes(acc.reshape(groups, SUBLANES, D_MODEL), 0, 1).reshape(tm, D_MODEL)
    out = x + gt_ref[0] * acc
    if final_norm:
        ms = jnp.mean(out * out, axis=-1, keepdims=True)
        out = out * lax.rsqrt(ms + RMS_EPS) * fg_ref[...]
    _tile(o_ref)[...] = out


def _ffn_call(x, n, l, g, shift, scale, gate2, prm, final_g, conv_state, tm, final_norm):
    assert CONV_W == 3
    d = x.shape[-1]
    has_state = conv_state is not None
    sh_arr, sh_spec = _mod_operand(shift, l, tm)
    sc_arr, sc_spec = _mod_operand(scale, l, tm)
    gt_arr, gt_spec = _mod_operand(gate2, l, tm)
    shape, x_spec = _tok_layout(n, l, tm, d)
    in_specs = [x_spec, _const_spec((1, d)), sh_spec, sc_spec, gt_spec,
                _const_spec((d, 2 * D_FF)), _const_spec((CONV_W, 2 * D_FF)), _const_spec((1, 2 * D_FF)),
                _const_spec((D_FF, d)), _const_spec((1, d))]
    operands = [x, g.reshape(1, d), sh_arr, sc_arr, gt_arr, prm["w_up"], prm["conv_w"], prm["conv_b"],
                prm["w_down"], final_g.reshape(1, d)]
    scratch = [pltpu.VMEM((tm, d), BF16), pltpu.VMEM((tm, D_FF), BF16),
               pltpu.VMEM((D_FF // FF_CHUNK, 2, CONV_W - 1, SUBLANES, FF_CHUNK), F32)]
    if has_state:
        assert tm % l == 0 and l >= CONV_W - 1
        state_spec = pl.BlockSpec((tm // l, CONV_W - 1, 2 * D_FF), lambda i: (i, 0, 0))
        in_specs.append(state_spec)
        operands.append(conv_state)
        tail_spec = state_spec
        scratch += [pltpu.VMEM((2, tm + SUBLANES, FF_CHUNK), F32), pltpu.VMEM((CONV_W - 1, tm, 2 * D_FF), F32)]
    else:
        assert l % tm == 0
        per = l // tm
        tail_spec = pl.BlockSpec((1, CONV_W - 1, 2 * D_FF), lambda i: (i // per, 0, 0))
    kern = functools.partial(_ffn_kernel, l=l, tm=tm, has_state=has_state, final_norm=final_norm)
    return pl.pallas_call(
        kern,
        out_shape=[jax.ShapeDtypeStruct(shape, F32), jax.ShapeDtypeStruct((n, CONV_W - 1, 2 * D_FF), F32)],
        grid=(n * l // tm,),
        in_specs=in_specs,
        out_specs=[x_spec, tail_spec],
        scratch_shapes=scratch,
        compiler_params=_cparams("arbitrary"),
        name="conv_ffn",
    )(*operands)


def _prepare_layer(p):
    out = {}
    w_in = p["w_in"]
    c1 = S5_WIDTH
    c2 = c1 + RW_COLS
    c3 = c2 + 4 * HG_WIDTH
    out["w_in_a"] = w_in[:, :c1].astype(BF16)
    out["w_in_b"] = w_in[:, c1:c2].astype(BF16)
    out["w_in_c"] = w_in[:, c2:c3].astype(BF16)
    out["w_in_g"] = w_in[:, c3:].astype(BF16)
    out["w_ada"] = p["w_ada"].astype(BF16)
    for name in ("w_lift_a", "w_lift_b", "w_lift_c", "w_out", "w_up", "w_down"):
        out[name] = p[name].astype(BF16)
    out["conv_w"] = p["conv_w"]
    out["conv_b"] = p["conv_b"].reshape(1, 2 * D_FF)

    lr = p["s5_lambda_re"]
    li = p["s5_lambda_im"]
    dt = jnp.exp(p["s5_log_dt"])[:, None]
    mag = jnp.exp(lr * dt)
    ar = mag * jnp.cos(li * dt)
    ai = mag * jnp.sin(li * dt)
    den = lr * lr + li * li
    zr = ((ar - 1.0) * lr + ai * li) / den
    zi = (ai * lr - (ar - 1.0) * li) / den
    bbr = zr[..., None] * p["s5_b_re"] - zi[..., None] * p["s5_b_im"]
    bbi = zr[..., None] * p["s5_b_im"] + zi[..., None] * p["s5_b_re"]
    eye = jnp.eye(S5_GROUPS, dtype=F32)
    bmat = jnp.einsum("gh,rgpc->gcrhp", eye, jnp.stack([bbr, bbi])).reshape(S5_WIDTH, 2 * S5_FLAT)
    cmat = jnp.einsum("hg,rgcp->rhpgc", eye, jnp.stack([p["s5_c_re"], -p["s5_c_im"]])).reshape(
        2 * S5_FLAT, S5_WIDTH)
    out["s5_b"] = bmat.astype(BF16)
    out["s5_c"] = cmat.astype(BF16)
    out["s5_a"] = jnp.stack([ar.reshape(S5_FLAT), ai.reshape(S5_FLAT)])
    out["s5_d"] = p["s5_d"].reshape(1, S5_WIDTH)
    out["s5_w_glu"] = p["s5_w_glu"].astype(BF16)
    out["s5_b_glu"] = p["s5_b_glu"].reshape(1, S5_WIDTH)

    out["rw_mu"] = p["rw_mu"].reshape(1, RW_COLS)
    out["rw_vec"] = jnp.stack([p["rw_w0"], p["rw_a0"], p["rw_k_k"], p["rw_k_a"],
                               p["rw_r_k"].reshape(RW_WIDTH), p["rw_ln_w"], p["rw_ln_b"],
                               jnp.zeros((RW_WIDTH,), F32)])
    zw = jnp.zeros((RW_LORA, RW_WIDTH), F32)
    out["rw_w2"] = zw.at[0:RW_DECAY_LORA].set(p["rw_w2"]).astype(BF16)
    out["rw_a2"] = zw.at[RW_DECAY_LORA:RW_DECAY_LORA + RW_AAA_LORA].set(p["rw_a2"]).astype(BF16)
    out["rw_g2"] = zw.at[RW_DECAY_LORA + RW_AAA_LORA:].set(p["rw_g2"]).astype(BF16)
    out["hg_norm"] = p["hg_norm"]
    out["g_mix"] = p["g_mix"]
    out["g_ffn"] = p["g_ffn"]
    out["b_ada"] = p["b_ada"]
    return out


def _chunk_for(l, sub, t_chunk):
    return t_chunk if l % t_chunk == 0 else -(-l // sub) * sub


def _run_trunk(x, mods, st_s5, st_shift, st_rw, st_hg, st_conv, lower, final_g, prms, t_chunk):
    n, l, d = x.shape
    rows = n * l
    tm = min(ROW_TILE, rows)
    depth = len(prms)
    assert (l % tm == 0) == (l % t_chunk == 0)
    h = x.reshape(_tok_layout(n, l, tm, d)[0])
    out_s5, out_shift, out_conv = [], [], []
    rw_new = hg_new = None
    for layer in range(depth):
        prm = prms[layer]
        sh1, sc1, gt1, sh2, sc2, gt2 = jnp.split(mods[layer], 6, axis=-1)
        za, zb, zc, gates = _in_proj_call(h, n, l, prm["g_mix"], sh1, sc1, prm["w_in_a"], prm["w_in_b"],
                                          prm["w_in_c"], prm["w_in_g"], tm)

        s5_in = jnp.concatenate([st_s5[layer][..., 0].reshape(n, S5_FLAT),
                                 st_s5[layer][..., 1].reshape(n, S5_FLAT)], axis=1)
        ya, s5_new = _s5_call(za, n, l, s5_in, prm, _chunk_for(l, SUBLANES, t_chunk))
        yb, shift_new, rw_new = _rw_call(zb, n, l, st_shift[layer].reshape(n, 1, RW_COLS), st_rw, layer, rw_new,
                                         prm, _chunk_for(l, RW_SUB, t_chunk))
        yc, hg_new = _hg_call(zc, n, l, st_hg, layer, hg_new, lower[layer], prm["hg_norm"],
                              _chunk_for(l, HG_SUB, t_chunk))

        h = _merge_call(h, n, l, ya, yb, yc, gates, gt1, prm, tm)
        ffn_tm = tm if st_conv is None else min(tm, 128)
        h, conv_new = _ffn_call(h, n, l, prm["g_ffn"], sh2, sc2, gt2, prm, final_g,
                                None if st_conv is None else st_conv[layer], ffn_tm,
                                final_norm=(layer == depth - 1))
        out_s5.append(jnp.stack([s5_new[:, :S5_FLAT].reshape(n, S5_GROUPS, S5_STATE),
                                 s5_new[:, S5_FLAT:].reshape(n, S5_GROUPS, S5_STATE)], axis=-1))
        out_shift.append(shift_new.reshape(n, RW_COLS))
        out_conv.append(conv_new)
    y = h.reshape(n, l, d).astype(x.dtype)
    return y, (jnp.stack(out_s5), jnp.stack(out_shift), rw_new, hg_new, jnp.stack(out_conv))


def kernel(x_prompt, x_sample, c_prompt, c_sample, state_s5, state_rwkv_shift, state_rwkv, state_hgrn, state_ffn_conv, w_ada, b_ada, g_mix, g_ffn, w_in, s5_lambda_re, s5_lambda_im, s5_log_dt, s5_b_re, s5_b_im, s5_c_re, s5_c_im, s5_d, s5_w_glu, s5_b_glu, rw_mu, rw_w0, rw_w2, rw_a0, rw_a2, rw_g2, rw_k_k, rw_k_a, rw_r_k, rw_ln_w, rw_ln_b, hg_lb, hg_norm, w_lift_a, w_lift_b, w_lift_c, w_out, w_up, conv_w, conv_b, w_down, final_g):
    per_layer = {
        "w_ada": w_ada, "b_ada": b_ada, "g_mix": g_mix, "g_ffn": g_ffn, "w_in": w_in,
        "s5_lambda_re": s5_lambda_re, "s5_lambda_im": s5_lambda_im, "s5_log_dt": s5_log_dt,
        "s5_b_re": s5_b_re, "s5_b_im": s5_b_im, "s5_c_re": s5_c_re, "s5_c_im": s5_c_im,
        "s5_d": s5_d, "s5_w_glu": s5_w_glu, "s5_b_glu": s5_b_glu,
        "rw_mu": rw_mu, "rw_w0": rw_w0, "rw_w2": rw_w2, "rw_a0": rw_a0, "rw_a2": rw_a2,
        "rw_g2": rw_g2, "rw_k_k": rw_k_k, "rw_k_a": rw_k_a, "rw_r_k": rw_r_k,
        "rw_ln_w": rw_ln_w, "rw_ln_b": rw_ln_b, "hg_norm": hg_norm,
        "w_lift_a": w_lift_a, "w_lift_b": w_lift_b, "w_lift_c": w_lift_c, "w_out": w_out,
        "w_up": w_up, "conv_w": conv_w, "conv_b": conv_b, "w_down": w_down,
    }
    depth = w_ada.shape[0]
    prms = [_prepare_layer({k: v[layer] for k, v in per_layer.items()}) for layer in range(depth)]

    lbp = jax.nn.softmax(hg_lb.astype(F32), axis=0)
    lower = jnp.cumsum(lbp, axis=0) - lbp[0]

    nb = x_prompt.shape[0]
    ns = x_sample.shape[0]
    c_all = jnp.concatenate([c_prompt, c_sample], axis=0).astype(F32)
    mods = [_ada_call(c_all, prms[layer]["w_ada"], prms[layer]["b_ada"]) for layer in range(depth)]
    mods_p = [m[:nb] for m in mods]
    mods_s = [m[nb:] for m in mods]

    z_s5 = jnp.zeros((depth, nb) + state_s5.shape[2:], F32)
    z_shift = jnp.zeros((depth, nb) + state_rwkv_shift.shape[2:], F32)
    z_rw = jnp.zeros((depth, nb) + state_rwkv.shape[2:], F32)
    z_hg = jnp.zeros((depth, nb) + state_hgrn.shape[2:], F32)

    y_prompt, (s5_p, shift_p, rw_p, hg_p, conv_p) = _run_trunk(
        x_prompt, mods_p, z_s5, z_shift, z_rw, z_hg, None, lower, final_g, prms, t_chunk=64)
    y_sample, (s5_s, shift_s, rw_s, hg_s, conv_s) = _run_trunk(
        x_sample, mods_s, state_s5, state_rwkv_shift, state_rwkv, state_hgrn, state_ffn_conv,
        lower, final_g, prms, t_chunk=64)
    return (y_prompt, y_sample, s5_p, shift_p, rw_p, hg_p, conv_p, s5_s, shift_s, rw_s, hg_s, conv_s)
```

```python
import functools
import math

import jax
import jax.numpy as jnp
from jax import lax
from jax.experimental import pallas as pl
from jax.experimental.pallas import tpu as pltpu

F32 = jnp.float32
BF16 = jnp.bfloat16

D_MODEL = 1024
S5_WIDTH = D_MODEL // 4
S5_GROUP = 16
S5_GROUPS = S5_WIDTH // S5_GROUP
S5_STATE = 64
S5_FLAT = S5_GROUPS * S5_STATE
RW_WIDTH = D_MODEL // 2
RW_HEAD = 64
RW_HEADS = RW_WIDTH // RW_HEAD
RW_DECAY_LORA = 32
RW_AAA_LORA = 32
RW_GATE_LORA = 64
RW_LORA = RW_DECAY_LORA + RW_AAA_LORA + RW_GATE_LORA
RW_COLS = 3 * RW_WIDTH + RW_LORA
RW_GN_EPS = 1e-5 * RW_HEAD
HG_WIDTH = D_MODEL // 4
HG_HEAD = 64
HG_HEADS = HG_WIDTH // HG_HEAD
HG_GATE_FLOOR = 1e-30
N_BRANCH = 3
D_FF = 256 * ((8 * D_MODEL // 3 + 255) // 256)
CONV_W = 3
RMS_EPS = 1e-6

LANES = 128
SUBLANES = 8
MXU_COLS = 256
VMEM_LIMIT_BYTES = 56 * 1024 * 1024

SEQ_BLOCK = SUBLANES
HEAD_PAIR = LANES // RW_HEAD
ROW_TILE = 512
FF_CHUNK = 256
IN_PROJ_CHUNK = 512
RW_SUB = 16
RW_PAIRS = RW_HEADS // HEAD_PAIR
HG_SUB = 16
HG_PAIRS = HG_HEADS // HEAD_PAIR

_NN = (((1,), (0,)), ((), ()))
_NT = (((1,), (1,)), ((), ()))


def _cparams(*sem):
    return pltpu.CompilerParams(dimension_semantics=sem, vmem_limit_bytes=VMEM_LIMIT_BYTES)


def _const_spec(shape):
    nd = len(shape)
    return pl.BlockSpec(shape, lambda *_: (0,) * nd, pipeline_mode=pl.Buffered(1))


def _dot(a, b):
    return jnp.dot(a, b, preferred_element_type=F32)


def _split_bf16(x):
    hi = x.astype(BF16)
    lo = (x - hi.astype(F32)).astype(BF16)
    return hi, lo


def _head_ones():
    r = lax.broadcasted_iota(jnp.int32, (LANES, LANES), 0) // RW_HEAD
    c = lax.broadcasted_iota(jnp.int32, (LANES, LANES), 1) // RW_HEAD
    return (r == c).astype(BF16)


def _head_sum(x, ones, split=True):
    hi, lo = _split_bf16(x) if split else (x.astype(BF16), None)
    tiles = []
    for c in range(0, x.shape[1], LANES):
        t = _dot(hi[:, c:c + LANES], ones)
        tiles.append(t + _dot(lo[:, c:c + LANES], ones) if split else t)
    return tiles[0] if len(tiles) == 1 else jnp.concatenate(tiles, axis=1)


def _softplus(x):
    return jnp.maximum(x, 0.0) + jnp.log(1.0 + jnp.exp(-jnp.abs(x)))


def _gelu(x):
    c = 2.0 * math.sqrt(2.0 / math.pi)
    return x * jax.nn.sigmoid(x * (c + (c * 0.044715) * (x * x)))


def _rms_modulate(x, g, shift, scale):
    ms = jnp.mean(x * x, axis=-1, keepdims=True)
    return x * lax.rsqrt(ms + RMS_EPS) * g * (1.0 + scale) + shift


def _seq_specs(n, l, t_chunk, widths_in, widths_out):
    nb = n // SEQ_BLOCK
    if l % t_chunk == 0:
        spec = lambda w: pl.BlockSpec((SEQ_BLOCK, t_chunk, w), lambda b, c: (b, c, 0))
        shape = lambda w: (n, l, w)
        nt = l // t_chunk
    else:
        assert l < t_chunk
        spec = lambda w: pl.BlockSpec((SEQ_BLOCK * l, w), lambda b, c: (b, 0))
        shape = lambda w: (n * l, w)
        nt = 1
    return ((nb, nt), [spec(w) for w in widths_in], [spec(w) for w in widths_out],
            [shape(w) for w in widths_in], [shape(w) for w in widths_out])


def _branch_dtype(shape):
    return BF16 if len(shape) == 3 else F32


def _seq_block_load(z_ref, pad_ref):
    if len(z_ref.shape) == 3:
        return z_ref[...]
    l = z_ref.shape[0] // SEQ_BLOCK
    pad_ref[...] = jnp.zeros_like(pad_ref)
    for n in range(SEQ_BLOCK):
        pad_ref[n, 0:l, :] = z_ref[n * l:(n + 1) * l, :]
    return pad_ref[...]


def _seq_block_store(y_ref, stage_ref, y3):
    if len(y_ref.shape) == 3:
        y_ref[...] = y3.astype(y_ref.dtype)
        return
    l = y_ref.shape[0] // SEQ_BLOCK
    stage_ref[...] = y3
    for n in range(SEQ_BLOCK):
        y_ref[n * l:(n + 1) * l, :] = stage_ref[n, 0:l, :]


def _heads_to_tiles(s_ref, tile_ref, n_pairs, transpose):
    zero = jnp.zeros((RW_HEAD, RW_HEAD), F32)
    for n in range(SEQ_BLOCK):
        for hp in range(n_pairs):
            a, b = s_ref[n, HEAD_PAIR * hp], s_ref[n, HEAD_PAIR * hp + 1]
            if transpose:
                a, b = a.T, b.T
            tile_ref[hp, n] = jnp.concatenate([jnp.concatenate([a, zero], axis=1),
                                               jnp.concatenate([zero, b], axis=1)], axis=0)


def _tiles_to_heads(tile_ref, s_ref, n_pairs, transpose):
    for n in range(SEQ_BLOCK):
        for hp in range(n_pairs):
            t = tile_ref[hp, n]
            a, b = t[0:RW_HEAD, 0:RW_HEAD], t[RW_HEAD:LANES, RW_HEAD:LANES]
            if transpose:
                a, b = a.T, b.T
            s_ref[n, HEAD_PAIR * hp] = a
            s_ref[n, HEAD_PAIR * hp + 1] = b


def _ada_kernel(c_ref, w_ref, b_ref, o_ref):
    c = c_ref[...]
    o_ref[...] = _dot(jax.nn.silu(c).astype(BF16), w_ref[...]) + b_ref[...]


def _ada_call(c, w_bf16, b):
    n, d = c.shape
    cols = w_bf16.shape[1]
    tn = cols // 4
    return pl.pallas_call(
        _ada_kernel,
        out_shape=jax.ShapeDtypeStruct((n, cols), F32),
        grid=(cols // tn,),
        in_specs=[_const_spec((n, d)),
                  pl.BlockSpec((d, tn), lambda j: (0, j)),
                  pl.BlockSpec((1, tn), lambda j: (0, j))],
        out_specs=pl.BlockSpec((n, tn), lambda j: (0, j)),
        compiler_params=_cparams("arbitrary"),
        name="ada_mod",
    )(c, w_bf16, b.reshape(1, cols))


def _mod_operand(m, l, tm):
    n, d = m.shape
    if l % tm == 0:
        per = l // tm
        return m.reshape(n, 1, d), pl.BlockSpec((1, 1, d), lambda i: (i // per, 0, 0))
    assert tm % l == 0
    return jnp.repeat(m, l, axis=0).reshape(1, n * l, d), pl.BlockSpec((1, tm, d), lambda i: (0, i, 0))


def _tok_layout(n, l, tm, w):
    if l % tm == 0:
        per = l // tm
        return (n, l, w), pl.BlockSpec((1, tm, w), lambda i: (i // per, i % per, 0))
    return (n * l, w), pl.BlockSpec((tm, w), lambda i: (i, 0))


def _tile(ref):
    return ref.at[0] if len(ref.shape) == 3 else ref


def _in_proj_kernel(x_ref, g_ref, sh_ref, sc_ref, wa_ref, wb_ref, wc_ref, wg_ref,
                    oa_ref, ob_ref, oc_ref, og_ref):
    h = _rms_modulate(_tile(x_ref)[...], g_ref[...], sh_ref[0], sc_ref[0]).astype(BF16)

    def project(w_ref, o_ref, fn=None):
        width = w_ref.shape[1]
        out = _tile(o_ref)
        for c0 in range(0, width, IN_PROJ_CHUNK):
            cols = slice(c0, min(c0 + IN_PROJ_CHUNK, width))
            z = _dot(h, w_ref[:, cols])
            out[:, cols] = (z if fn is None else fn(z)).astype(out.dtype)

    project(wa_ref, oa_ref)
    project(wb_ref, ob_ref)
    project(wc_ref, oc_ref)
    project(wg_ref, og_ref, jax.nn.sigmoid)


def _in_proj_call(x, n, l, g, shift, scale, w_a, w_b, w_c, w_g, tm):
    d = x.shape[-1]
    sh_arr, sh_spec = _mod_operand(shift, l, tm)
    sc_arr, sc_spec = _mod_operand(scale, l, tm)
    widths = [w.shape[1] for w in (w_a, w_b, w_c, w_g)]
    dtypes = [F32, F32, F32, BF16]
    outs = [_tok_layout(n, l, tm, w) for w in widths]
    return pl.pallas_call(
        _in_proj_kernel,
        out_shape=[jax.ShapeDtypeStruct(shape, dt) for (shape, _), dt in zip(outs, dtypes)],
        grid=(n * l // tm,),
        in_specs=[_tok_layout(n, l, tm, d)[1], _const_spec((1, d)), sh_spec, sc_spec]
        + [_const_spec(w.shape) for w in (w_a, w_b, w_c, w_g)],
        out_specs=[spec for _, spec in outs],
        compiler_params=_cparams("arbitrary"),
        name="in_proj",
    )(x, g.reshape(1, d), sh_arr, sc_arr, w_a, w_b, w_c, w_g)


def _s5_kernel(u_ref, s0_ref, b_ref, c_ref, a_ref, d_ref, wg_ref, bg_ref,
               y_ref, sl_ref, e_ref, st_ref, pad_ref, stage_ref, *, t_chunk, n_steps):
    @pl.when(pl.program_id(1) == 0)
    def _():
        st_ref[...] = s0_ref[...]

    rows = SEQ_BLOCK * t_chunk
    ut = jnp.swapaxes(_seq_block_load(u_ref, pad_ref), 0, 1).reshape(rows, S5_WIDTH)
    ub = ut.astype(BF16)
    for c0 in range(0, 2 * S5_FLAT, MXU_COLS):
        ch0 = (c0 % S5_FLAT) // S5_STATE * S5_GROUP
        k0 = ch0 // LANES * LANES
        assert ch0 + MXU_COLS // S5_STATE * S5_GROUP <= k0 + LANES
        e_ref[:, c0:c0 + MXU_COLS] = _dot(ub[:, k0:k0 + LANES], b_ref[k0:k0 + LANES, c0:c0 + MXU_COLS])

    ar = jnp.broadcast_to(a_ref[0:1, :], (SEQ_BLOCK, S5_FLAT))
    ai = jnp.broadcast_to(a_ref[1:2, :], (SEQ_BLOCK, S5_FLAT))

    def step(t, carry):
        sr, si = carry
        r = pl.ds(pl.multiple_of(t * SEQ_BLOCK, SEQ_BLOCK), SEQ_BLOCK)
        nr = ar * sr - ai * si + e_ref[r, 0:S5_FLAT]
        ni = ar * si + ai * sr + e_ref[r, S5_FLAT:2 * S5_FLAT]
        e_ref[r, 0:S5_FLAT] = nr
        e_ref[r, S5_FLAT:2 * S5_FLAT] = ni
        return nr, ni

    sr, si = lax.fori_loop(0, n_steps, step, (st_ref[:, 0:S5_FLAT], st_ref[:, S5_FLAT:2 * S5_FLAT]))
    st_ref[:, 0:S5_FLAT] = sr
    st_ref[:, S5_FLAT:2 * S5_FLAT] = si
    sl_ref[...] = st_ref[...]

    y = _dot(e_ref[...].astype(BF16), c_ref[...]) + d_ref[...] * ut
    y = _gelu(y)
    y = y * jax.nn.sigmoid(_dot(y.astype(BF16), wg_ref[...]) + bg_ref[...])
    _seq_block_store(y_ref, stage_ref, jnp.swapaxes(y.reshape(t_chunk, SEQ_BLOCK, S5_WIDTH), 0, 1))


def _io_scratch(l, t_chunk, width_in, width_out):
    if l % t_chunk == 0:
        return [pltpu.VMEM((SUBLANES, LANES), F32)] * 2
    return [pltpu.VMEM((SEQ_BLOCK, t_chunk, width_in), F32), pltpu.VMEM((SEQ_BLOCK, t_chunk, width_out), F32)]


def _s5_call(u2, n, l, s0, prm, t_chunk):
    grid, in_specs, out_specs, in_shapes, out_shapes = _seq_specs(n, l, t_chunk, [S5_WIDTH], [S5_WIDTH])
    kern = functools.partial(_s5_kernel, t_chunk=t_chunk, n_steps=min(t_chunk, l))
    y, s_new = pl.pallas_call(
        kern,
        out_shape=[jax.ShapeDtypeStruct(out_shapes[0], _branch_dtype(out_shapes[0])),
                   jax.ShapeDtypeStruct((n, 2 * S5_FLAT), F32)],
        grid=grid,
        in_specs=in_specs + [
            pl.BlockSpec((SEQ_BLOCK, 2 * S5_FLAT), lambda b, c: (b, 0)),
            _const_spec((S5_WIDTH, 2 * S5_FLAT)), _const_spec((2 * S5_FLAT, S5_WIDTH)),
            _const_spec((2, S5_FLAT)), _const_spec((1, S5_WIDTH)),
            _const_spec((S5_WIDTH, S5_WIDTH)), _const_spec((1, S5_WIDTH))],
        out_specs=out_specs + [pl.BlockSpec((SEQ_BLOCK, 2 * S5_FLAT), lambda b, c: (b, 0))],
        scratch_shapes=[pltpu.VMEM((SEQ_BLOCK * t_chunk, 2 * S5_FLAT), F32),
                        pltpu.VMEM((SEQ_BLOCK, 2 * S5_FLAT), F32)]
        + _io_scratch(l, t_chunk, S5_WIDTH, S5_WIDTH),
        compiler_params=_cparams("arbitrary", "arbitrary"),
        name="s5_mixer",
    )(u2.reshape(in_shapes[0]), s0, prm["s5_b"], prm["s5_c"], prm["s5_a"], prm["s5_d"], prm["s5_w_glu"],
      prm["s5_b_glu"])
    return y, s_new


def _mm(a, b, dims):
    return lax.dot_general(a.astype(BF16), b.astype(BF16), dims, preferred_element_type=F32)


def _cumsum_groups(x, group):
    pos = lax.broadcasted_iota(jnp.int32, (x.shape[0], 1), 0) % group
    s = 1
    while s < group:
        x = x + jnp.where(pos >= s, pltpu.roll(x, s, 0), 0.0)
        s *= 2
    return x


def _rw_kernel(p_ref, sh0_ref, s0_ref, mu_ref, vec_ref, w2_ref, a2_ref, g2_ref, *rest,
               t_chunk, l_valid, last_row, layer):
    earlier_ref = rest[0] if layer else None
    (y_ref, shl_ref, sl_ref,
     pbuf_ref, s_ref, at_ref, bt_ref, kt_ref, rt_ref, bh_ref, kh_ref, v_ref, gc_ref, ys_ref,
     pad_ref, stage_ref) = rest[1 if layer else 0:]
    rows = SEQ_BLOCK * t_chunk
    n_sub = t_chunk // RW_SUB

    @pl.when(pl.program_id(1) == 0)
    def _():
        pbuf_ref[...] = sh0_ref[...]
        _heads_to_tiles(s0_ref, s_ref, RW_PAIRS, transpose=False)

    p3 = _seq_block_load(p_ref, pad_ref)
    first = lax.broadcasted_iota(jnp.int32, (1, t_chunk, 1), 1) == 0
    prev3 = jnp.where(first, pbuf_ref[...], pltpu.roll(p3, 1, 1))
    last = p3[:, last_row:last_row + 1, :]
    pbuf_ref[...] = last
    shl_ref[...] = last

    xm = (p3 + (prev3 - p3) * mu_ref[...]).reshape(rows, RW_COLS)
    r = xm[:, 0:RW_WIDTH]
    k = xm[:, RW_WIDTH:2 * RW_WIDTH]
    v = xm[:, 2 * RW_WIDTH:3 * RW_WIDTH]
    lora = xm[:, 3 * RW_WIDTH:RW_COLS]
    w0, a0, k_k, k_a = vec_ref[0:1, :], vec_ref[1:2, :], vec_ref[2:3, :], vec_ref[3:4, :]
    r_k, ln_w, ln_b = vec_ref[4:5, :], vec_ref[5:6, :], vec_ref[6:7, :]

    w = -_softplus(-(w0 + _dot(jnp.tanh(lora).astype(BF16), w2_ref[...]))) - 0.5
    log_decay = -jnp.exp(w)
    a = jax.nn.sigmoid(a0 + _dot(lora.astype(BF16), a2_ref[...]))
    g = _dot(jax.nn.sigmoid(lora).astype(BF16), g2_ref[...])

    ones = _head_ones()
    kk = k * k_k
    kk = kk * lax.rsqrt(jnp.maximum(_head_sum(kk * kk, ones), 1e-24))
    kt = k * (1.0 + (a - 1.0) * k_a)
    kka = kk * a
    if l_valid < t_chunk:
        live = lax.broadcasted_iota(jnp.int32, (rows, 1), 0) % t_chunk < l_valid
        log_decay = jnp.where(live, log_decay, 0.0)
        kk = jnp.where(live, kk, 0.0)
        kka = jnp.where(live, kka, 0.0)
        kt_live = jnp.where(live, kt, 0.0)
    else:
        kt_live = kt

    cum = _cumsum_groups(log_decay, RW_SUB)
    shape_g = (rows // RW_SUB, RW_SUB, RW_WIDTH)
    cum_end = jnp.broadcast_to(cum.reshape(shape_g)[:, RW_SUB - 1:RW_SUB, :], shape_g).reshape(rows, RW_WIDTH)
    g_in = jnp.exp(cum)
    g_inv = jnp.exp(-cum)
    g_ex = jnp.exp(cum - log_decay)
    g_out = jnp.exp(cum_end - cum)
    shape3 = (SEQ_BLOCK, t_chunk, RW_WIDTH)
    at_ref[...] = (-kk * g_ex).reshape(shape3)
    bt_ref[...] = (kka * g_inv).reshape(shape3)
    kt_ref[...] = (kt_live * g_inv).reshape(shape3)
    rt_ref[...] = (r * g_in).reshape(shape3)
    bh_ref[...] = (kka * g_out).reshape(shape3)
    kh_ref[...] = (kt_live * g_out).reshape(shape3)
    v_ref[...] = v.reshape(shape3)
    gc_ref[...] = jnp.exp(cum_end).reshape(shape3)

    m_rows = SEQ_BLOCK * RW_SUB
    ri = lax.broadcasted_iota(jnp.int32, (m_rows, m_rows), 0)
    ci = lax.broadcasted_iota(jnp.int32, (m_rows, m_rows), 1)
    same_seq = (ri // RW_SUB) == (ci // RW_SUB)
    before = same_seq & ((ci % RW_SUB) < (ri % RW_SUB))
    upto = same_seq & ((ci % RW_SUB) <= (ri % RW_SUB))
    eye = (ri == ci).astype(F32)
    pair_mask = (ri // 2) == (ci // 2)
    level_masks = []
    s = 2
    while s < RW_SUB:
        level_masks.append(((ri // (2 * s)) == (ci // (2 * s))) & ((ri // s) != (ci // s)))
        s *= 2
    lane = lax.broadcasted_iota(jnp.int32, (m_rows, LANES), 1)
    head_mask = [lane < RW_HEAD, lane >= RW_HEAD]
    wide = (m_rows, SEQ_BLOCK * LANES)
    own_rows = (lax.broadcasted_iota(jnp.int32, wide, 1) // LANES
                == lax.broadcasted_iota(jnp.int32, wide, 0) // RW_SUB)
    same_head = (lax.broadcasted_iota(jnp.int32, wide, 0) // RW_HEAD
                 == (lax.broadcasted_iota(jnp.int32, wide, 1) % LANES) // RW_HEAD)
    group = max(g for g in (1, 2, 4) if n_sub % g == 0)
    units = [(j, hp) for j in range(group) for hp in range(RW_PAIRS)]
    heads = [(u, h) for u in range(len(units)) for h in range(HEAD_PAIR)]

    def sub_chunk(c, carry):
        rs = [pl.ds(pl.multiple_of((c * group + j) * RW_SUB, RW_SUB), RW_SUB) for j in range(group)]

        def ld(ref, j, hp):
            return ref[:, rs[j], pl.ds(hp * LANES, LANES)].reshape(m_rows, LANES)

        at, bt, ktl, rt, bh, kh, vv, gcv = ([ld(ref, j, hp) for j, hp in units]
                                            for ref in (at_ref, bt_ref, kt_ref, rt_ref, bh_ref, kh_ref,
                                                        v_ref, gc_ref))
        state = [[s_ref[hp, n] for n in range(SEQ_BLOCK)] for hp in range(RW_PAIRS)]
        zero = jnp.zeros((m_rows, LANES), F32)
        msk = lambda x, h: jnp.where(head_mask[h], x, zero)

        gram = [_mm(jnp.concatenate([msk(at[hp], 0), msk(rt[hp], 0), msk(at[hp], 1), msk(rt[hp], 1)], axis=0),
                    jnp.concatenate([bt[hp], ktl[hp]], axis=0), _NT)
                for hp in range(len(units))]

        def quad(hp, h, row, col, keep):
            blk = gram[hp][(2 * h + row) * m_rows:(2 * h + row + 1) * m_rows, col * m_rows:(col + 1) * m_rows]
            return jnp.where(keep, blk, 0.0)

        m_ab = [quad(hp, h, 0, 0, before) for hp, h in heads]
        m_ak = [quad(hp, h, 0, 1, before) for hp, h in heads]
        n_rb = [quad(hp, h, 1, 0, upto) for hp, h in heads]
        n_rk = [quad(hp, h, 1, 1, upto) for hp, h in heads]
        tinv = [eye + jnp.where(pair_mask, m, 0.0) for m in m_ab]
        for lm in level_masks:
            prod = [_mm(jnp.where(lm, m, 0.0), t, _NN) for m, t in zip(m_ab, tinv)]
            tinv = [t + _mm(t, q, _NN) for t, q in zip(tinv, prod)]
        vh = [msk(vv[hp], h) for hp, h in heads]
        w1 = [_mm(m, x, _NN) for m, x in zip(m_ak, vh)]
        ap = [_mm(t, jnp.concatenate([msk(at[hp], h), w], axis=1), _NN)
              for t, (hp, h), w in zip(tinv, heads, w1)]
        nb = [_mm(m, x, _NN) for m, x in zip(n_rb, ap)]
        nv = [_mm(m, x, _NN) for m, x in zip(n_rk, vh)]

        def per_seq(x):
            return jnp.where(own_rows, jnp.concatenate([x] * SEQ_BLOCK, axis=1), 0.0)

        pairs = range(len(units))
        both = lambda xs, hp, cols: xs[HEAD_PAIR * hp][:, cols] + xs[HEAD_PAIR * hp + 1][:, cols]
        lo, hi = slice(0, LANES), slice(LANES, 2 * LANES)
        bh_x = [per_seq(bh[hp]) for hp in pairs]
        kh_x = [per_seq(kh[hp]) for hp in pairs]
        gam = [jnp.where(same_head, _mm(both(ap, hp, lo).T, bh_x[hp], _NN), 0.0) for hp in pairs]
        u = [jnp.where(same_head, _mm(jnp.concatenate([both(ap, hp, hi).T, vv[hp].T], axis=1),
                                      jnp.concatenate([bh_x[hp], kh_x[hp]], axis=0), _NN), 0.0)
             for hp in pairs]
        for hp, (j, pair) in enumerate(units):
            r_hat = rt[hp] + both(nb, hp, lo)
            y_zero = both(nb, hp, hi) + nv[HEAD_PAIR * hp] + nv[HEAD_PAIR * hp + 1]
            cols = pl.ds(pair * LANES, LANES)
            for n in range(SEQ_BLOCK):
                q = slice(n * RW_SUB, (n + 1) * RW_SUB)
                blk = slice(n * LANES, (n + 1) * LANES)
                st = state[pair][n]
                ys_ref[n, rs[j], cols] = _mm(r_hat[q], st, _NT) + y_zero[q]
                state[pair][n] = (st * gcv[hp][n * RW_SUB:n * RW_SUB + 1, :]
                                  + _mm(st, gam[hp][:, blk], _NN) + u[hp][:, blk])
        for pair in range(RW_PAIRS):
            for n in range(SEQ_BLOCK):
                s_ref[pair, n] = state[pair][n]
        return carry

    lax.fori_loop(0, n_sub // group, sub_chunk, 0)

    @pl.when(pl.program_id(1) == pl.num_programs(1) - 1)
    def _():
        if layer:
            sl_ref[0:layer] = earlier_ref[...]
        _tiles_to_heads(s_ref, sl_ref.at[layer], RW_PAIRS, transpose=False)

    y = ys_ref[...].reshape(rows, RW_WIDTH)
    inv = 1.0 / RW_HEAD
    mean = _head_sum(y, ones) * inv
    yc = y - mean
    var = _head_sum(yc * yc, ones) * inv
    y = yc * lax.rsqrt(var + RW_GN_EPS) * ln_w + ln_b
    bonus = _head_sum(r * kt * r_k, ones, split=False) * v
    _seq_block_store(y_ref, stage_ref, ((y + bonus) * g).reshape(shape3))


def _layer_state_specs(states, layer):
    tail = states.shape[2:]
    zeros = (0,) * len(tail)
    read = pl.BlockSpec((None, SEQ_BLOCK) + tail, lambda b, c: (layer, b) + zeros)
    upto = lambda k: pl.BlockSpec((k, SEQ_BLOCK) + tail, lambda b, c: (0, b) + zeros)
    return read, upto(layer + 1), ([upto(layer)] if layer else [])


def _rw_call(p2, n, l, shift0, states, layer, earlier, prm, t_chunk):
    assert t_chunk % RW_SUB == 0
    grid, in_specs, out_specs, in_shapes, out_shapes = _seq_specs(n, l, t_chunk, [RW_COLS], [RW_WIDTH])
    valid_in_chunk = min(t_chunk, l)
    kern = functools.partial(_rw_kernel, t_chunk=t_chunk, l_valid=valid_in_chunk, last_row=valid_in_chunk - 1,
                             layer=layer)
    st_in, st_out, st_earlier = _layer_state_specs(states, layer)
    sh_spec = pl.BlockSpec((SEQ_BLOCK, 1, RW_COLS), lambda b, c: (b, 0, 0))
    q_scr = pltpu.VMEM((SEQ_BLOCK, t_chunk, RW_WIDTH), F32)
    return pl.pallas_call(
        kern,
        out_shape=[jax.ShapeDtypeStruct(out_shapes[0], _branch_dtype(out_shapes[0])),
                   jax.ShapeDtypeStruct((n, 1, RW_COLS), F32),
                   jax.ShapeDtypeStruct((layer + 1,) + states.shape[1:], F32)],
        grid=grid,
        in_specs=in_specs + [sh_spec, st_in,
                             _const_spec((1, RW_COLS)), _const_spec((SUBLANES, RW_WIDTH)),
                             _const_spec((RW_LORA, RW_WIDTH)), _const_spec((RW_LORA, RW_WIDTH)),
                             _const_spec((RW_LORA, RW_WIDTH))] + st_earlier,
        out_specs=out_specs + [sh_spec, st_out],
        scratch_shapes=[pltpu.VMEM((SEQ_BLOCK, 1, RW_COLS), F32),
                        pltpu.VMEM((RW_PAIRS, SEQ_BLOCK, LANES, LANES), F32)] + [q_scr] * 9
        + _io_scratch(l, t_chunk, RW_COLS, RW_WIDTH),
        compiler_params=_cparams("arbitrary", "arbitrary"),
        name="rwkv7_mixer",
    )(p2.reshape(in_shapes[0]), shift0, states, prm["rw_mu"], prm["rw_vec"], prm["rw_w2"], prm["rw_a2"],
      prm["rw_g2"], *([earlier] if layer else []))


def _hg_kernel(z_ref, s0_ref, lower_ref, ng_ref, *rest, t_chunk, l_valid, layer):
    earlier_ref = rest[0] if layer else None
    (y_ref, sl_ref, s_ref, qs_ref, q_ref, k_ref, kh_ref, v_ref, b_ref, gc_ref, ys_ref, pad_ref,
     stage_ref) = rest[1 if layer else 0:]
    rows = SEQ_BLOCK * t_chunk
    n_sub = t_chunk // HG_SUB

    @pl.when(pl.program_id(1) == 0)
    def _():
        _heads_to_tiles(s0_ref, s_ref, HG_PAIRS, transpose=True)

    z = _seq_block_load(z_ref, pad_ref).reshape(rows, 4 * HG_WIDTH)
    q = jax.nn.silu(z[:, 0:HG_WIDTH])
    f = z[:, HG_WIDTH:2 * HG_WIDTH]
    i = z[:, 2 * HG_WIDTH:3 * HG_WIDTH]
    og = z[:, 3 * HG_WIDTH:4 * HG_WIDTH]
    lower = lower_ref[...]
    fgate = lower + (1.0 - lower) * jax.nn.sigmoid(f)
    log_f = jnp.log(jnp.maximum(fgate, HG_GATE_FLOOR))
    k = 1.0 - fgate
    if l_valid < t_chunk:
        live = lax.broadcasted_iota(jnp.int32, (rows, 1), 0) % t_chunk < l_valid
        log_f = jnp.where(live, log_f, 0.0)
        k = jnp.where(live, k, 0.0)
    cum = _cumsum_groups(log_f, HG_SUB)
    shape_g = (rows // HG_SUB, HG_SUB, HG_WIDTH)
    cum_end = jnp.broadcast_to(cum.reshape(shape_g)[:, HG_SUB - 1:HG_SUB, :], shape_g).reshape(rows, HG_WIDTH)
    shape3 = (SEQ_BLOCK, t_chunk, HG_WIDTH)
    qs_ref[...] = (q * jnp.exp(cum)).reshape(shape3)
    q_ref[...] = q.reshape(shape3)
    k_ref[...] = k.reshape(shape3)
    kh_ref[...] = (k * jnp.exp(cum_end - cum)).reshape(shape3)
    v_ref[...] = i.reshape(shape3)
    b_ref[...] = cum.reshape(shape3)
    gc_ref[...] = jnp.exp(cum_end).reshape(shape3)

    m_rows = SEQ_BLOCK * HG_SUB
    ones = _head_ones()
    wide = (m_rows, SEQ_BLOCK * LANES)
    own_rows = (lax.broadcasted_iota(jnp.int32, wide, 1) // LANES
                == lax.broadcasted_iota(jnp.int32, wide, 0) // HG_SUB)
    same_head = (lax.broadcasted_iota(jnp.int32, wide, 0) // HG_HEAD
                 == (lax.broadcasted_iota(jnp.int32, wide, 1) % LANES) // HG_HEAD)
    step = lax.broadcasted_iota(jnp.int32, (HG_SUB, LANES), 0)

    def sub_chunk(c, carry):
        rs = pl.ds(pl.multiple_of(c * HG_SUB, HG_SUB), HG_SUB)
        for hp in range(HG_PAIRS):
            cols = pl.ds(hp * LANES, LANES)
            ld = lambda ref: ref[:, rs, cols].reshape(m_rows, LANES)
            qs, qq, kk, kh, vv, bb, gcv = (ld(qs_ref), ld(q_ref), ld(k_ref), ld(kh_ref), ld(v_ref),
                                           ld(b_ref), ld(gc_ref))
            state = [s_ref[hp, n] for n in range(SEQ_BLOCK)]
            kh_x = jnp.where(own_rows, jnp.concatenate([kh] * SEQ_BLOCK, axis=1), 0.0)
            u = jnp.where(same_head, _mm(vv.T, kh_x, _NN), 0.0)
            for n in range(SEQ_BLOCK):
                sl = slice(n * HG_SUB, (n + 1) * HG_SUB)
                bn, qn, kn, vn = bb[sl], qq[sl], kk[sl], vv[sl]
                prods = []
                for t in range(HG_SUB):
                    keep = step <= t
                    prods.append(jnp.where(keep, qn[t:t + 1, :] * kn * jnp.exp(bn[t:t + 1, :] - bn), 0.0))
                att = _dot(jnp.concatenate(prods, axis=0).astype(BF16), ones)
                o_rows = [jnp.sum(att[t * HG_SUB:(t + 1) * HG_SUB] * vn, axis=0, keepdims=True)
                          for t in range(HG_SUB)]
                ys_ref[n, rs, cols] = _mm(qs[sl], state[n], _NT) + jnp.concatenate(o_rows, axis=0)
                blk = slice(n * LANES, (n + 1) * LANES)
                s_ref[hp, n] = state[n] * gcv[n * HG_SUB:n * HG_SUB + 1, :] + u[:, blk]
        return carry

    lax.fori_loop(0, n_sub, sub_chunk, 0)

    @pl.when(pl.program_id(1) == pl.num_programs(1) - 1)
    def _():
        if layer:
            sl_ref[0:layer] = earlier_ref[...]
        _tiles_to_heads(s_ref, sl_ref.at[layer], HG_PAIRS, transpose=True)

    o = ys_ref[...].reshape(rows, HG_WIDTH)
    ms = _head_sum(o * o, ones) * (1.0 / HG_HEAD)
    o = o * lax.rsqrt(ms + RMS_EPS) * ng_ref[...] * jax.nn.sigmoid(og)
    _seq_block_store(y_ref, stage_ref, o.reshape(shape3))


def _hg_call(z2, n, l, states, layer, earlier, lower, norm_g, t_chunk):
    assert t_chunk % HG_SUB == 0
    grid, in_specs, out_specs, in_shapes, out_shapes = _seq_specs(n, l, t_chunk, [4 * HG_WIDTH], [HG_WIDTH])
    kern = functools.partial(_hg_kernel, t_chunk=t_chunk, l_valid=min(t_chunk, l), layer=layer)
    st_in, st_out, st_earlier = _layer_state_specs(states, layer)
    q_scr = pltpu.VMEM((SEQ_BLOCK, t_chunk, HG_WIDTH), F32)
    return pl.pallas_call(
        kern,
        out_shape=[jax.ShapeDtypeStruct(out_shapes[0], _branch_dtype(out_shapes[0])),
                   jax.ShapeDtypeStruct((layer + 1,) + states.shape[1:], F32)],
        grid=grid,
        in_specs=in_specs + [st_in, _const_spec((1, HG_WIDTH)), _const_spec((1, HG_WIDTH))] + st_earlier,
        out_specs=out_specs + [st_out],
        scratch_shapes=[pltpu.VMEM((HG_PAIRS, SEQ_BLOCK, LANES, LANES), F32)] + [q_scr] * 8
        + _io_scratch(l, t_chunk, 4 * HG_WIDTH, HG_WIDTH),
        compiler_params=_cparams("arbitrary", "arbitrary"),
        name="hgrn2_mixer",
    )(z2.reshape(in_shapes[0]), states, lower.reshape(1, HG_WIDTH), norm_g.reshape(1, HG_WIDTH),
      *([earlier] if layer else []))


def _merge_kernel(x_ref, ya_ref, yb_ref, yc_ref, gates_ref, gt_ref, la_ref, lb_ref, lc_ref, wo_ref, o_ref):
    d = D_MODEL
    gates = _tile(gates_ref)
    lift = lambda y_ref, w_ref: _dot(_tile(y_ref)[...].astype(BF16), w_ref[...])
    m = (gates[:, 0:d].astype(F32) * lift(ya_ref, la_ref)
         + gates[:, d:2 * d].astype(F32) * lift(yb_ref, lb_ref)
         + gates[:, 2 * d:3 * d].astype(F32) * lift(yc_ref, lc_ref))
    _tile(o_ref)[...] = _tile(x_ref)[...] + gt_ref[0] * _dot(m.astype(BF16), wo_ref[...])


def _merge_call(x, n, l, ya, yb, yc, gates, gate1, prm, tm):
    d = x.shape[-1]
    gt_arr, gt_spec = _mod_operand(gate1, l, tm)
    shape, x_spec = _tok_layout(n, l, tm, d)
    row_spec = lambda w: _tok_layout(n, l, tm, w)[1]
    return pl.pallas_call(
        _merge_kernel,
        out_shape=jax.ShapeDtypeStruct(shape, F32),
        grid=(n * l // tm,),
        in_specs=[x_spec, row_spec(S5_WIDTH), row_spec(RW_WIDTH), row_spec(HG_WIDTH),
                  row_spec(N_BRANCH * d), gt_spec,
                  _const_spec((S5_WIDTH, d)), _const_spec((RW_WIDTH, d)), _const_spec((HG_WIDTH, d)),
                  _const_spec((d, d))],
        out_specs=x_spec,
        compiler_params=_cparams("arbitrary"),
        name="branch_merge",
    )(x, ya, yb, yc, gates, gt_arr, prm["w_lift_a"], prm["w_lift_b"], prm["w_lift_c"], prm["w_out"])


def _ffn_kernel(*refs, l, tm, has_state, final_norm):
    if has_state:
        (x_ref, g_ref, sh_ref, sc_ref, gt_ref, wup_ref, cw_ref, cb_ref, wdn_ref, fg_ref, st_ref,
         o_ref, tail_ref, h_ref, act_ref, hist_ref, buf_ref, old_ref) = refs
    else:
        (x_ref, g_ref, sh_ref, sc_ref, gt_ref, wup_ref, cw_ref, cb_ref, wdn_ref, fg_ref,
         o_ref, tail_ref, h_ref, act_ref, hist_ref) = refs
    i = pl.program_id(0)
    n_chunks = D_FF // FF_CHUNK
    n_hist = CONV_W - 1

    if has_state:
        t_idx = lax.broadcasted_iota(jnp.int32, (tm, 1), 0) % l
        after = [t_idx >= j + 1 for j in range(n_hist)]
        old_ref[...] = jnp.zeros_like(old_ref)
        buf_ref[:, 0:SUBLANES, :] = jnp.zeros((2, SUBLANES, FF_CHUNK), F32)
        for s in range(tm // l):
            for j in range(n_hist):
                old_ref[j, s * l:s * l + j + 1, :] = st_ref[s, n_hist - 1 - j:n_hist, :]
    else:
        @pl.when(i % (l // tm) == 0)
        def _():
            hist_ref[...] = jnp.zeros_like(hist_ref)

    x = _tile(x_ref)[...]
    h = _rms_modulate(x, g_ref[...], sh_ref[0], sc_ref[0])
    groups = tm // SUBLANES
    if not has_state:
        h = jnp.swapaxes(h.reshape(SUBLANES, groups, D_MODEL), 0, 1).reshape(tm, D_MODEL)
        row8 = lax.broadcasted_iota(jnp.int32, (SUBLANES, FF_CHUNK), 0)
    h_ref[...] = h.astype(BF16)

    def conv_half(c, half):
        col0 = half * D_FF + c * FF_CHUNK
        cols = slice(col0, col0 + FF_CHUNK)
        up = _dot(h_ref[...], wup_ref[:, cols])
        if has_state:
            buf_ref[half, SUBLANES:SUBLANES + tm, :] = up
            prev = [jnp.where(after[j], buf_ref[half, SUBLANES - 1 - j:SUBLANES - 1 - j + tm, :], 0.0)
                    + old_ref[j, :, cols] for j in range(n_hist)]
            for s in range(tm // l):
                r0 = SUBLANES + (s + 1) * l - n_hist
                tail_ref[s, :, cols] = buf_ref[half, r0:r0 + n_hist, :]
        else:
            late = [up[tm - (j + 1) * SUBLANES:tm - j * SUBLANES, :] for j in range(n_hist)]
            first = [jnp.where(row8 == 0, pltpu.roll(hist_ref[c, half, j], 1, 0), pltpu.roll(late[j], 1, 0))
                     for j in range(n_hist)]
            for j in range(n_hist):
                hist_ref[c, half, j] = late[j]
            prev = [jnp.concatenate([first[0], up[0:tm - SUBLANES, :]], axis=0),
                    jnp.concatenate([first[1], first[0], up[0:tm - 2 * SUBLANES, :]], axis=0)]
            tail_ref[0, 0:1, cols] = up[tm - SUBLANES - 1:tm - SUBLANES, :]
            tail_ref[0, 1:2, cols] = up[tm - 1:tm, :]
        return (cb_ref[:, cols] + cw_ref[0:1, cols] * prev[1] + cw_ref[1:2, cols] * prev[0]
                + cw_ref[2:3, cols] * up)

    for c in range(n_chunks):
        act = _gelu(conv_half(c, 0)) * conv_half(c, 1)
        act_ref[:, c * FF_CHUNK:(c + 1) * FF_CHUNK] = act.astype(BF16)
    acc = _dot(act_ref[...], wdn_ref[...])
    if not has_state:
        acc = jnp.swapaxes(acc.reshape(groups, SUBLANES, D_MODEL), 0, 1).reshape(tm, D_MODEL)
    out = x + gt_ref[0] * acc
    if final_norm:
        ms = jnp.mean(out * out, axis=-1, keepdims=True)
        out = out * lax.rsqrt(ms + RMS_EPS) * fg_ref[...]
    _tile(o_ref)[...] = out


def _ffn_call(x, n, l, g, shift, scale, gate2, prm, final_g, conv_state, tm, final_norm):
    assert CONV_W == 3
    d = x.shape[-1]
    has_state = conv_state is not None
    sh_arr, sh_spec = _mod_operand(shift, l, tm)
    sc_arr, sc_spec = _mod_operand(scale, l, tm)
    gt_arr, gt_spec = _mod_operand(gate2, l, tm)
    shape, x_spec = _tok_layout(n, l, tm, d)
    in_specs = [x_spec, _const_spec((1, d)), sh_spec, sc_spec, gt_spec,
                _const_spec((d, 2 * D_FF)), _const_spec((CONV_W, 2 * D_FF)), _const_spec((1, 2 * D_FF)),
                _const_spec((D_FF, d)), _const_spec((1, d))]
    operands = [x, g.reshape(1, d), sh_arr, sc_arr, gt_arr, prm["w_up"], prm["conv_w"], prm["conv_b"],
                prm["w_down"], final_g.reshape(1, d)]
    scratch = [pltpu.VMEM((tm, d), BF16), pltpu.VMEM((tm, D_FF), BF16),
               pltpu.VMEM((D_FF // FF_CHUNK, 2, CONV_W - 1, SUBLANES, FF_CHUNK), F32)]
    if has_state:
        assert tm % l == 0 and l >= CONV_W - 1
        state_spec = pl.BlockSpec((tm // l, CONV_W - 1, 2 * D_FF), lambda i: (i, 0, 0))
        in_specs.append(state_spec)
        operands.append(conv_state)
        tail_spec = state_spec
        scratch += [pltpu.VMEM((2, tm + SUBLANES, FF_CHUNK), F32), pltpu.VMEM((CONV_W - 1, tm, 2 * D_FF), F32)]
    else:
        assert l % tm == 0
        per = l // tm
        tail_spec = pl.BlockSpec((1, CONV_W - 1, 2 * D_FF), lambda i: (i // per, 0, 0))
    kern = functools.partial(_ffn_kernel, l=l, tm=tm, has_state=has_state, final_norm=final_norm)
    return pl.pallas_call(
        kern,
        out_shape=[jax.ShapeDtypeStruct(shape, F32), jax.ShapeDtypeStruct((n, CONV_W - 1, 2 * D_FF), F32)],
        grid=(n * l // tm,),
        in_specs=in_specs,
        out_specs=[x_spec, tail_spec],
        scratch_shapes=scratch,
        compiler_params=_cparams("arbitrary"),
        name="conv_ffn",
    )(*operands)


def _prepare_layer(p):
    out = {}
    w_in = p["w_in"]
    c1 = S5_WIDTH
    c2 = c1 + RW_COLS
    c3 = c2 + 4 * HG_WIDTH
    out["w_in_a"] = w_in[:, :c1].astype(BF16)
    out["w_in_b"] = w_in[:, c1:c2].astype(BF16)
    out["w_in_c"] = w_in[:, c2:c3].astype(BF16)
    out["w_in_g"] = w_in[:, c3:].astype(BF16)
    out["w_ada"] = p["w_ada"].astype(BF16)
    for name in ("w_lift_a", "w_lift_b", "w_lift_c", "w_out", "w_up", "w_down"):
        out[name] = p[name].astype(BF16)
    out["conv_w"] = p["conv_w"]
    out["conv_b"] = p["conv_b"].reshape(1, 2 * D_FF)

    lr = p["s5_lambda_re"]
    li = p["s5_lambda_im"]
    dt = jnp.exp(p["s5_log_dt"])[:, None]
    mag = jnp.exp(lr * dt)
    ar = mag * jnp.cos(li * dt)
    ai = mag * jnp.sin(li * dt)
    den = lr * lr + li * li
    zr = ((ar - 1.0) * lr + ai * li) / den
    zi = (ai * lr - (ar - 1.0) * li) / den
    bbr = zr[..., None] * p["s5_b_re"] - zi[..., None] * p["s5_b_im"]
    bbi = zr[..., None] * p["s5_b_im"] + zi[..., None] * p["s5_b_re"]
    eye = jnp.eye(S5_GROUPS, dtype=F32)
    bmat = jnp.einsum("gh,rgpc->gcrhp", eye, jnp.stack([bbr, bbi])).reshape(S5_WIDTH, 2 * S5_FLAT)
    cmat = jnp.einsum("hg,rgcp->rhpgc", eye, jnp.stack([p["s5_c_re"], -p["s5_c_im"]])).reshape(
        2 * S5_FLAT, S5_WIDTH)
    out["s5_b"] = bmat.astype(BF16)
    out["s5_c"] = cmat.astype(BF16)
    out["s5_a"] = jnp.stack([ar.reshape(S5_FLAT), ai.reshape(S5_FLAT)])
    out["s5_d"] = p["s5_d"].reshape(1, S5_WIDTH)
    out["s5_w_glu"] = p["s5_w_glu"].astype(BF16)
    out["s5_b_glu"] = p["s5_b_glu"].reshape(1, S5_WIDTH)

    out["rw_mu"] = p["rw_mu"].reshape(1, RW_COLS)
    out["rw_vec"] = jnp.stack([p["rw_w0"], p["rw_a0"], p["rw_k_k"], p["rw_k_a"],
                               p["rw_r_k"].reshape(RW_WIDTH), p["rw_ln_w"], p["rw_ln_b"],
                               jnp.zeros((RW_WIDTH,), F32)])
    zw = jnp.zeros((RW_LORA, RW_WIDTH), F32)
    out["rw_w2"] = zw.at[0:RW_DECAY_LORA].set(p["rw_w2"]).astype(BF16)
    out["rw_a2"] = zw.at[RW_DECAY_LORA:RW_DECAY_LORA + RW_AAA_LORA].set(p["rw_a2"]).astype(BF16)
    out["rw_g2"] = zw.at[RW_DECAY_LORA + RW_AAA_LORA:].set(p["rw_g2"]).astype(BF16)
    out["hg_norm"] = p["hg_norm"]
    out["g_mix"] = p["g_mix"]
    out["g_ffn"] = p["g_ffn"]
    out["b_ada"] = p["b_ada"]
    return out


def _chunk_for(l, sub, t_chunk):
    return t_chunk if l % t_chunk == 0 else -(-l // sub) * sub


def _run_trunk(x, mods, st_s5, st_shift, st_rw, st_hg, st_conv, lower, final_g, prms, t_chunk):
    n, l, d = x.shape
    rows = n * l
    tm = min(ROW_TILE, rows)
    depth = len(prms)
    assert (l % tm == 0) == (l % t_chunk == 0)
    h = x.reshape(_tok_layout(n, l, tm, d)[0])
    out_s5, out_shift, out_conv = [], [], []
    rw_new = hg_new = None
    for layer in range(depth):
        prm = prms[layer]
        sh1, sc1, gt1, sh2, sc2, gt2 = jnp.split(mods[layer], 6, axis=-1)
        za, zb, zc, gates = _in_proj_call(h, n, l, prm["g_mix"], sh1, sc1, prm["w_in_a"], prm["w_in_b"],
                                          prm["w_in_c"], prm["w_in_g"], tm)

        s5_in = jnp.concatenate([st_s5[layer][..., 0].reshape(n, S5_FLAT),
                                 st_s5[layer][..., 1].reshape(n, S5_FLAT)], axis=1)
        ya, s5_new = _s5_call(za, n, l, s5_in, prm, _chunk_for(l, SUBLANES, t_chunk))
        yb, shift_new, rw_new = _rw_call(zb, n, l, st_shift[layer].reshape(n, 1, RW_COLS), st_rw, layer, rw_new,
                                         prm, _chunk_for(l, RW_SUB, t_chunk))
        yc, hg_new = _hg_call(zc, n, l, st_hg, layer, hg_new, lower[layer], prm["hg_norm"],
                              _chunk_for(l, HG_SUB, t_chunk))

        h = _merge_call(h, n, l, ya, yb, yc, gates, gt1, prm, tm)
        ffn_tm = tm if st_conv is None else min(tm, 128)
        h, conv_new = _ffn_call(h, n, l, prm["g_ffn"], sh2, sc2, gt2, prm, final_g,
                                None if st_conv is None else st_conv[layer], ffn_tm,
                                final_norm=(layer == depth - 1))
        out_s5.append(jnp.stack([s5_new[:, :S5_FLAT].reshape(n, S5_GROUPS, S5_STATE),
                                 s5_new[:, S5_FLAT:].reshape(n, S5_GROUPS, S5_STATE)], axis=-1))
        out_shift.append(shift_new.reshape(n, RW_COLS))
        out_conv.append(conv_new)
    y = h.reshape(n, l, d).astype(x.dtype)
    return y, (jnp.stack(out_s5), jnp.stack(out_shift), rw_new, hg_new, jnp.stack(out_conv))


def kernel(x_prompt, x_sample, c_prompt, c_sample, state_s5, state_rwkv_shift, state_rwkv, state_hgrn, state_ffn_conv, w_ada, b_ada, g_mix, g_ffn, w_in, s5_lambda_re, s5_lambda_im, s5_log_dt, s5_b_re, s5_b_im, s5_c_re, s5_c_im, s5_d, s5_w_glu, s5_b_glu, rw_mu, rw_w0, rw_w2, rw_a0, rw_a2, rw_g2, rw_k_k, rw_k_a, rw_r_k, rw_ln_w, rw_ln_b, hg_lb, hg_norm, w_lift_a, w_lift_b, w_lift_c, w_out, w_up, conv_w, conv_b, w_down, final_g):
    per_layer = {
        "w_ada": w_ada, "b_ada": b_ada, "g_mix": g_mix, "g_ffn": g_ffn, "w_in": w_in,
        "s5_lambda_re": s5_lambda_re, "s5_lambda_im": s5_lambda_im, "s5_log_dt": s5_log_dt,
        "s5_b_re": s5_b_re, "s5_b_im": s5_b_im, "s5_c_re": s5_c_re, "s5_c_im": s5_c_im,
        "s5_d": s5_d, "s5_w_glu": s5_w_glu, "s5_b_glu": s5_b_glu,
        "rw_mu": rw_mu, "rw_w0": rw_w0, "rw_w2": rw_w2, "rw_a0": rw_a0, "rw_a2": rw_a2,
        "rw_g2": rw_g2, "rw_k_k": rw_k_k, "rw_k_a": rw_k_a, "rw_r_k": rw_r_k,
        "rw_ln_w": rw_ln_w, "rw_ln_b": rw_ln_b, "hg_norm": hg_norm,
        "w_lift_a": w_lift_a, "w_lift_b": w_lift_b, "w_lift_c": w_lift_c, "w_out": w_out,
        "w_up": w_up, "conv_w": conv_w, "conv_b": conv_b, "w_down": w_down,
    }
    depth = w_ada.shape[0]
    prms = [_prepare_layer({k: v[layer] for k, v in per_layer.items()}) for layer in range(depth)]

    lbp = jax.nn.softmax(hg_lb.astype(F32), axis=0)
    lower = jnp.cumsum(lbp, axis=0) - lbp[0]

    nb = x_prompt.shape[0]
    ns = x_sample.shape[0]
    c_all = jnp.concatenate([c_prompt, c_sample], axis=0).astype(F32)
    mods = [_ada_call(c_all, prms[layer]["w_ada"], prms[layer]["b_ada"]) for layer in range(depth)]
    mods_p = [m[:nb] for m in mods]
    mods_s = [m[nb:] for m in mods]

    z_s5 = jnp.zeros((depth, nb) + state_s5.shape[2:], F32)
    z_shift = jnp.zeros((depth, nb) + state_rwkv_shift.shape[2:], F32)
    z_rw = jnp.zeros((depth, nb) + state_rwkv.shape[2:], F32)
    z_hg = jnp.zeros((depth, nb) + state_hgrn.shape[2:], F32)

    y_prompt, (s5_p, shift_p, rw_p, hg_p, conv_p) = _run_trunk(
        x_prompt, mods_p, z_s5, z_shift, z_rw, z_hg, None, lower, final_g, prms, t_chunk=64)
    y_sample, (s5_s, shift_s, rw_s, hg_s, conv_s) = _run_trunk(
        x_sample, mods_s, state_s5, state_rwkv_shift, state_rwkv, state_hgrn, state_ffn_conv,
        lower, final_g, prms, t_chunk=64)
    return (y_prompt, y_sample, s5_p, shift_p, rw_p, hg_p, conv_p, s5_s, shift_s, rw_s, hg_s, conv_s)
```

```python
import functools
import math

import jax
import jax.numpy as jnp
from jax import lax
from jax.experimental import pallas as pl
from jax.experimental.pallas import tpu as pltpu

F32 = jnp.float32
BF16 = jnp.bfloat16

D_MODEL = 1024
S5_WIDTH = D_MODEL // 4
S5_GROUP = 16
S5_GROUPS = S5_WIDTH // S5_GROUP
S5_STATE = 64
S5_FLAT = S5_GROUPS * S5_STATE
RW_WIDTH = D_MODEL // 2
RW_HEAD = 64
RW_HEADS = RW_WIDTH // RW_HEAD
RW_DECAY_LORA = 32
RW_AAA_LORA = 32
RW_GATE_LORA = 64
RW_LORA = RW_DECAY_LORA + RW_AAA_LORA + RW_GATE_LORA
RW_COLS = 3 * RW_WIDTH + RW_LORA
RW_GN_EPS = 1e-5 * RW_HEAD
HG_WIDTH = D_MODEL // 4
HG_HEAD = 64
HG_HEADS = HG_WIDTH // HG_HEAD
HG_GATE_FLOOR = 1e-30
N_BRANCH = 3
D_FF = 256 * ((8 * D_MODEL // 3 + 255) // 256)
CONV_W = 3
RMS_EPS = 1e-6

LANES = 128
SUBLANES = 8
MXU_COLS = 256
VMEM_LIMIT_BYTES = 56 * 1024 * 1024

SEQ_BLOCK = SUBLANES
HEAD_PAIR = LANES // RW_HEAD
ROW_TILE = 512
FF_CHUNK = 256
IN_PROJ_CHUNK = 512
MAX_SUB = 16
RW_PAIRS = RW_HEADS // HEAD_PAIR
HG_PAIRS = HG_HEADS // HEAD_PAIR

_NN = (((1,), (0,)), ((), ()))
_NT = (((1,), (1,)), ((), ()))


def _cparams(*sem):
    return pltpu.CompilerParams(dimension_semantics=sem, vmem_limit_bytes=VMEM_LIMIT_BYTES)


def _const_spec(shape):
    nd = len(shape)
    return pl.BlockSpec(shape, lambda *_: (0,) * nd, pipeline_mode=pl.Buffered(1))


def _dot(a, b):
    return jnp.dot(a, b, preferred_element_type=F32)


def _split_bf16(x):
    hi = x.astype(BF16)
    lo = (x - hi.astype(F32)).astype(BF16)
    return hi, lo


def _head_ones():
    r = lax.broadcasted_iota(jnp.int32, (LANES, LANES), 0) // RW_HEAD
    c = lax.broadcasted_iota(jnp.int32, (LANES, LANES), 1) // RW_HEAD
    return (r == c).astype(BF16)


def _head_sum(x, ones, split=True):
    hi, lo = _split_bf16(x) if split else (x.astype(BF16), None)
    tiles = []
    for c in range(0, x.shape[1], LANES):
        t = _dot(hi[:, c:c + LANES], ones)
        tiles.append(t + _dot(lo[:, c:c + LANES], ones) if split else t)
    return tiles[0] if len(tiles) == 1 else jnp.concatenate(tiles, axis=1)


def _softplus(x):
    return jnp.maximum(x, 0.0) + jnp.log(1.0 + jnp.exp(-jnp.abs(x)))


def _gelu(x):
    c = 2.0 * math.sqrt(2.0 / math.pi)
    return x * jax.nn.sigmoid(x * (c + (c * 0.044715) * (x * x)))


def _rms_modulate(x, g, shift, scale):
    ms = jnp.mean(x * x, axis=-1, keepdims=True)
    return x * lax.rsqrt(ms + RMS_EPS) * g * (1.0 + scale) + shift


def _seq_specs(n, l, t_chunk, widths_in, widths_out):
    nb = n // SEQ_BLOCK
    if l % t_chunk == 0:
        spec = lambda w: pl.BlockSpec((SEQ_BLOCK, t_chunk, w), lambda b, c: (b, c, 0))
        shape = lambda w: (n, l, w)
        nt = l // t_chunk
    else:
        assert l < t_chunk
        spec = lambda w: pl.BlockSpec((SEQ_BLOCK * l, w), lambda b, c: (b, 0))
        shape = lambda w: (n * l, w)
        nt = 1
    return ((nb, nt), [spec(w) for w in widths_in], [spec(w) for w in widths_out],
            [shape(w) for w in widths_in], [shape(w) for w in widths_out])


def _branch_dtype(shape):
    return BF16 if len(shape) == 3 else F32


def _seq_block_load(z_ref, pad_ref):
    if len(z_ref.shape) == 3:
        return z_ref[...]
    l = z_ref.shape[0] // SEQ_BLOCK
    pad_ref[...] = jnp.zeros_like(pad_ref)
    for n in range(SEQ_BLOCK):
        pad_ref[n, 0:l, :] = z_ref[n * l:(n + 1) * l, :]
    return pad_ref[...]


def _seq_block_store(y_ref, stage_ref, y3):
    if len(y_ref.shape) == 3:
        y_ref[...] = y3.astype(y_ref.dtype)
        return
    l = y_ref.shape[0] // SEQ_BLOCK
    stage_ref[...] = y3
    for n in range(SEQ_BLOCK):
        y_ref[n * l:(n + 1) * l, :] = stage_ref[n, 0:l, :]


def _heads_to_tiles(s_ref, tile_ref, n_pairs, transpose):
    zero = jnp.zeros((RW_HEAD, RW_HEAD), F32)
    for n in range(SEQ_BLOCK):
        for hp in range(n_pairs):
            a, b = s_ref[n, HEAD_PAIR * hp], s_ref[n, HEAD_PAIR * hp + 1]
            if transpose:
                a, b = a.T, b.T
            tile_ref[hp, n] = jnp.concatenate([jnp.concatenate([a, zero], axis=1),
                                               jnp.concatenate([zero, b], axis=1)], axis=0)


def _tiles_to_heads(tile_ref, s_ref, n_pairs, transpose):
    for n in range(SEQ_BLOCK):
        for hp in range(n_pairs):
            t = tile_ref[hp, n]
            a, b = t[0:RW_HEAD, 0:RW_HEAD], t[RW_HEAD:LANES, RW_HEAD:LANES]
            if transpose:
                a, b = a.T, b.T
            s_ref[n, HEAD_PAIR * hp] = a
            s_ref[n, HEAD_PAIR * hp + 1] = b


def _ada_kernel(c_ref, w_ref, b_ref, o_ref):
    c = c_ref[...]
    o_ref[...] = _dot(jax.nn.silu(c).astype(BF16), w_ref[...]) + b_ref[...]


def _ada_call(c, w_bf16, b):
    n, d = c.shape
    cols = w_bf16.shape[1]
    tn = cols // 4
    return pl.pallas_call(
        _ada_kernel,
        out_shape=jax.ShapeDtypeStruct((n, cols), F32),
        grid=(cols // tn,),
        in_specs=[_const_spec((n, d)),
                  pl.BlockSpec((d, tn), lambda j: (0, j)),
                  pl.BlockSpec((1, tn), lambda j: (0, j))],
        out_specs=pl.BlockSpec((n, tn), lambda j: (0, j)),
        compiler_params=_cparams("arbitrary"),
        name="ada_mod",
    )(c, w_bf16, b.reshape(1, cols))


def _mod_operand(m, l, tm):
    n, d = m.shape
    if l % tm == 0:
        per = l // tm
        return m.reshape(n, 1, d), pl.BlockSpec((1, 1, d), lambda i: (i // per, 0, 0))
    assert tm % l == 0
    return jnp.repeat(m, l, axis=0).reshape(1, n * l, d), pl.BlockSpec((1, tm, d), lambda i: (0, i, 0))


def _tok_layout(n, l, tm, w):
    if l % tm == 0:
        per = l // tm
        return (n, l, w), pl.BlockSpec((1, tm, w), lambda i: (i // per, i % per, 0))
    return (n * l, w), pl.BlockSpec((tm, w), lambda i: (i, 0))


def _tile(ref):
    return ref.at[0] if len(ref.shape) == 3 else ref


def _in_proj_kernel(x_ref, g_ref, sh_ref, sc_ref, wa_ref, wb_ref, wc_ref, wg_ref,
                    oa_ref, ob_ref, oc_ref, og_ref):
    h = _rms_modulate(_tile(x_ref)[...], g_ref[...], sh_ref[0], sc_ref[0]).astype(BF16)

    def project(w_ref, o_ref, fn=None):
        width = w_ref.shape[1]
        out = _tile(o_ref)
        for c0 in range(0, width, IN_PROJ_CHUNK):
            cols = slice(c0, min(c0 + IN_PROJ_CHUNK, width))
            z = _dot(h, w_ref[:, cols])
            out[:, cols] = (z if fn is None else fn(z)).astype(out.dtype)

    project(wa_ref, oa_ref)
    project(wb_ref, ob_ref)
    project(wc_ref, oc_ref)
    project(wg_ref, og_ref, jax.nn.sigmoid)


def _in_proj_call(x, n, l, g, shift, scale, w_a, w_b, w_c, w_g, tm):
    d = x.shape[-1]
    sh_arr, sh_spec = _mod_operand(shift, l, tm)
    sc_arr, sc_spec = _mod_operand(scale, l, tm)
    widths = [w.shape[1] for w in (w_a, w_b, w_c, w_g)]
    dtypes = [F32, F32, F32, BF16]
    outs = [_tok_layout(n, l, tm, w) for w in widths]
    return pl.pallas_call(
        _in_proj_kernel,
        out_shape=[jax.ShapeDtypeStruct(shape, dt) for (shape, _), dt in zip(outs, dtypes)],
        grid=(n * l // tm,),
        in_specs=[_tok_layout(n, l, tm, d)[1], _const_spec((1, d)), sh_spec, sc_spec]
        + [_const_spec(w.shape) for w in (w_a, w_b, w_c, w_g)],
        out_specs=[spec for _, spec in outs],
        compiler_params=_cparams("arbitrary"),
        name="in_proj",
    )(x, g.reshape(1, d), sh_arr, sc_arr, w_a, w_b, w_c, w_g)


def _s5_kernel(u_ref, s0_ref, b_ref, c_ref, a_ref, d_ref, wg_ref, bg_ref,
               y_ref, sl_ref, e_ref, st_ref, pad_ref, stage_ref, *, t_chunk, n_steps):
    @pl.when(pl.program_id(1) == 0)
    def _():
        st_ref[...] = s0_ref[...]

    rows = SEQ_BLOCK * t_chunk
    ut = jnp.swapaxes(_seq_block_load(u_ref, pad_ref), 0, 1).reshape(rows, S5_WIDTH)
    ub = ut.astype(BF16)
    for c0 in range(0, 2 * S5_FLAT, MXU_COLS):
        ch0 = (c0 % S5_FLAT) // S5_STATE * S5_GROUP
        k0 = ch0 // LANES * LANES
        assert ch0 + MXU_COLS // S5_STATE * S5_GROUP <= k0 + LANES
        e_ref[:, c0:c0 + MXU_COLS] = _dot(ub[:, k0:k0 + LANES], b_ref[k0:k0 + LANES, c0:c0 + MXU_COLS])

    ar = jnp.broadcast_to(a_ref[0:1, :], (SEQ_BLOCK, S5_FLAT))
    ai = jnp.broadcast_to(a_ref[1:2, :], (SEQ_BLOCK, S5_FLAT))

    def step(t, carry):
        sr, si = carry
        r = pl.ds(pl.multiple_of(t * SEQ_BLOCK, SEQ_BLOCK), SEQ_BLOCK)
        nr = ar * sr - ai * si + e_ref[r, 0:S5_FLAT]
        ni = ar * si + ai * sr + e_ref[r, S5_FLAT:2 * S5_FLAT]
        e_ref[r, 0:S5_FLAT] = nr
        e_ref[r, S5_FLAT:2 * S5_FLAT] = ni
        return nr, ni

    sr, si = lax.fori_loop(0, n_steps, step, (st_ref[:, 0:S5_FLAT], st_ref[:, S5_FLAT:2 * S5_FLAT]))
    st_ref[:, 0:S5_FLAT] = sr
    st_ref[:, S5_FLAT:2 * S5_FLAT] = si
    sl_ref[...] = st_ref[...]

    y = _dot(e_ref[...].astype(BF16), c_ref[...]) + d_ref[...] * ut
    y = _gelu(y)
    y = y * jax.nn.sigmoid(_dot(y.astype(BF16), wg_ref[...]) + bg_ref[...])
    _seq_block_store(y_ref, stage_ref, jnp.swapaxes(y.reshape(t_chunk, SEQ_BLOCK, S5_WIDTH), 0, 1))


def _io_scratch(l, t_chunk, width_in, width_out):
    if l % t_chunk == 0:
        return [pltpu.VMEM((SUBLANES, LANES), F32)] * 2
    return [pltpu.VMEM((SEQ_BLOCK, t_chunk, width_in), F32), pltpu.VMEM((SEQ_BLOCK, t_chunk, width_out), F32)]


def _s5_call(u2, n, l, s0, prm, t_chunk):
    grid, in_specs, out_specs, in_shapes, out_shapes = _seq_specs(n, l, t_chunk, [S5_WIDTH], [S5_WIDTH])
    kern = functools.partial(_s5_kernel, t_chunk=t_chunk, n_steps=min(t_chunk, l))
    y, s_new = pl.pallas_call(
        kern,
        out_shape=[jax.ShapeDtypeStruct(out_shapes[0], _branch_dtype(out_shapes[0])),
                   jax.ShapeDtypeStruct((n, 2 * S5_FLAT), F32)],
        grid=grid,
        in_specs=in_specs + [
            pl.BlockSpec((SEQ_BLOCK, 2 * S5_FLAT), lambda b, c: (b, 0)),
            _const_spec((S5_WIDTH, 2 * S5_FLAT)), _const_spec((2 * S5_FLAT, S5_WIDTH)),
            _const_spec((2, S5_FLAT)), _const_spec((1, S5_WIDTH)),
            _const_spec((S5_WIDTH, S5_WIDTH)), _const_spec((1, S5_WIDTH))],
        out_specs=out_specs + [pl.BlockSpec((SEQ_BLOCK, 2 * S5_FLAT), lambda b, c: (b, 0))],
        scratch_shapes=[pltpu.VMEM((SEQ_BLOCK * t_chunk, 2 * S5_FLAT), F32),
                        pltpu.VMEM((SEQ_BLOCK, 2 * S5_FLAT), F32)]
        + _io_scratch(l, t_chunk, S5_WIDTH, S5_WIDTH),
        compiler_params=_cparams("arbitrary", "arbitrary"),
        name="s5_mixer",
    )(u2.reshape(in_shapes[0]), s0, prm["s5_b"], prm["s5_c"], prm["s5_a"], prm["s5_d"], prm["s5_w_glu"],
      prm["s5_b_glu"])
    return y, s_new


def _mm(a, b, dims):
    return lax.dot_general(a.astype(BF16), b.astype(BF16), dims, preferred_element_type=F32)


def _cumsum_groups(x, group):
    pos = lax.broadcasted_iota(jnp.int32, (x.shape[0], 1), 0) % group
    s = 1
    while s < group:
        x = x + jnp.where(pos >= s, pltpu.roll(x, s, 0), 0.0)
        s *= 2
    return x


def _rw_kernel(p_ref, sh0_ref, s0_ref, mu_ref, vec_ref, w2_ref, a2_ref, g2_ref, *rest,
               t_chunk, l_valid, last_row, layer):
    earlier_ref = rest[0] if layer else None
    (y_ref, shl_ref, sl_ref,
     pbuf_ref, s_ref, at_ref, bt_ref, kt_ref, rt_ref, bh_ref, kh_ref, v_ref, gc_ref, ys_ref,
     pad_ref, stage_ref) = rest[1 if layer else 0:]
    rows = SEQ_BLOCK * t_chunk
    sub = min(MAX_SUB, t_chunk)
    n_sub = t_chunk // sub

    @pl.when(pl.program_id(1) == 0)
    def _():
        pbuf_ref[...] = sh0_ref[...]
        _heads_to_tiles(s0_ref, s_ref, RW_PAIRS, transpose=False)

    p3 = _seq_block_load(p_ref, pad_ref)
    first = lax.broadcasted_iota(jnp.int32, (1, t_chunk, 1), 1) == 0
    prev3 = jnp.where(first, pbuf_ref[...], pltpu.roll(p3, 1, 1))
    last = p3[:, last_row:last_row + 1, :]
    pbuf_ref[...] = last
    shl_ref[...] = last

    xm = (p3 + (prev3 - p3) * mu_ref[...]).reshape(rows, RW_COLS)
    r = xm[:, 0:RW_WIDTH]
    k = xm[:, RW_WIDTH:2 * RW_WIDTH]
    v = xm[:, 2 * RW_WIDTH:3 * RW_WIDTH]
    lora = xm[:, 3 * RW_WIDTH:RW_COLS]
    w0, a0, k_k, k_a = vec_ref[0:1, :], vec_ref[1:2, :], vec_ref[2:3, :], vec_ref[3:4, :]
    r_k, ln_w, ln_b = vec_ref[4:5, :], vec_ref[5:6, :], vec_ref[6:7, :]

    w = -_softplus(-(w0 + _dot(jnp.tanh(lora).astype(BF16), w2_ref[...]))) - 0.5
    log_decay = -jnp.exp(w)
    a = jax.nn.sigmoid(a0 + _dot(lora.astype(BF16), a2_ref[...]))
    g = _dot(jax.nn.sigmoid(lora).astype(BF16), g2_ref[...])

    ones = _head_ones()
    kk = k * k_k
    kk = kk * lax.rsqrt(jnp.maximum(_head_sum(kk * kk, ones), 1e-24))
    kt = k * (1.0 + (a - 1.0) * k_a)
    kka = kk * a
    if l_valid < t_chunk:
        live = lax.broadcasted_iota(jnp.int32, (rows, 1), 0) % t_chunk < l_valid
        log_decay = jnp.where(live, log_decay, 0.0)
        kk = jnp.where(live, kk, 0.0)
        kka = jnp.where(live, kka, 0.0)
        kt_live = jnp.where(live, kt, 0.0)
    else:
        kt_live = kt

    cum = _cumsum_groups(log_decay, sub)
    shape_g = (rows // sub, sub, RW_WIDTH)
    cum_end = jnp.broadcast_to(cum.reshape(shape_g)[:, sub - 1:sub, :], shape_g).reshape(rows, RW_WIDTH)
    g_in = jnp.exp(cum)
    g_inv = jnp.exp(-cum)
    g_ex = jnp.exp(cum - log_decay)
    g_out = jnp.exp(cum_end - cum)
    shape3 = (SEQ_BLOCK, t_chunk, RW_WIDTH)
    at_ref[...] = (-kk * g_ex).reshape(shape3)
    bt_ref[...] = (kka * g_inv).reshape(shape3)
    kt_ref[...] = (kt_live * g_inv).reshape(shape3)
    rt_ref[...] = (r * g_in).reshape(shape3)
    bh_ref[...] = (kka * g_out).reshape(shape3)
    kh_ref[...] = (kt_live * g_out).reshape(shape3)
    v_ref[...] = v.reshape(shape3)
    gc_ref[...] = jnp.exp(cum_end).reshape(shape3)

    m_rows = SEQ_BLOCK * sub
    ri = lax.broadcasted_iota(jnp.int32, (m_rows, m_rows), 0)
    ci = lax.broadcasted_iota(jnp.int32, (m_rows, m_rows), 1)
    same_seq = (ri // sub) == (ci // sub)
    before = same_seq & ((ci % sub) < (ri % sub))
    upto = same_seq & ((ci % sub) <= (ri % sub))
    eye = (ri == ci).astype(F32)
    pair_mask = (ri // 2) == (ci // 2)
    level_masks = []
    s = 2
    while s < sub:
        level_masks.append(((ri // (2 * s)) == (ci // (2 * s))) & ((ri // s) != (ci // s)))
        s *= 2
    lane = lax.broadcasted_iota(jnp.int32, (m_rows, LANES), 1)
    head_mask = [lane < RW_HEAD, lane >= RW_HEAD]
    wide = (m_rows, SEQ_BLOCK * LANES)
    own_rows = (lax.broadcasted_iota(jnp.int32, wide, 1) // LANES
                == lax.broadcasted_iota(jnp.int32, wide, 0) // sub)
    tiles = (LANES, SEQ_BLOCK * LANES)
    same_head = (lax.broadcasted_iota(jnp.int32, tiles, 0) // RW_HEAD
                 == (lax.broadcasted_iota(jnp.int32, tiles, 1) % LANES) // RW_HEAD)
    group = max(g for g in (1, 2, 4) if n_sub % g == 0)
    units = [(j, hp) for j in range(group) for hp in range(RW_PAIRS)]
    heads = [(u, h) for u in range(len(units)) for h in range(HEAD_PAIR)]

    def sub_chunk(c, carry):
        rs = [pl.ds(pl.multiple_of((c * group + j) * sub, sub), sub) for j in range(group)]

        def ld(ref, j, hp):
            return ref[:, rs[j], pl.ds(hp * LANES, LANES)].reshape(m_rows, LANES)

        at, bt, ktl, rt, bh, kh, vv, gcv = ([ld(ref, j, hp) for j, hp in units]
                                            for ref in (at_ref, bt_ref, kt_ref, rt_ref, bh_ref, kh_ref,
                                                        v_ref, gc_ref))
        state = [[s_ref[hp, n] for n in range(SEQ_BLOCK)] for hp in range(RW_PAIRS)]
        zero = jnp.zeros((m_rows, LANES), F32)
        msk = lambda x, h: jnp.where(head_mask[h], x, zero)

        gram = [_mm(jnp.concatenate([msk(at[hp], 0), msk(rt[hp], 0), msk(at[hp], 1), msk(rt[hp], 1)], axis=0),
                    jnp.concatenate([bt[hp], ktl[hp]], axis=0), _NT)
                for hp in range(len(units))]

        def quad(hp, h, row, col, keep):
            blk = gram[hp][(2 * h + row) * m_rows:(2 * h + row + 1) * m_rows, col * m_rows:(col + 1) * m_rows]
            return jnp.where(keep, blk, 0.0)

        m_ab = [quad(hp, h, 0, 0, before) for hp, h in heads]
        m_ak = [quad(hp, h, 0, 1, before) for hp, h in heads]
        n_rb = [quad(hp, h, 1, 0, upto) for hp, h in heads]
        n_rk = [quad(hp, h, 1, 1, upto) for hp, h in heads]
        tinv = [eye + jnp.where(pair_mask, m, 0.0) for m in m_ab]
        for lm in level_masks:
            prod = [_mm(jnp.where(lm, m, 0.0), t, _NN) for m, t in zip(m_ab, tinv)]
            tinv = [t + _mm(t, q, _NN) for t, q in zip(tinv, prod)]
        vh = [msk(vv[hp], h) for hp, h in heads]
        w1 = [_mm(m, x, _NN) for m, x in zip(m_ak, vh)]
        ap = [_mm(t, jnp.concatenate([msk(at[hp], h), w], axis=1), _NN)
              for t, (hp, h), w in zip(tinv, heads, w1)]
        nb = [_mm(m, x, _NN) for m, x in zip(n_rb, ap)]
        nv = [_mm(m, x, _NN) for m, x in zip(n_rk, vh)]

        def per_seq(x):
            return jnp.where(own_rows, jnp.concatenate([x] * SEQ_BLOCK, axis=1), 0.0)

        pairs = range(len(units))
        both = lambda xs, hp, cols: xs[HEAD_PAIR * hp][:, cols] + xs[HEAD_PAIR * hp + 1][:, cols]
        lo, hi = slice(0, LANES), slice(LANES, 2 * LANES)
        bh_x = [per_seq(bh[hp]) for hp in pairs]
        kh_x = [per_seq(kh[hp]) for hp in pairs]
        gam = [jnp.where(same_head, _mm(both(ap, hp, lo).T, bh_x[hp], _NN), 0.0) for hp in pairs]
        u = [jnp.where(same_head, _mm(jnp.concatenate([both(ap, hp, hi).T, vv[hp].T], axis=1),
                                      jnp.concatenate([bh_x[hp], kh_x[hp]], axis=0), _NN), 0.0)
             for hp in pairs]
        for hp, (j, pair) in enumerate(units):
            r_hat = rt[hp] + both(nb, hp, lo)
            y_zero = both(nb, hp, hi) + nv[HEAD_PAIR * hp] + nv[HEAD_PAIR * hp + 1]
            cols = pl.ds(pair * LANES, LANES)
            for n in range(SEQ_BLOCK):
                q = slice(n * sub, (n + 1) * sub)
                blk = slice(n * LANES, (n + 1) * LANES)
                st = state[pair][n]
                ys_ref[n, rs[j], cols] = _mm(r_hat[q], st, _NT) + y_zero[q]
                state[pair][n] = (st * gcv[hp][n * sub:n * sub + 1, :]
                                  + _mm(st, gam[hp][:, blk], _NN) + u[hp][:, blk])
        for pair in range(RW_PAIRS):
            for n in range(SEQ_BLOCK):
                s_ref[pair, n] = state[pair][n]
        return carry

    lax.fori_loop(0, n_sub // group, sub_chunk, 0)

    @pl.when(pl.program_id(1) == pl.num_programs(1) - 1)
    def _():
        if layer:
            sl_ref[0:layer] = earlier_ref[...]
        _tiles_to_heads(s_ref, sl_ref.at[layer], RW_PAIRS, transpose=False)

    y = ys_ref[...].reshape(rows, RW_WIDTH)
    inv = 1.0 / RW_HEAD
    mean = _head_sum(y, ones) * inv
    yc = y - mean
    var = _head_sum(yc * yc, ones) * inv
    y = yc * lax.rsqrt(var + RW_GN_EPS) * ln_w + ln_b
    bonus = _head_sum(r * kt * r_k, ones, split=False) * v
    _seq_block_store(y_ref, stage_ref, ((y + bonus) * g).reshape(shape3))


def _layer_state_specs(state, layer):
    tail = state.shape[1:]
    zeros = (0,) * len(tail)
    read = pl.BlockSpec((SEQ_BLOCK,) + tail, lambda b, c: (b,) + zeros)
    upto = lambda k: pl.BlockSpec((k, SEQ_BLOCK) + tail, lambda b, c: (0, b) + zeros)
    return read, upto(layer + 1), ([upto(layer)] if layer else [])


def _rw_call(p2, n, l, shift0, state, layer, earlier, prm, t_chunk):
    assert t_chunk % min(MAX_SUB, t_chunk) == 0
    grid, in_specs, out_specs, in_shapes, out_shapes = _seq_specs(n, l, t_chunk, [RW_COLS], [RW_WIDTH])
    valid_in_chunk = min(t_chunk, l)
    kern = functools.partial(_rw_kernel, t_chunk=t_chunk, l_valid=valid_in_chunk, last_row=valid_in_chunk - 1,
                             layer=layer)
    st_in, st_out, st_earlier = _layer_state_specs(state, layer)
    sh_spec = pl.BlockSpec((SEQ_BLOCK, 1, RW_COLS), lambda b, c: (b, 0, 0))
    q_scr = pltpu.VMEM((SEQ_BLOCK, t_chunk, RW_WIDTH), F32)
    return pl.pallas_call(
        kern,
        out_shape=[jax.ShapeDtypeStruct(out_shapes[0], _branch_dtype(out_shapes[0])),
                   jax.ShapeDtypeStruct((n, 1, RW_COLS), F32),
                   jax.ShapeDtypeStruct((layer + 1,) + state.shape, F32)],
        grid=grid,
        in_specs=in_specs + [sh_spec, st_in,
                             _const_spec((1, RW_COLS)), _const_spec((SUBLANES, RW_WIDTH)),
                             _const_spec((RW_LORA, RW_WIDTH)), _const_spec((RW_LORA, RW_WIDTH)),
                             _const_spec((RW_LORA, RW_WIDTH))] + st_earlier,
        out_specs=out_specs + [sh_spec, st_out],
        scratch_shapes=[pltpu.VMEM((SEQ_BLOCK, 1, RW_COLS), F32),
                        pltpu.VMEM((RW_PAIRS, SEQ_BLOCK, LANES, LANES), F32)] + [q_scr] * 9
        + _io_scratch(l, t_chunk, RW_COLS, RW_WIDTH),
        compiler_params=_cparams("arbitrary", "arbitrary"),
        name="rwkv7_mixer",
    )(p2.reshape(in_shapes[0]), shift0, state, prm["rw_mu"], prm["rw_vec"], prm["rw_w2"], prm["rw_a2"],
      prm["rw_g2"], *([earlier] if layer else []))


def _hg_kernel(z_ref, s0_ref, lower_ref, ng_ref, *rest, t_chunk, l_valid, layer):
    earlier_ref = rest[0] if layer else None
    (y_ref, sl_ref, s_ref, qs_ref, q_ref, k_ref, kh_ref, v_ref, b_ref, gc_ref, ys_ref, pad_ref,
     stage_ref) = rest[1 if layer else 0:]
    rows = SEQ_BLOCK * t_chunk
    sub = min(MAX_SUB, t_chunk)
    n_sub = t_chunk // sub

    @pl.when(pl.program_id(1) == 0)
    def _():
        _heads_to_tiles(s0_ref, s_ref, HG_PAIRS, transpose=True)

    z = _seq_block_load(z_ref, pad_ref).reshape(rows, 4 * HG_WIDTH)
    q = jax.nn.silu(z[:, 0:HG_WIDTH])
    f = z[:, HG_WIDTH:2 * HG_WIDTH]
    i = z[:, 2 * HG_WIDTH:3 * HG_WIDTH]
    og = z[:, 3 * HG_WIDTH:4 * HG_WIDTH]
    lower = lower_ref[...]
    fgate = lower + (1.0 - lower) * jax.nn.sigmoid(f)
    log_f = jnp.log(jnp.maximum(fgate, HG_GATE_FLOOR))
    k = 1.0 - fgate
    if l_valid < t_chunk:
        live = lax.broadcasted_iota(jnp.int32, (rows, 1), 0) % t_chunk < l_valid
        log_f = jnp.where(live, log_f, 0.0)
        k = jnp.where(live, k, 0.0)
    cum = _cumsum_groups(log_f, sub)
    shape_g = (rows // sub, sub, HG_WIDTH)
    cum_end = jnp.broadcast_to(cum.reshape(shape_g)[:, sub - 1:sub, :], shape_g).reshape(rows, HG_WIDTH)
    shape3 = (SEQ_BLOCK, t_chunk, HG_WIDTH)
    qs_ref[...] = (q * jnp.exp(cum)).reshape(shape3)
    q_ref[...] = q.reshape(shape3)
    k_ref[...] = k.reshape(shape3)
    kh_ref[...] = (k * jnp.exp(cum_end - cum)).reshape(shape3)
    v_ref[...] = i.reshape(shape3)
    b_ref[...] = cum.reshape(shape3)
    gc_ref[...] = jnp.exp(cum_end).reshape(shape3)

    m_rows = SEQ_BLOCK * sub
    ones = _head_ones()
    wide = (m_rows, SEQ_BLOCK * LANES)
    own_rows = (lax.broadcasted_iota(jnp.int32, wide, 1) // LANES
                == lax.broadcasted_iota(jnp.int32, wide, 0) // sub)
    tiles = (LANES, SEQ_BLOCK * LANES)
    same_head = (lax.broadcasted_iota(jnp.int32, tiles, 0) // HG_HEAD
                 == (lax.broadcasted_iota(jnp.int32, tiles, 1) % LANES) // HG_HEAD)
    step = lax.broadcasted_iota(jnp.int32, (sub, LANES), 0)

    def sub_chunk(c, carry):
        rs = pl.ds(pl.multiple_of(c * sub, sub), sub)
        for hp in range(HG_PAIRS):
            cols = pl.ds(hp * LANES, LANES)
            ld = lambda ref: ref[:, rs, cols].reshape(m_rows, LANES)
            qs, qq, kk, kh, vv, bb, gcv = (ld(qs_ref), ld(q_ref), ld(k_ref), ld(kh_ref), ld(v_ref),
                                           ld(b_ref), ld(gc_ref))
            state = [s_ref[hp, n] for n in range(SEQ_BLOCK)]
            kh_x = jnp.where(own_rows, jnp.concatenate([kh] * SEQ_BLOCK, axis=1), 0.0)
            u = jnp.where(same_head, _mm(vv.T, kh_x, _NN), 0.0)
            for n in range(SEQ_BLOCK):
                sl = slice(n * sub, (n + 1) * sub)
                bn, qn, kn, vn = bb[sl], qq[sl], kk[sl], vv[sl]
                prods = []
                for t in range(sub):
                    keep = step <= t
                    prods.append(jnp.where(keep, qn[t:t + 1, :] * kn * jnp.exp(bn[t:t + 1, :] - bn), 0.0))
                att = _dot(jnp.concatenate(prods, axis=0).astype(BF16), ones)
                o_rows = [jnp.sum(att[t * sub:(t + 1) * sub] * vn, axis=0, keepdims=True)
                          for t in range(sub)]
                ys_ref[n, rs, cols] = _mm(qs[sl], state[n], _NT) + jnp.concatenate(o_rows, axis=0)
                blk = slice(n * LANES, (n + 1) * LANES)
                s_ref[hp, n] = state[n] * gcv[n * sub:n * sub + 1, :] + u[:, blk]
        return carry

    lax.fori_loop(0, n_sub, sub_chunk, 0)

    @pl.when(pl.program_id(1) == pl.num_programs(1) - 1)
    def _():
        if layer:
            sl_ref[0:layer] = earlier_ref[...]
        _tiles_to_heads(s_ref, sl_ref.at[layer], HG_PAIRS, transpose=True)

    o = ys_ref[...].reshape(rows, HG_WIDTH)
    ms = _head_sum(o * o, ones) * (1.0 / HG_HEAD)
    o = o * lax.rsqrt(ms + RMS_EPS) * ng_ref[...] * jax.nn.sigmoid(og)
    _seq_block_store(y_ref, stage_ref, o.reshape(shape3))


def _hg_call(z2, n, l, state, layer, earlier, lower, norm_g, t_chunk):
    assert t_chunk % min(MAX_SUB, t_chunk) == 0
    grid, in_specs, out_specs, in_shapes, out_shapes = _seq_specs(n, l, t_chunk, [4 * HG_WIDTH], [HG_WIDTH])
    kern = functools.partial(_hg_kernel, t_chunk=t_chunk, l_valid=min(t_chunk, l), layer=layer)
    st_in, st_out, st_earlier = _layer_state_specs(state, layer)
    q_scr = pltpu.VMEM((SEQ_BLOCK, t_chunk, HG_WIDTH), F32)
    return pl.pallas_call(
        kern,
        out_shape=[jax.ShapeDtypeStruct(out_shapes[0], _branch_dtype(out_shapes[0])),
                   jax.ShapeDtypeStruct((layer + 1,) + state.shape, F32)],
        grid=grid,
        in_specs=in_specs + [st_in, _const_spec((1, HG_WIDTH)), _const_spec((1, HG_WIDTH))] + st_earlier,
        out_specs=out_specs + [st_out],
        scratch_shapes=[pltpu.VMEM((HG_PAIRS, SEQ_BLOCK, LANES, LANES), F32)] + [q_scr] * 8
        + _io_scratch(l, t_chunk, 4 * HG_WIDTH, HG_WIDTH),
        compiler_params=_cparams("arbitrary", "arbitrary"),
        name="hgrn2_mixer",
    )(z2.reshape(in_shapes[0]), state, lower.reshape(1, HG_WIDTH), norm_g.reshape(1, HG_WIDTH),
      *([earlier] if layer else []))


def _merge_kernel(x_ref, ya_ref, yb_ref, yc_ref, gates_ref, gt_ref, la_ref, lb_ref, lc_ref, wo_ref, o_ref):
    d = D_MODEL
    gates = _tile(gates_ref)
    lift = lambda y_ref, w_ref: _dot(_tile(y_ref)[...].astype(BF16), w_ref[...])
    m = (gates[:, 0:d].astype(F32) * lift(ya_ref, la_ref)
         + gates[:, d:2 * d].astype(F32) * lift(yb_ref, lb_ref)
         + gates[:, 2 * d:3 * d].astype(F32) * lift(yc_ref, lc_ref))
    _tile(o_ref)[...] = _tile(x_ref)[...] + gt_ref[0] * _dot(m.astype(BF16), wo_ref[...])


def _merge_call(x, n, l, ya, yb, yc, gates, gate1, prm, tm):
    d = x.shape[-1]
    gt_arr, gt_spec = _mod_operand(gate1, l, tm)
    shape, x_spec = _tok_layout(n, l, tm, d)
    row_spec = lambda w: _tok_layout(n, l, tm, w)[1]
    return pl.pallas_call(
        _merge_kernel,
        out_shape=jax.ShapeDtypeStruct(shape, F32),
        grid=(n * l // tm,),
        in_specs=[x_spec, row_spec(S5_WIDTH), row_spec(RW_WIDTH), row_spec(HG_WIDTH),
                  row_spec(N_BRANCH * d), gt_spec,
                  _const_spec((S5_WIDTH, d)), _const_spec((RW_WIDTH, d)), _const_spec((HG_WIDTH, d)),
                  _const_spec((d, d))],
        out_specs=x_spec,
        compiler_params=_cparams("arbitrary"),
        name="branch_merge",
    )(x, ya, yb, yc, gates, gt_arr, prm["w_lift_a"], prm["w_lift_b"], prm["w_lift_c"], prm["w_out"])


def _ffn_kernel(*refs, l, tm, has_state, final_norm):
    if has_state:
        (x_ref, g_ref, sh_ref, sc_ref, gt_ref, wup_ref, cw_ref, cb_ref, wdn_ref, fg_ref, st_ref,
         o_ref, tail_ref, h_ref, act_ref, hist_ref, buf_ref, old_ref) = refs
    else:
        (x_ref, g_ref, sh_ref, sc_ref, gt_ref, wup_ref, cw_ref, cb_ref, wdn_ref, fg_ref,
         o_ref, tail_ref, h_ref, act_ref, hist_ref) = refs
    i = pl.program_id(0)
    n_chunks = D_FF // FF_CHUNK
    n_hist = CONV_W - 1

    if has_state:
        t_idx = lax.broadcasted_iota(jnp.int32, (tm, 1), 0) % l
        after = [t_idx >= j + 1 for j in range(n_hist)]
        old_ref[...] = jnp.zeros_like(old_ref)
        buf_ref[:, 0:SUBLANES, :] = jnp.zeros((2, SUBLANES, FF_CHUNK), F32)
        for s in range(tm // l):
            for j in range(n_hist):
                old_ref[j, s * l:s * l + j + 1, :] = st_ref[s, n_hist - 1 - j:n_hist, :]
    else:
        @pl.when(i % (l // tm) == 0)
        def _():
            hist_ref[...] = jnp.zeros_like(hist_ref)

    x = _tile(x_ref)[...]
    h = _rms_modulate(x, g_ref[...], sh_ref[0], sc_ref[0])
    groups = tm // SUBLANES
    if not has_state:
        h = jnp.swapaxes(h.reshape(SUBLANES, groups, D_MODEL), 0, 1).reshape(tm, D_MODEL)
        row8 = lax.broadcasted_iota(jnp.int32, (SUBLANES, FF_CHUNK), 0)
    h_ref[...] = h.astype(BF16)

    def conv_half(c, half):
        col0 = half * D_FF + c * FF_CHUNK
        cols = slice(col0, col0 + FF_CHUNK)
        up = _dot(h_ref[...], wup_ref[:, cols])
        if has_state:
            buf_ref[half, SUBLANES:SUBLANES + tm, :] = up
            prev = [jnp.where(after[j], buf_ref[half, SUBLANES - 1 - j:SUBLANES - 1 - j + tm, :], 0.0)
                    + old_ref[j, :, cols] for j in range(n_hist)]
            for s in range(tm // l):
                r0 = SUBLANES + (s + 1) * l - n_hist
                tail_ref[s, :, cols] = buf_ref[half, r0:r0 + n_hist, :]
        else:
            late = [up[tm - (j + 1) * SUBLANES:tm - j * SUBLANES, :] for j in range(n_hist)]
            first = [jnp.where(row8 == 0, pltpu.roll(hist_ref[c, half, j], 1, 0), pltpu.roll(late[j], 1, 0))
                     for j in range(n_hist)]
            for j in range(n_hist):
                hist_ref[c, half, j] = late[j]
            prev = [jnp.concatenate([first[0], up[0:tm - SUBLANES, :]], axis=0),
                    jnp.concatenate([first[1], first[0], up[0:tm - 2 * SUBLANES, :]], axis=0)]
            tail_ref[0, 0:1, cols] = up[tm - SUBLANES - 1:tm - SUBLANES, :]
            tail_ref[0, 1:2, cols] = up[tm - 1:tm, :]
        return (cb_ref[:, cols] + cw_ref[0:1, cols] * prev[1] + cw_ref[1:2, cols] * prev[0]
                + cw_ref[2:3, cols] * up)

    for c in range(n_chunks):
        act = _gelu(conv_half(c, 0)) * conv_half(c, 1)
        act_ref[:, c * FF_CHUNK:(c + 1) * FF_CHUNK] = act.astype(BF16)
    acc = _dot(act_ref[...], wdn_ref[...])
    if not has_state:
        acc = jnp.swapaxes(acc.reshape(groups, SUBLANES, D_MODEL), 0, 1).reshape(tm, D_MODEL)
    out = x + gt_ref[0] * acc
    if final_norm:
        ms = jnp.mean(out * out, axis=-1, keepdims=True)
        out = out * lax.rsqrt(ms + RMS_EPS) * fg_ref[...]
    _tile(o_ref)[...] = out


def _ffn_call(x, n, l, g, shift, scale, gate2, prm, final_g, conv_state, tm, final_norm):
    assert CONV_W == 3
    d = x.shape[-1]
    has_state = conv_state is not None
    sh_arr, sh_spec = _mod_operand(shift, l, tm)
    sc_arr, sc_spec = _mod_operand(scale, l, tm)
    gt_arr, gt_spec = _mod_operand(gate2, l, tm)
    shape, x_spec = _tok_layout(n, l, tm, d)
    in_specs = [x_spec, _const_spec((1, d)), sh_spec, sc_spec, gt_spec,
                _const_spec((d, 2 * D_FF)), _const_spec((CONV_W, 2 * D_FF)), _const_spec((1, 2 * D_FF)),
                _const_spec((D_FF, d)), _const_spec((1, d))]
    operands = [x, g.reshape(1, d), sh_arr, sc_arr, gt_arr, prm["w_up"], prm["conv_w"], prm["conv_b"],
                prm["w_down"], final_g.reshape(1, d)]
    scratch = [pltpu.VMEM((tm, d), BF16), pltpu.VMEM((tm, D_FF), BF16),
               pltpu.VMEM((D_FF // FF_CHUNK, 2, CONV_W - 1, SUBLANES, FF_CHUNK), F32)]
    if has_state:
        assert tm % l == 0 and l >= CONV_W - 1
        state_spec = pl.BlockSpec((tm // l, CONV_W - 1, 2 * D_FF), lambda i: (i, 0, 0))
        in_specs.append(state_spec)
        operands.append(conv_state)
        tail_spec = state_spec
        scratch += [pltpu.VMEM((2, tm + SUBLANES, FF_CHUNK), F32), pltpu.VMEM((CONV_W - 1, tm, 2 * D_FF), F32)]
    else:
        assert l % tm == 0
        per = l // tm
        tail_spec = pl.BlockSpec((1, CONV_W - 1, 2 * D_FF), lambda i: (i // per, 0, 0))
    kern = functools.partial(_ffn_kernel, l=l, tm=tm, has_state=has_state, final_norm=final_norm)
    return pl.pallas_call(
        kern,
        out_shape=[jax.ShapeDtypeStruct(shape, F32), jax.ShapeDtypeStruct((n, CONV_W - 1, 2 * D_FF), F32)],
        grid=(n * l // tm,),
        in_specs=in_specs,
        out_specs=[x_spec, tail_spec],
        scratch_shapes=scratch,
        compiler_params=_cparams("arbitrary"),
        name="conv_ffn",
    )(*operands)


def _prepare_layer(p):
    out = {}
    w_in = p["w_in"]
    c1 = S5_WIDTH
    c2 = c1 + RW_COLS
    c3 = c2 + 4 * HG_WIDTH
    out["w_in_a"] = w_in[:, :c1].astype(BF16)
    out["w_in_b"] = w_in[:, c1:c2].astype(BF16)
    out["w_in_c"] = w_in[:, c2:c3].astype(BF16)
    out["w_in_g"] = w_in[:, c3:].astype(BF16)
    out["w_ada"] = p["w_ada"].astype(BF16)
    for name in ("w_lift_a", "w_lift_b", "w_lift_c", "w_out", "w_up", "w_down"):
        out[name] = p[name].astype(BF16)
    out["conv_w"] = p["conv_w"]
    out["conv_b"] = p["conv_b"].reshape(1, 2 * D_FF)

    lr = p["s5_lambda_re"]
    li = p["s5_lambda_im"]
    dt = jnp.exp(p["s5_log_dt"])[:, None]
    mag = jnp.exp(lr * dt)
    ar = mag * jnp.cos(li * dt)
    ai = mag * jnp.sin(li * dt)
    den = lr * lr + li * li
    zr = ((ar - 1.0) * lr + ai * li) / den
    zi = (ai * lr - (ar - 1.0) * li) / den
    bbr = zr[..., None] * p["s5_b_re"] - zi[..., None] * p["s5_b_im"]
    bbi = zr[..., None] * p["s5_b_im"] + zi[..., None] * p["s5_b_re"]
    eye = jnp.eye(S5_GROUPS, dtype=F32)
    bmat = jnp.einsum("gh,rgpc->gcrhp", eye, jnp.stack([bbr, bbi])).reshape(S5_WIDTH, 2 * S5_FLAT)
    cmat = jnp.einsum("hg,rgcp->rhpgc", eye, jnp.stack([p["s5_c_re"], -p["s5_c_im"]])).reshape(
        2 * S5_FLAT, S5_WIDTH)
    out["s5_b"] = bmat.astype(BF16)
    out["s5_c"] = cmat.astype(BF16)
    out["s5_a"] = jnp.stack([ar.reshape(S5_FLAT), ai.reshape(S5_FLAT)])
    out["s5_d"] = p["s5_d"].reshape(1, S5_WIDTH)
    out["s5_w_glu"] = p["s5_w_glu"].astype(BF16)
    out["s5_b_glu"] = p["s5_b_glu"].reshape(1, S5_WIDTH)

    out["rw_mu"] = p["rw_mu"].reshape(1, RW_COLS)
    out["rw_vec"] = jnp.stack([p["rw_w0"], p["rw_a0"], p["rw_k_k"], p["rw_k_a"],
                               p["rw_r_k"].reshape(RW_WIDTH), p["rw_ln_w"], p["rw_ln_b"],
                               jnp.zeros((RW_WIDTH,), F32)])
    zw = jnp.zeros((RW_LORA, RW_WIDTH), F32)
    out["rw_w2"] = zw.at[0:RW_DECAY_LORA].set(p["rw_w2"]).astype(BF16)
    out["rw_a2"] = zw.at[RW_DECAY_LORA:RW_DECAY_LORA + RW_AAA_LORA].set(p["rw_a2"]).astype(BF16)
    out["rw_g2"] = zw.at[RW_DECAY_LORA + RW_AAA_LORA:].set(p["rw_g2"]).astype(BF16)
    out["hg_norm"] = p["hg_norm"]
    out["g_mix"] = p["g_mix"]
    out["g_ffn"] = p["g_ffn"]
    out["b_ada"] = p["b_ada"]
    return out


def _chunk_for(l, sub, t_chunk):
    return t_chunk if l % t_chunk == 0 else -(-l // sub) * sub


def _run_trunk(x, mods, st_s5, st_shift, st_rw, st_hg, st_conv, lower, final_g, prms, t_chunk):
    n, l, d = x.shape
    rows = n * l
    tm = min(ROW_TILE, rows)
    depth = len(prms)
    assert (l % tm == 0) == (l % t_chunk == 0)
    h = x.reshape(_tok_layout(n, l, tm, d)[0])
    out_s5, out_shift, out_conv = [], [], []
    rw_new = hg_new = None
    for layer in range(depth):
        prm = prms[layer]
        sh1, sc1, gt1, sh2, sc2, gt2 = jnp.split(mods[layer], 6, axis=-1)
        za, zb, zc, gates = _in_proj_call(h, n, l, prm["g_mix"], sh1, sc1, prm["w_in_a"], prm["w_in_b"],
                                          prm["w_in_c"], prm["w_in_g"], tm)

        s5_in = jnp.concatenate([st_s5[layer][..., 0].reshape(n, S5_FLAT),
                                 st_s5[layer][..., 1].reshape(n, S5_FLAT)], axis=1)
        ya, s5_new = _s5_call(za, n, l, s5_in, prm, _chunk_for(l, SUBLANES, t_chunk))
        yb, shift_new, rw_new = _rw_call(zb, n, l, st_shift[layer].reshape(n, 1, RW_COLS), st_rw[layer], layer, rw_new,
                                         prm, _chunk_for(l, SUBLANES, t_chunk))
        yc, hg_new = _hg_call(zc, n, l, st_hg[layer], layer, hg_new, lower[layer], prm["hg_norm"],
                              _chunk_for(l, SUBLANES, t_chunk))

        h = _merge_call(h, n, l, ya, yb, yc, gates, gt1, prm, tm)
        ffn_tm = tm if st_conv is None else min(tm, 128)
        h, conv_new = _ffn_call(h, n, l, prm["g_ffn"], sh2, sc2, gt2, prm, final_g,
                                None if st_conv is None else st_conv[layer], ffn_tm,
                                final_norm=(layer == depth - 1))
        out_s5.append(jnp.stack([s5_new[:, :S5_FLAT].reshape(n, S5_GROUPS, S5_STATE),
                                 s5_new[:, S5_FLAT:].reshape(n, S5_GROUPS, S5_STATE)], axis=-1))
        out_shift.append(shift_new.reshape(n, RW_COLS))
        out_conv.append(conv_new)
    y = h.reshape(n, l, d).astype(x.dtype)
    return y, (jnp.stack(out_s5), jnp.stack(out_shift), rw_new, hg_new, jnp.stack(out_conv))


def kernel(x_prompt, x_sample, c_prompt, c_sample, state_s5, state_rwkv_shift, state_rwkv, state_hgrn, state_ffn_conv, w_ada, b_ada, g_mix, g_ffn, w_in, s5_lambda_re, s5_lambda_im, s5_log_dt, s5_b_re, s5_b_im, s5_c_re, s5_c_im, s5_d, s5_w_glu, s5_b_glu, rw_mu, rw_w0, rw_w2, rw_a0, rw_a2, rw_g2, rw_k_k, rw_k_a, rw_r_k, rw_ln_w, rw_ln_b, hg_lb, hg_norm, w_lift_a, w_lift_b, w_lift_c, w_out, w_up, conv_w, conv_b, w_down, final_g):
    per_layer = {
        "w_ada": w_ada, "b_ada": b_ada, "g_mix": g_mix, "g_ffn": g_ffn, "w_in": w_in,
        "s5_lambda_re": s5_lambda_re, "s5_lambda_im": s5_lambda_im, "s5_log_dt": s5_log_dt,
        "s5_b_re": s5_b_re, "s5_b_im": s5_b_im, "s5_c_re": s5_c_re, "s5_c_im": s5_c_im,
        "s5_d": s5_d, "s5_w_glu": s5_w_glu, "s5_b_glu": s5_b_glu,
        "rw_mu": rw_mu, "rw_w0": rw_w0, "rw_w2": rw_w2, "rw_a0": rw_a0, "rw_a2": rw_a2,
        "rw_g2": rw_g2, "rw_k_k": rw_k_k, "rw_k_a": rw_k_a, "rw_r_k": rw_r_k,
        "rw_ln_w": rw_ln_w, "rw_ln_b": rw_ln_b, "hg_norm": hg_norm,
        "w_lift_a": w_lift_a, "w_lift_b": w_lift_b, "w_lift_c": w_lift_c, "w_out": w_out,
        "w_up": w_up, "conv_w": conv_w, "conv_b": conv_b, "w_down": w_down,
    }
    depth = w_ada.shape[0]
    prms = [_prepare_layer({k: v[layer] for k, v in per_layer.items()}) for layer in range(depth)]

    lbp = jax.nn.softmax(hg_lb.astype(F32), axis=0)
    lower = jnp.cumsum(lbp, axis=0) - lbp[0]

    nb = x_prompt.shape[0]
    ns = x_sample.shape[0]
    c_all = jnp.concatenate([c_prompt, c_sample], axis=0).astype(F32)
    mods = [_ada_call(c_all, prms[layer]["w_ada"], prms[layer]["b_ada"]) for layer in range(depth)]
    mods_p = [m[:nb] for m in mods]
    mods_s = [m[nb:] for m in mods]

    z_s5 = jnp.zeros((depth, nb) + state_s5.shape[2:], F32)
    z_shift = jnp.zeros((depth, nb) + state_rwkv_shift.shape[2:], F32)
    z_rw = jnp.zeros((depth, nb) + state_rwkv.shape[2:], F32)
    z_hg = jnp.zeros((depth, nb) + state_hgrn.shape[2:], F32)

    y_prompt, (s5_p, shift_p, rw_p, hg_p, conv_p) = _run_trunk(
        x_prompt, mods_p, z_s5, z_shift, z_rw, z_hg, None, lower, final_g, prms, t_chunk=64)
    y_sample, (s5_s, shift_s, rw_s, hg_s, conv_s) = _run_trunk(
        x_sample, mods_s, state_s5, state_rwkv_shift, state_rwkv, state_hgrn, state_ffn_conv,
        lower, final_g, prms, t_chunk=64)
    return (y_prompt, y_sample, s5_p, shift_p, rw_p, hg_p, conv_p, s5_s, shift_s, rw_s, hg_s, conv_s)
```

```python
import functools
import math

import jax
import jax.numpy as jnp
from jax import lax
from jax.experimental import pallas as pl
from jax.experimental.pallas import tpu as pltpu

F32 = jnp.float32
BF16 = jnp.bfloat16

D_MODEL = 1024
S5_WIDTH = D_MODEL // 4
S5_GROUP = 16
S5_GROUPS = S5_WIDTH // S5_GROUP
S5_STATE = 64
S5_FLAT = S5_GROUPS * S5_STATE
RW_WIDTH = D_MODEL // 2
RW_HEAD = 64
RW_HEADS = RW_WIDTH // RW_HEAD
RW_DECAY_LORA = 32
RW_AAA_LORA = 32
RW_GATE_LORA = 64
RW_LORA = RW_DECAY_LORA + RW_AAA_LORA + RW_GATE_LORA
RW_COLS = 3 * RW_WIDTH + RW_LORA
RW_GN_EPS = 1e-5 * RW_HEAD
HG_WIDTH = D_MODEL // 4
HG_HEAD = 64
HG_HEADS = HG_WIDTH // HG_HEAD
HG_GATE_FLOOR = 1e-30
N_BRANCH = 3
D_FF = 256 * ((8 * D_MODEL // 3 + 255) // 256)
CONV_W = 3
RMS_EPS = 1e-6

LANES = 128
SUBLANES = 8
MXU_COLS = 256
VMEM_LIMIT_BYTES = 56 * 1024 * 1024

SEQ_BLOCK = SUBLANES
HEAD_PAIR = LANES // RW_HEAD
ROW_TILE = 512
FF_CHUNK = 256
IN_PROJ_CHUNK = 512
MAX_SUB = 16
RW_PAIRS = RW_HEADS // HEAD_PAIR
HG_PAIRS = HG_HEADS // HEAD_PAIR

_NN = (((1,), (0,)), ((), ()))
_NT = (((1,), (1,)), ((), ()))


def _cparams(*sem):
    return pltpu.CompilerParams(dimension_semantics=sem, vmem_limit_bytes=VMEM_LIMIT_BYTES)


def _const_spec(shape):
    nd = len(shape)
    return pl.BlockSpec(shape, lambda *_: (0,) * nd, pipeline_mode=pl.Buffered(1))


def _dot(a, b):
    return jnp.dot(a, b, preferred_element_type=F32)


def _split_bf16(x):
    hi = x.astype(BF16)
    lo = (x - hi.astype(F32)).astype(BF16)
    return hi, lo


def _head_ones():
    r = lax.broadcasted_iota(jnp.int32, (LANES, LANES), 0) // RW_HEAD
    c = lax.broadcasted_iota(jnp.int32, (LANES, LANES), 1) // RW_HEAD
    return (r == c).astype(BF16)


def _head_sum(x, ones, split=True):
    hi, lo = _split_bf16(x) if split else (x.astype(BF16), None)
    tiles = []
    for c in range(0, x.shape[1], LANES):
        t = _dot(hi[:, c:c + LANES], ones)
        tiles.append(t + _dot(lo[:, c:c + LANES], ones) if split else t)
    return tiles[0] if len(tiles) == 1 else jnp.concatenate(tiles, axis=1)


def _softplus(x):
    return jnp.maximum(x, 0.0) + jnp.log(1.0 + jnp.exp(-jnp.abs(x)))


def _gelu(x):
    c = 2.0 * math.sqrt(2.0 / math.pi)
    return x * jax.nn.sigmoid(x * (c + (c * 0.044715) * (x * x)))


def _rms_modulate(x, g, shift, scale):
    ms = jnp.mean(x * x, axis=-1, keepdims=True)
    return x * lax.rsqrt(ms + RMS_EPS) * g * (1.0 + scale) + shift


def _seq_specs(n, l, t_chunk, widths_in, widths_out):
    nb = n // SEQ_BLOCK
    if l % t_chunk == 0:
        spec = lambda w: pl.BlockSpec((SEQ_BLOCK, t_chunk, w), lambda b, c: (b, c, 0))
        shape = lambda w: (n, l, w)
        nt = l // t_chunk
    else:
        assert l < t_chunk
        spec = lambda w: pl.BlockSpec((SEQ_BLOCK * l, w), lambda b, c: (b, 0))
        shape = lambda w: (n * l, w)
        nt = 1
    return ((nb, nt), [spec(w) for w in widths_in], [spec(w) for w in widths_out],
            [shape(w) for w in widths_in], [shape(w) for w in widths_out])


def _branch_dtype(shape):
    return BF16 if len(shape) == 3 else F32


def _seq_block_load(z_ref, pad_ref):
    if len(z_ref.shape) == 3:
        return z_ref[...]
    l = z_ref.shape[0] // SEQ_BLOCK
    pad_ref[...] = jnp.zeros_like(pad_ref)
    for n in range(SEQ_BLOCK):
        pad_ref[n, 0:l, :] = z_ref[n * l:(n + 1) * l, :]
    return pad_ref[...]


def _seq_block_store(y_ref, stage_ref, y3):
    if len(y_ref.shape) == 3:
        y_ref[...] = y3.astype(y_ref.dtype)
        return
    l = y_ref.shape[0] // SEQ_BLOCK
    stage_ref[...] = y3
    for n in range(SEQ_BLOCK):
        y_ref[n * l:(n + 1) * l, :] = stage_ref[n, 0:l, :]


def _heads_to_tiles(s_ref, tile_ref, n_pairs, transpose):
    zero = jnp.zeros((RW_HEAD, RW_HEAD), F32)
    for n in range(SEQ_BLOCK):
        for hp in range(n_pairs):
            a, b = s_ref[n, HEAD_PAIR * hp], s_ref[n, HEAD_PAIR * hp + 1]
            if transpose:
                a, b = a.T, b.T
            tile_ref[hp, n] = jnp.concatenate([jnp.concatenate([a, zero], axis=1),
                                               jnp.concatenate([zero, b], axis=1)], axis=0)


def _tiles_to_heads(tile_ref, s_ref, n_pairs, transpose):
    for n in range(SEQ_BLOCK):
        for hp in range(n_pairs):
            t = tile_ref[hp, n]
            a, b = t[0:RW_HEAD, 0:RW_HEAD], t[RW_HEAD:LANES, RW_HEAD:LANES]
            if transpose:
                a, b = a.T, b.T
            s_ref[n, HEAD_PAIR * hp] = a
            s_ref[n, HEAD_PAIR * hp + 1] = b


def _ada_kernel(c_ref, w_ref, b_ref, o_ref):
    c = c_ref[...]
    o_ref[...] = _dot(jax.nn.silu(c).astype(BF16), w_ref[...]) + b_ref[...]


def _ada_call(c, w_bf16, b):
    n, d = c.shape
    cols = w_bf16.shape[1]
    tn = cols // 4
    return pl.pallas_call(
        _ada_kernel,
        out_shape=jax.ShapeDtypeStruct((n, cols), F32),
        grid=(cols // tn,),
        in_specs=[_const_spec((n, d)),
                  pl.BlockSpec((d, tn), lambda j: (0, j)),
                  pl.BlockSpec((1, tn), lambda j: (0, j))],
        out_specs=pl.BlockSpec((n, tn), lambda j: (0, j)),
        compiler_params=_cparams("arbitrary"),
        name="ada_mod",
    )(c, w_bf16, b.reshape(1, cols))


def _mod_operand(m, l, tm):
    n, d = m.shape
    if l % tm == 0:
        per = l // tm
        return m.reshape(n, 1, d), pl.BlockSpec((1, 1, d), lambda i: (i // per, 0, 0))
    assert tm % l == 0
    return jnp.repeat(m, l, axis=0).reshape(1, n * l, d), pl.BlockSpec((1, tm, d), lambda i: (0, i, 0))


def _tok_layout(n, l, tm, w):
    if l % tm == 0:
        per = l // tm
        return (n, l, w), pl.BlockSpec((1, tm, w), lambda i: (i // per, i % per, 0))
    return (n * l, w), pl.BlockSpec((tm, w), lambda i: (i, 0))


def _tile(ref):
    return ref.at[0] if len(ref.shape) == 3 else ref


def _in_proj_kernel(x_ref, g_ref, sh_ref, sc_ref, wa_ref, wb_ref, wc_ref, wg_ref,
                    oa_ref, ob_ref, oc_ref, og_ref):
    h = _rms_modulate(_tile(x_ref)[...], g_ref[...], sh_ref[0], sc_ref[0]).astype(BF16)

    def project(w_ref, o_ref, fn=None):
        width = w_ref.shape[1]
        out = _tile(o_ref)
        for c0 in range(0, width, IN_PROJ_CHUNK):
            cols = slice(c0, min(c0 + IN_PROJ_CHUNK, width))
            z = _dot(h, w_ref[:, cols])
            out[:, cols] = (z if fn is None else fn(z)).astype(out.dtype)

    project(wa_ref, oa_ref)
    project(wb_ref, ob_ref)
    project(wc_ref, oc_ref)
    project(wg_ref, og_ref, jax.nn.sigmoid)


def _in_proj_call(x, n, l, g, shift, scale, w_a, w_b, w_c, w_g, tm):
    d = x.shape[-1]
    sh_arr, sh_spec = _mod_operand(shift, l, tm)
    sc_arr, sc_spec = _mod_operand(scale, l, tm)
    widths = [w.shape[1] for w in (w_a, w_b, w_c, w_g)]
    dtypes = [F32, F32, F32, BF16]
    outs = [_tok_layout(n, l, tm, w) for w in widths]
    return pl.pallas_call(
        _in_proj_kernel,
        out_shape=[jax.ShapeDtypeStruct(shape, dt) for (shape, _), dt in zip(outs, dtypes)],
        grid=(n * l // tm,),
        in_specs=[_tok_layout(n, l, tm, d)[1], _const_spec((1, d)), sh_spec, sc_spec]
        + [_const_spec(w.shape) for w in (w_a, w_b, w_c, w_g)],
        out_specs=[spec for _, spec in outs],
        compiler_params=_cparams("arbitrary"),
        name="in_proj",
    )(x, g.reshape(1, d), sh_arr, sc_arr, w_a, w_b, w_c, w_g)


def _s5_kernel(u_ref, s0_ref, b_ref, c_ref, a_ref, d_ref, wg_ref, bg_ref,
               y_ref, sl_ref, e_ref, st_ref, pad_ref, stage_ref, *, t_chunk, n_steps):
    @pl.when(pl.program_id(1) == 0)
    def _():
        st_ref[...] = s0_ref[...]

    rows = SEQ_BLOCK * t_chunk
    ut = jnp.swapaxes(_seq_block_load(u_ref, pad_ref), 0, 1).reshape(rows, S5_WIDTH)
    ub = ut.astype(BF16)
    for c0 in range(0, 2 * S5_FLAT, MXU_COLS):
        ch0 = (c0 % S5_FLAT) // S5_STATE * S5_GROUP
        k0 = ch0 // LANES * LANES
        assert ch0 + MXU_COLS // S5_STATE * S5_GROUP <= k0 + LANES
        e_ref[:, c0:c0 + MXU_COLS] = _dot(ub[:, k0:k0 + LANES], b_ref[k0:k0 + LANES, c0:c0 + MXU_COLS])

    ar = jnp.broadcast_to(a_ref[0:1, :], (SEQ_BLOCK, S5_FLAT))
    ai = jnp.broadcast_to(a_ref[1:2, :], (SEQ_BLOCK, S5_FLAT))

    def step(t, carry):
        sr, si = carry
        r = pl.ds(pl.multiple_of(t * SEQ_BLOCK, SEQ_BLOCK), SEQ_BLOCK)
        nr = ar * sr - ai * si + e_ref[r, 0:S5_FLAT]
        ni = ar * si + ai * sr + e_ref[r, S5_FLAT:2 * S5_FLAT]
        e_ref[r, 0:S5_FLAT] = nr
        e_ref[r, S5_FLAT:2 * S5_FLAT] = ni
        return nr, ni

    sr, si = lax.fori_loop(0, n_steps, step, (st_ref[:, 0:S5_FLAT], st_ref[:, S5_FLAT:2 * S5_FLAT]),
                           unroll=4 if n_steps % 4 == 0 else 1)
    st_ref[:, 0:S5_FLAT] = sr
    st_ref[:, S5_FLAT:2 * S5_FLAT] = si
    sl_ref[...] = st_ref[...]

    y = _dot(e_ref[...].astype(BF16), c_ref[...]) + d_ref[...] * ut
    y = _gelu(y)
    y = y * jax.nn.sigmoid(_dot(y.astype(BF16), wg_ref[...]) + bg_ref[...])
    _seq_block_store(y_ref, stage_ref, jnp.swapaxes(y.reshape(t_chunk, SEQ_BLOCK, S5_WIDTH), 0, 1))


def _io_scratch(l, t_chunk, width_in, width_out):
    if l % t_chunk == 0:
        return [pltpu.VMEM((SUBLANES, LANES), F32)] * 2
    return [pltpu.VMEM((SEQ_BLOCK, t_chunk, width_in), F32), pltpu.VMEM((SEQ_BLOCK, t_chunk, width_out), F32)]


def _s5_call(u2, n, l, s0, prm, t_chunk):
    grid, in_specs, out_specs, in_shapes, out_shapes = _seq_specs(n, l, t_chunk, [S5_WIDTH], [S5_WIDTH])
    kern = functools.partial(_s5_kernel, t_chunk=t_chunk, n_steps=min(t_chunk, l))
    y, s_new = pl.pallas_call(
        kern,
        out_shape=[jax.ShapeDtypeStruct(out_shapes[0], _branch_dtype(out_shapes[0])),
                   jax.ShapeDtypeStruct((n, 2 * S5_FLAT), F32)],
        grid=grid,
        in_specs=in_specs + [
            pl.BlockSpec((SEQ_BLOCK, 2 * S5_FLAT), lambda b, c: (b, 0)),
            _const_spec((S5_WIDTH, 2 * S5_FLAT)), _const_spec((2 * S5_FLAT, S5_WIDTH)),
            _const_spec((2, S5_FLAT)), _const_spec((1, S5_WIDTH)),
            _const_spec((S5_WIDTH, S5_WIDTH)), _const_spec((1, S5_WIDTH))],
        out_specs=out_specs + [pl.BlockSpec((SEQ_BLOCK, 2 * S5_FLAT), lambda b, c: (b, 0))],
        scratch_shapes=[pltpu.VMEM((SEQ_BLOCK * t_chunk, 2 * S5_FLAT), F32),
                        pltpu.VMEM((SEQ_BLOCK, 2 * S5_FLAT), F32)]
        + _io_scratch(l, t_chunk, S5_WIDTH, S5_WIDTH),
        compiler_params=_cparams("arbitrary", "arbitrary"),
        name="s5_mixer",
    )(u2.reshape(in_shapes[0]), s0, prm["s5_b"], prm["s5_c"], prm["s5_a"], prm["s5_d"], prm["s5_w_glu"],
      prm["s5_b_glu"])
    return y, s_new


def _mm(a, b, dims):
    return lax.dot_general(a.astype(BF16), b.astype(BF16), dims, preferred_element_type=F32)


def _cumsum_groups(x, group):
    pos = lax.broadcasted_iota(jnp.int32, (x.shape[0], 1), 0) % group
    s = 1
    while s < group:
        x = x + jnp.where(pos >= s, pltpu.roll(x, s, 0), 0.0)
        s *= 2
    return x


def _rw_kernel(p_ref, sh0_ref, s0_ref, mu_ref, vec_ref, w2_ref, a2_ref, g2_ref, *rest,
               t_chunk, l_valid, last_row, layer):
    earlier_ref = rest[0] if layer else None
    (y_ref, shl_ref, sl_ref,
     pbuf_ref, s_ref, at_ref, bt_ref, kt_ref, rt_ref, bh_ref, kh_ref, v_ref, gc_ref, ys_ref,
     pad_ref, stage_ref) = rest[1 if layer else 0:]
    rows = SEQ_BLOCK * t_chunk
    sub = min(MAX_SUB, t_chunk)
    n_sub = t_chunk // sub

    @pl.when(pl.program_id(1) == 0)
    def _():
        pbuf_ref[...] = sh0_ref[...]
        _heads_to_tiles(s0_ref, s_ref, RW_PAIRS, transpose=False)

    p3 = _seq_block_load(p_ref, pad_ref)
    first = lax.broadcasted_iota(jnp.int32, (1, t_chunk, 1), 1) == 0
    prev3 = jnp.where(first, pbuf_ref[...], pltpu.roll(p3, 1, 1))
    last = p3[:, last_row:last_row + 1, :]
    pbuf_ref[...] = last
    shl_ref[...] = last

    xm = (p3 + (prev3 - p3) * mu_ref[...]).reshape(rows, RW_COLS)
    r = xm[:, 0:RW_WIDTH]
    k = xm[:, RW_WIDTH:2 * RW_WIDTH]
    v = xm[:, 2 * RW_WIDTH:3 * RW_WIDTH]
    lora = xm[:, 3 * RW_WIDTH:RW_COLS]
    w0, a0, k_k, k_a = vec_ref[0:1, :], vec_ref[1:2, :], vec_ref[2:3, :], vec_ref[3:4, :]
    r_k, ln_w, ln_b = vec_ref[4:5, :], vec_ref[5:6, :], vec_ref[6:7, :]

    w = -_softplus(-(w0 + _dot(jnp.tanh(lora).astype(BF16), w2_ref[...]))) - 0.5
    log_decay = -jnp.exp(w)
    a = jax.nn.sigmoid(a0 + _dot(lora.astype(BF16), a2_ref[...]))
    g = _dot(jax.nn.sigmoid(lora).astype(BF16), g2_ref[...])

    ones = _head_ones()
    kk = k * k_k
    kk = kk * lax.rsqrt(jnp.maximum(_head_sum(kk * kk, ones), 1e-24))
    kt = k * (1.0 + (a - 1.0) * k_a)
    kka = kk * a
    if l_valid < t_chunk:
        live = lax.broadcasted_iota(jnp.int32, (rows, 1), 0) % t_chunk < l_valid
        log_decay = jnp.where(live, log_decay, 0.0)
        kk = jnp.where(live, kk, 0.0)
        kka = jnp.where(live, kka, 0.0)
        kt_live = jnp.where(live, kt, 0.0)
    else:
        kt_live = kt

    cum = _cumsum_groups(log_decay, sub)
    shape_g = (rows // sub, sub, RW_WIDTH)
    cum_end = jnp.broadcast_to(cum.reshape(shape_g)[:, sub - 1:sub, :], shape_g).reshape(rows, RW_WIDTH)
    g_in = jnp.exp(cum)
    g_inv = jnp.exp(-cum)
    g_ex = jnp.exp(cum - log_decay)
    g_out = jnp.exp(cum_end - cum)
    shape3 = (SEQ_BLOCK, t_chunk, RW_WIDTH)
    at_ref[...] = (-kk * g_ex).reshape(shape3)
    bt_ref[...] = (kka * g_inv).reshape(shape3)
    kt_ref[...] = (kt_live * g_inv).reshape(shape3)
    rt_ref[...] = (r * g_in).reshape(shape3)
    bh_ref[...] = (kka * g_out).reshape(shape3)
    kh_ref[...] = (kt_live * g_out).reshape(shape3)
    v_ref[...] = v.reshape(shape3)
    gc_ref[...] = jnp.exp(cum_end).reshape(shape3)

    m_rows = SEQ_BLOCK * sub
    ri = lax.broadcasted_iota(jnp.int32, (m_rows, m_rows), 0)
    ci = lax.broadcasted_iota(jnp.int32, (m_rows, m_rows), 1)
    same_seq = (ri // sub) == (ci // sub)
    before = same_seq & ((ci % sub) < (ri % sub))
    upto = same_seq & ((ci % sub) <= (ri % sub))
    eye = (ri == ci).astype(F32)
    pair_mask = (ri // 2) == (ci // 2)
    level_masks = []
    s = 2
    while s < sub:
        level_masks.append(((ri // (2 * s)) == (ci // (2 * s))) & ((ri // s) != (ci // s)))
        s *= 2
    lane = lax.broadcasted_iota(jnp.int32, (m_rows, LANES), 1)
    head_mask = [lane < RW_HEAD, lane >= RW_HEAD]
    wide = (m_rows, SEQ_BLOCK * LANES)
    own_rows = (lax.broadcasted_iota(jnp.int32, wide, 1) // LANES
                == lax.broadcasted_iota(jnp.int32, wide, 0) // sub)
    tiles = (LANES, SEQ_BLOCK * LANES)
    same_head = (lax.broadcasted_iota(jnp.int32, tiles, 0) // RW_HEAD
                 == (lax.broadcasted_iota(jnp.int32, tiles, 1) % LANES) // RW_HEAD)
    group = max(g for g in (1, 2, 4) if n_sub % g == 0)
    units = [(j, hp) for j in range(group) for hp in range(RW_PAIRS)]
    heads = [(u, h) for u in range(len(units)) for h in range(HEAD_PAIR)]

    def sub_chunk(c, carry):
        rs = [pl.ds(pl.multiple_of((c * group + j) * sub, sub), sub) for j in range(group)]

        def ld(ref, j, hp):
            return ref[:, rs[j], pl.ds(hp * LANES, LANES)].reshape(m_rows, LANES)

        at, bt, ktl, rt, bh, kh, vv, gcv = ([ld(ref, j, hp) for j, hp in units]
                                            for ref in (at_ref, bt_ref, kt_ref, rt_ref, bh_ref, kh_ref,
                                                        v_ref, gc_ref))
        state = [[s_ref[hp, n] for n in range(SEQ_BLOCK)] for hp in range(RW_PAIRS)]
        zero = jnp.zeros((m_rows, LANES), F32)
        msk = lambda x, h: jnp.where(head_mask[h], x, zero)

        gram = [_mm(jnp.concatenate([msk(at[hp], 0), msk(rt[hp], 0), msk(at[hp], 1), msk(rt[hp], 1)], axis=0),
                    jnp.concatenate([bt[hp], ktl[hp]], axis=0), _NT)
                for hp in range(len(units))]

        def quad(hp, h, row, col, keep):
            blk = gram[hp][(2 * h + row) * m_rows:(2 * h + row + 1) * m_rows, col * m_rows:(col + 1) * m_rows]
            return jnp.where(keep, blk, 0.0)

        m_ab = [quad(hp, h, 0, 0, before) for hp, h in heads]
        m_ak = [quad(hp, h, 0, 1, before) for hp, h in heads]
        n_rb = [quad(hp, h, 1, 0, upto) for hp, h in heads]
        n_rk = [quad(hp, h, 1, 1, upto) for hp, h in heads]
        tinv = [eye + jnp.where(pair_mask, m, 0.0) for m in m_ab]
        for lm in level_masks:
            prod = [_mm(jnp.where(lm, m, 0.0), t, _NN) for m, t in zip(m_ab, tinv)]
            tinv = [t + _mm(t, q, _NN) for t, q in zip(tinv, prod)]
        vh = [msk(vv[hp], h) for hp, h in heads]
        w1 = [_mm(m, x, _NN) for m, x in zip(m_ak, vh)]
        ap = [_mm(t, jnp.concatenate([msk(at[hp], h), w], axis=1), _NN)
              for t, (hp, h), w in zip(tinv, heads, w1)]
        nb = [_mm(m, x, _NN) for m, x in zip(n_rb, ap)]
        nv = [_mm(m, x, _NN) for m, x in zip(n_rk, vh)]

        def per_seq(x):
            return jnp.where(own_rows, jnp.concatenate([x] * SEQ_BLOCK, axis=1), 0.0)

        pairs = range(len(units))
        both = lambda xs, hp, cols: xs[HEAD_PAIR * hp][:, cols] + xs[HEAD_PAIR * hp + 1][:, cols]
        lo, hi = slice(0, LANES), slice(LANES, 2 * LANES)
        bh_x = [per_seq(bh[hp]) for hp in pairs]
        kh_x = [per_seq(kh[hp]) for hp in pairs]
        gam = [jnp.where(same_head, _mm(both(ap, hp, lo).T, bh_x[hp], _NN), 0.0) for hp in pairs]
        u = [jnp.where(same_head, _mm(jnp.concatenate([both(ap, hp, hi).T, vv[hp].T], axis=1),
                                      jnp.concatenate([bh_x[hp], kh_x[hp]], axis=0), _NN), 0.0)
             for hp in pairs]
        for hp, (j, pair) in enumerate(units):
            r_hat = rt[hp] + both(nb, hp, lo)
            y_zero = both(nb, hp, hi) + nv[HEAD_PAIR * hp] + nv[HEAD_PAIR * hp + 1]
            cols = pl.ds(pair * LANES, LANES)
            for n in range(SEQ_BLOCK):
                q = slice(n * sub, (n + 1) * sub)
                blk = slice(n * LANES, (n + 1) * LANES)
                st = state[pair][n]
                ys_ref[n, rs[j], cols] = _mm(r_hat[q], st, _NT) + y_zero[q]
                state[pair][n] = (st * gcv[hp][n * sub:n * sub + 1, :]
                                  + _mm(st, gam[hp][:, blk], _NN) + u[hp][:, blk])
        for pair in range(RW_PAIRS):
            for n in range(SEQ_BLOCK):
                s_ref[pair, n] = state[pair][n]
        return carry

    lax.fori_loop(0, n_sub // group, sub_chunk, 0)

    @pl.when(pl.program_id(1) == pl.num_programs(1) - 1)
    def _():
        if layer:
            sl_ref[0:layer] = earlier_ref[...]
        _tiles_to_heads(s_ref, sl_ref.at[layer], RW_PAIRS, transpose=False)

    y = ys_ref[...].reshape(rows, RW_WIDTH)
    inv = 1.0 / RW_HEAD
    mean = _head_sum(y, ones) * inv
    yc = y - mean
    var = _head_sum(yc * yc, ones) * inv
    y = yc * lax.rsqrt(var + RW_GN_EPS) * ln_w + ln_b
    bonus = _head_sum(r * kt * r_k, ones, split=False) * v
    _seq_block_store(y_ref, stage_ref, ((y + bonus) * g).reshape(shape3))


def _layer_state_specs(state, layer):
    tail = state.shape[1:]
    zeros = (0,) * len(tail)
    read = pl.BlockSpec((SEQ_BLOCK,) + tail, lambda b, c: (b,) + zeros)
    upto = lambda k: pl.BlockSpec((k, SEQ_BLOCK) + tail, lambda b, c: (0, b) + zeros)
    return read, upto(layer + 1), ([upto(layer)] if layer else [])


def _rw_call(p2, n, l, shift0, state, layer, earlier, prm, t_chunk):
    assert t_chunk % min(MAX_SUB, t_chunk) == 0
    grid, in_specs, out_specs, in_shapes, out_shapes = _seq_specs(n, l, t_chunk, [RW_COLS], [RW_WIDTH])
    valid_in_chunk = min(t_chunk, l)
    kern = functools.partial(_rw_kernel, t_chunk=t_chunk, l_valid=valid_in_chunk, last_row=valid_in_chunk - 1,
                             layer=layer)
    st_in, st_out, st_earlier = _layer_state_specs(state, layer)
    sh_spec = pl.BlockSpec((SEQ_BLOCK, 1, RW_COLS), lambda b, c: (b, 0, 0))
    q_scr = pltpu.VMEM((SEQ_BLOCK, t_chunk, RW_WIDTH), F32)
    return pl.pallas_call(
        kern,
        out_shape=[jax.ShapeDtypeStruct(out_shapes[0], _branch_dtype(out_shapes[0])),
                   jax.ShapeDtypeStruct((n, 1, RW_COLS), F32),
                   jax.ShapeDtypeStruct((layer + 1,) + state.shape, F32)],
        grid=grid,
        in_specs=in_specs + [sh_spec, st_in,
                             _const_spec((1, RW_COLS)), _const_spec((SUBLANES, RW_WIDTH)),
                             _const_spec((RW_LORA, RW_WIDTH)), _const_spec((RW_LORA, RW_WIDTH)),
                             _const_spec((RW_LORA, RW_WIDTH))] + st_earlier,
        out_specs=out_specs + [sh_spec, st_out],
        scratch_shapes=[pltpu.VMEM((SEQ_BLOCK, 1, RW_COLS), F32),
                        pltpu.VMEM((RW_PAIRS, SEQ_BLOCK, LANES, LANES), F32)] + [q_scr] * 9
        + _io_scratch(l, t_chunk, RW_COLS, RW_WIDTH),
        compiler_params=_cparams("arbitrary", "arbitrary"),
        name="rwkv7_mixer",
    )(p2.reshape(in_shapes[0]), shift0, state, prm["rw_mu"], prm["rw_vec"], prm["rw_w2"], prm["rw_a2"],
      prm["rw_g2"], *([earlier] if layer else []))


def _hg_kernel(z_ref, s0_ref, lower_ref, ng_ref, *rest, t_chunk, l_valid, layer):
    earlier_ref = rest[0] if layer else None
    (y_ref, sl_ref, s_ref, qs_ref, q_ref, k_ref, kh_ref, v_ref, b_ref, gc_ref, ys_ref, pad_ref,
     stage_ref) = rest[1 if layer else 0:]
    rows = SEQ_BLOCK * t_chunk
    sub = min(MAX_SUB, t_chunk)
    n_sub = t_chunk // sub

    @pl.when(pl.program_id(1) == 0)
    def _():
        _heads_to_tiles(s0_ref, s_ref, HG_PAIRS, transpose=True)

    z = _seq_block_load(z_ref, pad_ref).reshape(rows, 4 * HG_WIDTH)
    q = jax.nn.silu(z[:, 0:HG_WIDTH])
    f = z[:, HG_WIDTH:2 * HG_WIDTH]
    i = z[:, 2 * HG_WIDTH:3 * HG_WIDTH]
    og = z[:, 3 * HG_WIDTH:4 * HG_WIDTH]
    lower = lower_ref[...]
    fgate = lower + (1.0 - lower) * jax.nn.sigmoid(f)
    log_f = jnp.log(jnp.maximum(fgate, HG_GATE_FLOOR))
    k = 1.0 - fgate
    if l_valid < t_chunk:
        live = lax.broadcasted_iota(jnp.int32, (rows, 1), 0) % t_chunk < l_valid
        log_f = jnp.where(live, log_f, 0.0)
        k = jnp.where(live, k, 0.0)
    cum = _cumsum_groups(log_f, sub)
    shape_g = (rows // sub, sub, HG_WIDTH)
    cum_end = jnp.broadcast_to(cum.reshape(shape_g)[:, sub - 1:sub, :], shape_g).reshape(rows, HG_WIDTH)
    shape3 = (SEQ_BLOCK, t_chunk, HG_WIDTH)
    qs_ref[...] = (q * jnp.exp(cum)).reshape(shape3)
    q_ref[...] = q.reshape(shape3)
    k_ref[...] = k.reshape(shape3)
    kh_ref[...] = (k * jnp.exp(cum_end - cum)).reshape(shape3)
    v_ref[...] = i.reshape(shape3)
    b_ref[...] = cum.reshape(shape3)
    gc_ref[...] = jnp.exp(cum_end).reshape(shape3)

    m_rows = SEQ_BLOCK * sub
    ones = _head_ones()
    wide = (m_rows, SEQ_BLOCK * LANES)
    own_rows = (lax.broadcasted_iota(jnp.int32, wide, 1) // LANES
                == lax.broadcasted_iota(jnp.int32, wide, 0) // sub)
    tiles = (LANES, SEQ_BLOCK * LANES)
    same_head = (lax.broadcasted_iota(jnp.int32, tiles, 0) // HG_HEAD
                 == (lax.broadcasted_iota(jnp.int32, tiles, 1) % LANES) // HG_HEAD)
    step = lax.broadcasted_iota(jnp.int32, (sub, LANES), 0)

    def sub_chunk(c, carry):
        rs = pl.ds(pl.multiple_of(c * sub, sub), sub)
        for hp in range(HG_PAIRS):
            cols = pl.ds(hp * LANES, LANES)
            ld = lambda ref: ref[:, rs, cols].reshape(m_rows, LANES)
            qs, qq, kk, kh, vv, bb, gcv = (ld(qs_ref), ld(q_ref), ld(k_ref), ld(kh_ref), ld(v_ref),
                                           ld(b_ref), ld(gc_ref))
            state = [s_ref[hp, n] for n in range(SEQ_BLOCK)]
            kh_x = jnp.where(own_rows, jnp.concatenate([kh] * SEQ_BLOCK, axis=1), 0.0)
            u = jnp.where(same_head, _mm(vv.T, kh_x, _NN), 0.0)
            for n in range(SEQ_BLOCK):
                sl = slice(n * sub, (n + 1) * sub)
                bn, qn, kn, vn = bb[sl], qq[sl], kk[sl], vv[sl]
                prods = []
                for t in range(sub):
                    keep = step <= t
                    prods.append(jnp.where(keep, qn[t:t + 1, :] * kn * jnp.exp(bn[t:t + 1, :] - bn), 0.0))
                att = _dot(jnp.concatenate(prods, axis=0).astype(BF16), ones)
                o_rows = [jnp.sum(att[t * sub:(t + 1) * sub] * vn, axis=0, keepdims=True)
                          for t in range(sub)]
                ys_ref[n, rs, cols] = _mm(qs[sl], state[n], _NT) + jnp.concatenate(o_rows, axis=0)
                blk = slice(n * LANES, (n + 1) * LANES)
                s_ref[hp, n] = state[n] * gcv[n * sub:n * sub + 1, :] + u[:, blk]
        return carry

    lax.fori_loop(0, n_sub, sub_chunk, 0)

    @pl.when(pl.program_id(1) == pl.num_programs(1) - 1)
    def _():
        if layer:
            sl_ref[0:layer] = earlier_ref[...]
        _tiles_to_heads(s_ref, sl_ref.at[layer], HG_PAIRS, transpose=True)

    o = ys_ref[...].reshape(rows, HG_WIDTH)
    ms = _head_sum(o * o, ones) * (1.0 / HG_HEAD)
    o = o * lax.rsqrt(ms + RMS_EPS) * ng_ref[...] * jax.nn.sigmoid(og)
    _seq_block_store(y_ref, stage_ref, o.reshape(shape3))


def _hg_call(z2, n, l, state, layer, earlier, lower, norm_g, t_chunk):
    assert t_chunk % min(MAX_SUB, t_chunk) == 0
    grid, in_specs, out_specs, in_shapes, out_shapes = _seq_specs(n, l, t_chunk, [4 * HG_WIDTH], [HG_WIDTH])
    kern = functools.partial(_hg_kernel, t_chunk=t_chunk, l_valid=min(t_chunk, l), layer=layer)
    st_in, st_out, st_earlier = _layer_state_specs(state, layer)
    q_scr = pltpu.VMEM((SEQ_BLOCK, t_chunk, HG_WIDTH), F32)
    return pl.pallas_call(
        kern,
        out_shape=[jax.ShapeDtypeStruct(out_shapes[0], _branch_dtype(out_shapes[0])),
                   jax.ShapeDtypeStruct((layer + 1,) + state.shape, F32)],
        grid=grid,
        in_specs=in_specs + [st_in, _const_spec((1, HG_WIDTH)), _const_spec((1, HG_WIDTH))] + st_earlier,
        out_specs=out_specs + [st_out],
        scratch_shapes=[pltpu.VMEM((HG_PAIRS, SEQ_BLOCK, LANES, LANES), F32)] + [q_scr] * 8
        + _io_scratch(l, t_chunk, 4 * HG_WIDTH, HG_WIDTH),
        compiler_params=_cparams("arbitrary", "arbitrary"),
        name="hgrn2_mixer",
    )(z2.reshape(in_shapes[0]), state, lower.reshape(1, HG_WIDTH), norm_g.reshape(1, HG_WIDTH),
      *([earlier] if layer else []))


def _merge_ffn_kernel(*refs, l, tm, has_state, final_norm):
    (x_ref, ya_ref, yb_ref, yc_ref, gates_ref, gt1_ref, la_ref, lb_ref, lc_ref, wo_ref) = refs[:10]
    refs = refs[10:]
    if has_state:
        (g_ref, sh_ref, sc_ref, gt_ref, wup_ref, cw_ref, cb_ref, wdn_ref, fg_ref, st_ref,
         o_ref, tail_ref, h_ref, act_ref, hist_ref, buf_ref, old_ref) = refs
    else:
        (g_ref, sh_ref, sc_ref, gt_ref, wup_ref, cw_ref, cb_ref, wdn_ref, fg_ref,
         o_ref, tail_ref, h_ref, act_ref, hist_ref) = refs
    i = pl.program_id(0)
    n_chunks = D_FF // FF_CHUNK
    n_hist = CONV_W - 1

    if has_state:
        t_idx = lax.broadcasted_iota(jnp.int32, (tm, 1), 0) % l
        after = [t_idx >= j + 1 for j in range(n_hist)]
        old_ref[...] = jnp.zeros_like(old_ref)
        buf_ref[:, 0:SUBLANES, :] = jnp.zeros((2, SUBLANES, FF_CHUNK), F32)
        for s in range(tm // l):
            for j in range(n_hist):
                old_ref[j, s * l:s * l + j + 1, :] = st_ref[s, n_hist - 1 - j:n_hist, :]
    else:
        @pl.when(i % (l // tm) == 0)
        def _():
            hist_ref[...] = jnp.zeros_like(hist_ref)

    d = D_MODEL
    gates = _tile(gates_ref)
    lift = lambda y_ref, w_ref: _dot(_tile(y_ref)[...].astype(BF16), w_ref[...])
    merged = (gates[:, 0:d].astype(F32) * lift(ya_ref, la_ref)
              + gates[:, d:2 * d].astype(F32) * lift(yb_ref, lb_ref)
              + gates[:, 2 * d:3 * d].astype(F32) * lift(yc_ref, lc_ref))
    x = _tile(x_ref)[...] + gt1_ref[0] * _dot(merged.astype(BF16), wo_ref[...])
    h = _rms_modulate(x, g_ref[...], sh_ref[0], sc_ref[0])
    groups = tm // SUBLANES
    if not has_state:
        h = jnp.swapaxes(h.reshape(SUBLANES, groups, D_MODEL), 0, 1).reshape(tm, D_MODEL)
        row8 = lax.broadcasted_iota(jnp.int32, (SUBLANES, FF_CHUNK), 0)
    h_ref[...] = h.astype(BF16)

    def conv_half(c, half):
        col0 = half * D_FF + c * FF_CHUNK
        cols = slice(col0, col0 + FF_CHUNK)
        up = _dot(h_ref[...], wup_ref[:, cols])
        if has_state:
            buf_ref[half, SUBLANES:SUBLANES + tm, :] = up
            prev = [jnp.where(after[j], buf_ref[half, SUBLANES - 1 - j:SUBLANES - 1 - j + tm, :], 0.0)
                    + old_ref[j, :, cols] for j in range(n_hist)]
            for s in range(tm // l):
                r0 = SUBLANES + (s + 1) * l - n_hist
                tail_ref[s, :, cols] = buf_ref[half, r0:r0 + n_hist, :]
        else:
            late = [up[tm - (j + 1) * SUBLANES:tm - j * SUBLANES, :] for j in range(n_hist)]
            first = [jnp.where(row8 == 0, pltpu.roll(hist_ref[c, half, j], 1, 0), pltpu.roll(late[j], 1, 0))
                     for j in range(n_hist)]
            for j in range(n_hist):
                hist_ref[c, half, j] = late[j]
            prev = [jnp.concatenate([first[0], up[0:tm - SUBLANES, :]], axis=0),
                    jnp.concatenate([first[1], first[0], up[0:tm - 2 * SUBLANES, :]], axis=0)]
            tail_ref[0, 0:1, cols] = up[tm - SUBLANES - 1:tm - SUBLANES, :]
            tail_ref[0, 1:2, cols] = up[tm - 1:tm, :]
        return (cb_ref[:, cols] + cw_ref[0:1, cols] * prev[1] + cw_ref[1:2, cols] * prev[0]
                + cw_ref[2:3, cols] * up)

    for c in range(n_chunks):
        act = _gelu(conv_half(c, 0)) * conv_half(c, 1)
        act_ref[:, c * FF_CHUNK:(c + 1) * FF_CHUNK] = act.astype(BF16)
    acc = _dot(act_ref[...], wdn_ref[...])
    if not has_state:
        acc = jnp.swapaxes(acc.reshape(groups, SUBLANES, D_MODEL), 0, 1).reshape(tm, D_MODEL)
    out = x + gt_ref[0] * acc
    if final_norm:
        ms = jnp.mean(out * out, axis=-1, keepdims=True)
        out = out * lax.rsqrt(ms + RMS_EPS) * fg_ref[...]
    _tile(o_ref)[...] = out


def _merge_ffn_call(x, n, l, ya, yb, yc, gates, gate1, g, shift, scale, gate2, prm, final_g, conv_state, tm,
                    final_norm):
    assert CONV_W == 3
    d = x.shape[-1]
    has_state = conv_state is not None
    g1_arr, g1_spec = _mod_operand(gate1, l, tm)
    sh_arr, sh_spec = _mod_operand(shift, l, tm)
    sc_arr, sc_spec = _mod_operand(scale, l, tm)
    gt_arr, gt_spec = _mod_operand(gate2, l, tm)
    shape, x_spec = _tok_layout(n, l, tm, d)
    row_spec = lambda w: _tok_layout(n, l, tm, w)[1]
    in_specs = [x_spec, row_spec(S5_WIDTH), row_spec(RW_WIDTH), row_spec(HG_WIDTH), row_spec(N_BRANCH * d),
                g1_spec, _const_spec((S5_WIDTH, d)), _const_spec((RW_WIDTH, d)), _const_spec((HG_WIDTH, d)),
                _const_spec((d, d)),
                _const_spec((1, d)), sh_spec, sc_spec, gt_spec,
                _const_spec((d, 2 * D_FF)), _const_spec((CONV_W, 2 * D_FF)), _const_spec((1, 2 * D_FF)),
                _const_spec((D_FF, d)), _const_spec((1, d))]
    operands = [x, ya, yb, yc, gates, g1_arr, prm["w_lift_a"], prm["w_lift_b"], prm["w_lift_c"], prm["w_out"],
                g.reshape(1, d), sh_arr, sc_arr, gt_arr, prm["w_up"], prm["conv_w"], prm["conv_b"],
                prm["w_down"], final_g.reshape(1, d)]
    scratch = [pltpu.VMEM((tm, d), BF16), pltpu.VMEM((tm, D_FF), BF16),
               pltpu.VMEM((D_FF // FF_CHUNK, 2, CONV_W - 1, SUBLANES, FF_CHUNK), F32)]
    if has_state:
        assert tm % l == 0 and l >= CONV_W - 1
        state_spec = pl.BlockSpec((tm // l, CONV_W - 1, 2 * D_FF), lambda i: (i, 0, 0))
        in_specs.append(state_spec)
        operands.append(conv_state)
        tail_spec = state_spec
        scratch += [pltpu.VMEM((2, tm + SUBLANES, FF_CHUNK), F32), pltpu.VMEM((CONV_W - 1, tm, 2 * D_FF), F32)]
    else:
        assert l % tm == 0
        per = l // tm
        tail_spec = pl.BlockSpec((1, CONV_W - 1, 2 * D_FF), lambda i: (i // per, 0, 0))
    kern = functools.partial(_merge_ffn_kernel, l=l, tm=tm, has_state=has_state, final_norm=final_norm)
    return pl.pallas_call(
        kern,
        out_shape=[jax.ShapeDtypeStruct(shape, F32), jax.ShapeDtypeStruct((n, CONV_W - 1, 2 * D_FF), F32)],
        grid=(n * l // tm,),
        in_specs=in_specs,
        out_specs=[x_spec, tail_spec],
        scratch_shapes=scratch,
        compiler_params=_cparams("arbitrary"),
        name="merge_conv_ffn",
    )(*operands)


def _prepare_layer(p):
    out = {}
    w_in = p["w_in"]
    c1 = S5_WIDTH
    c2 = c1 + RW_COLS
    c3 = c2 + 4 * HG_WIDTH
    out["w_in_a"] = w_in[:, :c1].astype(BF16)
    out["w_in_b"] = w_in[:, c1:c2].astype(BF16)
    out["w_in_c"] = w_in[:, c2:c3].astype(BF16)
    out["w_in_g"] = w_in[:, c3:].astype(BF16)
    out["w_ada"] = p["w_ada"].astype(BF16)
    for name in ("w_lift_a", "w_lift_b", "w_lift_c", "w_out", "w_up", "w_down"):
        out[name] = p[name].astype(BF16)
    out["conv_w"] = p["conv_w"]
    out["conv_b"] = p["conv_b"].reshape(1, 2 * D_FF)

    lr = p["s5_lambda_re"]
    li = p["s5_lambda_im"]
    dt = jnp.exp(p["s5_log_dt"])[:, None]
    mag = jnp.exp(lr * dt)
    ar = mag * jnp.cos(li * dt)
    ai = mag * jnp.sin(li * dt)
    den = lr * lr + li * li
    zr = ((ar - 1.0) * lr + ai * li) / den
    zi = (ai * lr - (ar - 1.0) * li) / den
    bbr = zr[..., None] * p["s5_b_re"] - zi[..., None] * p["s5_b_im"]
    bbi = zr[..., None] * p["s5_b_im"] + zi[..., None] * p["s5_b_re"]
    eye = jnp.eye(S5_GROUPS, dtype=F32)
    bmat = jnp.einsum("gh,rgpc->gcrhp", eye, jnp.stack([bbr, bbi])).reshape(S5_WIDTH, 2 * S5_FLAT)
    cmat = jnp.einsum("hg,rgcp->rhpgc", eye, jnp.stack([p["s5_c_re"], -p["s5_c_im"]])).reshape(
        2 * S5_FLAT, S5_WIDTH)
    out["s5_b"] = bmat.astype(BF16)
    out["s5_c"] = cmat.astype(BF16)
    out["s5_a"] = jnp.stack([ar.reshape(S5_FLAT), ai.reshape(S5_FLAT)])
    out["s5_d"] = p["s5_d"].reshape(1, S5_WIDTH)
    out["s5_w_glu"] = p["s5_w_glu"].astype(BF16)
    out["s5_b_glu"] = p["s5_b_glu"].reshape(1, S5_WIDTH)

    out["rw_mu"] = p["rw_mu"].reshape(1, RW_COLS)
    out["rw_vec"] = jnp.stack([p["rw_w0"], p["rw_a0"], p["rw_k_k"], p["rw_k_a"],
                               p["rw_r_k"].reshape(RW_WIDTH), p["rw_ln_w"], p["rw_ln_b"],
                               jnp.zeros((RW_WIDTH,), F32)])
    zw = jnp.zeros((RW_LORA, RW_WIDTH), F32)
    out["rw_w2"] = zw.at[0:RW_DECAY_LORA].set(p["rw_w2"]).astype(BF16)
    out["rw_a2"] = zw.at[RW_DECAY_LORA:RW_DECAY_LORA + RW_AAA_LORA].set(p["rw_a2"]).astype(BF16)
    out["rw_g2"] = zw.at[RW_DECAY_LORA + RW_AAA_LORA:].set(p["rw_g2"]).astype(BF16)
    out["hg_norm"] = p["hg_norm"]
    out["g_mix"] = p["g_mix"]
    out["g_ffn"] = p["g_ffn"]
    out["b_ada"] = p["b_ada"]
    return out


def _chunk_for(l, sub, t_chunk):
    return t_chunk if l % t_chunk == 0 else -(-l // sub) * sub


def _run_trunk(x, mods, st_s5, st_shift, st_rw, st_hg, st_conv, lower, final_g, prms, t_chunk):
    n, l, d = x.shape
    rows = n * l
    tm = min(ROW_TILE, rows)
    depth = len(prms)
    assert (l % tm == 0) == (l % t_chunk == 0)
    h = x.reshape(_tok_layout(n, l, tm, d)[0])
    out_s5, out_shift, out_conv = [], [], []
    rw_new = hg_new = None
    for layer in range(depth):
        prm = prms[layer]
        sh1, sc1, gt1, sh2, sc2, gt2 = jnp.split(mods[layer], 6, axis=-1)
        za, zb, zc, gates = _in_proj_call(h, n, l, prm["g_mix"], sh1, sc1, prm["w_in_a"], prm["w_in_b"],
                                          prm["w_in_c"], prm["w_in_g"], tm)

        s5_in = jnp.concatenate([st_s5[layer][..., 0].reshape(n, S5_FLAT),
                                 st_s5[layer][..., 1].reshape(n, S5_FLAT)], axis=1)
        ya, s5_new = _s5_call(za, n, l, s5_in, prm, _chunk_for(l, SUBLANES, t_chunk))
        yb, shift_new, rw_new = _rw_call(zb, n, l, st_shift[layer].reshape(n, 1, RW_COLS), st_rw[layer], layer, rw_new,
                                         prm, _chunk_for(l, SUBLANES, t_chunk))
        yc, hg_new = _hg_call(zc, n, l, st_hg[layer], layer, hg_new, lower[layer], prm["hg_norm"],
                              _chunk_for(l, SUBLANES, t_chunk))

        ffn_tm = tm if st_conv is None else min(tm, 128)
        h, conv_new = _merge_ffn_call(h, n, l, ya, yb, yc, gates, gt1, prm["g_ffn"], sh2, sc2, gt2, prm, final_g,
                                      None if st_conv is None else st_conv[layer], ffn_tm,
                                      final_norm=(layer == depth - 1))
        out_s5.append(jnp.stack([s5_new[:, :S5_FLAT].reshape(n, S5_GROUPS, S5_STATE),
                                 s5_new[:, S5_FLAT:].reshape(n, S5_GROUPS, S5_STATE)], axis=-1))
        out_shift.append(shift_new.reshape(n, RW_COLS))
        out_conv.append(conv_new)
    y = h.reshape(n, l, d).astype(x.dtype)
    return y, (jnp.stack(out_s5), jnp.stack(out_shift), rw_new, hg_new, jnp.stack(out_conv))


def kernel(x_prompt, x_sample, c_prompt, c_sample, state_s5, state_rwkv_shift, state_rwkv, state_hgrn, state_ffn_conv, w_ada, b_ada, g_mix, g_ffn, w_in, s5_lambda_re, s5_lambda_im, s5_log_dt, s5_b_re, s5_b_im, s5_c_re, s5_c_im, s5_d, s5_w_glu, s5_b_glu, rw_mu, rw_w0, rw_w2, rw_a0, rw_a2, rw_g2, rw_k_k, rw_k_a, rw_r_k, rw_ln_w, rw_ln_b, hg_lb, hg_norm, w_lift_a, w_lift_b, w_lift_c, w_out, w_up, conv_w, conv_b, w_down, final_g):
    per_layer = {
        "w_ada": w_ada, "b_ada": b_ada, "g_mix": g_mix, "g_ffn": g_ffn, "w_in": w_in,
        "s5_lambda_re": s5_lambda_re, "s5_lambda_im": s5_lambda_im, "s5_log_dt": s5_log_dt,
        "s5_b_re": s5_b_re, "s5_b_im": s5_b_im, "s5_c_re": s5_c_re, "s5_c_im": s5_c_im,
        "s5_d": s5_d, "s5_w_glu": s5_w_glu, "s5_b_glu": s5_b_glu,
        "rw_mu": rw_mu, "rw_w0": rw_w0, "rw_w2": rw_w2, "rw_a0": rw_a0, "rw_a2": rw_a2,
        "rw_g2": rw_g2, "rw_k_k": rw_k_k, "rw_k_a": rw_k_a, "rw_r_k": rw_r_k,
        "rw_ln_w": rw_ln_w, "rw_ln_b": rw_ln_b, "hg_norm": hg_norm,
        "w_lift_a": w_lift_a, "w_lift_b": w_lift_b, "w_lift_c": w_lift_c, "w_out": w_out,
        "w_up": w_up, "conv_w": conv_w, "conv_b": conv_b, "w_down": w_down,
    }
    depth = w_ada.shape[0]
    prms = [_prepare_layer({k: v[layer] for k, v in per_layer.items()}) for layer in range(depth)]

    lbp = jax.nn.softmax(hg_lb.astype(F32), axis=0)
    lower = jnp.cumsum(lbp, axis=0) - lbp[0]

    nb = x_prompt.shape[0]
    ns = x_sample.shape[0]
    c_all = jnp.concatenate([c_prompt, c_sample], axis=0).astype(F32)
    mods = [_ada_call(c_all, prms[layer]["w_ada"], prms[layer]["b_ada"]) for layer in range(depth)]
    mods_p = [m[:nb] for m in mods]
    mods_s = [m[nb:] for m in mods]

    z_s5 = jnp.zeros((depth, nb) + state_s5.shape[2:], F32)
    z_shift = jnp.zeros((depth, nb) + state_rwkv_shift.shape[2:], F32)
    z_rw = jnp.zeros((depth, nb) + state_rwkv.shape[2:], F32)
    z_hg = jnp.zeros((depth, nb) + state_hgrn.shape[2:], F32)

    y_prompt, (s5_p, shift_p, rw_p, hg_p, conv_p) = _run_trunk(
        x_prompt, mods_p, z_s5, z_shift, z_rw, z_hg, None, lower, final_g, prms, t_chunk=64)
    y_sample, (s5_s, shift_s, rw_s, hg_s, conv_s) = _run_trunk(
        x_sample, mods_s, state_s5, state_rwkv_shift, state_rwkv, state_hgrn, state_ffn_conv,
        lower, final_g, prms, t_chunk=64)
    return (y_prompt, y_sample, s5_p, shift_p, rw_p, hg_p, conv_p, s5_s, shift_s, rw_s, hg_s, conv_s)
```

```python
import functools
import math

import jax
import jax.numpy as jnp
from jax import lax
from jax.experimental import pallas as pl
from jax.experimental.pallas import tpu as pltpu

F32 = jnp.float32
BF16 = jnp.bfloat16

D_MODEL = 1024
S5_WIDTH = D_MODEL // 4
S5_GROUP = 16
S5_GROUPS = S5_WIDTH // S5_GROUP
S5_STATE = 64
S5_FLAT = S5_GROUPS * S5_STATE
RW_WIDTH = D_MODEL // 2
RW_HEAD = 64
RW_HEADS = RW_WIDTH // RW_HEAD
RW_DECAY_LORA = 32
RW_AAA_LORA = 32
RW_GATE_LORA = 64
RW_LORA = RW_DECAY_LORA + RW_AAA_LORA + RW_GATE_LORA
RW_COLS = 3 * RW_WIDTH + RW_LORA
RW_GN_EPS = 1e-5 * RW_HEAD
HG_WIDTH = D_MODEL // 4
HG_HEAD = 64
HG_HEADS = HG_WIDTH // HG_HEAD
HG_GATE_FLOOR = 1e-30
N_BRANCH = 3
D_FF = 256 * ((8 * D_MODEL // 3 + 255) // 256)
CONV_W = 3
RMS_EPS = 1e-6

LANES = 128
SUBLANES = 8
MXU_COLS = 256
VMEM_LIMIT_BYTES = 56 * 1024 * 1024

SEQ_BLOCK = SUBLANES
HEAD_PAIR = LANES // RW_HEAD
ROW_TILE = 512
FF_CHUNK = 256
IN_PROJ_CHUNK = 512
MAX_SUB = 16
RW_PAIRS = RW_HEADS // HEAD_PAIR
HG_PAIRS = HG_HEADS // HEAD_PAIR

_NN = (((1,), (0,)), ((), ()))
_NT = (((1,), (1,)), ((), ()))


def _cparams(*sem):
    return pltpu.CompilerParams(dimension_semantics=sem, vmem_limit_bytes=VMEM_LIMIT_BYTES)


def _const_spec(shape):
    nd = len(shape)
    return pl.BlockSpec(shape, lambda *_: (0,) * nd, pipeline_mode=pl.Buffered(1))


def _layer_weight_spec(stack, layer):
    shape = stack.shape[1:]
    zeros = (0,) * len(shape)
    return pl.BlockSpec((None,) + shape, lambda *_: (layer,) + zeros, pipeline_mode=pl.Buffered(1))


def _dot(a, b):
    return jnp.dot(a, b, preferred_element_type=F32)


def _split_bf16(x):
    hi = x.astype(BF16)
    lo = (x - hi.astype(F32)).astype(BF16)
    return hi, lo


def _head_ones():
    r = lax.broadcasted_iota(jnp.int32, (LANES, LANES), 0) // RW_HEAD
    c = lax.broadcasted_iota(jnp.int32, (LANES, LANES), 1) // RW_HEAD
    return (r == c).astype(BF16)


def _head_sum(x, ones, split=True):
    hi, lo = _split_bf16(x) if split else (x.astype(BF16), None)
    tiles = []
    for c in range(0, x.shape[1], LANES):
        t = _dot(hi[:, c:c + LANES], ones)
        tiles.append(t + _dot(lo[:, c:c + LANES], ones) if split else t)
    return tiles[0] if len(tiles) == 1 else jnp.concatenate(tiles, axis=1)


def _softplus(x):
    return jnp.maximum(x, 0.0) + jnp.log(1.0 + jnp.exp(-jnp.abs(x)))


def _gelu(x):
    c = 2.0 * math.sqrt(2.0 / math.pi)
    return x * jax.nn.sigmoid(x * (c + (c * 0.044715) * (x * x)))


def _rms_modulate(x, g, shift, scale):
    ms = jnp.mean(x * x, axis=-1, keepdims=True)
    return x * lax.rsqrt(ms + RMS_EPS) * g * (1.0 + scale) + shift


def _seq_specs(n, l, t_chunk, widths_in, widths_out):
    nb = n // SEQ_BLOCK
    if l % t_chunk == 0:
        spec = lambda w: pl.BlockSpec((SEQ_BLOCK, t_chunk, w), lambda b, c: (b, c, 0))
        shape = lambda w: (n, l, w)
        nt = l // t_chunk
    else:
        assert l < t_chunk
        spec = lambda w: pl.BlockSpec((SEQ_BLOCK * l, w), lambda b, c: (b, 0))
        shape = lambda w: (n * l, w)
        nt = 1
    return ((nb, nt), [spec(w) for w in widths_in], [spec(w) for w in widths_out],
            [shape(w) for w in widths_in], [shape(w) for w in widths_out])


def _branch_dtype(shape):
    return BF16 if len(shape) == 3 else F32


def _seq_block_load(z_ref, pad_ref):
    if len(z_ref.shape) == 3:
        return z_ref[...]
    l = z_ref.shape[0] // SEQ_BLOCK
    pad_ref[...] = jnp.zeros_like(pad_ref)
    for n in range(SEQ_BLOCK):
        pad_ref[n, 0:l, :] = z_ref[n * l:(n + 1) * l, :]
    return pad_ref[...]


def _seq_block_store(y_ref, stage_ref, y3):
    if len(y_ref.shape) == 3:
        y_ref[...] = y3.astype(y_ref.dtype)
        return
    l = y_ref.shape[0] // SEQ_BLOCK
    stage_ref[...] = y3
    for n in range(SEQ_BLOCK):
        y_ref[n * l:(n + 1) * l, :] = stage_ref[n, 0:l, :]


def _heads_to_tiles(s_ref, tile_ref, n_pairs, transpose):
    zero = jnp.zeros((RW_HEAD, RW_HEAD), F32)
    for n in range(SEQ_BLOCK):
        for hp in range(n_pairs):
            a, b = s_ref[n, HEAD_PAIR * hp], s_ref[n, HEAD_PAIR * hp + 1]
            if transpose:
                a, b = a.T, b.T
            tile_ref[hp, n] = jnp.concatenate([jnp.concatenate([a, zero], axis=1),
                                               jnp.concatenate([zero, b], axis=1)], axis=0)


def _tiles_to_heads(tile_ref, s_ref, n_pairs, transpose):
    for n in range(SEQ_BLOCK):
        for hp in range(n_pairs):
            t = tile_ref[hp, n]
            a, b = t[0:RW_HEAD, 0:RW_HEAD], t[RW_HEAD:LANES, RW_HEAD:LANES]
            if transpose:
                a, b = a.T, b.T
            s_ref[n, HEAD_PAIR * hp] = a
            s_ref[n, HEAD_PAIR * hp + 1] = b


def _ada_kernel(c_ref, w_ref, b_ref, o_ref):
    c = c_ref[...]
    o_ref[...] = _dot(jax.nn.silu(c).astype(BF16), w_ref[...]) + b_ref[...]


def _ada_call(c, w_stack, layer, b):
    n, d = c.shape
    cols = w_stack.shape[2]
    tn = cols // 4
    return pl.pallas_call(
        _ada_kernel,
        out_shape=jax.ShapeDtypeStruct((n, cols), F32),
        grid=(cols // tn,),
        in_specs=[_const_spec((n, d)),
                  pl.BlockSpec((None, d, tn), lambda j: (layer, 0, j)),
                  pl.BlockSpec((1, tn), lambda j: (0, j))],
        out_specs=pl.BlockSpec((n, tn), lambda j: (0, j)),
        compiler_params=_cparams("arbitrary"),
        name="ada_mod",
    )(c, w_stack, b.reshape(1, cols))


def _mod_operand(m, l, tm):
    n, d = m.shape
    if l % tm == 0:
        per = l // tm
        return m.reshape(n, 1, d), pl.BlockSpec((1, 1, d), lambda i: (i // per, 0, 0))
    assert tm % l == 0
    return jnp.repeat(m, l, axis=0).reshape(1, n * l, d), pl.BlockSpec((1, tm, d), lambda i: (0, i, 0))


def _tok_layout(n, l, tm, w):
    if l % tm == 0:
        per = l // tm
        return (n, l, w), pl.BlockSpec((1, tm, w), lambda i: (i // per, i % per, 0))
    return (n * l, w), pl.BlockSpec((tm, w), lambda i: (i, 0))


def _tile(ref):
    return ref.at[0] if len(ref.shape) == 3 else ref


def _in_proj_kernel(x_ref, g_ref, sh_ref, sc_ref, wa_ref, wb_ref, wc_ref, wg_ref,
                    oa_ref, ob_ref, oc_ref, og_ref):
    h = _rms_modulate(_tile(x_ref)[...], g_ref[...], sh_ref[0], sc_ref[0]).astype(BF16)

    def project(w_ref, o_ref, fn=None):
        width = w_ref.shape[1]
        out = _tile(o_ref)
        for c0 in range(0, width, IN_PROJ_CHUNK):
            cols = slice(c0, min(c0 + IN_PROJ_CHUNK, width))
            z = _dot(h, w_ref[:, cols])
            out[:, cols] = (z if fn is None else fn(z)).astype(out.dtype)

    project(wa_ref, oa_ref)
    project(wb_ref, ob_ref)
    project(wc_ref, oc_ref)
    project(wg_ref, og_ref, jax.nn.sigmoid)


def _in_proj_call(x, n, l, g, shift, scale, w_a, w_b, w_c, w_g, tm):
    d = x.shape[-1]
    sh_arr, sh_spec = _mod_operand(shift, l, tm)
    sc_arr, sc_spec = _mod_operand(scale, l, tm)
    widths = [w.shape[1] for w in (w_a, w_b, w_c, w_g)]
    dtypes = [F32, F32, F32, BF16]
    outs = [_tok_layout(n, l, tm, w) for w in widths]
    return pl.pallas_call(
        _in_proj_kernel,
        out_shape=[jax.ShapeDtypeStruct(shape, dt) for (shape, _), dt in zip(outs, dtypes)],
        grid=(n * l // tm,),
        in_specs=[_tok_layout(n, l, tm, d)[1], _const_spec((1, d)), sh_spec, sc_spec]
        + [_const_spec(w.shape) for w in (w_a, w_b, w_c, w_g)],
        out_specs=[spec for _, spec in outs],
        compiler_params=_cparams("arbitrary"),
        name="in_proj",
    )(x, g.reshape(1, d), sh_arr, sc_arr, w_a, w_b, w_c, w_g)


def _s5_kernel(u_ref, s0_ref, b_ref, c_ref, a_ref, d_ref, wg_ref, bg_ref,
               y_ref, sl_ref, e_ref, st_ref, pad_ref, stage_ref, *, t_chunk, n_steps):
    @pl.when(pl.program_id(1) == 0)
    def _():
        st_ref[...] = s0_ref[...]

    rows = SEQ_BLOCK * t_chunk
    ut = jnp.swapaxes(_seq_block_load(u_ref, pad_ref), 0, 1).reshape(rows, S5_WIDTH)
    ub = ut.astype(BF16)
    for c0 in range(0, 2 * S5_FLAT, MXU_COLS):
        ch0 = (c0 % S5_FLAT) // S5_STATE * S5_GROUP
        k0 = ch0 // LANES * LANES
        assert ch0 + MXU_COLS // S5_STATE * S5_GROUP <= k0 + LANES
        e_ref[:, c0:c0 + MXU_COLS] = _dot(ub[:, k0:k0 + LANES], b_ref[k0:k0 + LANES, c0:c0 + MXU_COLS])

    ar = jnp.broadcast_to(a_ref[0:1, :], (SEQ_BLOCK, S5_FLAT))
    ai = jnp.broadcast_to(a_ref[1:2, :], (SEQ_BLOCK, S5_FLAT))

    def step(t, carry):
        sr, si = carry
        r = pl.ds(pl.multiple_of(t * SEQ_BLOCK, SEQ_BLOCK), SEQ_BLOCK)
        nr = ar * sr - ai * si + e_ref[r, 0:S5_FLAT]
        ni = ar * si + ai * sr + e_ref[r, S5_FLAT:2 * S5_FLAT]
        e_ref[r, 0:S5_FLAT] = nr
        e_ref[r, S5_FLAT:2 * S5_FLAT] = ni
        return nr, ni

    sr, si = lax.fori_loop(0, n_steps, step, (st_ref[:, 0:S5_FLAT], st_ref[:, S5_FLAT:2 * S5_FLAT]),
                           unroll=4 if n_steps % 4 == 0 else 1)
    st_ref[:, 0:S5_FLAT] = sr
    st_ref[:, S5_FLAT:2 * S5_FLAT] = si
    sl_ref[...] = st_ref[...]

    y = _dot(e_ref[...].astype(BF16), c_ref[...]) + d_ref[...] * ut
    y = _gelu(y)
    y = y * jax.nn.sigmoid(_dot(y.astype(BF16), wg_ref[...]) + bg_ref[...])
    _seq_block_store(y_ref, stage_ref, jnp.swapaxes(y.reshape(t_chunk, SEQ_BLOCK, S5_WIDTH), 0, 1))


def _io_scratch(l, t_chunk, width_in, width_out):
    if l % t_chunk == 0:
        return [pltpu.VMEM((SUBLANES, LANES), F32)] * 2
    return [pltpu.VMEM((SEQ_BLOCK, t_chunk, width_in), F32), pltpu.VMEM((SEQ_BLOCK, t_chunk, width_out), F32)]


def _s5_call(u2, n, l, s0, prm, t_chunk):
    grid, in_specs, out_specs, in_shapes, out_shapes = _seq_specs(n, l, t_chunk, [S5_WIDTH], [S5_WIDTH])
    kern = functools.partial(_s5_kernel, t_chunk=t_chunk, n_steps=min(t_chunk, l))
    y, s_new = pl.pallas_call(
        kern,
        out_shape=[jax.ShapeDtypeStruct(out_shapes[0], _branch_dtype(out_shapes[0])),
                   jax.ShapeDtypeStruct((n, 2 * S5_FLAT), F32)],
        grid=grid,
        in_specs=in_specs + [
            pl.BlockSpec((SEQ_BLOCK, 2 * S5_FLAT), lambda b, c: (b, 0)),
            _const_spec((S5_WIDTH, 2 * S5_FLAT)), _const_spec((2 * S5_FLAT, S5_WIDTH)),
            _const_spec((2, S5_FLAT)), _const_spec((1, S5_WIDTH)),
            _const_spec((S5_WIDTH, S5_WIDTH)), _const_spec((1, S5_WIDTH))],
        out_specs=out_specs + [pl.BlockSpec((SEQ_BLOCK, 2 * S5_FLAT), lambda b, c: (b, 0))],
        scratch_shapes=[pltpu.VMEM((SEQ_BLOCK * t_chunk, 2 * S5_FLAT), F32),
                        pltpu.VMEM((SEQ_BLOCK, 2 * S5_FLAT), F32)]
        + _io_scratch(l, t_chunk, S5_WIDTH, S5_WIDTH),
        compiler_params=_cparams("arbitrary", "arbitrary"),
        name="s5_mixer",
    )(u2.reshape(in_shapes[0]), s0, prm["s5_b"], prm["s5_c"], prm["s5_a"], prm["s5_d"], prm["s5_w_glu"],
      prm["s5_b_glu"])
    return y, s_new


def _mm(a, b, dims):
    return lax.dot_general(a.astype(BF16), b.astype(BF16), dims, preferred_element_type=F32)


def _cumsum_groups(x, group):
    pos = lax.broadcasted_iota(jnp.int32, (x.shape[0], 1), 0) % group
    s = 1
    while s < group:
        x = x + jnp.where(pos >= s, pltpu.roll(x, s, 0), 0.0)
        s *= 2
    return x


def _rw_kernel(p_ref, sh0_ref, s0_ref, mu_ref, vec_ref, w2_ref, a2_ref, g2_ref, *rest,
               t_chunk, l_valid, last_row, layer):
    earlier_ref = rest[0] if layer else None
    (y_ref, shl_ref, sl_ref,
     pbuf_ref, s_ref, at_ref, bt_ref, kt_ref, rt_ref, bh_ref, kh_ref, v_ref, gc_ref, ys_ref,
     pad_ref, stage_ref) = rest[1 if layer else 0:]
    rows = SEQ_BLOCK * t_chunk
    sub = min(MAX_SUB, t_chunk)
    n_sub = t_chunk // sub

    @pl.when(pl.program_id(1) == 0)
    def _():
        pbuf_ref[...] = sh0_ref[...]
        _heads_to_tiles(s0_ref, s_ref, RW_PAIRS, transpose=False)

    p3 = _seq_block_load(p_ref, pad_ref)
    first = lax.broadcasted_iota(jnp.int32, (1, t_chunk, 1), 1) == 0
    prev3 = jnp.where(first, pbuf_ref[...], pltpu.roll(p3, 1, 1))
    last = p3[:, last_row:last_row + 1, :]
    pbuf_ref[...] = last
    shl_ref[...] = last

    xm = (p3 + (prev3 - p3) * mu_ref[...]).reshape(rows, RW_COLS)
    r = xm[:, 0:RW_WIDTH]
    k = xm[:, RW_WIDTH:2 * RW_WIDTH]
    v = xm[:, 2 * RW_WIDTH:3 * RW_WIDTH]
    lora = xm[:, 3 * RW_WIDTH:RW_COLS]
    w0, a0, k_k, k_a = vec_ref[0:1, :], vec_ref[1:2, :], vec_ref[2:3, :], vec_ref[3:4, :]
    r_k, ln_w, ln_b = vec_ref[4:5, :], vec_ref[5:6, :], vec_ref[6:7, :]

    w = -_softplus(-(w0 + _dot(jnp.tanh(lora).astype(BF16), w2_ref[...]))) - 0.5
    log_decay = -jnp.exp(w)
    a = jax.nn.sigmoid(a0 + _dot(lora.astype(BF16), a2_ref[...]))
    g = _dot(jax.nn.sigmoid(lora).astype(BF16), g2_ref[...])

    ones = _head_ones()
    kk = k * k_k
    kk = kk * lax.rsqrt(jnp.maximum(_head_sum(kk * kk, ones), 1e-24))
    kt = k * (1.0 + (a - 1.0) * k_a)
    kka = kk * a
    if l_valid < t_chunk:
        live = lax.broadcasted_iota(jnp.int32, (rows, 1), 0) % t_chunk < l_valid
        log_decay = jnp.where(live, log_decay, 0.0)
        kk = jnp.where(live, kk, 0.0)
        kka = jnp.where(live, kka, 0.0)
        kt_live = jnp.where(live, kt, 0.0)
    else:
        kt_live = kt

    cum = _cumsum_groups(log_decay, sub)
    shape_g = (rows // sub, sub, RW_WIDTH)
    cum_end = jnp.broadcast_to(cum.reshape(shape_g)[:, sub - 1:sub, :], shape_g).reshape(rows, RW_WIDTH)
    g_in = jnp.exp(cum)
    g_inv = jnp.exp(-cum)
    g_ex = jnp.exp(cum - log_decay)
    g_out = jnp.exp(cum_end - cum)
    shape3 = (SEQ_BLOCK, t_chunk, RW_WIDTH)
    at_ref[...] = (-kk * g_ex).reshape(shape3)
    bt_ref[...] = (kka * g_inv).reshape(shape3)
    kt_ref[...] = (kt_live * g_inv).reshape(shape3)
    rt_ref[...] = (r * g_in).reshape(shape3)
    bh_ref[...] = (kka * g_out).reshape(shape3)
    kh_ref[...] = (kt_live * g_out).reshape(shape3)
    v_ref[...] = v.reshape(shape3)
    gc_ref[...] = jnp.exp(cum_end).reshape(shape3)

    m_rows = SEQ_BLOCK * sub
    ri = lax.broadcasted_iota(jnp.int32, (m_rows, m_rows), 0)
    ci = lax.broadcasted_iota(jnp.int32, (m_rows, m_rows), 1)
    same_seq = (ri // sub) == (ci // sub)
    before = same_seq & ((ci % sub) < (ri % sub))
    upto = same_seq & ((ci % sub) <= (ri % sub))
    eye = (ri == ci).astype(F32)
    pair_mask = (ri // 2) == (ci // 2)
    level_masks = []
    s = 2
    while s < sub:
        level_masks.append(((ri // (2 * s)) == (ci // (2 * s))) & ((ri // s) != (ci // s)))
        s *= 2
    lane = lax.broadcasted_iota(jnp.int32, (m_rows, LANES), 1)
    head_mask = [lane < RW_HEAD, lane >= RW_HEAD]
    wide = (m_rows, SEQ_BLOCK * LANES)
    own_rows = (lax.broadcasted_iota(jnp.int32, wide, 1) // LANES
                == lax.broadcasted_iota(jnp.int32, wide, 0) // sub)
    tiles = (LANES, SEQ_BLOCK * LANES)
    same_head = (lax.broadcasted_iota(jnp.int32, tiles, 0) // RW_HEAD
                 == (lax.broadcasted_iota(jnp.int32, tiles, 1) % LANES) // RW_HEAD)
    group = max(g for g in (1, 2, 4) if n_sub % g == 0)
    units = [(j, hp) for j in range(group) for hp in range(RW_PAIRS)]
    heads = [(u, h) for u in range(len(units)) for h in range(HEAD_PAIR)]

    def sub_chunk(c, carry):
        rs = [pl.ds(pl.multiple_of((c * group + j) * sub, sub), sub) for j in range(group)]

        def ld(ref, j, hp):
            return ref[:, rs[j], pl.ds(hp * LANES, LANES)].reshape(m_rows, LANES)

        at, bt, ktl, rt, bh, kh, vv, gcv = ([ld(ref, j, hp) for j, hp in units]
                                            for ref in (at_ref, bt_ref, kt_ref, rt_ref, bh_ref, kh_ref,
                                                        v_ref, gc_ref))
        state = [[s_ref[hp, n] for n in range(SEQ_BLOCK)] for hp in range(RW_PAIRS)]
        zero = jnp.zeros((m_rows, LANES), F32)
        msk = lambda x, h: jnp.where(head_mask[h], x, zero)

        gram = [_mm(jnp.concatenate([msk(at[hp], 0), msk(rt[hp], 0), msk(at[hp], 1), msk(rt[hp], 1)], axis=0),
                    jnp.concatenate([bt[hp], ktl[hp]], axis=0), _NT)
                for hp in range(len(units))]

        def quad(hp, h, row, col, keep):
            blk = gram[hp][(2 * h + row) * m_rows:(2 * h + row + 1) * m_rows, col * m_rows:(col + 1) * m_rows]
            return jnp.where(keep, blk, 0.0)

        m_ab = [quad(hp, h, 0, 0, before) for hp, h in heads]
        m_ak = [quad(hp, h, 0, 1, before) for hp, h in heads]
        n_rb = [quad(hp, h, 1, 0, upto) for hp, h in heads]
        n_rk = [quad(hp, h, 1, 1, upto) for hp, h in heads]
        tinv = [eye + jnp.where(pair_mask, m, 0.0) for m in m_ab]
        for lm in level_masks:
            prod = [_mm(jnp.where(lm, m, 0.0), t, _NN) for m, t in zip(m_ab, tinv)]
            tinv = [t + _mm(t, q, _NN) for t, q in zip(tinv, prod)]
        vh = [msk(vv[hp], h) for hp, h in heads]
        w1 = [_mm(m, x, _NN) for m, x in zip(m_ak, vh)]
        ap = [_mm(t, jnp.concatenate([msk(at[hp], h), w], axis=1), _NN)
              for t, (hp, h), w in zip(tinv, heads, w1)]
        nb = [_mm(m, x, _NN) for m, x in zip(n_rb, ap)]
        nv = [_mm(m, x, _NN) for m, x in zip(n_rk, vh)]

        def per_seq(x):
            return jnp.where(own_rows, jnp.concatenate([x] * SEQ_BLOCK, axis=1), 0.0)

        pairs = range(len(units))
        both = lambda xs, hp, cols: xs[HEAD_PAIR * hp][:, cols] + xs[HEAD_PAIR * hp + 1][:, cols]
        lo, hi = slice(0, LANES), slice(LANES, 2 * LANES)
        bh_x = [per_seq(bh[hp]) for hp in pairs]
        kh_x = [per_seq(kh[hp]) for hp in pairs]
        gam = [jnp.where(same_head, _mm(both(ap, hp, lo).T, bh_x[hp], _NN), 0.0) for hp in pairs]
        u = [jnp.where(same_head, _mm(jnp.concatenate([both(ap, hp, hi).T, vv[hp].T], axis=1),
                                      jnp.concatenate([bh_x[hp], kh_x[hp]], axis=0), _NN), 0.0)
             for hp in pairs]
        for hp, (j, pair) in enumerate(units):
            r_hat = rt[hp] + both(nb, hp, lo)
            y_zero = both(nb, hp, hi) + nv[HEAD_PAIR * hp] + nv[HEAD_PAIR * hp + 1]
            cols = pl.ds(pair * LANES, LANES)
            for n in range(SEQ_BLOCK):
                q = slice(n * sub, (n + 1) * sub)
                blk = slice(n * LANES, (n + 1) * LANES)
                st = state[pair][n]
                ys_ref[n, rs[j], cols] = _mm(r_hat[q], st, _NT) + y_zero[q]
                state[pair][n] = (st * gcv[hp][n * sub:n * sub + 1, :]
                                  + _mm(st, gam[hp][:, blk], _NN) + u[hp][:, blk])
        for pair in range(RW_PAIRS):
            for n in range(SEQ_BLOCK):
                s_ref[pair, n] = state[pair][n]
        return carry

    lax.fori_loop(0, n_sub // group, sub_chunk, 0)

    @pl.when(pl.program_id(1) == pl.num_programs(1) - 1)
    def _():
        if layer:
            sl_ref[0:layer] = earlier_ref[...]
        _tiles_to_heads(s_ref, sl_ref.at[layer], RW_PAIRS, transpose=False)

    y = ys_ref[...].reshape(rows, RW_WIDTH)
    inv = 1.0 / RW_HEAD
    mean = _head_sum(y, ones) * inv
    yc = y - mean
    var = _head_sum(yc * yc, ones) * inv
    y = yc * lax.rsqrt(var + RW_GN_EPS) * ln_w + ln_b
    bonus = _head_sum(r * kt * r_k, ones, split=False) * v
    _seq_block_store(y_ref, stage_ref, ((y + bonus) * g).reshape(shape3))


def _layer_state_specs(state, layer):
    tail = state.shape[1:]
    zeros = (0,) * len(tail)
    read = pl.BlockSpec((SEQ_BLOCK,) + tail, lambda b, c: (b,) + zeros)
    upto = lambda k: pl.BlockSpec((k, SEQ_BLOCK) + tail, lambda b, c: (0, b) + zeros)
    return read, upto(layer + 1), ([upto(layer)] if layer else [])


def _rw_call(p2, n, l, shift0, state, layer, earlier, prm, t_chunk):
    assert t_chunk % min(MAX_SUB, t_chunk) == 0
    grid, in_specs, out_specs, in_shapes, out_shapes = _seq_specs(n, l, t_chunk, [RW_COLS], [RW_WIDTH])
    valid_in_chunk = min(t_chunk, l)
    kern = functools.partial(_rw_kernel, t_chunk=t_chunk, l_valid=valid_in_chunk, last_row=valid_in_chunk - 1,
                             layer=layer)
    st_in, st_out, st_earlier = _layer_state_specs(state, layer)
    sh_spec = pl.BlockSpec((SEQ_BLOCK, 1, RW_COLS), lambda b, c: (b, 0, 0))
    q_scr = pltpu.VMEM((SEQ_BLOCK, t_chunk, RW_WIDTH), F32)
    return pl.pallas_call(
        kern,
        out_shape=[jax.ShapeDtypeStruct(out_shapes[0], _branch_dtype(out_shapes[0])),
                   jax.ShapeDtypeStruct((n, 1, RW_COLS), F32),
                   jax.ShapeDtypeStruct((layer + 1,) + state.shape, F32)],
        grid=grid,
        in_specs=in_specs + [sh_spec, st_in,
                             _const_spec((1, RW_COLS)), _const_spec((SUBLANES, RW_WIDTH)),
                             _const_spec((RW_LORA, RW_WIDTH)), _const_spec((RW_LORA, RW_WIDTH)),
                             _const_spec((RW_LORA, RW_WIDTH))] + st_earlier,
        out_specs=out_specs + [sh_spec, st_out],
        scratch_shapes=[pltpu.VMEM((SEQ_BLOCK, 1, RW_COLS), F32),
                        pltpu.VMEM((RW_PAIRS, SEQ_BLOCK, LANES, LANES), F32)] + [q_scr] * 9
        + _io_scratch(l, t_chunk, RW_COLS, RW_WIDTH),
        compiler_params=_cparams("arbitrary", "arbitrary"),
        name="rwkv7_mixer",
    )(p2.reshape(in_shapes[0]), shift0, state, prm["rw_mu"], prm["rw_vec"], prm["rw_w2"], prm["rw_a2"],
      prm["rw_g2"], *([earlier] if layer else []))


def _hg_kernel(z_ref, s0_ref, lower_ref, ng_ref, *rest, t_chunk, l_valid, layer):
    earlier_ref = rest[0] if layer else None
    (y_ref, sl_ref, s_ref, qs_ref, q_ref, k_ref, kh_ref, v_ref, b_ref, gc_ref, ys_ref, pad_ref,
     stage_ref) = rest[1 if layer else 0:]
    rows = SEQ_BLOCK * t_chunk
    sub = min(MAX_SUB, t_chunk)
    n_sub = t_chunk // sub

    @pl.when(pl.program_id(1) == 0)
    def _():
        _heads_to_tiles(s0_ref, s_ref, HG_PAIRS, transpose=True)

    z = _seq_block_load(z_ref, pad_ref).reshape(rows, 4 * HG_WIDTH)
    q = jax.nn.silu(z[:, 0:HG_WIDTH])
    f = z[:, HG_WIDTH:2 * HG_WIDTH]
    i = z[:, 2 * HG_WIDTH:3 * HG_WIDTH]
    og = z[:, 3 * HG_WIDTH:4 * HG_WIDTH]
    lower = lower_ref[...]
    fgate = lower + (1.0 - lower) * jax.nn.sigmoid(f)
    log_f = jnp.log(jnp.maximum(fgate, HG_GATE_FLOOR))
    k = 1.0 - fgate
    if l_valid < t_chunk:
        live = lax.broadcasted_iota(jnp.int32, (rows, 1), 0) % t_chunk < l_valid
        log_f = jnp.where(live, log_f, 0.0)
        k = jnp.where(live, k, 0.0)
    cum = _cumsum_groups(log_f, sub)
    shape_g = (rows // sub, sub, HG_WIDTH)
    cum_end = jnp.broadcast_to(cum.reshape(shape_g)[:, sub - 1:sub, :], shape_g).reshape(rows, HG_WIDTH)
    shape3 = (SEQ_BLOCK, t_chunk, HG_WIDTH)
    qs_ref[...] = (q * jnp.exp(cum)).reshape(shape3)
    q_ref[...] = q.reshape(shape3)
    k_ref[...] = k.reshape(shape3)
    kh_ref[...] = (k * jnp.exp(cum_end - cum)).reshape(shape3)
    v_ref[...] = i.reshape(shape3)
    b_ref[...] = cum.reshape(shape3)
    gc_ref[...] = jnp.exp(cum_end).reshape(shape3)

    m_rows = SEQ_BLOCK * sub
    ones = _head_ones()
    wide = (m_rows, SEQ_BLOCK * LANES)
    own_rows = (lax.broadcasted_iota(jnp.int32, wide, 1) // LANES
                == lax.broadcasted_iota(jnp.int32, wide, 0) // sub)
    tiles = (LANES, SEQ_BLOCK * LANES)
    same_head = (lax.broadcasted_iota(jnp.int32, tiles, 0) // HG_HEAD
                 == (lax.broadcasted_iota(jnp.int32, tiles, 1) % LANES) // HG_HEAD)
    step = lax.broadcasted_iota(jnp.int32, (sub, LANES), 0)

    def sub_chunk(c, carry):
        rs = pl.ds(pl.multiple_of(c * sub, sub), sub)
        for hp in range(HG_PAIRS):
            cols = pl.ds(hp * LANES, LANES)
            ld = lambda ref: ref[:, rs, cols].reshape(m_rows, LANES)
            qs, qq, kk, kh, vv, bb, gcv = (ld(qs_ref), ld(q_ref), ld(k_ref), ld(kh_ref), ld(v_ref),
                                           ld(b_ref), ld(gc_ref))
            state = [s_ref[hp, n] for n in range(SEQ_BLOCK)]
            kh_x = jnp.where(own_rows, jnp.concatenate([kh] * SEQ_BLOCK, axis=1), 0.0)
            u = jnp.where(same_head, _mm(vv.T, kh_x, _NN), 0.0)
            for n in range(SEQ_BLOCK):
                sl = slice(n * sub, (n + 1) * sub)
                bn, qn, kn, vn = bb[sl], qq[sl], kk[sl], vv[sl]
                prods = []
                for t in range(sub):
                    keep = step <= t
                    prods.append(jnp.where(keep, qn[t:t + 1, :] * kn * jnp.exp(bn[t:t + 1, :] - bn), 0.0))
                att = _dot(jnp.concatenate(prods, axis=0).astype(BF16), ones)
                o_rows = [jnp.sum(att[t * sub:(t + 1) * sub] * vn, axis=0, keepdims=True)
                          for t in range(sub)]
                ys_ref[n, rs, cols] = _mm(qs[sl], state[n], _NT) + jnp.concatenate(o_rows, axis=0)
                blk = slice(n * LANES, (n + 1) * LANES)
                s_ref[hp, n] = state[n] * gcv[n * sub:n * sub + 1, :] + u[:, blk]
        return carry

    lax.fori_loop(0, n_sub, sub_chunk, 0)

    @pl.when(pl.program_id(1) == pl.num_programs(1) - 1)
    def _():
        if layer:
            sl_ref[0:layer] = earlier_ref[...]
        _tiles_to_heads(s_ref, sl_ref.at[layer], HG_PAIRS, transpose=True)

    o = ys_ref[...].reshape(rows, HG_WIDTH)
    ms = _head_sum(o * o, ones) * (1.0 / HG_HEAD)
    o = o * lax.rsqrt(ms + RMS_EPS) * ng_ref[...] * jax.nn.sigmoid(og)
    _seq_block_store(y_ref, stage_ref, o.reshape(shape3))


def _hg_call(z2, n, l, state, layer, earlier, lower, norm_g, t_chunk):
    assert t_chunk % min(MAX_SUB, t_chunk) == 0
    grid, in_specs, out_specs, in_shapes, out_shapes = _seq_specs(n, l, t_chunk, [4 * HG_WIDTH], [HG_WIDTH])
    kern = functools.partial(_hg_kernel, t_chunk=t_chunk, l_valid=min(t_chunk, l), layer=layer)
    st_in, st_out, st_earlier = _layer_state_specs(state, layer)
    q_scr = pltpu.VMEM((SEQ_BLOCK, t_chunk, HG_WIDTH), F32)
    return pl.pallas_call(
        kern,
        out_shape=[jax.ShapeDtypeStruct(out_shapes[0], _branch_dtype(out_shapes[0])),
                   jax.ShapeDtypeStruct((layer + 1,) + state.shape, F32)],
        grid=grid,
        in_specs=in_specs + [st_in, _const_spec((1, HG_WIDTH)), _const_spec((1, HG_WIDTH))] + st_earlier,
        out_specs=out_specs + [st_out],
        scratch_shapes=[pltpu.VMEM((HG_PAIRS, SEQ_BLOCK, LANES, LANES), F32)] + [q_scr] * 8
        + _io_scratch(l, t_chunk, 4 * HG_WIDTH, HG_WIDTH),
        compiler_params=_cparams("arbitrary", "arbitrary"),
        name="hgrn2_mixer",
    )(z2.reshape(in_shapes[0]), state, lower.reshape(1, HG_WIDTH), norm_g.reshape(1, HG_WIDTH),
      *([earlier] if layer else []))


def _merge_ffn_kernel(*refs, l, tm, has_state, final_norm):
    (x_ref, ya_ref, yb_ref, yc_ref, gates_ref, gt1_ref, la_ref, lb_ref, lc_ref, wo_ref) = refs[:10]
    refs = refs[10:]
    if has_state:
        (g_ref, sh_ref, sc_ref, gt_ref, wup_ref, cw_ref, cb_ref, wdn_ref, fg_ref, st_ref,
         o_ref, tail_ref, h_ref, act_ref, hist_ref, buf_ref, old_ref) = refs
    else:
        (g_ref, sh_ref, sc_ref, gt_ref, wup_ref, cw_ref, cb_ref, wdn_ref, fg_ref,
         o_ref, tail_ref, h_ref, act_ref, hist_ref) = refs
    i = pl.program_id(0)
    n_chunks = D_FF // FF_CHUNK
    n_hist = CONV_W - 1

    if has_state:
        t_idx = lax.broadcasted_iota(jnp.int32, (tm, 1), 0) % l
        after = [t_idx >= j + 1 for j in range(n_hist)]
        old_ref[...] = jnp.zeros_like(old_ref)
        buf_ref[:, 0:SUBLANES, :] = jnp.zeros((2, SUBLANES, FF_CHUNK), F32)
        for s in range(tm // l):
            for j in range(n_hist):
                old_ref[j, s * l:s * l + j + 1, :] = st_ref[s, n_hist - 1 - j:n_hist, :]
    else:
        @pl.when(i % (l // tm) == 0)
        def _():
            hist_ref[...] = jnp.zeros_like(hist_ref)

    d = D_MODEL
    gates = _tile(gates_ref)
    lift = lambda y_ref, w_ref: _dot(_tile(y_ref)[...].astype(BF16), w_ref[...])
    merged = (gates[:, 0:d].astype(F32) * lift(ya_ref, la_ref)
              + gates[:, d:2 * d].astype(F32) * lift(yb_ref, lb_ref)
              + gates[:, 2 * d:3 * d].astype(F32) * lift(yc_ref, lc_ref))
    x = _tile(x_ref)[...] + gt1_ref[0] * _dot(merged.astype(BF16), wo_ref[...])
    h = _rms_modulate(x, g_ref[...], sh_ref[0], sc_ref[0])
    groups = tm // SUBLANES
    if not has_state:
        h = jnp.swapaxes(h.reshape(SUBLANES, groups, D_MODEL), 0, 1).reshape(tm, D_MODEL)
        row8 = lax.broadcasted_iota(jnp.int32, (SUBLANES, FF_CHUNK), 0)
    h_ref[...] = h.astype(BF16)

    def conv_half(c, half):
        col0 = half * D_FF + c * FF_CHUNK
        cols = slice(col0, col0 + FF_CHUNK)
        up = _dot(h_ref[...], wup_ref[:, cols])
        if has_state:
            buf_ref[half, SUBLANES:SUBLANES + tm, :] = up
            prev = [jnp.where(after[j], buf_ref[half, SUBLANES - 1 - j:SUBLANES - 1 - j + tm, :], 0.0)
                    + old_ref[j, :, cols] for j in range(n_hist)]
            for s in range(tm // l):
                r0 = SUBLANES + (s + 1) * l - n_hist
                tail_ref[s, :, cols] = buf_ref[half, r0:r0 + n_hist, :]
        else:
            late = [up[tm - (j + 1) * SUBLANES:tm - j * SUBLANES, :] for j in range(n_hist)]
            first = [jnp.where(row8 == 0, pltpu.roll(hist_ref[c, half, j], 1, 0), pltpu.roll(late[j], 1, 0))
                     for j in range(n_hist)]
            for j in range(n_hist):
                hist_ref[c, half, j] = late[j]
            prev = [jnp.concatenate([first[0], up[0:tm - SUBLANES, :]], axis=0),
                    jnp.concatenate([first[1], first[0], up[0:tm - 2 * SUBLANES, :]], axis=0)]
            tail_ref[0, 0:1, cols] = up[tm - SUBLANES - 1:tm - SUBLANES, :]
            tail_ref[0, 1:2, cols] = up[tm - 1:tm, :]
        return (cb_ref[:, cols] + cw_ref[0:1, cols] * prev[1] + cw_ref[1:2, cols] * prev[0]
                + cw_ref[2:3, cols] * up)

    for c in range(n_chunks):
        act = _gelu(conv_half(c, 0)) * conv_half(c, 1)
        act_ref[:, c * FF_CHUNK:(c + 1) * FF_CHUNK] = act.astype(BF16)
    acc = _dot(act_ref[...], wdn_ref[...])
    if not has_state:
        acc = jnp.swapaxes(acc.reshape(groups, SUBLANES, D_MODEL), 0, 1).reshape(tm, D_MODEL)
    out = x + gt_ref[0] * acc
    if final_norm:
        ms = jnp.mean(out * out, axis=-1, keepdims=True)
        out = out * lax.rsqrt(ms + RMS_EPS) * fg_ref[...]
    _tile(o_ref)[...] = out


def _merge_ffn_call(x, n, l, ya, yb, yc, gates, gate1, g, shift, scale, gate2, prm, final_g, conv_state, tm,
                    final_norm):
    assert CONV_W == 3
    d = x.shape[-1]
    has_state = conv_state is not None
    g1_arr, g1_spec = _mod_operand(gate1, l, tm)
    sh_arr, sh_spec = _mod_operand(shift, l, tm)
    sc_arr, sc_spec = _mod_operand(scale, l, tm)
    gt_arr, gt_spec = _mod_operand(gate2, l, tm)
    shape, x_spec = _tok_layout(n, l, tm, d)
    row_spec = lambda w: _tok_layout(n, l, tm, w)[1]
    stacks, layer = prm["stacks"], prm["layer"]
    w_spec = lambda name: _layer_weight_spec(stacks[name], layer)
    in_specs = [x_spec, row_spec(S5_WIDTH), row_spec(RW_WIDTH), row_spec(HG_WIDTH), row_spec(N_BRANCH * d),
                g1_spec, w_spec("w_lift_a"), w_spec("w_lift_b"), w_spec("w_lift_c"), w_spec("w_out"),
                _const_spec((1, d)), sh_spec, sc_spec, gt_spec,
                w_spec("w_up"), _const_spec((CONV_W, 2 * D_FF)), _const_spec((1, 2 * D_FF)),
                w_spec("w_down"), _const_spec((1, d))]
    operands = [x, ya, yb, yc, gates, g1_arr, stacks["w_lift_a"], stacks["w_lift_b"], stacks["w_lift_c"],
                stacks["w_out"], g.reshape(1, d), sh_arr, sc_arr, gt_arr, stacks["w_up"], prm["conv_w"],
                prm["conv_b"], stacks["w_down"], final_g.reshape(1, d)]
    scratch = [pltpu.VMEM((tm, d), BF16), pltpu.VMEM((tm, D_FF), BF16),
               pltpu.VMEM((D_FF // FF_CHUNK, 2, CONV_W - 1, SUBLANES, FF_CHUNK), F32)]
    if has_state:
        assert tm % l == 0 and l >= CONV_W - 1
        block = (tm // l, CONV_W - 1, 2 * D_FF)
        in_specs.append(pl.BlockSpec((None,) + block, lambda i: (layer, i, 0, 0)))
        operands.append(conv_state)
        tail_spec = pl.BlockSpec(block, lambda i: (i, 0, 0))
        scratch += [pltpu.VMEM((2, tm + SUBLANES, FF_CHUNK), F32), pltpu.VMEM((CONV_W - 1, tm, 2 * D_FF), F32)]
    else:
        assert l % tm == 0
        per = l // tm
        tail_spec = pl.BlockSpec((1, CONV_W - 1, 2 * D_FF), lambda i: (i // per, 0, 0))
    kern = functools.partial(_merge_ffn_kernel, l=l, tm=tm, has_state=has_state, final_norm=final_norm)
    return pl.pallas_call(
        kern,
        out_shape=[jax.ShapeDtypeStruct(shape, F32), jax.ShapeDtypeStruct((n, CONV_W - 1, 2 * D_FF), F32)],
        grid=(n * l // tm,),
        in_specs=in_specs,
        out_specs=[x_spec, tail_spec],
        scratch_shapes=scratch,
        compiler_params=_cparams("arbitrary"),
        name="merge_conv_ffn",
    )(*operands)


def _prepare_layer(p):
    out = {}
    w_in = p["w_in"]
    c1 = S5_WIDTH
    c2 = c1 + RW_COLS
    c3 = c2 + 4 * HG_WIDTH
    out["w_in_a"] = w_in[:, :c1].astype(BF16)
    out["w_in_b"] = w_in[:, c1:c2].astype(BF16)
    out["w_in_c"] = w_in[:, c2:c3].astype(BF16)
    out["w_in_g"] = w_in[:, c3:].astype(BF16)
    out["conv_w"] = p["conv_w"]
    out["conv_b"] = p["conv_b"].reshape(1, 2 * D_FF)

    lr = p["s5_lambda_re"]
    li = p["s5_lambda_im"]
    dt = jnp.exp(p["s5_log_dt"])[:, None]
    mag = jnp.exp(lr * dt)
    ar = mag * jnp.cos(li * dt)
    ai = mag * jnp.sin(li * dt)
    den = lr * lr + li * li
    zr = ((ar - 1.0) * lr + ai * li) / den
    zi = (ai * lr - (ar - 1.0) * li) / den
    bbr = zr[..., None] * p["s5_b_re"] - zi[..., None] * p["s5_b_im"]
    bbi = zr[..., None] * p["s5_b_im"] + zi[..., None] * p["s5_b_re"]
    eye = jnp.eye(S5_GROUPS, dtype=F32)
    bmat = jnp.einsum("gh,rgpc->gcrhp", eye, jnp.stack([bbr, bbi])).reshape(S5_WIDTH, 2 * S5_FLAT)
    cmat = jnp.einsum("hg,rgcp->rhpgc", eye, jnp.stack([p["s5_c_re"], -p["s5_c_im"]])).reshape(
        2 * S5_FLAT, S5_WIDTH)
    out["s5_b"] = bmat.astype(BF16)
    out["s5_c"] = cmat.astype(BF16)
    out["s5_a"] = jnp.stack([ar.reshape(S5_FLAT), ai.reshape(S5_FLAT)])
    out["s5_d"] = p["s5_d"].reshape(1, S5_WIDTH)
    out["s5_w_glu"] = p["s5_w_glu"].astype(BF16)
    out["s5_b_glu"] = p["s5_b_glu"].reshape(1, S5_WIDTH)

    out["rw_mu"] = p["rw_mu"].reshape(1, RW_COLS)
    out["rw_vec"] = jnp.stack([p["rw_w0"], p["rw_a0"], p["rw_k_k"], p["rw_k_a"],
                               p["rw_r_k"].reshape(RW_WIDTH), p["rw_ln_w"], p["rw_ln_b"],
                               jnp.zeros((RW_WIDTH,), F32)])
    zw = jnp.zeros((RW_LORA, RW_WIDTH), F32)
    out["rw_w2"] = zw.at[0:RW_DECAY_LORA].set(p["rw_w2"]).astype(BF16)
    out["rw_a2"] = zw.at[RW_DECAY_LORA:RW_DECAY_LORA + RW_AAA_LORA].set(p["rw_a2"]).astype(BF16)
    out["rw_g2"] = zw.at[RW_DECAY_LORA + RW_AAA_LORA:].set(p["rw_g2"]).astype(BF16)
    out["hg_norm"] = p["hg_norm"]
    out["g_mix"] = p["g_mix"]
    out["g_ffn"] = p["g_ffn"]
    out["b_ada"] = p["b_ada"]
    return out


def _chunk_for(l, sub, t_chunk):
    return t_chunk if l % t_chunk == 0 else -(-l // sub) * sub


def _run_trunk(x, mods, st_s5, st_shift, st_rw, st_hg, st_conv, lower, final_g, prms, t_chunk):
    n, l, d = x.shape
    rows = n * l
    tm = min(ROW_TILE, rows)
    depth = len(prms)
    assert (l % tm == 0) == (l % t_chunk == 0)
    h = x.reshape(_tok_layout(n, l, tm, d)[0])
    out_s5, out_shift, out_conv = [], [], []
    rw_new = hg_new = None
    for layer in range(depth):
        prm = prms[layer]
        sh1, sc1, gt1, sh2, sc2, gt2 = jnp.split(mods[layer], 6, axis=-1)
        za, zb, zc, gates = _in_proj_call(h, n, l, prm["g_mix"], sh1, sc1, prm["w_in_a"], prm["w_in_b"],
                                          prm["w_in_c"], prm["w_in_g"], tm)

        s5_in = jnp.concatenate([st_s5[layer][..., 0].reshape(n, S5_FLAT),
                                 st_s5[layer][..., 1].reshape(n, S5_FLAT)], axis=1)
        ya, s5_new = _s5_call(za, n, l, s5_in, prm, _chunk_for(l, SUBLANES, t_chunk))
        yb, shift_new, rw_new = _rw_call(zb, n, l, st_shift[layer].reshape(n, 1, RW_COLS), st_rw[layer], layer, rw_new,
                                         prm, _chunk_for(l, SUBLANES, t_chunk))
        yc, hg_new = _hg_call(zc, n, l, st_hg[layer], layer, hg_new, lower[layer], prm["hg_norm"],
                              _chunk_for(l, SUBLANES, t_chunk))

        ffn_tm = tm if st_conv is None else min(tm, 128)
        h, conv_new = _merge_ffn_call(h, n, l, ya, yb, yc, gates, gt1, prm["g_ffn"], sh2, sc2, gt2, prm, final_g,
                                      st_conv, ffn_tm,
                                      final_norm=(layer == depth - 1))
        out_s5.append(jnp.stack([s5_new[:, :S5_FLAT].reshape(n, S5_GROUPS, S5_STATE),
                                 s5_new[:, S5_FLAT:].reshape(n, S5_GROUPS, S5_STATE)], axis=-1))
        out_shift.append(shift_new.reshape(n, RW_COLS))
        out_conv.append(conv_new)
    y = h.reshape(n, l, d).astype(x.dtype)
    return y, (jnp.stack(out_s5), jnp.stack(out_shift), rw_new, hg_new, jnp.stack(out_conv))


def kernel(x_prompt, x_sample, c_prompt, c_sample, state_s5, state_rwkv_shift, state_rwkv, state_hgrn, state_ffn_conv, w_ada, b_ada, g_mix, g_ffn, w_in, s5_lambda_re, s5_lambda_im, s5_log_dt, s5_b_re, s5_b_im, s5_c_re, s5_c_im, s5_d, s5_w_glu, s5_b_glu, rw_mu, rw_w0, rw_w2, rw_a0, rw_a2, rw_g2, rw_k_k, rw_k_a, rw_r_k, rw_ln_w, rw_ln_b, hg_lb, hg_norm, w_lift_a, w_lift_b, w_lift_c, w_out, w_up, conv_w, conv_b, w_down, final_g):
    per_layer = {
        "w_ada": w_ada, "b_ada": b_ada, "g_mix": g_mix, "g_ffn": g_ffn, "w_in": w_in,
        "s5_lambda_re": s5_lambda_re, "s5_lambda_im": s5_lambda_im, "s5_log_dt": s5_log_dt,
        "s5_b_re": s5_b_re, "s5_b_im": s5_b_im, "s5_c_re": s5_c_re, "s5_c_im": s5_c_im,
        "s5_d": s5_d, "s5_w_glu": s5_w_glu, "s5_b_glu": s5_b_glu,
        "rw_mu": rw_mu, "rw_w0": rw_w0, "rw_w2": rw_w2, "rw_a0": rw_a0, "rw_a2": rw_a2,
        "rw_g2": rw_g2, "rw_k_k": rw_k_k, "rw_k_a": rw_k_a, "rw_r_k": rw_r_k,
        "rw_ln_w": rw_ln_w, "rw_ln_b": rw_ln_b, "hg_norm": hg_norm,
        "w_lift_a": w_lift_a, "w_lift_b": w_lift_b, "w_lift_c": w_lift_c, "w_out": w_out,
        "w_up": w_up, "conv_w": conv_w, "conv_b": conv_b, "w_down": w_down,
    }
    depth = w_ada.shape[0]
    prms = [_prepare_layer({k: v[layer] for k, v in per_layer.items()}) for layer in range(depth)]

    lbp = jax.nn.softmax(hg_lb.astype(F32), axis=0)
    lower = jnp.cumsum(lbp, axis=0) - lbp[0]

    nb = x_prompt.shape[0]
    ns = x_sample.shape[0]
    c_all = jnp.concatenate([c_prompt, c_sample], axis=0).astype(F32)
    stacks = {name: per_layer[name].astype(BF16)
              for name in ("w_ada", "w_lift_a", "w_lift_b", "w_lift_c", "w_out", "w_up", "w_down")}
    for layer in range(depth):
        prms[layer]["stacks"] = stacks
        prms[layer]["layer"] = layer
    mods = [_ada_call(c_all, stacks["w_ada"], layer, prms[layer]["b_ada"]) for layer in range(depth)]
    mods_p = [m[:nb] for m in mods]
    mods_s = [m[nb:] for m in mods]

    z_s5 = jnp.zeros((depth, nb) + state_s5.shape[2:], F32)
    z_shift = jnp.zeros((depth, nb) + state_rwkv_shift.shape[2:], F32)
    z_rw = jnp.zeros((depth, nb) + state_rwkv.shape[2:], F32)
    z_hg = jnp.zeros((depth, nb) + state_hgrn.shape[2:], F32)

    y_prompt, (s5_p, shift_p, rw_p, hg_p, conv_p) = _run_trunk(
        x_prompt, mods_p, z_s5, z_shift, z_rw, z_hg, None, lower, final_g, prms, t_chunk=64)
    y_sample, (s5_s, shift_s, rw_s, hg_s, conv_s) = _run_trunk(
        x_sample, mods_s, state_s5, state_rwkv_shift, state_rwkv, state_hgrn, state_ffn_conv,
        lower, final_g, prms, t_chunk=64)
    return (y_prompt, y_sample, s5_p, shift_p, rw_p, hg_p, conv_p, s5_s, shift_s, rw_s, hg_s, conv_s)
```

```python
import functools
import math

import jax
import jax.numpy as jnp
from jax import lax
from jax.experimental import pallas as pl
from jax.experimental.pallas import tpu as pltpu

F32 = jnp.float32
BF16 = jnp.bfloat16

D_MODEL = 1024
S5_WIDTH = D_MODEL // 4
S5_GROUP = 16
S5_GROUPS = S5_WIDTH // S5_GROUP
S5_STATE = 64
S5_FLAT = S5_GROUPS * S5_STATE
RW_WIDTH = D_MODEL // 2
RW_HEAD = 64
RW_HEADS = RW_WIDTH // RW_HEAD
RW_DECAY_LORA = 32
RW_AAA_LORA = 32
RW_GATE_LORA = 64
RW_LORA = RW_DECAY_LORA + RW_AAA_LORA + RW_GATE_LORA
RW_COLS = 3 * RW_WIDTH + RW_LORA
RW_GN_EPS = 1e-5 * RW_HEAD
HG_WIDTH = D_MODEL // 4
HG_HEAD = 64
HG_HEADS = HG_WIDTH // HG_HEAD
HG_GATE_FLOOR = 1e-30
N_BRANCH = 3
D_FF = 256 * ((8 * D_MODEL // 3 + 255) // 256)
CONV_W = 3
RMS_EPS = 1e-6

LANES = 128
SUBLANES = 8
MXU_COLS = 256
VMEM_LIMIT_BYTES = 56 * 1024 * 1024

SEQ_BLOCK = SUBLANES
HEAD_PAIR = LANES // RW_HEAD
ROW_TILE = 512
FF_CHUNK = 256
IN_PROJ_CHUNK = 512
S5_CHUNK_FACTOR = 2
MAX_SUB = 16
RW_PAIRS = RW_HEADS // HEAD_PAIR
HG_PAIRS = HG_HEADS // HEAD_PAIR

_NN = (((1,), (0,)), ((), ()))
_NT = (((1,), (1,)), ((), ()))


def _cparams(*sem):
    return pltpu.CompilerParams(dimension_semantics=sem, vmem_limit_bytes=VMEM_LIMIT_BYTES)


def _const_spec(shape):
    nd = len(shape)
    return pl.BlockSpec(shape, lambda *_: (0,) * nd, pipeline_mode=pl.Buffered(1))


def _layer_weight_spec(stack, layer):
    shape = stack.shape[1:]
    zeros = (0,) * len(shape)
    return pl.BlockSpec((None,) + shape, lambda *_: (layer,) + zeros, pipeline_mode=pl.Buffered(1))


def _dot(a, b):
    return jnp.dot(a, b, preferred_element_type=F32)


def _split_bf16(x):
    hi = x.astype(BF16)
    lo = (x - hi.astype(F32)).astype(BF16)
    return hi, lo


def _head_ones():
    r = lax.broadcasted_iota(jnp.int32, (LANES, LANES), 0) // RW_HEAD
    c = lax.broadcasted_iota(jnp.int32, (LANES, LANES), 1) // RW_HEAD
    return (r == c).astype(BF16)


def _head_sum(x, ones, split=True):
    hi, lo = _split_bf16(x) if split else (x.astype(BF16), None)
    tiles = []
    for c in range(0, x.shape[1], LANES):
        t = _dot(hi[:, c:c + LANES], ones)
        tiles.append(t + _dot(lo[:, c:c + LANES], ones) if split else t)
    return tiles[0] if len(tiles) == 1 else jnp.concatenate(tiles, axis=1)


def _softplus(x):
    return jnp.maximum(x, 0.0) + jnp.log(1.0 + jnp.exp(-jnp.abs(x)))


def _gelu(x):
    c = 2.0 * math.sqrt(2.0 / math.pi)
    return x * jax.nn.sigmoid(x * (c + (c * 0.044715) * (x * x)))


def _rms_modulate(x, g, shift, scale):
    ms = jnp.mean(x * x, axis=-1, keepdims=True)
    return x * lax.rsqrt(ms + RMS_EPS) * g * (1.0 + scale) + shift


def _seq_specs(n, l, t_chunk, widths_in, widths_out):
    nb = n // SEQ_BLOCK
    if l % t_chunk == 0:
        spec = lambda w: pl.BlockSpec((SEQ_BLOCK, t_chunk, w), lambda b, c: (b, c, 0))
        shape = lambda w: (n, l, w)
        nt = l // t_chunk
    else:
        assert l < t_chunk
        spec = lambda w: pl.BlockSpec((SEQ_BLOCK * l, w), lambda b, c: (b, 0))
        shape = lambda w: (n * l, w)
        nt = 1
    return ((nb, nt), [spec(w) for w in widths_in], [spec(w) for w in widths_out],
            [shape(w) for w in widths_in], [shape(w) for w in widths_out])


def _branch_dtype(shape):
    return BF16 if len(shape) == 3 else F32


def _seq_block_load(z_ref, pad_ref):
    if len(z_ref.shape) == 3:
        return z_ref[...]
    l = z_ref.shape[0] // SEQ_BLOCK
    pad_ref[...] = jnp.zeros_like(pad_ref)
    for n in range(SEQ_BLOCK):
        pad_ref[n, 0:l, :] = z_ref[n * l:(n + 1) * l, :]
    return pad_ref[...]


def _seq_block_store(y_ref, stage_ref, y3):
    if len(y_ref.shape) == 3:
        y_ref[...] = y3.astype(y_ref.dtype)
        return
    l = y_ref.shape[0] // SEQ_BLOCK
    stage_ref[...] = y3
    for n in range(SEQ_BLOCK):
        y_ref[n * l:(n + 1) * l, :] = stage_ref[n, 0:l, :]


def _heads_to_tiles(s_ref, tile_ref, n_pairs, transpose):
    zero = jnp.zeros((RW_HEAD, RW_HEAD), F32)
    for n in range(SEQ_BLOCK):
        for hp in range(n_pairs):
            a, b = s_ref[n, HEAD_PAIR * hp], s_ref[n, HEAD_PAIR * hp + 1]
            if transpose:
                a, b = a.T, b.T
            tile_ref[hp, n] = jnp.concatenate([jnp.concatenate([a, zero], axis=1),
                                               jnp.concatenate([zero, b], axis=1)], axis=0)


def _tiles_to_heads(tile_ref, s_ref, n_pairs, transpose):
    for n in range(SEQ_BLOCK):
        for hp in range(n_pairs):
            t = tile_ref[hp, n]
            a, b = t[0:RW_HEAD, 0:RW_HEAD], t[RW_HEAD:LANES, RW_HEAD:LANES]
            if transpose:
                a, b = a.T, b.T
            s_ref[n, HEAD_PAIR * hp] = a
            s_ref[n, HEAD_PAIR * hp + 1] = b


def _ada_kernel(c_ref, w_ref, b_ref, o_ref):
    c = c_ref[...]
    o_ref[...] = _dot(jax.nn.silu(c).astype(BF16), w_ref[...]) + b_ref[...]


def _ada_call(c, w_stack, layer, b):
    n, d = c.shape
    cols = w_stack.shape[2]
    tn = cols // 4
    return pl.pallas_call(
        _ada_kernel,
        out_shape=jax.ShapeDtypeStruct((n, cols), F32),
        grid=(cols // tn,),
        in_specs=[_const_spec((n, d)),
                  pl.BlockSpec((None, d, tn), lambda j: (layer, 0, j)),
                  pl.BlockSpec((1, tn), lambda j: (0, j))],
        out_specs=pl.BlockSpec((n, tn), lambda j: (0, j)),
        compiler_params=_cparams("arbitrary"),
        name="ada_mod",
    )(c, w_stack, b.reshape(1, cols))


def _mod_operand(m, l, tm):
    n, d = m.shape
    if l % tm == 0:
        per = l // tm
        return m.reshape(n, 1, d), pl.BlockSpec((1, 1, d), lambda i: (i // per, 0, 0))
    assert tm % l == 0
    return jnp.repeat(m, l, axis=0).reshape(1, n * l, d), pl.BlockSpec((1, tm, d), lambda i: (0, i, 0))


def _tok_layout(n, l, tm, w):
    if l % tm == 0:
        per = l // tm
        return (n, l, w), pl.BlockSpec((1, tm, w), lambda i: (i // per, i % per, 0))
    return (n * l, w), pl.BlockSpec((tm, w), lambda i: (i, 0))


def _tile(ref):
    return ref.at[0] if len(ref.shape) == 3 else ref


def _in_proj_kernel(x_ref, g_ref, sh_ref, sc_ref, wa_ref, wb_ref, wc_ref, wg_ref,
                    oa_ref, ob_ref, oc_ref, og_ref):
    h = _rms_modulate(_tile(x_ref)[...], g_ref[...], sh_ref[0], sc_ref[0]).astype(BF16)

    def project(w_ref, o_ref, fn=None):
        width = w_ref.shape[1]
        out = _tile(o_ref)
        for c0 in range(0, width, IN_PROJ_CHUNK):
            cols = slice(c0, min(c0 + IN_PROJ_CHUNK, width))
            z = _dot(h, w_ref[:, cols])
            out[:, cols] = (z if fn is None else fn(z)).astype(out.dtype)

    project(wa_ref, oa_ref)
    project(wb_ref, ob_ref)
    project(wc_ref, oc_ref)
    project(wg_ref, og_ref, jax.nn.sigmoid)


def _in_proj_call(x, n, l, g, shift, scale, w_a, w_b, w_c, w_g, tm):
    d = x.shape[-1]
    sh_arr, sh_spec = _mod_operand(shift, l, tm)
    sc_arr, sc_spec = _mod_operand(scale, l, tm)
    widths = [w.shape[1] for w in (w_a, w_b, w_c, w_g)]
    dtypes = [F32, F32, F32, BF16]
    outs = [_tok_layout(n, l, tm, w) for w in widths]
    return pl.pallas_call(
        _in_proj_kernel,
        out_shape=[jax.ShapeDtypeStruct(shape, dt) for (shape, _), dt in zip(outs, dtypes)],
        grid=(n * l // tm,),
        in_specs=[_tok_layout(n, l, tm, d)[1], _const_spec((1, d)), sh_spec, sc_spec]
        + [_const_spec(w.shape) for w in (w_a, w_b, w_c, w_g)],
        out_specs=[spec for _, spec in outs],
        compiler_params=_cparams("arbitrary"),
        name="in_proj",
    )(x, g.reshape(1, d), sh_arr, sc_arr, w_a, w_b, w_c, w_g)


def _s5_kernel(u_ref, s0_ref, b_ref, c_ref, a_ref, d_ref, wg_ref, bg_ref,
               y_ref, sl_ref, e_ref, st_ref, pad_ref, stage_ref, *, t_chunk, n_steps):
    @pl.when(pl.program_id(1) == 0)
    def _():
        st_ref[...] = s0_ref[...]

    rows = SEQ_BLOCK * t_chunk
    ut = jnp.swapaxes(_seq_block_load(u_ref, pad_ref), 0, 1).reshape(rows, S5_WIDTH)
    ub = ut.astype(BF16)
    for c0 in range(0, 2 * S5_FLAT, MXU_COLS):
        ch0 = (c0 % S5_FLAT) // S5_STATE * S5_GROUP
        k0 = ch0 // LANES * LANES
        assert ch0 + MXU_COLS // S5_STATE * S5_GROUP <= k0 + LANES
        e_ref[:, c0:c0 + MXU_COLS] = _dot(ub[:, k0:k0 + LANES], b_ref[k0:k0 + LANES, c0:c0 + MXU_COLS])

    ar = jnp.broadcast_to(a_ref[0:1, :], (SEQ_BLOCK, S5_FLAT))
    ai = jnp.broadcast_to(a_ref[1:2, :], (SEQ_BLOCK, S5_FLAT))

    def step(t, carry):
        sr, si = carry
        r = pl.ds(pl.multiple_of(t * SEQ_BLOCK, SEQ_BLOCK), SEQ_BLOCK)
        nr = ar * sr - ai * si + e_ref[r, 0:S5_FLAT]
        ni = ar * si + ai * sr + e_ref[r, S5_FLAT:2 * S5_FLAT]
        e_ref[r, 0:S5_FLAT] = nr
        e_ref[r, S5_FLAT:2 * S5_FLAT] = ni
        return nr, ni

    sr, si = lax.fori_loop(0, n_steps, step, (st_ref[:, 0:S5_FLAT], st_ref[:, S5_FLAT:2 * S5_FLAT]),
                           unroll=4 if n_steps % 4 == 0 else 1)
    st_ref[:, 0:S5_FLAT] = sr
    st_ref[:, S5_FLAT:2 * S5_FLAT] = si
    sl_ref[...] = st_ref[...]

    y = _dot(e_ref[...].astype(BF16), c_ref[...]) + d_ref[...] * ut
    y = _gelu(y)
    y = y * jax.nn.sigmoid(_dot(y.astype(BF16), wg_ref[...]) + bg_ref[...])
    _seq_block_store(y_ref, stage_ref, jnp.swapaxes(y.reshape(t_chunk, SEQ_BLOCK, S5_WIDTH), 0, 1))


def _io_scratch(l, t_chunk, width_in, width_out):
    if l % t_chunk == 0:
        return [pltpu.VMEM((SUBLANES, LANES), F32)] * 2
    return [pltpu.VMEM((SEQ_BLOCK, t_chunk, width_in), F32), pltpu.VMEM((SEQ_BLOCK, t_chunk, width_out), F32)]


def _s5_call(u2, n, l, s0, prm, t_chunk):
    grid, in_specs, out_specs, in_shapes, out_shapes = _seq_specs(n, l, t_chunk, [S5_WIDTH], [S5_WIDTH])
    kern = functools.partial(_s5_kernel, t_chunk=t_chunk, n_steps=min(t_chunk, l))
    y, s_new = pl.pallas_call(
        kern,
        out_shape=[jax.ShapeDtypeStruct(out_shapes[0], _branch_dtype(out_shapes[0])),
                   jax.ShapeDtypeStruct((n, 2 * S5_FLAT), F32)],
        grid=grid,
        in_specs=in_specs + [
            pl.BlockSpec((SEQ_BLOCK, 2 * S5_FLAT), lambda b, c: (b, 0)),
            _const_spec((S5_WIDTH, 2 * S5_FLAT)), _const_spec((2 * S5_FLAT, S5_WIDTH)),
            _const_spec((2, S5_FLAT)), _const_spec((1, S5_WIDTH)),
            _const_spec((S5_WIDTH, S5_WIDTH)), _const_spec((1, S5_WIDTH))],
        out_specs=out_specs + [pl.BlockSpec((SEQ_BLOCK, 2 * S5_FLAT), lambda b, c: (b, 0))],
        scratch_shapes=[pltpu.VMEM((SEQ_BLOCK * t_chunk, 2 * S5_FLAT), F32),
                        pltpu.VMEM((SEQ_BLOCK, 2 * S5_FLAT), F32)]
        + _io_scratch(l, t_chunk, S5_WIDTH, S5_WIDTH),
        compiler_params=_cparams("arbitrary", "arbitrary"),
        name="s5_mixer",
    )(u2.reshape(in_shapes[0]), s0, prm["s5_b"], prm["s5_c"], prm["s5_a"], prm["s5_d"], prm["s5_w_glu"],
      prm["s5_b_glu"])
    return y, s_new


def _mm(a, b, dims):
    return lax.dot_general(a.astype(BF16), b.astype(BF16), dims, preferred_element_type=F32)


def _cumsum_groups(x, group):
    pos = lax.broadcasted_iota(jnp.int32, (x.shape[0], 1), 0) % group
    s = 1
    while s < group:
        x = x + jnp.where(pos >= s, pltpu.roll(x, s, 0), 0.0)
        s *= 2
    return x


def _rw_kernel(p_ref, sh0_ref, s0_ref, mu_ref, vec_ref, w2_ref, a2_ref, g2_ref, *rest,
               t_chunk, l_valid, last_row, layer):
    earlier_ref = rest[0] if layer else None
    (y_ref, shl_ref, sl_ref,
     pbuf_ref, s_ref, at_ref, bt_ref, kt_ref, rt_ref, bh_ref, kh_ref, v_ref, gc_ref, ys_ref,
     pad_ref, stage_ref) = rest[1 if layer else 0:]
    rows = SEQ_BLOCK * t_chunk
    sub = min(MAX_SUB, t_chunk)
    n_sub = t_chunk // sub

    @pl.when(pl.program_id(1) == 0)
    def _():
        pbuf_ref[...] = sh0_ref[...]
        _heads_to_tiles(s0_ref, s_ref, RW_PAIRS, transpose=False)

    p3 = _seq_block_load(p_ref, pad_ref)
    first = lax.broadcasted_iota(jnp.int32, (1, t_chunk, 1), 1) == 0
    prev3 = jnp.where(first, pbuf_ref[...], pltpu.roll(p3, 1, 1))
    last = p3[:, last_row:last_row + 1, :]
    pbuf_ref[...] = last
    shl_ref[...] = last

    xm = (p3 + (prev3 - p3) * mu_ref[...]).reshape(rows, RW_COLS)
    r = xm[:, 0:RW_WIDTH]
    k = xm[:, RW_WIDTH:2 * RW_WIDTH]
    v = xm[:, 2 * RW_WIDTH:3 * RW_WIDTH]
    lora = xm[:, 3 * RW_WIDTH:RW_COLS]
    w0, a0, k_k, k_a = vec_ref[0:1, :], vec_ref[1:2, :], vec_ref[2:3, :], vec_ref[3:4, :]
    r_k, ln_w, ln_b = vec_ref[4:5, :], vec_ref[5:6, :], vec_ref[6:7, :]

    w = -_softplus(-(w0 + _dot(jnp.tanh(lora).astype(BF16), w2_ref[...]))) - 0.5
    log_decay = -jnp.exp(w)
    a = jax.nn.sigmoid(a0 + _dot(lora.astype(BF16), a2_ref[...]))
    g = _dot(jax.nn.sigmoid(lora).astype(BF16), g2_ref[...])

    ones = _head_ones()
    kk = k * k_k
    kk = kk * lax.rsqrt(jnp.maximum(_head_sum(kk * kk, ones), 1e-24))
    kt = k * (1.0 + (a - 1.0) * k_a)
    kka = kk * a
    if l_valid < t_chunk:
        live = lax.broadcasted_iota(jnp.int32, (rows, 1), 0) % t_chunk < l_valid
        log_decay = jnp.where(live, log_decay, 0.0)
        kk = jnp.where(live, kk, 0.0)
        kka = jnp.where(live, kka, 0.0)
        kt_live = jnp.where(live, kt, 0.0)
    else:
        kt_live = kt

    cum = _cumsum_groups(log_decay, sub)
    shape_g = (rows // sub, sub, RW_WIDTH)
    cum_end = jnp.broadcast_to(cum.reshape(shape_g)[:, sub - 1:sub, :], shape_g).reshape(rows, RW_WIDTH)
    g_in = jnp.exp(cum)
    g_inv = jnp.exp(-cum)
    g_ex = jnp.exp(cum - log_decay)
    g_out = jnp.exp(cum_end - cum)
    shape3 = (SEQ_BLOCK, t_chunk, RW_WIDTH)
    at_ref[...] = (-kk * g_ex).reshape(shape3)
    bt_ref[...] = (kka * g_inv).reshape(shape3)
    kt_ref[...] = (kt_live * g_inv).reshape(shape3)
    rt_ref[...] = (r * g_in).reshape(shape3)
    bh_ref[...] = (kka * g_out).reshape(shape3)
    kh_ref[...] = (kt_live * g_out).reshape(shape3)
    v_ref[...] = v.reshape(shape3)
    gc_ref[...] = jnp.exp(cum_end).reshape(shape3)

    m_rows = SEQ_BLOCK * sub
    ri = lax.broadcasted_iota(jnp.int32, (m_rows, m_rows), 0)
    ci = lax.broadcasted_iota(jnp.int32, (m_rows, m_rows), 1)
    same_seq = (ri // sub) == (ci // sub)
    before = same_seq & ((ci % sub) < (ri % sub))
    upto = same_seq & ((ci % sub) <= (ri % sub))
    eye = (ri == ci).astype(F32)
    pair_mask = (ri // 2) == (ci // 2)
    level_masks = []
    s = 2
    while s < sub:
        level_masks.append(((ri // (2 * s)) == (ci // (2 * s))) & ((ri // s) != (ci // s)))
        s *= 2
    lane = lax.broadcasted_iota(jnp.int32, (m_rows, LANES), 1)
    head_mask = [lane < RW_HEAD, lane >= RW_HEAD]
    wide = (m_rows, SEQ_BLOCK * LANES)
    own_rows = (lax.broadcasted_iota(jnp.int32, wide, 1) // LANES
                == lax.broadcasted_iota(jnp.int32, wide, 0) // sub)
    tiles = (LANES, SEQ_BLOCK * LANES)
    same_head = (lax.broadcasted_iota(jnp.int32, tiles, 0) // RW_HEAD
                 == (lax.broadcasted_iota(jnp.int32, tiles, 1) % LANES) // RW_HEAD)
    group = max(g for g in (1, 2, 4) if n_sub % g == 0)
    units = [(j, hp) for j in range(group) for hp in range(RW_PAIRS)]
    heads = [(u, h) for u in range(len(units)) for h in range(HEAD_PAIR)]

    def sub_chunk(c, carry):
        rs = [pl.ds(pl.multiple_of((c * group + j) * sub, sub), sub) for j in range(group)]

        def ld(ref, j, hp):
            return ref[:, rs[j], pl.ds(hp * LANES, LANES)].reshape(m_rows, LANES)

        at, bt, ktl, rt, bh, kh, vv, gcv = ([ld(ref, j, hp) for j, hp in units]
                                            for ref in (at_ref, bt_ref, kt_ref, rt_ref, bh_ref, kh_ref,
                                                        v_ref, gc_ref))
        state = [[s_ref[hp, n] for n in range(SEQ_BLOCK)] for hp in range(RW_PAIRS)]
        zero = jnp.zeros((m_rows, LANES), F32)
        msk = lambda x, h: jnp.where(head_mask[h], x, zero)

        gram = [_mm(jnp.concatenate([msk(at[hp], 0), msk(rt[hp], 0), msk(at[hp], 1), msk(rt[hp], 1)], axis=0),
                    jnp.concatenate([bt[hp], ktl[hp]], axis=0), _NT)
                for hp in range(len(units))]

        def quad(hp, h, row, col, keep):
            blk = gram[hp][(2 * h + row) * m_rows:(2 * h + row + 1) * m_rows, col * m_rows:(col + 1) * m_rows]
            return jnp.where(keep, blk, 0.0)

        m_ab = [quad(hp, h, 0, 0, before) for hp, h in heads]
        m_ak = [quad(hp, h, 0, 1, before) for hp, h in heads]
        n_rb = [quad(hp, h, 1, 0, upto) for hp, h in heads]
        n_rk = [quad(hp, h, 1, 1, upto) for hp, h in heads]
        tinv = [eye + jnp.where(pair_mask, m, 0.0) for m in m_ab]
        for lm in level_masks:
            prod = [_mm(jnp.where(lm, m, 0.0), t, _NN) for m, t in zip(m_ab, tinv)]
            tinv = [t + _mm(t, q, _NN) for t, q in zip(tinv, prod)]
        vh = [msk(vv[hp], h) for hp, h in heads]
        w1 = [_mm(m, x, _NN) for m, x in zip(m_ak, vh)]
        ap = [_mm(t, jnp.concatenate([msk(at[hp], h), w], axis=1), _NN)
              for t, (hp, h), w in zip(tinv, heads, w1)]
        nb = [_mm(m, x, _NN) for m, x in zip(n_rb, ap)]
        nv = [_mm(m, x, _NN) for m, x in zip(n_rk, vh)]

        def per_seq(x):
            return jnp.where(own_rows, jnp.concatenate([x] * SEQ_BLOCK, axis=1), 0.0)

        pairs = range(len(units))
        both = lambda xs, hp, cols: xs[HEAD_PAIR * hp][:, cols] + xs[HEAD_PAIR * hp + 1][:, cols]
        lo, hi = slice(0, LANES), slice(LANES, 2 * LANES)
        bh_x = [per_seq(bh[hp]) for hp in pairs]
        kh_x = [per_seq(kh[hp]) for hp in pairs]
        gam = [jnp.where(same_head, _mm(both(ap, hp, lo).T, bh_x[hp], _NN), 0.0) for hp in pairs]
        u = [jnp.where(same_head, _mm(jnp.concatenate([both(ap, hp, hi).T, vv[hp].T], axis=1),
                                      jnp.concatenate([bh_x[hp], kh_x[hp]], axis=0), _NN), 0.0)
             for hp in pairs]
        for hp, (j, pair) in enumerate(units):
            r_hat = rt[hp] + both(nb, hp, lo)
            y_zero = both(nb, hp, hi) + nv[HEAD_PAIR * hp] + nv[HEAD_PAIR * hp + 1]
            cols = pl.ds(pair * LANES, LANES)
            for n in range(SEQ_BLOCK):
                q = slice(n * sub, (n + 1) * sub)
                blk = slice(n * LANES, (n + 1) * LANES)
                st = state[pair][n]
                ys_ref[n, rs[j], cols] = _mm(r_hat[q], st, _NT) + y_zero[q]
                state[pair][n] = (st * gcv[hp][n * sub:n * sub + 1, :]
                                  + _mm(st, gam[hp][:, blk], _NN) + u[hp][:, blk])
        for pair in range(RW_PAIRS):
            for n in range(SEQ_BLOCK):
                s_ref[pair, n] = state[pair][n]
        return carry

    lax.fori_loop(0, n_sub // group, sub_chunk, 0)

    @pl.when(pl.program_id(1) == pl.num_programs(1) - 1)
    def _():
        if layer:
            sl_ref[0:layer] = earlier_ref[...]
        _tiles_to_heads(s_ref, sl_ref.at[layer], RW_PAIRS, transpose=False)

    y = ys_ref[...].reshape(rows, RW_WIDTH)
    inv = 1.0 / RW_HEAD
    mean = _head_sum(y, ones) * inv
    yc = y - mean
    var = _head_sum(yc * yc, ones, split=False) * inv
    y = yc * lax.rsqrt(var + RW_GN_EPS) * ln_w + ln_b
    bonus = _head_sum(r * kt * r_k, ones, split=False) * v
    _seq_block_store(y_ref, stage_ref, ((y + bonus) * g).reshape(shape3))


def _layer_state_specs(state, layer):
    tail = state.shape[1:]
    zeros = (0,) * len(tail)
    read = pl.BlockSpec((SEQ_BLOCK,) + tail, lambda b, c: (b,) + zeros)
    upto = lambda k: pl.BlockSpec((k, SEQ_BLOCK) + tail, lambda b, c: (0, b) + zeros)
    return read, upto(layer + 1), ([upto(layer)] if layer else [])


def _rw_call(p2, n, l, shift0, state, layer, earlier, prm, t_chunk):
    assert t_chunk % min(MAX_SUB, t_chunk) == 0
    grid, in_specs, out_specs, in_shapes, out_shapes = _seq_specs(n, l, t_chunk, [RW_COLS], [RW_WIDTH])
    valid_in_chunk = min(t_chunk, l)
    kern = functools.partial(_rw_kernel, t_chunk=t_chunk, l_valid=valid_in_chunk, last_row=valid_in_chunk - 1,
                             layer=layer)
    st_in, st_out, st_earlier = _layer_state_specs(state, layer)
    sh_spec = pl.BlockSpec((SEQ_BLOCK, 1, RW_COLS), lambda b, c: (b, 0, 0))
    q_scr = pltpu.VMEM((SEQ_BLOCK, t_chunk, RW_WIDTH), F32)
    return pl.pallas_call(
        kern,
        out_shape=[jax.ShapeDtypeStruct(out_shapes[0], _branch_dtype(out_shapes[0])),
                   jax.ShapeDtypeStruct((n, 1, RW_COLS), F32),
                   jax.ShapeDtypeStruct((layer + 1,) + state.shape, F32)],
        grid=grid,
        in_specs=in_specs + [sh_spec, st_in,
                             _const_spec((1, RW_COLS)), _const_spec((SUBLANES, RW_WIDTH)),
                             _const_spec((RW_LORA, RW_WIDTH)), _const_spec((RW_LORA, RW_WIDTH)),
                             _const_spec((RW_LORA, RW_WIDTH))] + st_earlier,
        out_specs=out_specs + [sh_spec, st_out],
        scratch_shapes=[pltpu.VMEM((SEQ_BLOCK, 1, RW_COLS), F32),
                        pltpu.VMEM((RW_PAIRS, SEQ_BLOCK, LANES, LANES), F32)] + [q_scr] * 9
        + _io_scratch(l, t_chunk, RW_COLS, RW_WIDTH),
        compiler_params=_cparams("arbitrary", "arbitrary"),
        name="rwkv7_mixer",
    )(p2.reshape(in_shapes[0]), shift0, state, prm["rw_mu"], prm["rw_vec"], prm["rw_w2"], prm["rw_a2"],
      prm["rw_g2"], *([earlier] if layer else []))


def _hg_kernel(z_ref, s0_ref, lower_ref, ng_ref, *rest, t_chunk, l_valid, layer):
    earlier_ref = rest[0] if layer else None
    (y_ref, sl_ref, s_ref, qs_ref, q_ref, k_ref, kh_ref, v_ref, b_ref, gc_ref, ys_ref, pad_ref,
     stage_ref) = rest[1 if layer else 0:]
    rows = SEQ_BLOCK * t_chunk
    sub = min(MAX_SUB, t_chunk)
    n_sub = t_chunk // sub

    @pl.when(pl.program_id(1) == 0)
    def _():
        _heads_to_tiles(s0_ref, s_ref, HG_PAIRS, transpose=True)

    z = _seq_block_load(z_ref, pad_ref).reshape(rows, 4 * HG_WIDTH)
    q = jax.nn.silu(z[:, 0:HG_WIDTH])
    f = z[:, HG_WIDTH:2 * HG_WIDTH]
    i = z[:, 2 * HG_WIDTH:3 * HG_WIDTH]
    og = z[:, 3 * HG_WIDTH:4 * HG_WIDTH]
    lower = lower_ref[...]
    fgate = lower + (1.0 - lower) * jax.nn.sigmoid(f)
    log_f = jnp.log(jnp.maximum(fgate, HG_GATE_FLOOR))
    k = 1.0 - fgate
    if l_valid < t_chunk:
        live = lax.broadcasted_iota(jnp.int32, (rows, 1), 0) % t_chunk < l_valid
        log_f = jnp.where(live, log_f, 0.0)
        k = jnp.where(live, k, 0.0)
    cum = _cumsum_groups(log_f, sub)
    shape_g = (rows // sub, sub, HG_WIDTH)
    cum_end = jnp.broadcast_to(cum.reshape(shape_g)[:, sub - 1:sub, :], shape_g).reshape(rows, HG_WIDTH)
    shape3 = (SEQ_BLOCK, t_chunk, HG_WIDTH)
    qs_ref[...] = (q * jnp.exp(cum)).reshape(shape3)
    q_ref[...] = q.reshape(shape3)
    k_ref[...] = k.reshape(shape3)
    kh_ref[...] = (k * jnp.exp(cum_end - cum)).reshape(shape3)
    v_ref[...] = i.reshape(shape3)
    b_ref[...] = cum.reshape(shape3)
    gc_ref[...] = jnp.exp(cum_end).reshape(shape3)

    m_rows = SEQ_BLOCK * sub
    ones = _head_ones()
    wide = (m_rows, SEQ_BLOCK * LANES)
    own_rows = (lax.broadcasted_iota(jnp.int32, wide, 1) // LANES
                == lax.broadcasted_iota(jnp.int32, wide, 0) // sub)
    tiles = (LANES, SEQ_BLOCK * LANES)
    same_head = (lax.broadcasted_iota(jnp.int32, tiles, 0) // HG_HEAD
                 == (lax.broadcasted_iota(jnp.int32, tiles, 1) % LANES) // HG_HEAD)
    step = lax.broadcasted_iota(jnp.int32, (sub, LANES), 0)

    def sub_chunk(c, carry):
        rs = pl.ds(pl.multiple_of(c * sub, sub), sub)
        for hp in range(HG_PAIRS):
            cols = pl.ds(hp * LANES, LANES)
            ld = lambda ref: ref[:, rs, cols].reshape(m_rows, LANES)
            qs, qq, kk, kh, vv, bb, gcv = (ld(qs_ref), ld(q_ref), ld(k_ref), ld(kh_ref), ld(v_ref),
                                           ld(b_ref), ld(gc_ref))
            state = [s_ref[hp, n] for n in range(SEQ_BLOCK)]
            kh_x = jnp.where(own_rows, jnp.concatenate([kh] * SEQ_BLOCK, axis=1), 0.0)
            u = jnp.where(same_head, _mm(vv.T, kh_x, _NN), 0.0)
            for n in range(SEQ_BLOCK):
                sl = slice(n * sub, (n + 1) * sub)
                bn, qn, kn, vn = bb[sl], qq[sl], kk[sl], vv[sl]
                prods = []
                for t in range(sub):
                    keep = step <= t
                    prods.append(jnp.where(keep, qn[t:t + 1, :] * kn * jnp.exp(bn[t:t + 1, :] - bn), 0.0))
                att = _dot(jnp.concatenate(prods, axis=0).astype(BF16), ones)
                o_rows = [jnp.sum(att[t * sub:(t + 1) * sub] * vn, axis=0, keepdims=True)
                          for t in range(sub)]
                ys_ref[n, rs, cols] = _mm(qs[sl], state[n], _NT) + jnp.concatenate(o_rows, axis=0)
                blk = slice(n * LANES, (n + 1) * LANES)
                s_ref[hp, n] = state[n] * gcv[n * sub:n * sub + 1, :] + u[:, blk]
        return carry

    lax.fori_loop(0, n_sub, sub_chunk, 0)

    @pl.when(pl.program_id(1) == pl.num_programs(1) - 1)
    def _():
        if layer:
            sl_ref[0:layer] = earlier_ref[...]
        _tiles_to_heads(s_ref, sl_ref.at[layer], HG_PAIRS, transpose=True)

    o = ys_ref[...].reshape(rows, HG_WIDTH)
    ms = _head_sum(o * o, ones, split=False) * (1.0 / HG_HEAD)
    o = o * lax.rsqrt(ms + RMS_EPS) * ng_ref[...] * jax.nn.sigmoid(og)
    _seq_block_store(y_ref, stage_ref, o.reshape(shape3))


def _hg_call(z2, n, l, state, layer, earlier, lower, norm_g, t_chunk):
    assert t_chunk % min(MAX_SUB, t_chunk) == 0
    grid, in_specs, out_specs, in_shapes, out_shapes = _seq_specs(n, l, t_chunk, [4 * HG_WIDTH], [HG_WIDTH])
    kern = functools.partial(_hg_kernel, t_chunk=t_chunk, l_valid=min(t_chunk, l), layer=layer)
    st_in, st_out, st_earlier = _layer_state_specs(state, layer)
    q_scr = pltpu.VMEM((SEQ_BLOCK, t_chunk, HG_WIDTH), F32)
    return pl.pallas_call(
        kern,
        out_shape=[jax.ShapeDtypeStruct(out_shapes[0], _branch_dtype(out_shapes[0])),
                   jax.ShapeDtypeStruct((layer + 1,) + state.shape, F32)],
        grid=grid,
        in_specs=in_specs + [st_in, _const_spec((1, HG_WIDTH)), _const_spec((1, HG_WIDTH))] + st_earlier,
        out_specs=out_specs + [st_out],
        scratch_shapes=[pltpu.VMEM((HG_PAIRS, SEQ_BLOCK, LANES, LANES), F32)] + [q_scr] * 8
        + _io_scratch(l, t_chunk, 4 * HG_WIDTH, HG_WIDTH),
        compiler_params=_cparams("arbitrary", "arbitrary"),
        name="hgrn2_mixer",
    )(z2.reshape(in_shapes[0]), state, lower.reshape(1, HG_WIDTH), norm_g.reshape(1, HG_WIDTH),
      *([earlier] if layer else []))


def _merge_ffn_kernel(*refs, l, tm, has_state, final_norm):
    (x_ref, ya_ref, yb_ref, yc_ref, gates_ref, gt1_ref, la_ref, lb_ref, lc_ref, wo_ref) = refs[:10]
    refs = refs[10:]
    if has_state:
        (g_ref, sh_ref, sc_ref, gt_ref, wup_ref, cw_ref, cb_ref, wdn_ref, fg_ref, st_ref,
         o_ref, tail_ref, h_ref, act_ref, hist_ref, buf_ref, old_ref) = refs
    else:
        (g_ref, sh_ref, sc_ref, gt_ref, wup_ref, cw_ref, cb_ref, wdn_ref, fg_ref,
         o_ref, tail_ref, h_ref, act_ref, hist_ref) = refs
    i = pl.program_id(0)
    n_chunks = D_FF // FF_CHUNK
    n_hist = CONV_W - 1

    if has_state:
        t_idx = lax.broadcasted_iota(jnp.int32, (tm, 1), 0) % l
        after = [t_idx >= j + 1 for j in range(n_hist)]
        old_ref[...] = jnp.zeros_like(old_ref)
        buf_ref[:, 0:SUBLANES, :] = jnp.zeros((2, SUBLANES, FF_CHUNK), F32)
        for s in range(tm // l):
            for j in range(n_hist):
                old_ref[j, s * l:s * l + j + 1, :] = st_ref[s, n_hist - 1 - j:n_hist, :]
    else:
        @pl.when(i % (l // tm) == 0)
        def _():
            hist_ref[...] = jnp.zeros_like(hist_ref)

    d = D_MODEL
    gates = _tile(gates_ref)
    lift = lambda y_ref, w_ref: _dot(_tile(y_ref)[...].astype(BF16), w_ref[...])
    merged = (gates[:, 0:d].astype(F32) * lift(ya_ref, la_ref)
              + gates[:, d:2 * d].astype(F32) * lift(yb_ref, lb_ref)
              + gates[:, 2 * d:3 * d].astype(F32) * lift(yc_ref, lc_ref))
    x = _tile(x_ref)[...] + gt1_ref[0] * _dot(merged.astype(BF16), wo_ref[...])
    h = _rms_modulate(x, g_ref[...], sh_ref[0], sc_ref[0])
    groups = tm // SUBLANES
    if not has_state:
        h = jnp.swapaxes(h.reshape(SUBLANES, groups, D_MODEL), 0, 1).reshape(tm, D_MODEL)
        row8 = lax.broadcasted_iota(jnp.int32, (SUBLANES, FF_CHUNK), 0)
    h_ref[...] = h.astype(BF16)

    def conv_half(c, half):
        col0 = half * D_FF + c * FF_CHUNK
        cols = slice(col0, col0 + FF_CHUNK)
        up = _dot(h_ref[...], wup_ref[:, cols])
        if has_state:
            buf_ref[half, SUBLANES:SUBLANES + tm, :] = up
            prev = [jnp.where(after[j], buf_ref[half, SUBLANES - 1 - j:SUBLANES - 1 - j + tm, :], 0.0)
                    + old_ref[j, :, cols] for j in range(n_hist)]
            for s in range(tm // l):
                r0 = SUBLANES + (s + 1) * l - n_hist
                tail_ref[s, :, cols] = buf_ref[half, r0:r0 + n_hist, :]
        else:
            late = [up[tm - (j + 1) * SUBLANES:tm - j * SUBLANES, :] for j in range(n_hist)]
            first = [jnp.where(row8 == 0, pltpu.roll(hist_ref[c, half, j], 1, 0), pltpu.roll(late[j], 1, 0))
                     for j in range(n_hist)]
            for j in range(n_hist):
                hist_ref[c, half, j] = late[j]
            prev = [jnp.concatenate([first[0], up[0:tm - SUBLANES, :]], axis=0),
                    jnp.concatenate([first[1], first[0], up[0:tm - 2 * SUBLANES, :]], axis=0)]
            tail_ref[0, 0:1, cols] = up[tm - SUBLANES - 1:tm - SUBLANES, :]
            tail_ref[0, 1:2, cols] = up[tm - 1:tm, :]
        return (cb_ref[:, cols] + cw_ref[0:1, cols] * prev[1] + cw_ref[1:2, cols] * prev[0]
                + cw_ref[2:3, cols] * up)

    for c in range(n_chunks):
        act = _gelu(conv_half(c, 0)) * conv_half(c, 1)
        act_ref[:, c * FF_CHUNK:(c + 1) * FF_CHUNK] = act.astype(BF16)
    acc = _dot(act_ref[...], wdn_ref[...])
    if not has_state:
        acc = jnp.swapaxes(acc.reshape(groups, SUBLANES, D_MODEL), 0, 1).reshape(tm, D_MODEL)
    out = x + gt_ref[0] * acc
    if final_norm:
        ms = jnp.mean(out * out, axis=-1, keepdims=True)
        out = out * lax.rsqrt(ms + RMS_EPS) * fg_ref[...]
    _tile(o_ref)[...] = out


def _merge_ffn_call(x, n, l, ya, yb, yc, gates, gate1, g, shift, scale, gate2, prm, final_g, conv_state, tm,
                    final_norm):
    assert CONV_W == 3
    d = x.shape[-1]
    has_state = conv_state is not None
    g1_arr, g1_spec = _mod_operand(gate1, l, tm)
    sh_arr, sh_spec = _mod_operand(shift, l, tm)
    sc_arr, sc_spec = _mod_operand(scale, l, tm)
    gt_arr, gt_spec = _mod_operand(gate2, l, tm)
    shape, x_spec = _tok_layout(n, l, tm, d)
    row_spec = lambda w: _tok_layout(n, l, tm, w)[1]
    stacks, layer = prm["stacks"], prm["layer"]
    w_spec = lambda name: _layer_weight_spec(stacks[name], layer)
    in_specs = [x_spec, row_spec(S5_WIDTH), row_spec(RW_WIDTH), row_spec(HG_WIDTH), row_spec(N_BRANCH * d),
                g1_spec, w_spec("w_lift_a"), w_spec("w_lift_b"), w_spec("w_lift_c"), w_spec("w_out"),
                _const_spec((1, d)), sh_spec, sc_spec, gt_spec,
                w_spec("w_up"), _const_spec((CONV_W, 2 * D_FF)), _const_spec((1, 2 * D_FF)),
                w_spec("w_down"), _const_spec((1, d))]
    operands = [x, ya, yb, yc, gates, g1_arr, stacks["w_lift_a"], stacks["w_lift_b"], stacks["w_lift_c"],
                stacks["w_out"], g.reshape(1, d), sh_arr, sc_arr, gt_arr, stacks["w_up"], prm["conv_w"],
                prm["conv_b"], stacks["w_down"], final_g.reshape(1, d)]
    scratch = [pltpu.VMEM((tm, d), BF16), pltpu.VMEM((tm, D_FF), BF16),
               pltpu.VMEM((D_FF // FF_CHUNK, 2, CONV_W - 1, SUBLANES, FF_CHUNK), F32)]
    if has_state:
        assert tm % l == 0 and l >= CONV_W - 1
        block = (tm // l, CONV_W - 1, 2 * D_FF)
        in_specs.append(pl.BlockSpec((None,) + block, lambda i: (layer, i, 0, 0)))
        operands.append(conv_state)
        tail_spec = pl.BlockSpec(block, lambda i: (i, 0, 0))
        scratch += [pltpu.VMEM((2, tm + SUBLANES, FF_CHUNK), F32), pltpu.VMEM((CONV_W - 1, tm, 2 * D_FF), F32)]
    else:
        assert l % tm == 0
        per = l // tm
        tail_spec = pl.BlockSpec((1, CONV_W - 1, 2 * D_FF), lambda i: (i // per, 0, 0))
    kern = functools.partial(_merge_ffn_kernel, l=l, tm=tm, has_state=has_state, final_norm=final_norm)
    return pl.pallas_call(
        kern,
        out_shape=[jax.ShapeDtypeStruct(shape, F32), jax.ShapeDtypeStruct((n, CONV_W - 1, 2 * D_FF), F32)],
        grid=(n * l // tm,),
        in_specs=in_specs,
        out_specs=[x_spec, tail_spec],
        scratch_shapes=scratch,
        compiler_params=_cparams("arbitrary"),
        name="merge_conv_ffn",
    )(*operands)


def _prepare_layer(p):
    out = {}
    w_in = p["w_in"]
    c1 = S5_WIDTH
    c2 = c1 + RW_COLS
    c3 = c2 + 4 * HG_WIDTH
    out["w_in_a"] = w_in[:, :c1].astype(BF16)
    out["w_in_b"] = w_in[:, c1:c2].astype(BF16)
    out["w_in_c"] = w_in[:, c2:c3].astype(BF16)
    out["w_in_g"] = w_in[:, c3:].astype(BF16)
    out["conv_w"] = p["conv_w"]
    out["conv_b"] = p["conv_b"].reshape(1, 2 * D_FF)

    lr = p["s5_lambda_re"]
    li = p["s5_lambda_im"]
    dt = jnp.exp(p["s5_log_dt"])[:, None]
    mag = jnp.exp(lr * dt)
    ar = mag * jnp.cos(li * dt)
    ai = mag * jnp.sin(li * dt)
    den = lr * lr + li * li
    zr = ((ar - 1.0) * lr + ai * li) / den
    zi = (ai * lr - (ar - 1.0) * li) / den
    bbr = zr[..., None] * p["s5_b_re"] - zi[..., None] * p["s5_b_im"]
    bbi = zr[..., None] * p["s5_b_im"] + zi[..., None] * p["s5_b_re"]
    eye = jnp.eye(S5_GROUPS, dtype=F32)
    bmat = jnp.einsum("gh,rgpc->gcrhp", eye, jnp.stack([bbr, bbi])).reshape(S5_WIDTH, 2 * S5_FLAT)
    cmat = jnp.einsum("hg,rgcp->rhpgc", eye, jnp.stack([p["s5_c_re"], -p["s5_c_im"]])).reshape(
        2 * S5_FLAT, S5_WIDTH)
    out["s5_b"] = bmat.astype(BF16)
    out["s5_c"] = cmat.astype(BF16)
    out["s5_a"] = jnp.stack([ar.reshape(S5_FLAT), ai.reshape(S5_FLAT)])
    out["s5_d"] = p["s5_d"].reshape(1, S5_WIDTH)
    out["s5_w_glu"] = p["s5_w_glu"].astype(BF16)
    out["s5_b_glu"] = p["s5_b_glu"].reshape(1, S5_WIDTH)

    out["rw_mu"] = p["rw_mu"].reshape(1, RW_COLS)
    out["rw_vec"] = jnp.stack([p["rw_w0"], p["rw_a0"], p["rw_k_k"], p["rw_k_a"],
                               p["rw_r_k"].reshape(RW_WIDTH), p["rw_ln_w"], p["rw_ln_b"],
                               jnp.zeros((RW_WIDTH,), F32)])
    zw = jnp.zeros((RW_LORA, RW_WIDTH), F32)
    out["rw_w2"] = zw.at[0:RW_DECAY_LORA].set(p["rw_w2"]).astype(BF16)
    out["rw_a2"] = zw.at[RW_DECAY_LORA:RW_DECAY_LORA + RW_AAA_LORA].set(p["rw_a2"]).astype(BF16)
    out["rw_g2"] = zw.at[RW_DECAY_LORA + RW_AAA_LORA:].set(p["rw_g2"]).astype(BF16)
    out["hg_norm"] = p["hg_norm"]
    out["g_mix"] = p["g_mix"]
    out["g_ffn"] = p["g_ffn"]
    out["b_ada"] = p["b_ada"]
    return out


def _chunk_for(l, sub, t_chunk):
    return t_chunk if l % t_chunk == 0 else -(-l // sub) * sub


def _run_trunk(x, mods, st_s5, st_shift, st_rw, st_hg, st_conv, lower, final_g, prms, t_chunk):
    n, l, d = x.shape
    rows = n * l
    tm = min(ROW_TILE, rows)
    depth = len(prms)
    assert (l % tm == 0) == (l % t_chunk == 0)
    h = x.reshape(_tok_layout(n, l, tm, d)[0])
    out_s5, out_shift, out_conv = [], [], []
    rw_new = hg_new = None
    for layer in range(depth):
        prm = prms[layer]
        sh1, sc1, gt1, sh2, sc2, gt2 = jnp.split(mods[layer], 6, axis=-1)
        za, zb, zc, gates = _in_proj_call(h, n, l, prm["g_mix"], sh1, sc1, prm["w_in_a"], prm["w_in_b"],
                                          prm["w_in_c"], prm["w_in_g"], tm)

        s5_in = jnp.concatenate([st_s5[layer][..., 0].reshape(n, S5_FLAT),
                                 st_s5[layer][..., 1].reshape(n, S5_FLAT)], axis=1)
        s5_chunk = S5_CHUNK_FACTOR * t_chunk if l % (S5_CHUNK_FACTOR * t_chunk) == 0 else t_chunk
        ya, s5_new = _s5_call(za, n, l, s5_in, prm, _chunk_for(l, SUBLANES, s5_chunk))
        yb, shift_new, rw_new = _rw_call(zb, n, l, st_shift[layer].reshape(n, 1, RW_COLS), st_rw[layer], layer, rw_new,
                                         prm, _chunk_for(l, SUBLANES, t_chunk))
        yc, hg_new = _hg_call(zc, n, l, st_hg[layer], layer, hg_new, lower[layer], prm["hg_norm"],
                              _chunk_for(l, SUBLANES, t_chunk))

        ffn_tm = tm if st_conv is None else min(tm, 128)
        h, conv_new = _merge_ffn_call(h, n, l, ya, yb, yc, gates, gt1, prm["g_ffn"], sh2, sc2, gt2, prm, final_g,
                                      st_conv, ffn_tm,
                                      final_norm=(layer == depth - 1))
        out_s5.append(jnp.stack([s5_new[:, :S5_FLAT].reshape(n, S5_GROUPS, S5_STATE),
                                 s5_new[:, S5_FLAT:].reshape(n, S5_GROUPS, S5_STATE)], axis=-1))
        out_shift.append(shift_new.reshape(n, RW_COLS))
        out_conv.append(conv_new)
    y = h.reshape(n, l, d).astype(x.dtype)
    return y, (jnp.stack(out_s5), jnp.stack(out_shift), rw_new, hg_new, jnp.stack(out_conv))


def kernel(x_prompt, x_sample, c_prompt, c_sample, state_s5, state_rwkv_shift, state_rwkv, state_hgrn, state_ffn_conv, w_ada, b_ada, g_mix, g_ffn, w_in, s5_lambda_re, s5_lambda_im, s5_log_dt, s5_b_re, s5_b_im, s5_c_re, s5_c_im, s5_d, s5_w_glu, s5_b_glu, rw_mu, rw_w0, rw_w2, rw_a0, rw_a2, rw_g2, rw_k_k, rw_k_a, rw_r_k, rw_ln_w, rw_ln_b, hg_lb, hg_norm, w_lift_a, w_lift_b, w_lift_c, w_out, w_up, conv_w, conv_b, w_down, final_g):
    per_layer = {
        "w_ada": w_ada, "b_ada": b_ada, "g_mix": g_mix, "g_ffn": g_ffn, "w_in": w_in,
        "s5_lambda_re": s5_lambda_re, "s5_lambda_im": s5_lambda_im, "s5_log_dt": s5_log_dt,
        "s5_b_re": s5_b_re, "s5_b_im": s5_b_im, "s5_c_re": s5_c_re, "s5_c_im": s5_c_im,
        "s5_d": s5_d, "s5_w_glu": s5_w_glu, "s5_b_glu": s5_b_glu,
        "rw_mu": rw_mu, "rw_w0": rw_w0, "rw_w2": rw_w2, "rw_a0": rw_a0, "rw_a2": rw_a2,
        "rw_g2": rw_g2, "rw_k_k": rw_k_k, "rw_k_a": rw_k_a, "rw_r_k": rw_r_k,
        "rw_ln_w": rw_ln_w, "rw_ln_b": rw_ln_b, "hg_norm": hg_norm,
        "w_lift_a": w_lift_a, "w_lift_b": w_lift_b, "w_lift_c": w_lift_c, "w_out": w_out,
        "w_up": w_up, "conv_w": conv_w, "conv_b": conv_b, "w_down": w_down,
    }
    depth = w_ada.shape[0]
    prms = [_prepare_layer({k: v[layer] for k, v in per_layer.items()}) for layer in range(depth)]

    lbp = jax.nn.softmax(hg_lb.astype(F32), axis=0)
    lower = jnp.cumsum(lbp, axis=0) - lbp[0]

    nb = x_prompt.shape[0]
    ns = x_sample.shape[0]
    c_all = jnp.concatenate([c_prompt, c_sample], axis=0).astype(F32)
    stacks = {name: per_layer[name].astype(BF16)
              for name in ("w_ada", "w_lift_a", "w_lift_b", "w_lift_c", "w_out", "w_up", "w_down")}
    for layer in range(depth):
        prms[layer]["stacks"] = stacks
        prms[layer]["layer"] = layer
    mods = [_ada_call(c_all, stacks["w_ada"], layer, prms[layer]["b_ada"]) for layer in range(depth)]
    mods_p = [m[:nb] for m in mods]
    mods_s = [m[nb:] for m in mods]

    z_s5 = jnp.zeros((depth, nb) + state_s5.shape[2:], F32)
    z_shift = jnp.zeros((depth, nb) + state_rwkv_shift.shape[2:], F32)
    z_rw = jnp.zeros((depth, nb) + state_rwkv.shape[2:], F32)
    z_hg = jnp.zeros((depth, nb) + state_hgrn.shape[2:], F32)

    y_prompt, (s5_p, shift_p, rw_p, hg_p, conv_p) = _run_trunk(
        x_prompt, mods_p, z_s5, z_shift, z_rw, z_hg, None, lower, final_g, prms, t_chunk=64)
    y_sample, (s5_s, shift_s, rw_s, hg_s, conv_s) = _run_trunk(
        x_sample, mods_s, state_s5, state_rwkv_shift, state_rwkv, state_hgrn, state_ffn_conv,
        lower, final_g, prms, t_chunk=64)
    return (y_prompt, y_sample, s5_p, shift_p, rw_p, hg_p, conv_p, s5_s, shift_s, rw_s, hg_s, conv_s)
```

```python
import functools
import math

import jax
import jax.numpy as jnp
from jax import lax
from jax.experimental import pallas as pl
from jax.experimental.pallas import tpu as pltpu

F32 = jnp.float32
BF16 = jnp.bfloat16

D_MODEL = 1024
S5_WIDTH = D_MODEL // 4
S5_GROUP = 16
S5_GROUPS = S5_WIDTH // S5_GROUP
S5_STATE = 64
S5_FLAT = S5_GROUPS * S5_STATE
RW_WIDTH = D_MODEL // 2
RW_HEAD = 64
RW_HEADS = RW_WIDTH // RW_HEAD
RW_DECAY_LORA = 32
RW_AAA_LORA = 32
RW_GATE_LORA = 64
RW_LORA = RW_DECAY_LORA + RW_AAA_LORA + RW_GATE_LORA
RW_COLS = 3 * RW_WIDTH + RW_LORA
RW_GN_EPS = 1e-5 * RW_HEAD
HG_WIDTH = D_MODEL // 4
HG_HEAD = 64
HG_HEADS = HG_WIDTH // HG_HEAD
HG_GATE_FLOOR = 1e-30
N_BRANCH = 3
D_FF = 256 * ((8 * D_MODEL // 3 + 255) // 256)
CONV_W = 3
RMS_EPS = 1e-6

LANES = 128
SUBLANES = 8
MXU_COLS = 256
VMEM_LIMIT_BYTES = 56 * 1024 * 1024

SEQ_BLOCK = SUBLANES
HEAD_PAIR = LANES // RW_HEAD
ROW_TILE = 512
FF_CHUNK = 256
IN_PROJ_CHUNK = 512
S5_CHUNK_FACTOR = 2
MAX_SUB = 16
RW_PAIRS = RW_HEADS // HEAD_PAIR
HG_PAIRS = HG_HEADS // HEAD_PAIR

_NN = (((1,), (0,)), ((), ()))
_NT = (((1,), (1,)), ((), ()))


def _cparams(*sem):
    return pltpu.CompilerParams(dimension_semantics=sem, vmem_limit_bytes=VMEM_LIMIT_BYTES)


def _const_spec(shape):
    nd = len(shape)
    return pl.BlockSpec(shape, lambda *_: (0,) * nd, pipeline_mode=pl.Buffered(1))


def _layer_weight_spec(stack, layer):
    shape = stack.shape[1:]
    zeros = (0,) * len(shape)
    return pl.BlockSpec((None,) + shape, lambda *_: (layer,) + zeros, pipeline_mode=pl.Buffered(1))


def _dot(a, b):
    return jnp.dot(a, b, preferred_element_type=F32)


def _split_bf16(x):
    hi = x.astype(BF16)
    lo = (x - hi.astype(F32)).astype(BF16)
    return hi, lo


def _head_ones():
    r = lax.broadcasted_iota(jnp.int32, (LANES, LANES), 0) // RW_HEAD
    c = lax.broadcasted_iota(jnp.int32, (LANES, LANES), 1) // RW_HEAD
    return (r == c).astype(BF16)


def _head_sum(x, ones, split=True):
    hi, lo = _split_bf16(x) if split else (x.astype(BF16), None)
    tiles = []
    for c in range(0, x.shape[1], LANES):
        t = _dot(hi[:, c:c + LANES], ones)
        tiles.append(t + _dot(lo[:, c:c + LANES], ones) if split else t)
    return tiles[0] if len(tiles) == 1 else jnp.concatenate(tiles, axis=1)


def _softplus(x):
    return jnp.maximum(x, 0.0) + jnp.log(1.0 + jnp.exp(-jnp.abs(x)))


def _gelu(x):
    c = 2.0 * math.sqrt(2.0 / math.pi)
    return x * jax.nn.sigmoid(x * (c + (c * 0.044715) * (x * x)))


def _rms_modulate(x, g, shift, scale):
    ms = jnp.mean(x * x, axis=-1, keepdims=True)
    return x * lax.rsqrt(ms + RMS_EPS) * g * (1.0 + scale) + shift


def _seq_specs(n, l, t_chunk, widths_in, widths_out):
    nb = n // SEQ_BLOCK
    if l % t_chunk == 0:
        spec = lambda w: pl.BlockSpec((SEQ_BLOCK, t_chunk, w), lambda b, c: (b, c, 0))
        shape = lambda w: (n, l, w)
        nt = l // t_chunk
    else:
        assert l < t_chunk
        spec = lambda w: pl.BlockSpec((SEQ_BLOCK * l, w), lambda b, c: (b, 0))
        shape = lambda w: (n * l, w)
        nt = 1
    return ((nb, nt), [spec(w) for w in widths_in], [spec(w) for w in widths_out],
            [shape(w) for w in widths_in], [shape(w) for w in widths_out])


def _branch_dtype(shape):
    return BF16 if len(shape) == 3 else F32


def _seq_block_load(z_ref, pad_ref):
    if len(z_ref.shape) == 3:
        return z_ref[...]
    l = z_ref.shape[0] // SEQ_BLOCK
    pad_ref[...] = jnp.zeros_like(pad_ref)
    for n in range(SEQ_BLOCK):
        pad_ref[n, 0:l, :] = z_ref[n * l:(n + 1) * l, :]
    return pad_ref[...]


def _seq_block_store(y_ref, stage_ref, y3):
    if len(y_ref.shape) == 3:
        y_ref[...] = y3.astype(y_ref.dtype)
        return
    l = y_ref.shape[0] // SEQ_BLOCK
    stage_ref[...] = y3
    for n in range(SEQ_BLOCK):
        y_ref[n * l:(n + 1) * l, :] = stage_ref[n, 0:l, :]


def _heads_to_tiles(s_ref, tile_ref, n_pairs, transpose):
    zero = jnp.zeros((RW_HEAD, RW_HEAD), F32)
    for n in range(SEQ_BLOCK):
        for hp in range(n_pairs):
            a, b = s_ref[n, HEAD_PAIR * hp], s_ref[n, HEAD_PAIR * hp + 1]
            if transpose:
                a, b = a.T, b.T
            tile_ref[hp, n] = jnp.concatenate([jnp.concatenate([a, zero], axis=1),
                                               jnp.concatenate([zero, b], axis=1)], axis=0)


def _tiles_to_heads(tile_ref, s_ref, n_pairs, transpose):
    for n in range(SEQ_BLOCK):
        for hp in range(n_pairs):
            t = tile_ref[hp, n]
            a, b = t[0:RW_HEAD, 0:RW_HEAD], t[RW_HEAD:LANES, RW_HEAD:LANES]
            if transpose:
                a, b = a.T, b.T
            s_ref[n, HEAD_PAIR * hp] = a
            s_ref[n, HEAD_PAIR * hp + 1] = b


def _ada_kernel(c_ref, w_ref, b_ref, o_ref):
    c = c_ref[...]
    o_ref[...] = _dot(jax.nn.silu(c).astype(BF16), w_ref[...]) + b_ref[...]


def _ada_call(c, w_stack, layer, b):
    n, d = c.shape
    cols = w_stack.shape[2]
    tn = cols // 4
    return pl.pallas_call(
        _ada_kernel,
        out_shape=jax.ShapeDtypeStruct((n, cols), F32),
        grid=(cols // tn,),
        in_specs=[_const_spec((n, d)),
                  pl.BlockSpec((None, d, tn), lambda j: (layer, 0, j)),
                  pl.BlockSpec((1, tn), lambda j: (0, j))],
        out_specs=pl.BlockSpec((n, tn), lambda j: (0, j)),
        compiler_params=_cparams("arbitrary"),
        name="ada_mod",
    )(c, w_stack, b.reshape(1, cols))


def _mod_operand(m, l, tm):
    n, d = m.shape
    if l % tm == 0:
        per = l // tm
        return m.reshape(n, 1, d), pl.BlockSpec((1, 1, d), lambda i: (i // per, 0, 0))
    assert tm % l == 0
    return jnp.repeat(m, l, axis=0).reshape(1, n * l, d), pl.BlockSpec((1, tm, d), lambda i: (0, i, 0))


def _tok_layout(n, l, tm, w):
    if l % tm == 0:
        per = l // tm
        return (n, l, w), pl.BlockSpec((1, tm, w), lambda i: (i // per, i % per, 0))
    return (n * l, w), pl.BlockSpec((tm, w), lambda i: (i, 0))


def _tile(ref):
    return ref.at[0] if len(ref.shape) == 3 else ref


def _in_proj_kernel(x_ref, g_ref, sh_ref, sc_ref, wa_ref, wb_ref, wc_ref, wg_ref,
                    oa_ref, ob_ref, oc_ref, og_ref):
    h = _rms_modulate(_tile(x_ref)[...], g_ref[...], sh_ref[0], sc_ref[0]).astype(BF16)

    def project(w_ref, o_ref, fn=None):
        width = w_ref.shape[1]
        out = _tile(o_ref)
        for c0 in range(0, width, IN_PROJ_CHUNK):
            cols = slice(c0, min(c0 + IN_PROJ_CHUNK, width))
            z = _dot(h, w_ref[:, cols])
            out[:, cols] = (z if fn is None else fn(z)).astype(out.dtype)

    project(wa_ref, oa_ref)
    project(wb_ref, ob_ref)
    project(wc_ref, oc_ref)
    project(wg_ref, og_ref, jax.nn.sigmoid)


def _in_proj_call(x, n, l, g, shift, scale, w_a, w_b, w_c, w_g, tm):
    d = x.shape[-1]
    sh_arr, sh_spec = _mod_operand(shift, l, tm)
    sc_arr, sc_spec = _mod_operand(scale, l, tm)
    widths = [w.shape[1] for w in (w_a, w_b, w_c, w_g)]
    dtypes = [F32, F32, F32, BF16]
    outs = [_tok_layout(n, l, tm, w) for w in widths]
    return pl.pallas_call(
        _in_proj_kernel,
        out_shape=[jax.ShapeDtypeStruct(shape, dt) for (shape, _), dt in zip(outs, dtypes)],
        grid=(n * l // tm,),
        in_specs=[_tok_layout(n, l, tm, d)[1], _const_spec((1, d)), sh_spec, sc_spec]
        + [_const_spec(w.shape) for w in (w_a, w_b, w_c, w_g)],
        out_specs=[spec for _, spec in outs],
        compiler_params=_cparams("arbitrary"),
        name="in_proj",
    )(x, g.reshape(1, d), sh_arr, sc_arr, w_a, w_b, w_c, w_g)


def _s5_kernel(u_ref, s0_ref, b_ref, c_ref, a_ref, d_ref, wg_ref, bg_ref,
               y_ref, sl_ref, e_ref, st_ref, pad_ref, stage_ref, *, t_chunk, n_steps):
    @pl.when(pl.program_id(1) == 0)
    def _():
        st_ref[...] = s0_ref[...]

    rows = SEQ_BLOCK * t_chunk
    ut = jnp.swapaxes(_seq_block_load(u_ref, pad_ref), 0, 1).reshape(rows, S5_WIDTH)
    ub = ut.astype(BF16)
    for c0 in range(0, 2 * S5_FLAT, MXU_COLS):
        ch0 = (c0 % S5_FLAT) // S5_STATE * S5_GROUP
        k0 = ch0 // LANES * LANES
        assert ch0 + MXU_COLS // S5_STATE * S5_GROUP <= k0 + LANES
        e_ref[:, c0:c0 + MXU_COLS] = _dot(ub[:, k0:k0 + LANES], b_ref[k0:k0 + LANES, c0:c0 + MXU_COLS])

    ar = jnp.broadcast_to(a_ref[0:1, :], (SEQ_BLOCK, S5_FLAT))
    ai = jnp.broadcast_to(a_ref[1:2, :], (SEQ_BLOCK, S5_FLAT))

    def step(t, carry):
        sr, si = carry
        r = pl.ds(pl.multiple_of(t * SEQ_BLOCK, SEQ_BLOCK), SEQ_BLOCK)
        nr = ar * sr - ai * si + e_ref[r, 0:S5_FLAT]
        ni = ar * si + ai * sr + e_ref[r, S5_FLAT:2 * S5_FLAT]
        e_ref[r, 0:S5_FLAT] = nr
        e_ref[r, S5_FLAT:2 * S5_FLAT] = ni
        return nr, ni

    sr, si = lax.fori_loop(0, n_steps, step, (st_ref[:, 0:S5_FLAT], st_ref[:, S5_FLAT:2 * S5_FLAT]),
                           unroll=4 if n_steps % 4 == 0 else 1)
    st_ref[:, 0:S5_FLAT] = sr
    st_ref[:, S5_FLAT:2 * S5_FLAT] = si
    sl_ref[...] = st_ref[...]

    y = _dot(e_ref[...].astype(BF16), c_ref[...]) + d_ref[...] * ut
    y = _gelu(y)
    y = y * jax.nn.sigmoid(_dot(y.astype(BF16), wg_ref[...]) + bg_ref[...])
    _seq_block_store(y_ref, stage_ref, jnp.swapaxes(y.reshape(t_chunk, SEQ_BLOCK, S5_WIDTH), 0, 1))


def _io_scratch(l, t_chunk, width_in, width_out):
    if l % t_chunk == 0:
        return [pltpu.VMEM((SUBLANES, LANES), F32)] * 2
    return [pltpu.VMEM((SEQ_BLOCK, t_chunk, width_in), F32), pltpu.VMEM((SEQ_BLOCK, t_chunk, width_out), F32)]


def _s5_call(u2, n, l, s0, prm, t_chunk):
    grid, in_specs, out_specs, in_shapes, out_shapes = _seq_specs(n, l, t_chunk, [S5_WIDTH], [S5_WIDTH])
    kern = functools.partial(_s5_kernel, t_chunk=t_chunk, n_steps=min(t_chunk, l))
    y, s_new = pl.pallas_call(
        kern,
        out_shape=[jax.ShapeDtypeStruct(out_shapes[0], _branch_dtype(out_shapes[0])),
                   jax.ShapeDtypeStruct((n, 2 * S5_FLAT), F32)],
        grid=grid,
        in_specs=in_specs + [
            pl.BlockSpec((SEQ_BLOCK, 2 * S5_FLAT), lambda b, c: (b, 0)),
            _const_spec((S5_WIDTH, 2 * S5_FLAT)), _const_spec((2 * S5_FLAT, S5_WIDTH)),
            _const_spec((2, S5_FLAT)), _const_spec((1, S5_WIDTH)),
            _const_spec((S5_WIDTH, S5_WIDTH)), _const_spec((1, S5_WIDTH))],
        out_specs=out_specs + [pl.BlockSpec((SEQ_BLOCK, 2 * S5_FLAT), lambda b, c: (b, 0))],
        scratch_shapes=[pltpu.VMEM((SEQ_BLOCK * t_chunk, 2 * S5_FLAT), F32),
                        pltpu.VMEM((SEQ_BLOCK, 2 * S5_FLAT), F32)]
        + _io_scratch(l, t_chunk, S5_WIDTH, S5_WIDTH),
        compiler_params=_cparams("arbitrary", "arbitrary"),
        name="s5_mixer",
    )(u2.reshape(in_shapes[0]), s0, prm["s5_b"], prm["s5_c"], prm["s5_a"], prm["s5_d"], prm["s5_w_glu"],
      prm["s5_b_glu"])
    return y, s_new


def _mm(a, b, dims):
    return lax.dot_general(a.astype(BF16), b.astype(BF16), dims, preferred_element_type=F32)


def _cumsum_groups(x, group):
    pos = lax.broadcasted_iota(jnp.int32, (x.shape[0], 1), 0) % group
    s = 1
    while s < group:
        x = x + jnp.where(pos >= s, pltpu.roll(x, s, 0), 0.0)
        s *= 2
    return x


def _rw_kernel(p_ref, sh0_ref, s0_ref, mu_ref, vec_ref, w2_ref, a2_ref, g2_ref, *rest,
               t_chunk, l_valid, last_row, layer):
    (y_ref, shl_ref, sl_ref,
     pbuf_ref, s_ref, at_ref, bt_ref, kt_ref, rt_ref, bh_ref, kh_ref, v_ref, gc_ref, ys_ref,
     pad_ref, stage_ref) = rest[1 if layer else 0:]
    rows = SEQ_BLOCK * t_chunk
    sub = min(MAX_SUB, t_chunk)
    n_sub = t_chunk // sub

    @pl.when(pl.program_id(1) == 0)
    def _():
        pbuf_ref[...] = sh0_ref[...]
        _heads_to_tiles(s0_ref, s_ref, RW_PAIRS, transpose=False)

    p3 = _seq_block_load(p_ref, pad_ref)
    first = lax.broadcasted_iota(jnp.int32, (1, t_chunk, 1), 1) == 0
    prev3 = jnp.where(first, pbuf_ref[...], pltpu.roll(p3, 1, 1))
    last = p3[:, last_row:last_row + 1, :]
    pbuf_ref[...] = last
    shl_ref[...] = last

    xm = (p3 + (prev3 - p3) * mu_ref[...]).reshape(rows, RW_COLS)
    r = xm[:, 0:RW_WIDTH]
    k = xm[:, RW_WIDTH:2 * RW_WIDTH]
    v = xm[:, 2 * RW_WIDTH:3 * RW_WIDTH]
    lora = xm[:, 3 * RW_WIDTH:RW_COLS]
    w0, a0, k_k, k_a = vec_ref[0:1, :], vec_ref[1:2, :], vec_ref[2:3, :], vec_ref[3:4, :]
    r_k, ln_w, ln_b = vec_ref[4:5, :], vec_ref[5:6, :], vec_ref[6:7, :]

    w = -_softplus(-(w0 + _dot(jnp.tanh(lora).astype(BF16), w2_ref[...]))) - 0.5
    log_decay = -jnp.exp(w)
    a = jax.nn.sigmoid(a0 + _dot(lora.astype(BF16), a2_ref[...]))
    g = _dot(jax.nn.sigmoid(lora).astype(BF16), g2_ref[...])

    ones = _head_ones()
    kk = k * k_k
    kk = kk * lax.rsqrt(jnp.maximum(_head_sum(kk * kk, ones), 1e-24))
    kt = k * (1.0 + (a - 1.0) * k_a)
    kka = kk * a
    if l_valid < t_chunk:
        live = lax.broadcasted_iota(jnp.int32, (rows, 1), 0) % t_chunk < l_valid
        log_decay = jnp.where(live, log_decay, 0.0)
        kk = jnp.where(live, kk, 0.0)
        kka = jnp.where(live, kka, 0.0)
        kt_live = jnp.where(live, kt, 0.0)
    else:
        kt_live = kt

    cum = _cumsum_groups(log_decay, sub)
    shape_g = (rows // sub, sub, RW_WIDTH)
    cum_end = jnp.broadcast_to(cum.reshape(shape_g)[:, sub - 1:sub, :], shape_g).reshape(rows, RW_WIDTH)
    g_in = jnp.exp(cum)
    g_inv = jnp.exp(-cum)
    g_ex = jnp.exp(cum - log_decay)
    g_out = jnp.exp(cum_end - cum)
    shape3 = (SEQ_BLOCK, t_chunk, RW_WIDTH)
    at_ref[...] = (-kk * g_ex).reshape(shape3)
    bt_ref[...] = (kka * g_inv).reshape(shape3)
    kt_ref[...] = (kt_live * g_inv).reshape(shape3)
    rt_ref[...] = (r * g_in).reshape(shape3)
    bh_ref[...] = (kka * g_out).reshape(shape3)
    kh_ref[...] = (kt_live * g_out).reshape(shape3)
    v_ref[...] = v.reshape(shape3)
    gc_ref[...] = jnp.exp(cum_end).reshape(shape3)

    m_rows = SEQ_BLOCK * sub
    ri = lax.broadcasted_iota(jnp.int32, (m_rows, m_rows), 0)
    ci = lax.broadcasted_iota(jnp.int32, (m_rows, m_rows), 1)
    same_seq = (ri // sub) == (ci // sub)
    before = same_seq & ((ci % sub) < (ri % sub))
    upto = same_seq & ((ci % sub) <= (ri % sub))
    eye = (ri == ci).astype(F32)
    pair_mask = (ri // 2) == (ci // 2)
    level_masks = []
    s = 2
    while s < sub:
        level_masks.append(((ri // (2 * s)) == (ci // (2 * s))) & ((ri // s) != (ci // s)))
        s *= 2
    lane = lax.broadcasted_iota(jnp.int32, (m_rows, LANES), 1)
    head_mask = [lane < RW_HEAD, lane >= RW_HEAD]
    wide = (m_rows, SEQ_BLOCK * LANES)
    own_rows = (lax.broadcasted_iota(jnp.int32, wide, 1) // LANES
                == lax.broadcasted_iota(jnp.int32, wide, 0) // sub)
    tiles = (LANES, SEQ_BLOCK * LANES)
    same_head = (lax.broadcasted_iota(jnp.int32, tiles, 0) // RW_HEAD
                 == (lax.broadcasted_iota(jnp.int32, tiles, 1) % LANES) // RW_HEAD)
    group = max(g for g in (1, 2, 4) if n_sub % g == 0)
    units = [(j, hp) for j in range(group) for hp in range(RW_PAIRS)]
    heads = [(u, h) for u in range(len(units)) for h in range(HEAD_PAIR)]

    def sub_chunk(c, carry):
        rs = [pl.ds(pl.multiple_of((c * group + j) * sub, sub), sub) for j in range(group)]

        def ld(ref, j, hp):
            return ref[:, rs[j], pl.ds(hp * LANES, LANES)].reshape(m_rows, LANES)

        at, bt, ktl, rt, bh, kh, vv, gcv = ([ld(ref, j, hp) for j, hp in units]
                                            for ref in (at_ref, bt_ref, kt_ref, rt_ref, bh_ref, kh_ref,
                                                        v_ref, gc_ref))
        state = [[s_ref[hp, n] for n in range(SEQ_BLOCK)] for hp in range(RW_PAIRS)]
        zero = jnp.zeros((m_rows, LANES), F32)
        msk = lambda x, h: jnp.where(head_mask[h], x, zero)

        gram = [_mm(jnp.concatenate([msk(at[hp], 0), msk(rt[hp], 0), msk(at[hp], 1), msk(rt[hp], 1)], axis=0),
                    jnp.concatenate([bt[hp], ktl[hp]], axis=0), _NT)
                for hp in range(len(units))]

        def quad(hp, h, row, col, keep):
            blk = gram[hp][(2 * h + row) * m_rows:(2 * h + row + 1) * m_rows, col * m_rows:(col + 1) * m_rows]
            return jnp.where(keep, blk, 0.0)

        m_ab = [quad(hp, h, 0, 0, before) for hp, h in heads]
        m_ak = [quad(hp, h, 0, 1, before) for hp, h in heads]
        n_rb = [quad(hp, h, 1, 0, upto) for hp, h in heads]
        n_rk = [quad(hp, h, 1, 1, upto) for hp, h in heads]
        tinv = [eye + jnp.where(pair_mask, m, 0.0) for m in m_ab]
        for lm in level_masks:
            prod = [_mm(jnp.where(lm, m, 0.0), t, _NN) for m, t in zip(m_ab, tinv)]
            tinv = [t + _mm(t, q, _NN) for t, q in zip(tinv, prod)]
        vh = [msk(vv[hp], h) for hp, h in heads]
        w1 = [_mm(m, x, _NN) for m, x in zip(m_ak, vh)]
        ap = [_mm(t, jnp.concatenate([msk(at[hp], h), w], axis=1), _NN)
              for t, (hp, h), w in zip(tinv, heads, w1)]
        nb = [_mm(m, x, _NN) for m, x in zip(n_rb, ap)]
        nv = [_mm(m, x, _NN) for m, x in zip(n_rk, vh)]

        def per_seq(x):
            return jnp.where(own_rows, jnp.concatenate([x] * SEQ_BLOCK, axis=1), 0.0)

        pairs = range(len(units))
        both = lambda xs, hp, cols: xs[HEAD_PAIR * hp][:, cols] + xs[HEAD_PAIR * hp + 1][:, cols]
        lo, hi = slice(0, LANES), slice(LANES, 2 * LANES)
        bh_x = [per_seq(bh[hp]) for hp in pairs]
        kh_x = [per_seq(kh[hp]) for hp in pairs]
        gam = [jnp.where(same_head, _mm(both(ap, hp, lo).T, bh_x[hp], _NN), 0.0) for hp in pairs]
        u = [jnp.where(same_head, _mm(jnp.concatenate([both(ap, hp, hi).T, vv[hp].T], axis=1),
                                      jnp.concatenate([bh_x[hp], kh_x[hp]], axis=0), _NN), 0.0)
             for hp in pairs]
        for hp, (j, pair) in enumerate(units):
            r_hat = rt[hp] + both(nb, hp, lo)
            y_zero = both(nb, hp, hi) + nv[HEAD_PAIR * hp] + nv[HEAD_PAIR * hp + 1]
            cols = pl.ds(pair * LANES, LANES)
            for n in range(SEQ_BLOCK):
                q = slice(n * sub, (n + 1) * sub)
                blk = slice(n * LANES, (n + 1) * LANES)
                st = state[pair][n]
                ys_ref[n, rs[j], cols] = _mm(r_hat[q], st, _NT) + y_zero[q]
                state[pair][n] = (st * gcv[hp][n * sub:n * sub + 1, :]
                                  + _mm(st, gam[hp][:, blk], _NN) + u[hp][:, blk])
        for pair in range(RW_PAIRS):
            for n in range(SEQ_BLOCK):
                s_ref[pair, n] = state[pair][n]
        return carry

    lax.fori_loop(0, n_sub // group, sub_chunk, 0)

    @pl.when(pl.program_id(1) == pl.num_programs(1) - 1)
    def _():
        _tiles_to_heads(s_ref, sl_ref, RW_PAIRS, transpose=False)

    y = ys_ref[...].reshape(rows, RW_WIDTH)
    inv = 1.0 / RW_HEAD
    mean = _head_sum(y, ones) * inv
    yc = y - mean
    var = _head_sum(yc * yc, ones, split=False) * inv
    y = yc * lax.rsqrt(var + RW_GN_EPS) * ln_w + ln_b
    bonus = _head_sum(r * kt * r_k, ones, split=False) * v
    _seq_block_store(y_ref, stage_ref, ((y + bonus) * g).reshape(shape3))


def _layer_state_specs(state, layer):
    tail = state.shape[1:]
    zeros = (0,) * len(tail)
    read = pl.BlockSpec((SEQ_BLOCK,) + tail, lambda b, c: (b,) + zeros)
    write = pl.BlockSpec((None, SEQ_BLOCK) + tail, lambda b, c: (layer, b) + zeros)
    return read, write, ([pl.BlockSpec(memory_space=pl.ANY)] if layer else [])


def _rw_call(p2, n, l, shift0, state, layer, depth, earlier, prm, t_chunk):
    assert t_chunk % min(MAX_SUB, t_chunk) == 0
    grid, in_specs, out_specs, in_shapes, out_shapes = _seq_specs(n, l, t_chunk, [RW_COLS], [RW_WIDTH])
    valid_in_chunk = min(t_chunk, l)
    kern = functools.partial(_rw_kernel, t_chunk=t_chunk, l_valid=valid_in_chunk, last_row=valid_in_chunk - 1,
                             layer=layer)
    st_in, st_out, st_earlier = _layer_state_specs(state, layer)
    sh_spec = pl.BlockSpec((SEQ_BLOCK, 1, RW_COLS), lambda b, c: (b, 0, 0))
    q_scr = pltpu.VMEM((SEQ_BLOCK, t_chunk, RW_WIDTH), F32)
    return pl.pallas_call(
        kern,
        out_shape=[jax.ShapeDtypeStruct(out_shapes[0], _branch_dtype(out_shapes[0])),
                   jax.ShapeDtypeStruct((n, 1, RW_COLS), F32),
                   jax.ShapeDtypeStruct((depth,) + state.shape, F32)],
        grid=grid,
        in_specs=in_specs + [sh_spec, st_in,
                             _const_spec((1, RW_COLS)), _const_spec((SUBLANES, RW_WIDTH)),
                             _const_spec((RW_LORA, RW_WIDTH)), _const_spec((RW_LORA, RW_WIDTH)),
                             _const_spec((RW_LORA, RW_WIDTH))] + st_earlier,
        out_specs=out_specs + [sh_spec, st_out],
        scratch_shapes=[pltpu.VMEM((SEQ_BLOCK, 1, RW_COLS), F32),
                        pltpu.VMEM((RW_PAIRS, SEQ_BLOCK, LANES, LANES), F32)] + [q_scr] * 9
        + _io_scratch(l, t_chunk, RW_COLS, RW_WIDTH),
        input_output_aliases={8: 2} if layer else {},
        compiler_params=_cparams("arbitrary", "arbitrary"),
        name="rwkv7_mixer",
    )(p2.reshape(in_shapes[0]), shift0, state, prm["rw_mu"], prm["rw_vec"], prm["rw_w2"], prm["rw_a2"],
      prm["rw_g2"], *([earlier] if layer else []))


def _hg_kernel(z_ref, s0_ref, lower_ref, ng_ref, *rest, t_chunk, l_valid, layer):
    (y_ref, sl_ref, s_ref, qs_ref, q_ref, k_ref, kh_ref, v_ref, b_ref, gc_ref, ys_ref, pad_ref,
     stage_ref) = rest[1 if layer else 0:]
    rows = SEQ_BLOCK * t_chunk
    sub = min(MAX_SUB, t_chunk)
    n_sub = t_chunk // sub

    @pl.when(pl.program_id(1) == 0)
    def _():
        _heads_to_tiles(s0_ref, s_ref, HG_PAIRS, transpose=True)

    z = _seq_block_load(z_ref, pad_ref).reshape(rows, 4 * HG_WIDTH)
    q = jax.nn.silu(z[:, 0:HG_WIDTH])
    f = z[:, HG_WIDTH:2 * HG_WIDTH]
    i = z[:, 2 * HG_WIDTH:3 * HG_WIDTH]
    og = z[:, 3 * HG_WIDTH:4 * HG_WIDTH]
    lower = lower_ref[...]
    fgate = lower + (1.0 - lower) * jax.nn.sigmoid(f)
    log_f = jnp.log(jnp.maximum(fgate, HG_GATE_FLOOR))
    k = 1.0 - fgate
    if l_valid < t_chunk:
        live = lax.broadcasted_iota(jnp.int32, (rows, 1), 0) % t_chunk < l_valid
        log_f = jnp.where(live, log_f, 0.0)
        k = jnp.where(live, k, 0.0)
    cum = _cumsum_groups(log_f, sub)
    shape_g = (rows // sub, sub, HG_WIDTH)
    cum_end = jnp.broadcast_to(cum.reshape(shape_g)[:, sub - 1:sub, :], shape_g).reshape(rows, HG_WIDTH)
    shape3 = (SEQ_BLOCK, t_chunk, HG_WIDTH)
    qs_ref[...] = (q * jnp.exp(cum)).reshape(shape3)
    q_ref[...] = q.reshape(shape3)
    k_ref[...] = k.reshape(shape3)
    kh_ref[...] = (k * jnp.exp(cum_end - cum)).reshape(shape3)
    v_ref[...] = i.reshape(shape3)
    b_ref[...] = cum.reshape(shape3)
    gc_ref[...] = jnp.exp(cum_end).reshape(shape3)

    m_rows = SEQ_BLOCK * sub
    ones = _head_ones()
    wide = (m_rows, SEQ_BLOCK * LANES)
    own_rows = (lax.broadcasted_iota(jnp.int32, wide, 1) // LANES
                == lax.broadcasted_iota(jnp.int32, wide, 0) // sub)
    tiles = (LANES, SEQ_BLOCK * LANES)
    same_head = (lax.broadcasted_iota(jnp.int32, tiles, 0) // HG_HEAD
                 == (lax.broadcasted_iota(jnp.int32, tiles, 1) % LANES) // HG_HEAD)
    step = lax.broadcasted_iota(jnp.int32, (sub, LANES), 0)

    def sub_chunk(c, carry):
        rs = pl.ds(pl.multiple_of(c * sub, sub), sub)
        for hp in range(HG_PAIRS):
            cols = pl.ds(hp * LANES, LANES)
            ld = lambda ref: ref[:, rs, cols].reshape(m_rows, LANES)
            qs, qq, kk, kh, vv, bb, gcv = (ld(qs_ref), ld(q_ref), ld(k_ref), ld(kh_ref), ld(v_ref),
                                           ld(b_ref), ld(gc_ref))
            state = [s_ref[hp, n] for n in range(SEQ_BLOCK)]
            kh_x = jnp.where(own_rows, jnp.concatenate([kh] * SEQ_BLOCK, axis=1), 0.0)
            u = jnp.where(same_head, _mm(vv.T, kh_x, _NN), 0.0)
            for n in range(SEQ_BLOCK):
                sl = slice(n * sub, (n + 1) * sub)
                bn, qn, kn, vn = bb[sl], qq[sl], kk[sl], vv[sl]
                prods = []
                for t in range(sub):
                    keep = step <= t
                    prods.append(jnp.where(keep, qn[t:t + 1, :] * kn * jnp.exp(bn[t:t + 1, :] - bn), 0.0))
                att = _dot(jnp.concatenate(prods, axis=0).astype(BF16), ones)
                o_rows = [jnp.sum(att[t * sub:(t + 1) * sub] * vn, axis=0, keepdims=True)
                          for t in range(sub)]
                ys_ref[n, rs, cols] = _mm(qs[sl], state[n], _NT) + jnp.concatenate(o_rows, axis=0)
                blk = slice(n * LANES, (n + 1) * LANES)
                s_ref[hp, n] = state[n] * gcv[n * sub:n * sub + 1, :] + u[:, blk]
        return carry

    lax.fori_loop(0, n_sub, sub_chunk, 0)

    @pl.when(pl.program_id(1) == pl.num_programs(1) - 1)
    def _():
        _tiles_to_heads(s_ref, sl_ref, HG_PAIRS, transpose=True)

    o = ys_ref[...].reshape(rows, HG_WIDTH)
    ms = _head_sum(o * o, ones, split=False) * (1.0 / HG_HEAD)
    o = o * lax.rsqrt(ms + RMS_EPS) * ng_ref[...] * jax.nn.sigmoid(og)
    _seq_block_store(y_ref, stage_ref, o.reshape(shape3))


def _hg_call(z2, n, l, state, layer, depth, earlier, lower, norm_g, t_chunk):
    assert t_chunk % min(MAX_SUB, t_chunk) == 0
    grid, in_specs, out_specs, in_shapes, out_shapes = _seq_specs(n, l, t_chunk, [4 * HG_WIDTH], [HG_WIDTH])
    kern = functools.partial(_hg_kernel, t_chunk=t_chunk, l_valid=min(t_chunk, l), layer=layer)
    st_in, st_out, st_earlier = _layer_state_specs(state, layer)
    q_scr = pltpu.VMEM((SEQ_BLOCK, t_chunk, HG_WIDTH), F32)
    return pl.pallas_call(
        kern,
        out_shape=[jax.ShapeDtypeStruct(out_shapes[0], _branch_dtype(out_shapes[0])),
                   jax.ShapeDtypeStruct((depth,) + state.shape, F32)],
        grid=grid,
        in_specs=in_specs + [st_in, _const_spec((1, HG_WIDTH)), _const_spec((1, HG_WIDTH))] + st_earlier,
        out_specs=out_specs + [st_out],
        scratch_shapes=[pltpu.VMEM((HG_PAIRS, SEQ_BLOCK, LANES, LANES), F32)] + [q_scr] * 8
        + _io_scratch(l, t_chunk, 4 * HG_WIDTH, HG_WIDTH),
        input_output_aliases={4: 1} if layer else {},
        compiler_params=_cparams("arbitrary", "arbitrary"),
        name="hgrn2_mixer",
    )(z2.reshape(in_shapes[0]), state, lower.reshape(1, HG_WIDTH), norm_g.reshape(1, HG_WIDTH),
      *([earlier] if layer else []))


def _merge_ffn_kernel(*refs, l, tm, has_state, final_norm):
    (x_ref, ya_ref, yb_ref, yc_ref, gates_ref, gt1_ref, la_ref, lb_ref, lc_ref, wo_ref) = refs[:10]
    refs = refs[10:]
    if has_state:
        (g_ref, sh_ref, sc_ref, gt_ref, wup_ref, cw_ref, cb_ref, wdn_ref, fg_ref, st_ref,
         o_ref, tail_ref, h_ref, act_ref, hist_ref, buf_ref, old_ref) = refs
    else:
        (g_ref, sh_ref, sc_ref, gt_ref, wup_ref, cw_ref, cb_ref, wdn_ref, fg_ref,
         o_ref, tail_ref, h_ref, act_ref, hist_ref) = refs
    i = pl.program_id(0)
    n_chunks = D_FF // FF_CHUNK
    n_hist = CONV_W - 1

    if has_state:
        t_idx = lax.broadcasted_iota(jnp.int32, (tm, 1), 0) % l
        after = [t_idx >= j + 1 for j in range(n_hist)]
        old_ref[...] = jnp.zeros_like(old_ref)
        buf_ref[:, 0:SUBLANES, :] = jnp.zeros((2, SUBLANES, FF_CHUNK), F32)
        for s in range(tm // l):
            for j in range(n_hist):
                old_ref[j, s * l:s * l + j + 1, :] = st_ref[s, n_hist - 1 - j:n_hist, :]
    else:
        @pl.when(i % (l // tm) == 0)
        def _():
            hist_ref[...] = jnp.zeros_like(hist_ref)

    d = D_MODEL
    gates = _tile(gates_ref)
    lift = lambda y_ref, w_ref: _dot(_tile(y_ref)[...].astype(BF16), w_ref[...])
    merged = (gates[:, 0:d].astype(F32) * lift(ya_ref, la_ref)
              + gates[:, d:2 * d].astype(F32) * lift(yb_ref, lb_ref)
              + gates[:, 2 * d:3 * d].astype(F32) * lift(yc_ref, lc_ref))
    x = _tile(x_ref)[...] + gt1_ref[0] * _dot(merged.astype(BF16), wo_ref[...])
    h = _rms_modulate(x, g_ref[...], sh_ref[0], sc_ref[0])
    groups = tm // SUBLANES
    if not has_state:
        h = jnp.swapaxes(h.reshape(SUBLANES, groups, D_MODEL), 0, 1).reshape(tm, D_MODEL)
        row8 = lax.broadcasted_iota(jnp.int32, (SUBLANES, FF_CHUNK), 0)
    h_ref[...] = h.astype(BF16)

    def conv_half(c, half):
        col0 = half * D_FF + c * FF_CHUNK
        cols = slice(col0, col0 + FF_CHUNK)
        up = _dot(h_ref[...], wup_ref[:, cols])
        if has_state:
            buf_ref[half, SUBLANES:SUBLANES + tm, :] = up
            prev = [jnp.where(after[j], buf_ref[half, SUBLANES - 1 - j:SUBLANES - 1 - j + tm, :], 0.0)
                    + old_ref[j, :, cols] for j in range(n_hist)]
            for s in range(tm // l):
                r0 = SUBLANES + (s + 1) * l - n_hist
                tail_ref[s, :, cols] = buf_ref[half, r0:r0 + n_hist, :]
        else:
            late = [up[tm - (j + 1) * SUBLANES:tm - j * SUBLANES, :] for j in range(n_hist)]
            first = [jnp.where(row8 == 0, pltpu.roll(hist_ref[c, half, j], 1, 0), pltpu.roll(late[j], 1, 0))
                     for j in range(n_hist)]
            for j in range(n_hist):
                hist_ref[c, half, j] = late[j]
            prev = [jnp.concatenate([first[0], up[0:tm - SUBLANES, :]], axis=0),
                    jnp.concatenate([first[1], first[0], up[0:tm - 2 * SUBLANES, :]], axis=0)]
            tail_ref[0, 0:1, cols] = up[tm - SUBLANES - 1:tm - SUBLANES, :]
            tail_ref[0, 1:2, cols] = up[tm - 1:tm, :]
        return (cb_ref[:, cols] + cw_ref[0:1, cols] * prev[1] + cw_ref[1:2, cols] * prev[0]
                + cw_ref[2:3, cols] * up)

    for c in range(n_chunks):
        act = _gelu(conv_half(c, 0)) * conv_half(c, 1)
        act_ref[:, c * FF_CHUNK:(c + 1) * FF_CHUNK] = act.astype(BF16)
    acc = _dot(act_ref[...], wdn_ref[...])
    if not has_state:
        acc = jnp.swapaxes(acc.reshape(groups, SUBLANES, D_MODEL), 0, 1).reshape(tm, D_MODEL)
    out = x + gt_ref[0] * acc
    if final_norm:
        ms = jnp.mean(out * out, axis=-1, keepdims=True)
        out = out * lax.rsqrt(ms + RMS_EPS) * fg_ref[...]
    _tile(o_ref)[...] = out


def _merge_ffn_call(x, n, l, ya, yb, yc, gates, gate1, g, shift, scale, gate2, prm, final_g, conv_state, tm,
                    final_norm):
    assert CONV_W == 3
    d = x.shape[-1]
    has_state = conv_state is not None
    g1_arr, g1_spec = _mod_operand(gate1, l, tm)
    sh_arr, sh_spec = _mod_operand(shift, l, tm)
    sc_arr, sc_spec = _mod_operand(scale, l, tm)
    gt_arr, gt_spec = _mod_operand(gate2, l, tm)
    shape, x_spec = _tok_layout(n, l, tm, d)
    row_spec = lambda w: _tok_layout(n, l, tm, w)[1]
    stacks, layer = prm["stacks"], prm["layer"]
    w_spec = lambda name: _layer_weight_spec(stacks[name], layer)
    in_specs = [x_spec, row_spec(S5_WIDTH), row_spec(RW_WIDTH), row_spec(HG_WIDTH), row_spec(N_BRANCH * d),
                g1_spec, w_spec("w_lift_a"), w_spec("w_lift_b"), w_spec("w_lift_c"), w_spec("w_out"),
                _const_spec((1, d)), sh_spec, sc_spec, gt_spec,
                w_spec("w_up"), _const_spec((CONV_W, 2 * D_FF)), _const_spec((1, 2 * D_FF)),
                w_spec("w_down"), _const_spec((1, d))]
    operands = [x, ya, yb, yc, gates, g1_arr, stacks["w_lift_a"], stacks["w_lift_b"], stacks["w_lift_c"],
                stacks["w_out"], g.reshape(1, d), sh_arr, sc_arr, gt_arr, stacks["w_up"], prm["conv_w"],
                prm["conv_b"], stacks["w_down"], final_g.reshape(1, d)]
    scratch = [pltpu.VMEM((tm, d), BF16), pltpu.VMEM((tm, D_FF), BF16),
               pltpu.VMEM((D_FF // FF_CHUNK, 2, CONV_W - 1, SUBLANES, FF_CHUNK), F32)]
    if has_state:
        assert tm % l == 0 and l >= CONV_W - 1
        block = (tm // l, CONV_W - 1, 2 * D_FF)
        in_specs.append(pl.BlockSpec((None,) + block, lambda i: (layer, i, 0, 0)))
        operands.append(conv_state)
        tail_spec = pl.BlockSpec(block, lambda i: (i, 0, 0))
        scratch += [pltpu.VMEM((2, tm + SUBLANES, FF_CHUNK), F32), pltpu.VMEM((CONV_W - 1, tm, 2 * D_FF), F32)]
    else:
        assert l % tm == 0
        per = l // tm
        tail_spec = pl.BlockSpec((1, CONV_W - 1, 2 * D_FF), lambda i: (i // per, 0, 0))
    kern = functools.partial(_merge_ffn_kernel, l=l, tm=tm, has_state=has_state, final_norm=final_norm)
    return pl.pallas_call(
        kern,
        out_shape=[jax.ShapeDtypeStruct(shape, F32), jax.ShapeDtypeStruct((n, CONV_W - 1, 2 * D_FF), F32)],
        grid=(n * l // tm,),
        in_specs=in_specs,
        out_specs=[x_spec, tail_spec],
        scratch_shapes=scratch,
        compiler_params=_cparams("arbitrary"),
        name="merge_conv_ffn",
    )(*operands)


def _prepare_layer(p):
    out = {}
    w_in = p["w_in"]
    c1 = S5_WIDTH
    c2 = c1 + RW_COLS
    c3 = c2 + 4 * HG_WIDTH
    out["w_in_a"] = w_in[:, :c1].astype(BF16)
    out["w_in_b"] = w_in[:, c1:c2].astype(BF16)
    out["w_in_c"] = w_in[:, c2:c3].astype(BF16)
    out["w_in_g"] = w_in[:, c3:].astype(BF16)
    out["conv_w"] = p["conv_w"]
    out["conv_b"] = p["conv_b"].reshape(1, 2 * D_FF)

    lr = p["s5_lambda_re"]
    li = p["s5_lambda_im"]
    dt = jnp.exp(p["s5_log_dt"])[:, None]
    mag = jnp.exp(lr * dt)
    ar = mag * jnp.cos(li * dt)
    ai = mag * jnp.sin(li * dt)
    den = lr * lr + li * li
    zr = ((ar - 1.0) * lr + ai * li) / den
    zi = (ai * lr - (ar - 1.0) * li) / den
    bbr = zr[..., None] * p["s5_b_re"] - zi[..., None] * p["s5_b_im"]
    bbi = zr[..., None] * p["s5_b_im"] + zi[..., None] * p["s5_b_re"]
    eye = jnp.eye(S5_GROUPS, dtype=F32)
    bmat = jnp.einsum("gh,rgpc->gcrhp", eye, jnp.stack([bbr, bbi])).reshape(S5_WIDTH, 2 * S5_FLAT)
    cmat = jnp.einsum("hg,rgcp->rhpgc", eye, jnp.stack([p["s5_c_re"], -p["s5_c_im"]])).reshape(
        2 * S5_FLAT, S5_WIDTH)
    out["s5_b"] = bmat.astype(BF16)
    out["s5_c"] = cmat.astype(BF16)
    out["s5_a"] = jnp.stack([ar.reshape(S5_FLAT), ai.reshape(S5_FLAT)])
    out["s5_d"] = p["s5_d"].reshape(1, S5_WIDTH)
    out["s5_w_glu"] = p["s5_w_glu"].astype(BF16)
    out["s5_b_glu"] = p["s5_b_glu"].reshape(1, S5_WIDTH)

    out["rw_mu"] = p["rw_mu"].reshape(1, RW_COLS)
    out["rw_vec"] = jnp.stack([p["rw_w0"], p["rw_a0"], p["rw_k_k"], p["rw_k_a"],
                               p["rw_r_k"].reshape(RW_WIDTH), p["rw_ln_w"], p["rw_ln_b"],
                               jnp.zeros((RW_WIDTH,), F32)])
    zw = jnp.zeros((RW_LORA, RW_WIDTH), F32)
    out["rw_w2"] = zw.at[0:RW_DECAY_LORA].set(p["rw_w2"]).astype(BF16)
    out["rw_a2"] = zw.at[RW_DECAY_LORA:RW_DECAY_LORA + RW_AAA_LORA].set(p["rw_a2"]).astype(BF16)
    out["rw_g2"] = zw.at[RW_DECAY_LORA + RW_AAA_LORA:].set(p["rw_g2"]).astype(BF16)
    out["hg_norm"] = p["hg_norm"]
    out["g_mix"] = p["g_mix"]
    out["g_ffn"] = p["g_ffn"]
    out["b_ada"] = p["b_ada"]
    return out


def _chunk_for(l, sub, t_chunk):
    return t_chunk if l % t_chunk == 0 else -(-l // sub) * sub


def _run_trunk(x, mods, st_s5, st_shift, st_rw, st_hg, st_conv, lower, final_g, prms, t_chunk):
    n, l, d = x.shape
    rows = n * l
    tm = min(ROW_TILE, rows)
    depth = len(prms)
    assert (l % tm == 0) == (l % t_chunk == 0)
    h = x.reshape(_tok_layout(n, l, tm, d)[0])
    out_s5, out_shift, out_conv = [], [], []
    rw_new = hg_new = None
    for layer in range(depth):
        prm = prms[layer]
        sh1, sc1, gt1, sh2, sc2, gt2 = jnp.split(mods[layer], 6, axis=-1)
        za, zb, zc, gates = _in_proj_call(h, n, l, prm["g_mix"], sh1, sc1, prm["w_in_a"], prm["w_in_b"],
                                          prm["w_in_c"], prm["w_in_g"], tm)

        s5_in = jnp.concatenate([st_s5[layer][..., 0].reshape(n, S5_FLAT),
                                 st_s5[layer][..., 1].reshape(n, S5_FLAT)], axis=1)
        s5_chunk = S5_CHUNK_FACTOR * t_chunk if l % (S5_CHUNK_FACTOR * t_chunk) == 0 else t_chunk
        ya, s5_new = _s5_call(za, n, l, s5_in, prm, _chunk_for(l, SUBLANES, s5_chunk))
        yb, shift_new, rw_new = _rw_call(zb, n, l, st_shift[layer].reshape(n, 1, RW_COLS), st_rw[layer], layer, depth, rw_new,
                                         prm, _chunk_for(l, SUBLANES, t_chunk))
        yc, hg_new = _hg_call(zc, n, l, st_hg[layer], layer, depth, hg_new, lower[layer], prm["hg_norm"],
                              _chunk_for(l, SUBLANES, t_chunk))

        ffn_tm = tm if st_conv is None else min(tm, 128)
        h, conv_new = _merge_ffn_call(h, n, l, ya, yb, yc, gates, gt1, prm["g_ffn"], sh2, sc2, gt2, prm, final_g,
                                      st_conv, ffn_tm,
                                      final_norm=(layer == depth - 1))
        out_s5.append(jnp.stack([s5_new[:, :S5_FLAT].reshape(n, S5_GROUPS, S5_STATE),
                                 s5_new[:, S5_FLAT:].reshape(n, S5_GROUPS, S5_STATE)], axis=-1))
        out_shift.append(shift_new.reshape(n, RW_COLS))
        out_conv.append(conv_new)
    y = h.reshape(n, l, d).astype(x.dtype)
    return y, (jnp.stack(out_s5), jnp.stack(out_shift), rw_new, hg_new, jnp.stack(out_conv))


def kernel(x_prompt, x_sample, c_prompt, c_sample, state_s5, state_rwkv_shift, state_rwkv, state_hgrn, state_ffn_conv, w_ada, b_ada, g_mix, g_ffn, w_in, s5_lambda_re, s5_lambda_im, s5_log_dt, s5_b_re, s5_b_im, s5_c_re, s5_c_im, s5_d, s5_w_glu, s5_b_glu, rw_mu, rw_w0, rw_w2, rw_a0, rw_a2, rw_g2, rw_k_k, rw_k_a, rw_r_k, rw_ln_w, rw_ln_b, hg_lb, hg_norm, w_lift_a, w_lift_b, w_lift_c, w_out, w_up, conv_w, conv_b, w_down, final_g):
    per_layer = {
        "w_ada": w_ada, "b_ada": b_ada, "g_mix": g_mix, "g_ffn": g_ffn, "w_in": w_in,
        "s5_lambda_re": s5_lambda_re, "s5_lambda_im": s5_lambda_im, "s5_log_dt": s5_log_dt,
        "s5_b_re": s5_b_re, "s5_b_im": s5_b_im, "s5_c_re": s5_c_re, "s5_c_im": s5_c_im,
        "s5_d": s5_d, "s5_w_glu": s5_w_glu, "s5_b_glu": s5_b_glu,
        "rw_mu": rw_mu, "rw_w0": rw_w0, "rw_w2": rw_w2, "rw_a0": rw_a0, "rw_a2": rw_a2,
        "rw_g2": rw_g2, "rw_k_k": rw_k_k, "rw_k_a": rw_k_a, "rw_r_k": rw_r_k,
        "rw_ln_w": rw_ln_w, "rw_ln_b": rw_ln_b, "hg_norm": hg_norm,
        "w_lift_a": w_lift_a, "w_lift_b": w_lift_b, "w_lift_c": w_lift_c, "w_out": w_out,
        "w_up": w_up, "conv_w": conv_w, "conv_b": conv_b, "w_down": w_down,
    }
    depth = w_ada.shape[0]
    prms = [_prepare_layer({k: v[layer] for k, v in per_layer.items()}) for layer in range(depth)]

    lbp = jax.nn.softmax(hg_lb.astype(F32), axis=0)
    lower = jnp.cumsum(lbp, axis=0) - lbp[0]

    nb = x_prompt.shape[0]
    ns = x_sample.shape[0]
    c_all = jnp.concatenate([c_prompt, c_sample], axis=0).astype(F32)
    stacks = {name: per_layer[name].astype(BF16)
              for name in ("w_ada", "w_lift_a", "w_lift_b", "w_lift_c", "w_out", "w_up", "w_down")}
    for layer in range(depth):
        prms[layer]["stacks"] = stacks
        prms[layer]["layer"] = layer
    mods = [_ada_call(c_all, stacks["w_ada"], layer, prms[layer]["b_ada"]) for layer in range(depth)]
    mods_p = [m[:nb] for m in mods]
    mods_s = [m[nb:] for m in mods]

    z_s5 = jnp.zeros((depth, nb) + state_s5.shape[2:], F32)
    z_shift = jnp.zeros((depth, nb) + state_rwkv_shift.shape[2:], F32)
    z_rw = jnp.zeros((depth, nb) + state_rwkv.shape[2:], F32)
    z_hg = jnp.zeros((depth, nb) + state_hgrn.shape[2:], F32)

    y_prompt, (s5_p, shift_p, rw_p, hg_p, conv_p) = _run_trunk(
        x_prompt, mods_p, z_s5, z_shift, z_rw, z_hg, None, lower, final_g, prms, t_chunk=64)
    y_sample, (s5_s, shift_s, rw_s, hg_s, conv_s) = _run_trunk(
        x_sample, mods_s, state_s5, state_rwkv_shift, state_rwkv, state_hgrn, state_ffn_conv,
        lower, final_g, prms, t_chunk=64)
    return (y_prompt, y_sample, s5_p, shift_p, rw_p, hg_p, conv_p, s5_s, shift_s, rw_s, hg_s, conv_s)
```

```python
import functools
import math

import jax
import jax.numpy as jnp
from jax import lax
from jax.experimental import pallas as pl
from jax.experimental.pallas import tpu as pltpu

F32 = jnp.float32
BF16 = jnp.bfloat16

D_MODEL = 1024
S5_WIDTH = D_MODEL // 4
S5_GROUP = 16
S5_GROUPS = S5_WIDTH // S5_GROUP
S5_STATE = 64
S5_FLAT = S5_GROUPS * S5_STATE
RW_WIDTH = D_MODEL // 2
RW_HEAD = 64
RW_HEADS = RW_WIDTH // RW_HEAD
RW_DECAY_LORA = 32
RW_AAA_LORA = 32
RW_GATE_LORA = 64
RW_LORA = RW_DECAY_LORA + RW_AAA_LORA + RW_GATE_LORA
RW_COLS = 3 * RW_WIDTH + RW_LORA
RW_GN_EPS = 1e-5 * RW_HEAD
HG_WIDTH = D_MODEL // 4
HG_HEAD = 64
HG_HEADS = HG_WIDTH // HG_HEAD
HG_GATE_FLOOR = 1e-30
N_BRANCH = 3
D_FF = 256 * ((8 * D_MODEL // 3 + 255) // 256)
CONV_W = 3
RMS_EPS = 1e-6

LANES = 128
SUBLANES = 8
MXU_COLS = 256
VMEM_LIMIT_BYTES = 56 * 1024 * 1024

SEQ_BLOCK = SUBLANES
HEAD_PAIR = LANES // RW_HEAD
ROW_TILE = 512
FF_CHUNK = 256
IN_PROJ_CHUNK = 512
S5_CHUNK_FACTOR = 2
MAX_SUB = 16
RW_PAIRS = RW_HEADS // HEAD_PAIR
HG_PAIRS = HG_HEADS // HEAD_PAIR

_NN = (((1,), (0,)), ((), ()))
_NT = (((1,), (1,)), ((), ()))


def _cparams(*sem):
    return pltpu.CompilerParams(dimension_semantics=sem, vmem_limit_bytes=VMEM_LIMIT_BYTES)


def _const_spec(shape):
    nd = len(shape)
    return pl.BlockSpec(shape, lambda *_: (0,) * nd, pipeline_mode=pl.Buffered(1))


def _layer_weight_spec(stack, layer):
    shape = stack.shape[1:]
    zeros = (0,) * len(shape)
    return pl.BlockSpec((None,) + shape, lambda *_: (layer,) + zeros, pipeline_mode=pl.Buffered(1))


def _dot(a, b):
    return jnp.dot(a, b, preferred_element_type=F32)


def _split_bf16(x):
    hi = x.astype(BF16)
    lo = (x - hi.astype(F32)).astype(BF16)
    return hi, lo


def _head_ones():
    r = lax.broadcasted_iota(jnp.int32, (LANES, LANES), 0) // RW_HEAD
    c = lax.broadcasted_iota(jnp.int32, (LANES, LANES), 1) // RW_HEAD
    return (r == c).astype(BF16)


def _head_sum(x, ones, split=True):
    hi, lo = _split_bf16(x) if split else (x.astype(BF16), None)
    tiles = []
    for c in range(0, x.shape[1], LANES):
        t = _dot(hi[:, c:c + LANES], ones)
        tiles.append(t + _dot(lo[:, c:c + LANES], ones) if split else t)
    return tiles[0] if len(tiles) == 1 else jnp.concatenate(tiles, axis=1)


def _softplus(x):
    return jnp.maximum(x, 0.0) + jnp.log(1.0 + jnp.exp(-jnp.abs(x)))


def _gelu(x):
    c = 2.0 * math.sqrt(2.0 / math.pi)
    return x * jax.nn.sigmoid(x * (c + (c * 0.044715) * (x * x)))


def _rms_modulate(x, g, shift, scale):
    ms = jnp.mean(x * x, axis=-1, keepdims=True)
    return x * lax.rsqrt(ms + RMS_EPS) * g * (1.0 + scale) + shift


def _seq_specs(n, l, t_chunk, widths_in, widths_out):
    nb = n // SEQ_BLOCK
    if l % t_chunk == 0:
        spec = lambda w: pl.BlockSpec((SEQ_BLOCK, t_chunk, w), lambda b, c: (b, c, 0))
        shape = lambda w: (n, l, w)
        nt = l // t_chunk
    else:
        assert l < t_chunk
        spec = lambda w: pl.BlockSpec((SEQ_BLOCK * l, w), lambda b, c: (b, 0))
        shape = lambda w: (n * l, w)
        nt = 1
    return ((nb, nt), [spec(w) for w in widths_in], [spec(w) for w in widths_out],
            [shape(w) for w in widths_in], [shape(w) for w in widths_out])


def _branch_dtype(shape):
    return BF16 if len(shape) == 3 else F32


def _seq_block_load(z_ref, pad_ref):
    if len(z_ref.shape) == 3:
        return z_ref[...].astype(F32)
    l = z_ref.shape[0] // SEQ_BLOCK
    pad_ref[...] = jnp.zeros_like(pad_ref)
    for n in range(SEQ_BLOCK):
        pad_ref[n, 0:l, :] = z_ref[n * l:(n + 1) * l, :]
    return pad_ref[...]


def _seq_block_store(y_ref, stage_ref, y3):
    if len(y_ref.shape) == 3:
        y_ref[...] = y3.astype(y_ref.dtype)
        return
    l = y_ref.shape[0] // SEQ_BLOCK
    stage_ref[...] = y3
    for n in range(SEQ_BLOCK):
        y_ref[n * l:(n + 1) * l, :] = stage_ref[n, 0:l, :]


def _heads_to_tiles(s_ref, tile_ref, n_pairs, transpose):
    zero = jnp.zeros((RW_HEAD, RW_HEAD), F32)
    for n in range(SEQ_BLOCK):
        for hp in range(n_pairs):
            a, b = s_ref[n, HEAD_PAIR * hp], s_ref[n, HEAD_PAIR * hp + 1]
            if transpose:
                a, b = a.T, b.T
            tile_ref[hp, n] = jnp.concatenate([jnp.concatenate([a, zero], axis=1),
                                               jnp.concatenate([zero, b], axis=1)], axis=0)


def _tiles_to_heads(tile_ref, s_ref, n_pairs, transpose):
    for n in range(SEQ_BLOCK):
        for hp in range(n_pairs):
            t = tile_ref[hp, n]
            a, b = t[0:RW_HEAD, 0:RW_HEAD], t[RW_HEAD:LANES, RW_HEAD:LANES]
            if transpose:
                a, b = a.T, b.T
            s_ref[n, HEAD_PAIR * hp] = a
            s_ref[n, HEAD_PAIR * hp + 1] = b


def _ada_kernel(c_ref, w_ref, b_ref, o_ref):
    c = c_ref[...]
    o_ref[...] = _dot(jax.nn.silu(c).astype(BF16), w_ref[...]) + b_ref[...]


def _ada_call(c, w_stack, layer, b):
    n, d = c.shape
    cols = w_stack.shape[2]
    tn = cols // 4
    return pl.pallas_call(
        _ada_kernel,
        out_shape=jax.ShapeDtypeStruct((n, cols), F32),
        grid=(cols // tn,),
        in_specs=[_const_spec((n, d)),
                  pl.BlockSpec((None, d, tn), lambda j: (layer, 0, j)),
                  pl.BlockSpec((1, tn), lambda j: (0, j))],
        out_specs=pl.BlockSpec((n, tn), lambda j: (0, j)),
        compiler_params=_cparams("arbitrary"),
        name="ada_mod",
    )(c, w_stack, b.reshape(1, cols))


def _mod_operand(m, l, tm):
    n, d = m.shape
    if l % tm == 0:
        per = l // tm
        return m.reshape(n, 1, d), pl.BlockSpec((1, 1, d), lambda i: (i // per, 0, 0))
    assert tm % l == 0
    return jnp.repeat(m, l, axis=0).reshape(1, n * l, d), pl.BlockSpec((1, tm, d), lambda i: (0, i, 0))


def _tok_layout(n, l, tm, w):
    if l % tm == 0:
        per = l // tm
        return (n, l, w), pl.BlockSpec((1, tm, w), lambda i: (i // per, i % per, 0))
    return (n * l, w), pl.BlockSpec((tm, w), lambda i: (i, 0))


def _tile(ref):
    return ref.at[0] if len(ref.shape) == 3 else ref


def _in_proj_kernel(x_ref, g_ref, sh_ref, sc_ref, wa_ref, wb_ref, wc_ref, wg_ref,
                    oa_ref, ob_ref, oc_ref, og_ref):
    h = _rms_modulate(_tile(x_ref)[...], g_ref[...], sh_ref[0], sc_ref[0]).astype(BF16)

    def project(w_ref, o_ref, fn=None):
        width = w_ref.shape[1]
        out = _tile(o_ref)
        for c0 in range(0, width, IN_PROJ_CHUNK):
            cols = slice(c0, min(c0 + IN_PROJ_CHUNK, width))
            z = _dot(h, w_ref[:, cols])
            out[:, cols] = (z if fn is None else fn(z)).astype(out.dtype)

    project(wa_ref, oa_ref)
    project(wb_ref, ob_ref)
    project(wc_ref, oc_ref)
    project(wg_ref, og_ref, jax.nn.sigmoid)


def _in_proj_call(x, n, l, g, shift, scale, w_a, w_b, w_c, w_g, tm):
    d = x.shape[-1]
    sh_arr, sh_spec = _mod_operand(shift, l, tm)
    sc_arr, sc_spec = _mod_operand(scale, l, tm)
    widths = [w.shape[1] for w in (w_a, w_b, w_c, w_g)]
    outs = [_tok_layout(n, l, tm, w) for w in widths]
    dtypes = [_branch_dtype(outs[0][0])] * 3 + [BF16]
    return pl.pallas_call(
        _in_proj_kernel,
        out_shape=[jax.ShapeDtypeStruct(shape, dt) for (shape, _), dt in zip(outs, dtypes)],
        grid=(n * l // tm,),
        in_specs=[_tok_layout(n, l, tm, d)[1], _const_spec((1, d)), sh_spec, sc_spec]
        + [_const_spec(w.shape) for w in (w_a, w_b, w_c, w_g)],
        out_specs=[spec for _, spec in outs],
        compiler_params=_cparams("arbitrary"),
        name="in_proj",
    )(x, g.reshape(1, d), sh_arr, sc_arr, w_a, w_b, w_c, w_g)


def _s5_kernel(u_ref, s0_ref, b_ref, c_ref, a_ref, d_ref, wg_ref, bg_ref,
               y_ref, sl_ref, e_ref, st_ref, pad_ref, stage_ref, *, t_chunk, n_steps):
    @pl.when(pl.program_id(1) == 0)
    def _():
        st_ref[...] = s0_ref[...]

    rows = SEQ_BLOCK * t_chunk
    ut = jnp.swapaxes(_seq_block_load(u_ref, pad_ref), 0, 1).reshape(rows, S5_WIDTH)
    ub = ut.astype(BF16)
    for c0 in range(0, 2 * S5_FLAT, MXU_COLS):
        ch0 = (c0 % S5_FLAT) // S5_STATE * S5_GROUP
        k0 = ch0 // LANES * LANES
        assert ch0 + MXU_COLS // S5_STATE * S5_GROUP <= k0 + LANES
        e_ref[:, c0:c0 + MXU_COLS] = _dot(ub[:, k0:k0 + LANES], b_ref[k0:k0 + LANES, c0:c0 + MXU_COLS])

    ar = jnp.broadcast_to(a_ref[0:1, :], (SEQ_BLOCK, S5_FLAT))
    ai = jnp.broadcast_to(a_ref[1:2, :], (SEQ_BLOCK, S5_FLAT))

    def step(t, carry):
        sr, si = carry
        r = pl.ds(pl.multiple_of(t * SEQ_BLOCK, SEQ_BLOCK), SEQ_BLOCK)
        nr = ar * sr - ai * si + e_ref[r, 0:S5_FLAT]
        ni = ar * si + ai * sr + e_ref[r, S5_FLAT:2 * S5_FLAT]
        e_ref[r, 0:S5_FLAT] = nr
        e_ref[r, S5_FLAT:2 * S5_FLAT] = ni
        return nr, ni

    sr, si = lax.fori_loop(0, n_steps, step, (st_ref[:, 0:S5_FLAT], st_ref[:, S5_FLAT:2 * S5_FLAT]),
                           unroll=4 if n_steps % 4 == 0 else 1)
    st_ref[:, 0:S5_FLAT] = sr
    st_ref[:, S5_FLAT:2 * S5_FLAT] = si
    sl_ref[...] = st_ref[...]

    y = _dot(e_ref[...].astype(BF16), c_ref[...]) + d_ref[...] * ut
    y = _gelu(y)
    y = y * jax.nn.sigmoid(_dot(y.astype(BF16), wg_ref[...]) + bg_ref[...])
    _seq_block_store(y_ref, stage_ref, jnp.swapaxes(y.reshape(t_chunk, SEQ_BLOCK, S5_WIDTH), 0, 1))


def _io_scratch(l, t_chunk, width_in, width_out):
    if l % t_chunk == 0:
        return [pltpu.VMEM((SUBLANES, LANES), F32)] * 2
    return [pltpu.VMEM((SEQ_BLOCK, t_chunk, width_in), F32), pltpu.VMEM((SEQ_BLOCK, t_chunk, width_out), F32)]


def _s5_call(u2, n, l, s0, prm, t_chunk):
    grid, in_specs, out_specs, in_shapes, out_shapes = _seq_specs(n, l, t_chunk, [S5_WIDTH], [S5_WIDTH])
    kern = functools.partial(_s5_kernel, t_chunk=t_chunk, n_steps=min(t_chunk, l))
    y, s_new = pl.pallas_call(
        kern,
        out_shape=[jax.ShapeDtypeStruct(out_shapes[0], _branch_dtype(out_shapes[0])),
                   jax.ShapeDtypeStruct((n, 2 * S5_FLAT), F32)],
        grid=grid,
        in_specs=in_specs + [
            pl.BlockSpec((SEQ_BLOCK, 2 * S5_FLAT), lambda b, c: (b, 0)),
            _const_spec((S5_WIDTH, 2 * S5_FLAT)), _const_spec((2 * S5_FLAT, S5_WIDTH)),
            _const_spec((2, S5_FLAT)), _const_spec((1, S5_WIDTH)),
            _const_spec((S5_WIDTH, S5_WIDTH)), _const_spec((1, S5_WIDTH))],
        out_specs=out_specs + [pl.BlockSpec((SEQ_BLOCK, 2 * S5_FLAT), lambda b, c: (b, 0))],
        scratch_shapes=[pltpu.VMEM((SEQ_BLOCK * t_chunk, 2 * S5_FLAT), F32),
                        pltpu.VMEM((SEQ_BLOCK, 2 * S5_FLAT), F32)]
        + _io_scratch(l, t_chunk, S5_WIDTH, S5_WIDTH),
        compiler_params=_cparams("arbitrary", "arbitrary"),
        name="s5_mixer",
    )(u2.reshape(in_shapes[0]), s0, prm["s5_b"], prm["s5_c"], prm["s5_a"], prm["s5_d"], prm["s5_w_glu"],
      prm["s5_b_glu"])
    return y, s_new


def _mm(a, b, dims):
    return lax.dot_general(a.astype(BF16), b.astype(BF16), dims, preferred_element_type=F32)


def _cumsum_groups(x, group):
    pos = lax.broadcasted_iota(jnp.int32, (x.shape[0], 1), 0) % group
    s = 1
    while s < group:
        x = x + jnp.where(pos >= s, pltpu.roll(x, s, 0), 0.0)
        s *= 2
    return x


def _rw_kernel(p_ref, sh0_ref, s0_ref, mu_ref, vec_ref, w2_ref, a2_ref, g2_ref, *rest,
               t_chunk, l_valid, last_row, layer):
    (y_ref, shl_ref, sl_ref,
     pbuf_ref, s_ref, at_ref, bt_ref, kt_ref, rt_ref, bh_ref, kh_ref, v_ref, gc_ref, ys_ref,
     pad_ref, stage_ref) = rest[1 if layer else 0:]
    rows = SEQ_BLOCK * t_chunk
    sub = min(MAX_SUB, t_chunk)
    n_sub = t_chunk // sub

    @pl.when(pl.program_id(1) == 0)
    def _():
        pbuf_ref[...] = sh0_ref[...]
        _heads_to_tiles(s0_ref, s_ref, RW_PAIRS, transpose=False)

    p3 = _seq_block_load(p_ref, pad_ref)
    first = lax.broadcasted_iota(jnp.int32, (1, t_chunk, 1), 1) == 0
    prev3 = jnp.where(first, pbuf_ref[...], pltpu.roll(p3, 1, 1))
    last = p3[:, last_row:last_row + 1, :]
    pbuf_ref[...] = last
    shl_ref[...] = last

    xm = (p3 + (prev3 - p3) * mu_ref[...]).reshape(rows, RW_COLS)
    r = xm[:, 0:RW_WIDTH]
    k = xm[:, RW_WIDTH:2 * RW_WIDTH]
    v = xm[:, 2 * RW_WIDTH:3 * RW_WIDTH]
    lora = xm[:, 3 * RW_WIDTH:RW_COLS]
    w0, a0, k_k, k_a = vec_ref[0:1, :], vec_ref[1:2, :], vec_ref[2:3, :], vec_ref[3:4, :]
    r_k, ln_w, ln_b = vec_ref[4:5, :], vec_ref[5:6, :], vec_ref[6:7, :]

    w = -_softplus(-(w0 + _dot(jnp.tanh(lora).astype(BF16), w2_ref[...]))) - 0.5
    log_decay = -jnp.exp(w)
    a = jax.nn.sigmoid(a0 + _dot(lora.astype(BF16), a2_ref[...]))
    g = _dot(jax.nn.sigmoid(lora).astype(BF16), g2_ref[...])

    ones = _head_ones()
    kk = k * k_k
    kk = kk * lax.rsqrt(jnp.maximum(_head_sum(kk * kk, ones), 1e-24))
    kt = k * (1.0 + (a - 1.0) * k_a)
    kka = kk * a
    if l_valid < t_chunk:
        live = lax.broadcasted_iota(jnp.int32, (rows, 1), 0) % t_chunk < l_valid
        log_decay = jnp.where(live, log_decay, 0.0)
        kk = jnp.where(live, kk, 0.0)
        kka = jnp.where(live, kka, 0.0)
        kt_live = jnp.where(live, kt, 0.0)
    else:
        kt_live = kt

    cum = _cumsum_groups(log_decay, sub)
    shape_g = (rows // sub, sub, RW_WIDTH)
    cum_end = jnp.broadcast_to(cum.reshape(shape_g)[:, sub - 1:sub, :], shape_g).reshape(rows, RW_WIDTH)
    g_in = jnp.exp(cum)
    g_inv = jnp.exp(-cum)
    g_ex = jnp.exp(cum - log_decay)
    g_out = jnp.exp(cum_end - cum)
    shape3 = (SEQ_BLOCK, t_chunk, RW_WIDTH)
    at_ref[...] = (-kk * g_ex).reshape(shape3)
    bt_ref[...] = (kka * g_inv).reshape(shape3)
    kt_ref[...] = (kt_live * g_inv).reshape(shape3)
    rt_ref[...] = (r * g_in).reshape(shape3)
    bh_ref[...] = (kka * g_out).reshape(shape3)
    kh_ref[...] = (kt_live * g_out).reshape(shape3)
    v_ref[...] = v.reshape(shape3)
    gc_ref[...] = jnp.exp(cum_end).reshape(shape3)

    m_rows = SEQ_BLOCK * sub
    ri = lax.broadcasted_iota(jnp.int32, (m_rows, m_rows), 0)
    ci = lax.broadcasted_iota(jnp.int32, (m_rows, m_rows), 1)
    same_seq = (ri // sub) == (ci // sub)
    before = same_seq & ((ci % sub) < (ri % sub))
    upto = same_seq & ((ci % sub) <= (ri % sub))
    eye = (ri == ci).astype(F32)
    pair_mask = (ri // 2) == (ci // 2)
    level_masks = []
    s = 2
    while s < sub:
        level_masks.append(((ri // (2 * s)) == (ci // (2 * s))) & ((ri // s) != (ci // s)))
        s *= 2
    lane = lax.broadcasted_iota(jnp.int32, (m_rows, LANES), 1)
    head_mask = [lane < RW_HEAD, lane >= RW_HEAD]
    wide = (m_rows, SEQ_BLOCK * LANES)
    own_rows = (lax.broadcasted_iota(jnp.int32, wide, 1) // LANES
                == lax.broadcasted_iota(jnp.int32, wide, 0) // sub)
    tiles = (LANES, SEQ_BLOCK * LANES)
    same_head = (lax.broadcasted_iota(jnp.int32, tiles, 0) // RW_HEAD
                 == (lax.broadcasted_iota(jnp.int32, tiles, 1) % LANES) // RW_HEAD)
    group = max(g for g in (1, 2, 4) if n_sub % g == 0)
    units = [(j, hp) for j in range(group) for hp in range(RW_PAIRS)]
    heads = [(u, h) for u in range(len(units)) for h in range(HEAD_PAIR)]

    def sub_chunk(c, carry):
        rs = [pl.ds(pl.multiple_of((c * group + j) * sub, sub), sub) for j in range(group)]

        def ld(ref, j, hp):
            return ref[:, rs[j], pl.ds(hp * LANES, LANES)].reshape(m_rows, LANES)

        at, bt, ktl, rt, bh, kh, vv, gcv = ([ld(ref, j, hp) for j, hp in units]
                                            for ref in (at_ref, bt_ref, kt_ref, rt_ref, bh_ref, kh_ref,
                                                        v_ref, gc_ref))
        state = [[s_ref[hp, n] for n in range(SEQ_BLOCK)] for hp in range(RW_PAIRS)]
        zero = jnp.zeros((m_rows, LANES), F32)
        msk = lambda x, h: jnp.where(head_mask[h], x, zero)

        gram = [_mm(jnp.concatenate([msk(at[hp], 0), msk(rt[hp], 0), msk(at[hp], 1), msk(rt[hp], 1)], axis=0),
                    jnp.concatenate([bt[hp], ktl[hp]], axis=0), _NT)
                for hp in range(len(units))]

        def quad(hp, h, row, col, keep):
            blk = gram[hp][(2 * h + row) * m_rows:(2 * h + row + 1) * m_rows, col * m_rows:(col + 1) * m_rows]
            return jnp.where(keep, blk, 0.0)

        m_ab = [quad(hp, h, 0, 0, before) for hp, h in heads]
        m_ak = [quad(hp, h, 0, 1, before) for hp, h in heads]
        n_rb = [quad(hp, h, 1, 0, upto) for hp, h in heads]
        n_rk = [quad(hp, h, 1, 1, upto) for hp, h in heads]
        tinv = [eye + jnp.where(pair_mask, m, 0.0) for m in m_ab]
        for lm in level_masks:
            prod = [_mm(jnp.where(lm, m, 0.0), t, _NN) for m, t in zip(m_ab, tinv)]
            tinv = [t + _mm(t, q, _NN) for t, q in zip(tinv, prod)]
        vh = [msk(vv[hp], h) for hp, h in heads]
        w1 = [_mm(m, x, _NN) for m, x in zip(m_ak, vh)]
        ap = [_mm(t, jnp.concatenate([msk(at[hp], h), w], axis=1), _NN)
              for t, (hp, h), w in zip(tinv, heads, w1)]
        nb = [_mm(m, x, _NN) for m, x in zip(n_rb, ap)]
        nv = [_mm(m, x, _NN) for m, x in zip(n_rk, vh)]

        def per_seq(x):
            return jnp.where(own_rows, jnp.concatenate([x] * SEQ_BLOCK, axis=1), 0.0)

        pairs = range(len(units))
        both = lambda xs, hp, cols: xs[HEAD_PAIR * hp][:, cols] + xs[HEAD_PAIR * hp + 1][:, cols]
        lo, hi = slice(0, LANES), slice(LANES, 2 * LANES)
        bh_x = [per_seq(bh[hp]) for hp in pairs]
        kh_x = [per_seq(kh[hp]) for hp in pairs]
        gam = [jnp.where(same_head, _mm(both(ap, hp, lo).T, bh_x[hp], _NN), 0.0) for hp in pairs]
        u = [jnp.where(same_head, _mm(jnp.concatenate([both(ap, hp, hi).T, vv[hp].T], axis=1),
                                      jnp.concatenate([bh_x[hp], kh_x[hp]], axis=0), _NN), 0.0)
             for hp in pairs]
        for hp, (j, pair) in enumerate(units):
            r_hat = rt[hp] + both(nb, hp, lo)
            y_zero = both(nb, hp, hi) + nv[HEAD_PAIR * hp] + nv[HEAD_PAIR * hp + 1]
            cols = pl.ds(pair * LANES, LANES)
            for n in range(SEQ_BLOCK):
                q = slice(n * sub, (n + 1) * sub)
                blk = slice(n * LANES, (n + 1) * LANES)
                st = state[pair][n]
                ys_ref[n, rs[j], cols] = _mm(r_hat[q], st, _NT) + y_zero[q]
                state[pair][n] = (st * gcv[hp][n * sub:n * sub + 1, :]
                                  + _mm(st, gam[hp][:, blk], _NN) + u[hp][:, blk])
        for pair in range(RW_PAIRS):
            for n in range(SEQ_BLOCK):
                s_ref[pair, n] = state[pair][n]
        return carry

    lax.fori_loop(0, n_sub // group, sub_chunk, 0)

    @pl.when(pl.program_id(1) == pl.num_programs(1) - 1)
    def _():
        _tiles_to_heads(s_ref, sl_ref, RW_PAIRS, transpose=False)

    y = ys_ref[...].reshape(rows, RW_WIDTH)
    inv = 1.0 / RW_HEAD
    mean = _head_sum(y, ones) * inv
    yc = y - mean
    var = _head_sum(yc * yc, ones, split=False) * inv
    y = yc * lax.rsqrt(var + RW_GN_EPS) * ln_w + ln_b
    bonus = _head_sum(r * kt * r_k, ones, split=False) * v
    _seq_block_store(y_ref, stage_ref, ((y + bonus) * g).reshape(shape3))


def _layer_state_specs(state, layer):
    tail = state.shape[1:]
    zeros = (0,) * len(tail)
    read = pl.BlockSpec((SEQ_BLOCK,) + tail, lambda b, c: (b,) + zeros)
    write = pl.BlockSpec((None, SEQ_BLOCK) + tail, lambda b, c: (layer, b) + zeros)
    return read, write, ([pl.BlockSpec(memory_space=pl.ANY)] if layer else [])


def _rw_call(p2, n, l, shift0, state, layer, depth, earlier, prm, t_chunk):
    assert t_chunk % min(MAX_SUB, t_chunk) == 0
    grid, in_specs, out_specs, in_shapes, out_shapes = _seq_specs(n, l, t_chunk, [RW_COLS], [RW_WIDTH])
    valid_in_chunk = min(t_chunk, l)
    kern = functools.partial(_rw_kernel, t_chunk=t_chunk, l_valid=valid_in_chunk, last_row=valid_in_chunk - 1,
                             layer=layer)
    st_in, st_out, st_earlier = _layer_state_specs(state, layer)
    sh_spec = pl.BlockSpec((SEQ_BLOCK, 1, RW_COLS), lambda b, c: (b, 0, 0))
    q_scr = pltpu.VMEM((SEQ_BLOCK, t_chunk, RW_WIDTH), F32)
    return pl.pallas_call(
        kern,
        out_shape=[jax.ShapeDtypeStruct(out_shapes[0], _branch_dtype(out_shapes[0])),
                   jax.ShapeDtypeStruct((n, 1, RW_COLS), F32),
                   jax.ShapeDtypeStruct((depth,) + state.shape, F32)],
        grid=grid,
        in_specs=in_specs + [sh_spec, st_in,
                             _const_spec((1, RW_COLS)), _const_spec((SUBLANES, RW_WIDTH)),
                             _const_spec((RW_LORA, RW_WIDTH)), _const_spec((RW_LORA, RW_WIDTH)),
                             _const_spec((RW_LORA, RW_WIDTH))] + st_earlier,
        out_specs=out_specs + [sh_spec, st_out],
        scratch_shapes=[pltpu.VMEM((SEQ_BLOCK, 1, RW_COLS), F32),
                        pltpu.VMEM((RW_PAIRS, SEQ_BLOCK, LANES, LANES), F32)] + [q_scr] * 9
        + _io_scratch(l, t_chunk, RW_COLS, RW_WIDTH),
        input_output_aliases={8: 2} if layer else {},
        compiler_params=_cparams("arbitrary", "arbitrary"),
        name="rwkv7_mixer",
    )(p2.reshape(in_shapes[0]), shift0, state, prm["rw_mu"], prm["rw_vec"], prm["rw_w2"], prm["rw_a2"],
      prm["rw_g2"], *([earlier] if layer else []))


def _hg_kernel(z_ref, s0_ref, lower_ref, ng_ref, *rest, t_chunk, l_valid, layer):
    (y_ref, sl_ref, s_ref, qs_ref, q_ref, k_ref, kh_ref, v_ref, b_ref, gc_ref, ys_ref, pad_ref,
     stage_ref) = rest[1 if layer else 0:]
    rows = SEQ_BLOCK * t_chunk
    sub = min(MAX_SUB, t_chunk)
    n_sub = t_chunk // sub

    @pl.when(pl.program_id(1) == 0)
    def _():
        _heads_to_tiles(s0_ref, s_ref, HG_PAIRS, transpose=True)

    z = _seq_block_load(z_ref, pad_ref).reshape(rows, 4 * HG_WIDTH)
    q = jax.nn.silu(z[:, 0:HG_WIDTH])
    f = z[:, HG_WIDTH:2 * HG_WIDTH]
    i = z[:, 2 * HG_WIDTH:3 * HG_WIDTH]
    og = z[:, 3 * HG_WIDTH:4 * HG_WIDTH]
    lower = lower_ref[...]
    fgate = lower + (1.0 - lower) * jax.nn.sigmoid(f)
    log_f = jnp.log(jnp.maximum(fgate, HG_GATE_FLOOR))
    k = 1.0 - fgate
    if l_valid < t_chunk:
        live = lax.broadcasted_iota(jnp.int32, (rows, 1), 0) % t_chunk < l_valid
        log_f = jnp.where(live, log_f, 0.0)
        k = jnp.where(live, k, 0.0)
    cum = _cumsum_groups(log_f, sub)
    shape_g = (rows // sub, sub, HG_WIDTH)
    cum_end = jnp.broadcast_to(cum.reshape(shape_g)[:, sub - 1:sub, :], shape_g).reshape(rows, HG_WIDTH)
    shape3 = (SEQ_BLOCK, t_chunk, HG_WIDTH)
    qs_ref[...] = (q * jnp.exp(cum)).reshape(shape3)
    q_ref[...] = q.reshape(shape3)
    k_ref[...] = k.reshape(shape3)
    kh_ref[...] = (k * jnp.exp(cum_end - cum)).reshape(shape3)
    v_ref[...] = i.reshape(shape3)
    b_ref[...] = cum.reshape(shape3)
    gc_ref[...] = jnp.exp(cum_end).reshape(shape3)

    m_rows = SEQ_BLOCK * sub
    ones = _head_ones()
    wide = (m_rows, SEQ_BLOCK * LANES)
    own_rows = (lax.broadcasted_iota(jnp.int32, wide, 1) // LANES
                == lax.broadcasted_iota(jnp.int32, wide, 0) // sub)
    tiles = (LANES, SEQ_BLOCK * LANES)
    same_head = (lax.broadcasted_iota(jnp.int32, tiles, 0) // HG_HEAD
                 == (lax.broadcasted_iota(jnp.int32, tiles, 1) % LANES) // HG_HEAD)
    step = lax.broadcasted_iota(jnp.int32, (sub, LANES), 0)

    def sub_chunk(c, carry):
        rs = pl.ds(pl.multiple_of(c * sub, sub), sub)
        for hp in range(HG_PAIRS):
            cols = pl.ds(hp * LANES, LANES)
            ld = lambda ref: ref[:, rs, cols].reshape(m_rows, LANES)
            qs, qq, kk, kh, vv, bb, gcv = (ld(qs_ref), ld(q_ref), ld(k_ref), ld(kh_ref), ld(v_ref),
                                           ld(b_ref), ld(gc_ref))
            state = [s_ref[hp, n] for n in range(SEQ_BLOCK)]
            kh_x = jnp.where(own_rows, jnp.concatenate([kh] * SEQ_BLOCK, axis=1), 0.0)
            u = jnp.where(same_head, _mm(vv.T, kh_x, _NN), 0.0)
            for n in range(SEQ_BLOCK):
                sl = slice(n * sub, (n + 1) * sub)
                bn, qn, kn, vn = bb[sl], qq[sl], kk[sl], vv[sl]
                prods = []
                for t in range(sub):
                    keep = step <= t
                    prods.append(jnp.where(keep, qn[t:t + 1, :] * kn * jnp.exp(bn[t:t + 1, :] - bn), 0.0))
                att = _dot(jnp.concatenate(prods, axis=0).astype(BF16), ones)
                o_rows = [jnp.sum(att[t * sub:(t + 1) * sub] * vn, axis=0, keepdims=True)
                          for t in range(sub)]
                ys_ref[n, rs, cols] = _mm(qs[sl], state[n], _NT) + jnp.concatenate(o_rows, axis=0)
                blk = slice(n * LANES, (n + 1) * LANES)
                s_ref[hp, n] = state[n] * gcv[n * sub:n * sub + 1, :] + u[:, blk]
        return carry

    lax.fori_loop(0, n_sub, sub_chunk, 0)

    @pl.when(pl.program_id(1) == pl.num_programs(1) - 1)
    def _():
        _tiles_to_heads(s_ref, sl_ref, HG_PAIRS, transpose=True)

    o = ys_ref[...].reshape(rows, HG_WIDTH)
    ms = _head_sum(o * o, ones, split=False) * (1.0 / HG_HEAD)
    o = o * lax.rsqrt(ms + RMS_EPS) * ng_ref[...] * jax.nn.sigmoid(og)
    _seq_block_store(y_ref, stage_ref, o.reshape(shape3))


def _hg_call(z2, n, l, state, layer, depth, earlier, lower, norm_g, t_chunk):
    assert t_chunk % min(MAX_SUB, t_chunk) == 0
    grid, in_specs, out_specs, in_shapes, out_shapes = _seq_specs(n, l, t_chunk, [4 * HG_WIDTH], [HG_WIDTH])
    kern = functools.partial(_hg_kernel, t_chunk=t_chunk, l_valid=min(t_chunk, l), layer=layer)
    st_in, st_out, st_earlier = _layer_state_specs(state, layer)
    q_scr = pltpu.VMEM((SEQ_BLOCK, t_chunk, HG_WIDTH), F32)
    return pl.pallas_call(
        kern,
        out_shape=[jax.ShapeDtypeStruct(out_shapes[0], _branch_dtype(out_shapes[0])),
                   jax.ShapeDtypeStruct((depth,) + state.shape, F32)],
        grid=grid,
        in_specs=in_specs + [st_in, _const_spec((1, HG_WIDTH)), _const_spec((1, HG_WIDTH))] + st_earlier,
        out_specs=out_specs + [st_out],
        scratch_shapes=[pltpu.VMEM((HG_PAIRS, SEQ_BLOCK, LANES, LANES), F32)] + [q_scr] * 8
        + _io_scratch(l, t_chunk, 4 * HG_WIDTH, HG_WIDTH),
        input_output_aliases={4: 1} if layer else {},
        compiler_params=_cparams("arbitrary", "arbitrary"),
        name="hgrn2_mixer",
    )(z2.reshape(in_shapes[0]), state, lower.reshape(1, HG_WIDTH), norm_g.reshape(1, HG_WIDTH),
      *([earlier] if layer else []))


def _merge_ffn_kernel(*refs, l, tm, has_state, final_norm):
    (x_ref, ya_ref, yb_ref, yc_ref, gates_ref, gt1_ref, la_ref, lb_ref, lc_ref, wo_ref) = refs[:10]
    refs = refs[10:]
    if has_state:
        (g_ref, sh_ref, sc_ref, gt_ref, wup_ref, cw_ref, cb_ref, wdn_ref, fg_ref, st_ref,
         o_ref, tail_ref, h_ref, act_ref, hist_ref, buf_ref, old_ref) = refs
    else:
        (g_ref, sh_ref, sc_ref, gt_ref, wup_ref, cw_ref, cb_ref, wdn_ref, fg_ref,
         o_ref, tail_ref, h_ref, act_ref, hist_ref) = refs
    i = pl.program_id(0)
    n_chunks = D_FF // FF_CHUNK
    n_hist = CONV_W - 1

    if has_state:
        t_idx = lax.broadcasted_iota(jnp.int32, (tm, 1), 0) % l
        after = [t_idx >= j + 1 for j in range(n_hist)]
        old_ref[...] = jnp.zeros_like(old_ref)
        buf_ref[:, 0:SUBLANES, :] = jnp.zeros((2, SUBLANES, FF_CHUNK), F32)
        for s in range(tm // l):
            for j in range(n_hist):
                old_ref[j, s * l:s * l + j + 1, :] = st_ref[s, n_hist - 1 - j:n_hist, :]
    else:
        @pl.when(i % (l // tm) == 0)
        def _():
            hist_ref[...] = jnp.zeros_like(hist_ref)

    d = D_MODEL
    gates = _tile(gates_ref)
    lift = lambda y_ref, w_ref: _dot(_tile(y_ref)[...].astype(BF16), w_ref[...])
    merged = (gates[:, 0:d].astype(F32) * lift(ya_ref, la_ref)
              + gates[:, d:2 * d].astype(F32) * lift(yb_ref, lb_ref)
              + gates[:, 2 * d:3 * d].astype(F32) * lift(yc_ref, lc_ref))
    x = _tile(x_ref)[...] + gt1_ref[0] * _dot(merged.astype(BF16), wo_ref[...])
    h = _rms_modulate(x, g_ref[...], sh_ref[0], sc_ref[0])
    groups = tm // SUBLANES
    if not has_state:
        h = jnp.swapaxes(h.reshape(SUBLANES, groups, D_MODEL), 0, 1).reshape(tm, D_MODEL)
        row8 = lax.broadcasted_iota(jnp.int32, (SUBLANES, FF_CHUNK), 0)
    h_ref[...] = h.astype(BF16)

    def conv_half(c, half):
        col0 = half * D_FF + c * FF_CHUNK
        cols = slice(col0, col0 + FF_CHUNK)
        up = _dot(h_ref[...], wup_ref[:, cols])
        if has_state:
            buf_ref[half, SUBLANES:SUBLANES + tm, :] = up
            prev = [jnp.where(after[j], buf_ref[half, SUBLANES - 1 - j:SUBLANES - 1 - j + tm, :], 0.0)
                    + old_ref[j, :, cols] for j in range(n_hist)]
            for s in range(tm // l):
                r0 = SUBLANES + (s + 1) * l - n_hist
                tail_ref[s, :, cols] = buf_ref[half, r0:r0 + n_hist, :]
        else:
            late = [up[tm - (j + 1) * SUBLANES:tm - j * SUBLANES, :] for j in range(n_hist)]
            first = [jnp.where(row8 == 0, pltpu.roll(hist_ref[c, half, j], 1, 0), pltpu.roll(late[j], 1, 0))
                     for j in range(n_hist)]
            for j in range(n_hist):
                hist_ref[c, half, j] = late[j]
            prev = [jnp.concatenate([first[0], up[0:tm - SUBLANES, :]], axis=0),
                    jnp.concatenate([first[1], first[0], up[0:tm - 2 * SUBLANES, :]], axis=0)]
            tail_ref[0, 0:1, cols] = up[tm - SUBLANES - 1:tm - SUBLANES, :]
            tail_ref[0, 1:2, cols] = up[tm - 1:tm, :]
        return (cb_ref[:, cols] + cw_ref[0:1, cols] * prev[1] + cw_ref[1:2, cols] * prev[0]
                + cw_ref[2:3, cols] * up)

    for c in range(n_chunks):
        act = _gelu(conv_half(c, 0)) * conv_half(c, 1)
        act_ref[:, c * FF_CHUNK:(c + 1) * FF_CHUNK] = act.astype(BF16)
    acc = _dot(act_ref[...], wdn_ref[...])
    if not has_state:
        acc = jnp.swapaxes(acc.reshape(groups, SUBLANES, D_MODEL), 0, 1).reshape(tm, D_MODEL)
    out = x + gt_ref[0] * acc
    if final_norm:
        ms = jnp.mean(out * out, axis=-1, keepdims=True)
        out = out * lax.rsqrt(ms + RMS_EPS) * fg_ref[...]
    _tile(o_ref)[...] = out


def _merge_ffn_call(x, n, l, ya, yb, yc, gates, gate1, g, shift, scale, gate2, prm, final_g, conv_state, tm,
                    final_norm):
    assert CONV_W == 3
    d = x.shape[-1]
    has_state = conv_state is not None
    g1_arr, g1_spec = _mod_operand(gate1, l, tm)
    sh_arr, sh_spec = _mod_operand(shift, l, tm)
    sc_arr, sc_spec = _mod_operand(scale, l, tm)
    gt_arr, gt_spec = _mod_operand(gate2, l, tm)
    shape, x_spec = _tok_layout(n, l, tm, d)
    row_spec = lambda w: _tok_layout(n, l, tm, w)[1]
    stacks, layer = prm["stacks"], prm["layer"]
    w_spec = lambda name: _layer_weight_spec(stacks[name], layer)
    in_specs = [x_spec, row_spec(S5_WIDTH), row_spec(RW_WIDTH), row_spec(HG_WIDTH), row_spec(N_BRANCH * d),
                g1_spec, w_spec("w_lift_a"), w_spec("w_lift_b"), w_spec("w_lift_c"), w_spec("w_out"),
                _const_spec((1, d)), sh_spec, sc_spec, gt_spec,
                w_spec("w_up"), _const_spec((CONV_W, 2 * D_FF)), _const_spec((1, 2 * D_FF)),
                w_spec("w_down"), _const_spec((1, d))]
    operands = [x, ya, yb, yc, gates, g1_arr, stacks["w_lift_a"], stacks["w_lift_b"], stacks["w_lift_c"],
                stacks["w_out"], g.reshape(1, d), sh_arr, sc_arr, gt_arr, stacks["w_up"], prm["conv_w"],
                prm["conv_b"], stacks["w_down"], final_g.reshape(1, d)]
    scratch = [pltpu.VMEM((tm, d), BF16), pltpu.VMEM((tm, D_FF), BF16),
               pltpu.VMEM((D_FF // FF_CHUNK, 2, CONV_W - 1, SUBLANES, FF_CHUNK), F32)]
    if has_state:
        assert tm % l == 0 and l >= CONV_W - 1
        block = (tm // l, CONV_W - 1, 2 * D_FF)
        in_specs.append(pl.BlockSpec((None,) + block, lambda i: (layer, i, 0, 0)))
        operands.append(conv_state)
        tail_spec = pl.BlockSpec(block, lambda i: (i, 0, 0))
        scratch += [pltpu.VMEM((2, tm + SUBLANES, FF_CHUNK), F32), pltpu.VMEM((CONV_W - 1, tm, 2 * D_FF), F32)]
    else:
        assert l % tm == 0
        per = l // tm
        tail_spec = pl.BlockSpec((1, CONV_W - 1, 2 * D_FF), lambda i: (i // per, 0, 0))
    kern = functools.partial(_merge_ffn_kernel, l=l, tm=tm, has_state=has_state, final_norm=final_norm)
    return pl.pallas_call(
        kern,
        out_shape=[jax.ShapeDtypeStruct(shape, F32), jax.ShapeDtypeStruct((n, CONV_W - 1, 2 * D_FF), F32)],
        grid=(n * l // tm,),
        in_specs=in_specs,
        out_specs=[x_spec, tail_spec],
        scratch_shapes=scratch,
        compiler_params=_cparams("arbitrary"),
        name="merge_conv_ffn",
    )(*operands)


def _prepare_layer(p):
    out = {}
    w_in = p["w_in"]
    c1 = S5_WIDTH
    c2 = c1 + RW_COLS
    c3 = c2 + 4 * HG_WIDTH
    out["w_in_a"] = w_in[:, :c1].astype(BF16)
    out["w_in_b"] = w_in[:, c1:c2].astype(BF16)
    out["w_in_c"] = w_in[:, c2:c3].astype(BF16)
    out["w_in_g"] = w_in[:, c3:].astype(BF16)
    out["conv_w"] = p["conv_w"]
    out["conv_b"] = p["conv_b"].reshape(1, 2 * D_FF)

    lr = p["s5_lambda_re"]
    li = p["s5_lambda_im"]
    dt = jnp.exp(p["s5_log_dt"])[:, None]
    mag = jnp.exp(lr * dt)
    ar = mag * jnp.cos(li * dt)
    ai = mag * jnp.sin(li * dt)
    den = lr * lr + li * li
    zr = ((ar - 1.0) * lr + ai * li) / den
    zi = (ai * lr - (ar - 1.0) * li) / den
    bbr = zr[..., None] * p["s5_b_re"] - zi[..., None] * p["s5_b_im"]
    bbi = zr[..., None] * p["s5_b_im"] + zi[..., None] * p["s5_b_re"]
    eye = jnp.eye(S5_GROUPS, dtype=F32)
    bmat = jnp.einsum("gh,rgpc->gcrhp", eye, jnp.stack([bbr, bbi])).reshape(S5_WIDTH, 2 * S5_FLAT)
    cmat = jnp.einsum("hg,rgcp->rhpgc", eye, jnp.stack([p["s5_c_re"], -p["s5_c_im"]])).reshape(
        2 * S5_FLAT, S5_WIDTH)
    out["s5_b"] = bmat.astype(BF16)
    out["s5_c"] = cmat.astype(BF16)
    out["s5_a"] = jnp.stack([ar.reshape(S5_FLAT), ai.reshape(S5_FLAT)])
    out["s5_d"] = p["s5_d"].reshape(1, S5_WIDTH)
    out["s5_w_glu"] = p["s5_w_glu"].astype(BF16)
    out["s5_b_glu"] = p["s5_b_glu"].reshape(1, S5_WIDTH)

    out["rw_mu"] = p["rw_mu"].reshape(1, RW_COLS)
    out["rw_vec"] = jnp.stack([p["rw_w0"], p["rw_a0"], p["rw_k_k"], p["rw_k_a"],
                               p["rw_r_k"].reshape(RW_WIDTH), p["rw_ln_w"], p["rw_ln_b"],
                               jnp.zeros((RW_WIDTH,), F32)])
    zw = jnp.zeros((RW_LORA, RW_WIDTH), F32)
    out["rw_w2"] = zw.at[0:RW_DECAY_LORA].set(p["rw_w2"]).astype(BF16)
    out["rw_a2"] = zw.at[RW_DECAY_LORA:RW_DECAY_LORA + RW_AAA_LORA].set(p["rw_a2"]).astype(BF16)
    out["rw_g2"] = zw.at[RW_DECAY_LORA + RW_AAA_LORA:].set(p["rw_g2"]).astype(BF16)
    out["hg_norm"] = p["hg_norm"]
    out["g_mix"] = p["g_mix"]
    out["g_ffn"] = p["g_ffn"]
    out["b_ada"] = p["b_ada"]
    return out


def _chunk_for(l, sub, t_chunk):
    return t_chunk if l % t_chunk == 0 else -(-l // sub) * sub


def _run_trunk(x, mods, st_s5, st_shift, st_rw, st_hg, st_conv, lower, final_g, prms, t_chunk):
    n, l, d = x.shape
    rows = n * l
    tm = min(ROW_TILE, rows)
    depth = len(prms)
    assert (l % tm == 0) == (l % t_chunk == 0)
    h = x.reshape(_tok_layout(n, l, tm, d)[0])
    out_s5, out_shift, out_conv = [], [], []
    rw_new = hg_new = None
    for layer in range(depth):
        prm = prms[layer]
        sh1, sc1, gt1, sh2, sc2, gt2 = jnp.split(mods[layer], 6, axis=-1)
        za, zb, zc, gates = _in_proj_call(h, n, l, prm["g_mix"], sh1, sc1, prm["w_in_a"], prm["w_in_b"],
                                          prm["w_in_c"], prm["w_in_g"], tm)

        s5_in = jnp.concatenate([st_s5[layer][..., 0].reshape(n, S5_FLAT),
                                 st_s5[layer][..., 1].reshape(n, S5_FLAT)], axis=1)
        s5_chunk = S5_CHUNK_FACTOR * t_chunk if l % (S5_CHUNK_FACTOR * t_chunk) == 0 else t_chunk
        ya, s5_new = _s5_call(za, n, l, s5_in, prm, _chunk_for(l, SUBLANES, s5_chunk))
        yb, shift_new, rw_new = _rw_call(zb, n, l, st_shift[layer].reshape(n, 1, RW_COLS), st_rw[layer], layer, depth, rw_new,
                                         prm, _chunk_for(l, SUBLANES, t_chunk))
        yc, hg_new = _hg_call(zc, n, l, st_hg[layer], layer, depth, hg_new, lower[layer], prm["hg_norm"],
                              _chunk_for(l, SUBLANES, t_chunk))

        ffn_tm = tm if st_conv is None else min(tm, 128)
        h, conv_new = _merge_ffn_call(h, n, l, ya, yb, yc, gates, gt1, prm["g_ffn"], sh2, sc2, gt2, prm, final_g,
                                      st_conv, ffn_tm,
                                      final_norm=(layer == depth - 1))
        out_s5.append(jnp.stack([s5_new[:, :S5_FLAT].reshape(n, S5_GROUPS, S5_STATE),
                                 s5_new[:, S5_FLAT:].reshape(n, S5_GROUPS, S5_STATE)], axis=-1))
        out_shift.append(shift_new.reshape(n, RW_COLS))
        out_conv.append(conv_new)
    y = h.reshape(n, l, d).astype(x.dtype)
    return y, (jnp.stack(out_s5), jnp.stack(out_shift), rw_new, hg_new, jnp.stack(out_conv))


def kernel(x_prompt, x_sample, c_prompt, c_sample, state_s5, state_rwkv_shift, state_rwkv, state_hgrn, state_ffn_conv, w_ada, b_ada, g_mix, g_ffn, w_in, s5_lambda_re, s5_lambda_im, s5_log_dt, s5_b_re, s5_b_im, s5_c_re, s5_c_im, s5_d, s5_w_glu, s5_b_glu, rw_mu, rw_w0, rw_w2, rw_a0, rw_a2, rw_g2, rw_k_k, rw_k_a, rw_r_k, rw_ln_w, rw_ln_b, hg_lb, hg_norm, w_lift_a, w_lift_b, w_lift_c, w_out, w_up, conv_w, conv_b, w_down, final_g):
    per_layer = {
        "w_ada": w_ada, "b_ada": b_ada, "g_mix": g_mix, "g_ffn": g_ffn, "w_in": w_in,
        "s5_lambda_re": s5_lambda_re, "s5_lambda_im": s5_lambda_im, "s5_log_dt": s5_log_dt,
        "s5_b_re": s5_b_re, "s5_b_im": s5_b_im, "s5_c_re": s5_c_re, "s5_c_im": s5_c_im,
        "s5_d": s5_d, "s5_w_glu": s5_w_glu, "s5_b_glu": s5_b_glu,
        "rw_mu": rw_mu, "rw_w0": rw_w0, "rw_w2": rw_w2, "rw_a0": rw_a0, "rw_a2": rw_a2,
        "rw_g2": rw_g2, "rw_k_k": rw_k_k, "rw_k_a": rw_k_a, "rw_r_k": rw_r_k,
        "rw_ln_w": rw_ln_w, "rw_ln_b": rw_ln_b, "hg_norm": hg_norm,
        "w_lift_a": w_lift_a, "w_lift_b": w_lift_b, "w_lift_c": w_lift_c, "w_out": w_out,
        "w_up": w_up, "conv_w": conv_w, "conv_b": conv_b, "w_down": w_down,
    }
    depth = w_ada.shape[0]
    prms = [_prepare_layer({k: v[layer] for k, v in per_layer.items()}) for layer in range(depth)]

    lbp = jax.nn.softmax(hg_lb.astype(F32), axis=0)
    lower = jnp.cumsum(lbp, axis=0) - lbp[0]

    nb = x_prompt.shape[0]
    ns = x_sample.shape[0]
    c_all = jnp.concatenate([c_prompt, c_sample], axis=0).astype(F32)
    stacks = {name: per_layer[name].astype(BF16)
              for name in ("w_ada", "w_lift_a", "w_lift_b", "w_lift_c", "w_out", "w_up", "w_down")}
    for layer in range(depth):
        prms[layer]["stacks"] = stacks
        prms[layer]["layer"] = layer
    mods = [_ada_call(c_all, stacks["w_ada"], layer, prms[layer]["b_ada"]) for layer in range(depth)]
    mods_p = [m[:nb] for m in mods]
    mods_s = [m[nb:] for m in mods]

    z_s5 = jnp.zeros((depth, nb) + state_s5.shape[2:], F32)
    z_shift = jnp.zeros((depth, nb) + state_rwkv_shift.shape[2:], F32)
    z_rw = jnp.zeros((depth, nb) + state_rwkv.shape[2:], F32)
    z_hg = jnp.zeros((depth, nb) + state_hgrn.shape[2:], F32)

    y_prompt, (s5_p, shift_p, rw_p, hg_p, conv_p) = _run_trunk(
        x_prompt, mods_p, z_s5, z_shift, z_rw, z_hg, None, lower, final_g, prms, t_chunk=64)
    y_sample, (s5_s, shift_s, rw_s, hg_s, conv_s) = _run_trunk(
        x_sample, mods_s, state_s5, state_rwkv_shift, state_rwkv, state_hgrn, state_ffn_conv,
        lower, final_g, prms, t_chunk=64)
    return (y_prompt, y_sample, s5_p, shift_p, rw_p, hg_p, conv_p, s5_s, shift_s, rw_s, hg_s, conv_s)
```
